```python
import math
import jax
import jax.numpy as jnp
from jax import lax
import numpy as np

D_MODEL = 1024
BATCH = 8
SEQ = 2048
DEPTH = 2

CTX_LEN = 256
GRID_W = 64
NORM_EPS = 1e-6
N_MOD = 6

GROUP_W = D_MODEL // 4
HEAD_DIM = 64

RWKV_HEADS = GROUP_W // HEAD_DIM
RWKV_DECAY_LORA = 32
RWKV_AAA_LORA = 32
RWKV_GATE_LORA = 64
RWKV_GN_EPS = 64e-5
RWKV_COLS = 3 * GROUP_W + RWKV_GATE_LORA + 2 * RWKV_DECAY_LORA + 2 * RWKV_AAA_LORA

RET_HEADS = GROUP_W // HEAD_DIM
RET_CHUNK = 128
RET_GN_EPS = 1e-5
RET_FWD_OFFSET = 5.0
RET_BWD_OFFSET = 5.5
ROPE_BASE = 10000.0
RET_COLS = 4 * GROUP_W

LRU_BLOCKS = 4
LRU_BLOCK_W = GROUP_W // LRU_BLOCKS
LRU_CONV_W = 4
LRU_C = 8.0
LRU_COLS = 2 * GROUP_W

S5_CH = 16
S5_GROUPS = GROUP_W // S5_CH
S5_STATE = 64
S5_DT_MIN = 0.001
S5_DT_MAX = 0.1
S5_COLS = GROUP_W

IN_COLS = RWKV_COLS + RET_COLS + LRU_COLS + S5_COLS

PEER_HEADS = 8
PEER_NKEYS = 128
PEER_EXPERTS = PEER_NKEYS * PEER_NKEYS
PEER_DKEY = 128
PEER_TOPK = 16
PEER_BLOCK = 128

kernel_name = 'hybrid_parallel_heads_dit_peer'


def _f32(*ts):
    return [t.astype(jnp.float32) for t in ts]


def rmsnorm(x, g):
    xf = x.astype(jnp.float32)
    y = xf * lax.rsqrt(jnp.mean(xf * xf, axis=-1, keepdims=True) + NORM_EPS)
    return (y * g.astype(jnp.float32)).astype(x.dtype)


def modulate(h, shift, scale):
    return h * (1.0 + scale) + shift


def head_norm(y, eps):
    mu = jnp.mean(y, axis=-1, keepdims=True)
    var = jnp.mean(jnp.square(y - mu), axis=-1, keepdims=True)
    return (y - mu) * lax.rsqrt(var + eps)


def to_heads(t, n_heads):
    return t.reshape(t.shape[0], t.shape[1], n_heads, t.shape[-1] // n_heads)


def maybe_flip(t, rev):
    return jnp.flip(t, axis=1) if rev else t


def shift_prev(z):
    return jnp.pad(z, ((0, 0), (1, 0), (0, 0)))[:, :-1]


def shift_next(z):
    return jnp.pad(z, ((0, 0), (0, 1), (0, 0)))[:, 1:]


def linear_combine(left, right):
    a_l, b_l = left
    a_r, b_r = right
    return a_l * a_r, a_r * b_l + b_r


def linear_scan(a, b, h0, reverse):
    a_cum, h = lax.associative_scan(linear_combine, (a, b), axis=1, reverse=reverse)
    h = h + a_cum * h0[:, None]
    h_final = h[:, 0] if reverse else h[:, -1]
    return h, h_final


def rwkv7_features(za, mu, w0, w_up, a0, a_up, g_up, k_k, k_a):
    z = za + mu[0] * (shift_prev(za) - za) + mu[1] * (shift_next(za) - za)
    splits = [GROUP_W, 2 * GROUP_W, 3 * GROUP_W, 3 * GROUP_W + RWKV_GATE_LORA,
              3 * GROUP_W + RWKV_GATE_LORA + 2 * RWKV_DECAY_LORA]
    r, k, v, g_low, w_low, a_low = jnp.split(z, splits, axis=-1)
    g = jax.nn.sigmoid(g_low) @ g_up
    kk = to_heads(k * k_k, RWKV_HEADS)
    kk = kk / jnp.maximum(jnp.sqrt(jnp.sum(kk * kk, axis=-1, keepdims=True)), 1e-12)
    per_dir = []
    for d in range(2):
        w_lo = w_low[..., d * RWKV_DECAY_LORA:(d + 1) * RWKV_DECAY_LORA]
        a_lo = a_low[..., d * RWKV_AAA_LORA:(d + 1) * RWKV_AAA_LORA]
        w_log = -jax.nn.softplus(-(w0[d] + jnp.tanh(w_lo) @ w_up[d])) - 0.5
        decay = jnp.exp(-jnp.exp(w_log))
        a = jax.nn.sigmoid(a0[d] + a_lo @ a_up[d])
        k_d = k * (1.0 + (a - 1.0) * k_a)
        per_dir.append((to_heads(decay, RWKV_HEADS), to_heads(k_d, RWKV_HEADS),
                        -kk, kk * to_heads(a, RWKV_HEADS)))
    return to_heads(r, RWKV_HEADS), to_heads(k, RWKV_HEADS), to_heads(v, RWKV_HEADS), g, per_dir


def rwkv7_scan(r, v, w, k, a_vec, b_vec, s0, reverse):
    def step(s, inp):
        r_t, v_t, w_t, k_t, a_t, b_t = inp
        sa = jnp.einsum('bhvk,bhk->bhv', s, a_t)
        s = s * w_t[:, :, None, :] + sa[..., None] * b_t[:, :, None, :] + v_t[..., None] * k_t[:, :, None, :]
        return s, jnp.einsum('bhvk,bhk->bhv', s, r_t)
    xs = tuple(jnp.swapaxes(t, 0, 1) for t in (r, v, w, k, a_vec, b_vec))
    s_final, y = lax.scan(step, s0, xs, reverse=reverse)
    return jnp.swapaxes(y, 0, 1), s_final


def rwkv7_output(feat, y, r_k, ln_g, ln_b):
    r, k, v, g, _ = feat
    n_b, seq_len = g.shape[:2]
    yn = head_norm(y, RWKV_GN_EPS) * ln_g.reshape(RWKV_HEADS, HEAD_DIM) + ln_b.reshape(RWKV_HEADS, HEAD_DIM)
    bonus = jnp.sum(r * k * r_k, axis=-1, keepdims=True) * v
    return (yn + bonus).reshape(n_b, seq_len, GROUP_W) * g


def rwkv7_mixer(za_ctx, za_lat, mu, w0, w_up, a0, a_up, g_up, k_k, k_a, r_k, ln_g, ln_b):
    mu, w0, w_up, a0, a_up, g_up, k_k, k_a, r_k, ln_g, ln_b = _f32(mu, w0, w_up, a0, a_up, g_up, k_k, k_a, r_k, ln_g, ln_b)
    feats = [rwkv7_features(z, mu, w0, w_up, a0, a_up, g_up, k_k, k_a) for z in (za_ctx, za_lat)]
    s0 = jnp.zeros((za_lat.shape[0], RWKV_HEADS, HEAD_DIM, HEAD_DIM), jnp.float32)
    y_ctx, y_lat = 0.0, 0.0
    for d, rev in enumerate((False, True)):
        r, _, v, _, dirs = feats[0]
        yc, s_ctx = rwkv7_scan(r, v, *dirs[d], s0, rev)
        r, _, v, _, dirs = feats[1]
        yl, _ = rwkv7_scan(r, v, *dirs[d], s_ctx, rev)
        y_ctx, y_lat = y_ctx + yc, y_lat + yl
    return rwkv7_output(feats[0], y_ctx, r_k, ln_g, ln_b), rwkv7_output(feats[1], y_lat, r_k, ln_g, ln_b)


def rope(x, pos):
    half = x.shape[-1] // 2
    freqs = ROPE_BASE ** (-jnp.arange(half, dtype=jnp.float32) / half)
    ang = pos.astype(jnp.float32)[:, None] * freqs[None]
    cos = jnp.cos(ang)[None, :, None, :]
    sin = jnp.sin(ang)[None, :, None, :]
    x1, x2 = x[..., :half], x[..., half:]
    return jnp.concatenate([x1 * cos - x2 * sin, x1 * sin + x2 * cos], axis=-1)


def retention_log_gammas(offset):
    return jnp.log1p(-jnp.exp2(-(offset + jnp.arange(RET_HEADS, dtype=jnp.float32))))


def retention_chunkwise(q, k, v, log_gamma, strict, s0):
    n_b, seq_len, n_h, _ = q.shape
    n_c = seq_len // RET_CHUNK
    idx = jnp.arange(RET_CHUNK, dtype=jnp.float32)
    diff = idx[:, None] - idx[None, :]
    mask = diff > 0 if strict else diff >= 0
    inner_decay = jnp.where(mask[None], jnp.exp(log_gamma[:, None, None] * jnp.where(mask, diff, 0.0)[None]), 0.0)
    read_decay = jnp.exp(log_gamma[:, None] * (idx + 1.0)[None])
    write_decay = jnp.exp(log_gamma[:, None] * (RET_CHUNK - 1.0 - idx)[None])
    chunk_decay = jnp.exp(log_gamma * RET_CHUNK)

    def chunks(t):
        return t.reshape(n_b, n_c, RET_CHUNK, n_h, t.shape[-1]).transpose(1, 0, 3, 2, 4)

    def step(s, inp):
        q_i, k_i, v_i = inp
        att = jnp.einsum('bhid,bhjd->bhij', q_i, k_i) * inner_decay
        o = (jnp.einsum('bhij,bhjv->bhiv', att, v_i)
             + jnp.einsum('bhid,bhdv->bhiv', q_i, s) * read_decay[None, :, :, None])
        s = s * chunk_decay[None, :, None, None] + jnp.einsum('bhjd,bhjv->bhdv', k_i * write_decay[None, :, :, None], v_i)
        return s, o

    s_final, o = lax.scan(step, s0, (chunks(q), chunks(k), chunks(v)))
    return o.transpose(1, 0, 3, 2, 4).reshape(n_b, seq_len, n_h, -1), s_final


def retention_mixer(zb_ctx, zb_lat, gn_g):
    (gn_g,) = _f32(gn_g)
    n_b, l_ctx = zb_ctx.shape[:2]
    l_lat = zb_lat.shape[1]
    positions = (jnp.arange(l_ctx), l_ctx + jnp.arange(l_lat))
    seqs = []
    for z, pos in zip((zb_ctx, zb_lat), positions):
        q, k, v, gate = jnp.split(z, 4, axis=-1)
        q = rope(to_heads(q, RET_HEADS), pos) * HEAD_DIM ** -0.5
        k = rope(to_heads(k, RET_HEADS), pos)
        seqs.append((q, k, to_heads(v, RET_HEADS), gate))
    s0 = jnp.zeros((n_b, RET_HEADS, HEAD_DIM, HEAD_DIM), jnp.float32)
    (qc, kc, vc, gc), (ql, kl, vl, gl) = seqs
    y_ctx, y_lat = 0.0, 0.0
    for offset, rev in ((RET_FWD_OFFSET, False), (RET_BWD_OFFSET, True)):
        lg = retention_log_gammas(offset)
        oc, s_ctx = retention_chunkwise(maybe_flip(qc, rev), maybe_flip(kc, rev), maybe_flip(vc, rev), lg, rev, s0)
        ol, _ = retention_chunkwise(maybe_flip(ql, rev), maybe_flip(kl, rev), maybe_flip(vl, rev), lg, rev, s_ctx)
        y_ctx, y_lat = y_ctx + maybe_flip(oc, rev), y_lat + maybe_flip(ol, rev)
    g_h = gn_g.reshape(RET_HEADS, HEAD_DIM)
    out_ctx = jax.nn.silu(gc) * (head_norm(y_ctx, RET_GN_EPS) * g_h).reshape(n_b, l_ctx, GROUP_W)
    out_lat = jax.nn.silu(gl) * (head_norm(y_lat, RET_GN_EPS) * g_h).reshape(n_b, l_lat, GROUP_W)
    return out_ctx, out_lat


def dwconv_centred(z, w, b):
    left = LRU_CONV_W // 2
    right = LRU_CONV_W - 1 - left
    y = lax.conv_general_dilated(z, w[:, None, :], window_strides=(1,), padding=[(left, right)],
                                 dimension_numbers=('NWC', 'WIO', 'NWC'), feature_group_count=z.shape[-1])
    return y + b


def rglru_coeffs(xc, wa, ba, wx, bx, lam):
    xb = to_heads(xc, LRU_BLOCKS)
    r = jax.nn.sigmoid(jnp.einsum('blhi,hij->blhj', xb, wa).reshape(xc.shape) + ba)
    i = jax.nn.sigmoid(jnp.einsum('blhi,hij->blhj', xb, wx).reshape(xc.shape) + bx)
    log_a = -LRU_C * r * jax.nn.softplus(-lam)
    a = jnp.exp(log_a)
    b = jnp.sqrt(-jnp.expm1(2.0 * log_a)) * (i * xc)
    return a, b


def rglru_mixer(zc_ctx, zc_lat, conv_w, conv_b, wa, ba, wx, bx, lam):
    conv_w, conv_b, wa, ba, wx, bx, lam = _f32(conv_w, conv_b, wa, ba, wx, bx, lam)
    xs, gates = [], []
    for z in (zc_ctx, zc_lat):
        x_in, gate = jnp.split(z, 2, axis=-1)
        xs.append(dwconv_centred(x_in, conv_w, conv_b))
        gates.append(gate)
    h0 = jnp.zeros((zc_lat.shape[0], GROUP_W), jnp.float32)
    h_ctx, h_lat = 0.0, 0.0
    for d, rev in enumerate((False, True)):
        a, b = rglru_coeffs(xs[0], wa[d], ba[d], wx[d], bx[d], lam[d])
        hc, hc_final = linear_scan(a, b, h0, rev)
        a, b = rglru_coeffs(xs[1], wa[d], ba[d], wx[d], bx[d], lam[d])
        hl, _ = linear_scan(a, b, hc_final, rev)
        h_ctx, h_lat = h_ctx + hc, h_lat + hl
    return jax.nn.gelu(gates[0]) * h_ctx, jax.nn.gelu(gates[1]) * h_lat


def grid_to_colmajor(z):
    n_b, seq_len, ch = z.shape
    rows = seq_len // GRID_W
    return z.reshape(n_b, rows, GRID_W, ch).transpose(0, 2, 1, 3).reshape(n_b, seq_len, ch)


def colmajor_to_grid(z):
    n_b, seq_len, ch = z.shape
    rows = seq_len // GRID_W
    return z.reshape(n_b, GRID_W, rows, ch).transpose(0, 2, 1, 3).reshape(n_b, seq_len, ch)


def s5_discretise(a_re, a_im, log_dt, b_re, b_im):
    lam = lax.complex(a_re, a_im)
    lam_bar = jnp.exp(lam * jnp.exp(log_dt)[:, None])
    b_bar = ((lam_bar - 1.0) / lam)[..., None] * lax.complex(b_re, b_im)
    return lam_bar, b_bar


def s5_direction(u, lam_bar, b_bar, c_re, c_im, h0, reverse):
    n_b, seq_len, _ = u.shape
    ug = u.reshape(n_b, seq_len, S5_GROUPS, S5_CH).astype(jnp.complex64)
    bu = jnp.einsum('blgc,gpc->blgp', ug, b_bar)
    a = jnp.broadcast_to(lam_bar, bu.shape)
    h, h_final = linear_scan(a, bu, h0, reverse)
    y = jnp.einsum('blgp,gcp->blgc', h.real, c_re) - jnp.einsum('blgp,gcp->blgc', h.imag, c_im)
    return y.reshape(n_b, seq_len, GROUP_W), h_final


def s5_mixer(zd_ctx, zd_lat, a_re, a_im, log_dt, b_re, b_im, c_re, c_im, d_skip, glu_w, glu_b):
    a_re, a_im, log_dt, b_re, b_im, c_re, c_im, d_skip, glu_w, glu_b = _f32(
        a_re, a_im, log_dt, b_re, b_im, c_re, c_im, d_skip, glu_w, glu_b)
    u_ctx = zd_ctx
    u_lat = grid_to_colmajor(zd_lat)
    h0 = jnp.zeros((zd_lat.shape[0], S5_GROUPS, S5_STATE), jnp.complex64)
    y_ctx, y_lat = d_skip * u_ctx, d_skip * u_lat
    for d, rev in enumerate((False, True)):
        lam_bar, b_bar = s5_discretise(a_re[d], a_im[d], log_dt[d], b_re[d], b_im[d])
        yc, hc_final = s5_direction(u_ctx, lam_bar, b_bar, c_re[d], c_im[d], h0, rev)
        yl, _ = s5_direction(u_lat, lam_bar, b_bar, c_re[d], c_im[d], hc_final, rev)
        y_ctx, y_lat = y_ctx + yc, y_lat + yl
    y_lat = colmajor_to_grid(y_lat)

    def glu(y):
        val, gate = jnp.split(jax.nn.gelu(y) @ glu_w + glu_b, 2, axis=-1)
        return val * jax.nn.sigmoid(gate)

    return glu(y_ctx), glu(y_lat)


def peer_ffn(h, w_q, sub_keys, u, v):
    n_b, seq_len, dim = h.shape
    tok = h.reshape(-1, dim)
    n_tok = tok.shape[0]
    q = (tok @ w_q).reshape(n_tok, PEER_HEADS, 2, PEER_DKEY).astype(jnp.float32)
    s = jnp.einsum('thpd,hpnd->thpn', q, sub_keys.astype(jnp.float32))
    s1, i1 = lax.top_k(s[:, :, 0], PEER_TOPK)
    s2, i2 = lax.top_k(s[:, :, 1], PEER_TOPK)
    n_cand = PEER_TOPK * PEER_TOPK
    cand_s = (s1[..., :, None] + s2[..., None, :]).reshape(n_tok, PEER_HEADS, n_cand)
    cand_i = (i1[..., :, None] * PEER_NKEYS + i2[..., None, :]).reshape(n_tok, PEER_HEADS, n_cand)
    top_s, top_pos = lax.top_k(cand_s, PEER_TOPK)
    expert = jnp.take_along_axis(cand_i, top_pos, axis=-1)
    gate = jax.nn.softmax(top_s, axis=-1).astype(h.dtype)
    n_blk = n_tok // PEER_BLOCK

    def block(args):
        x_b, e_b, g_b = args
        u_sel = jnp.take(u, e_b, axis=0)
        v_sel = jnp.take(v, e_b, axis=0)
        act = jax.nn.gelu(jnp.einsum('td,thkd->thk', x_b, u_sel))
        return jnp.einsum('thk,thkd->td', g_b * act, v_sel)

    out = lax.map(block, (tok.reshape(n_blk, PEER_BLOCK, dim),
                          expert.reshape(n_blk, PEER_BLOCK, PEER_HEADS, PEER_TOPK),
                          gate.reshape(n_blk, PEER_BLOCK, PEER_HEADS, PEER_TOPK)))
    return out.reshape(n_b, seq_len, dim)


def setup_inputs(seed: int = 0) -> dict:
    key = jax.random.key(seed)
    keys = jax.random.split(key, 64)
    counter = [0]
    f32 = jnp.float32

    def nk():
        k = keys[counter[0]]
        counter[0] += 1
        return k

    def nrm(shape, scale):
        return scale * jax.random.normal(nk(), shape, f32)

    def uni(shape, lo, hi):
        return jax.random.uniform(nk(), shape, f32, lo, hi)

    lru_s = uni((DEPTH, 2, GROUP_W), 0.9, 0.999) ** (1.0 / LRU_C)
    s5_shape = (DEPTH, 2, S5_GROUPS, S5_STATE)
    return {
        'x': nrm((BATCH, SEQ, D_MODEL), 1.0),
        'c': nrm((BATCH, D_MODEL), 1.0),
        'ctx': nrm((BATCH, CTX_LEN, D_MODEL), 1.0),
        'c_ctx': nrm((D_MODEL,), 1.0),
        'norm1_g': 1.0 + nrm((DEPTH, D_MODEL), 0.05),
        'norm2_g': 1.0 + nrm((DEPTH, D_MODEL), 0.05),
        'ada_w': nrm((DEPTH, D_MODEL, N_MOD * D_MODEL), 0.5 * D_MODEL ** -0.5),
        'ada_b': nrm((DEPTH, N_MOD * D_MODEL), 0.1),
        'w_in': nrm((DEPTH, D_MODEL, IN_COLS), D_MODEL ** -0.5),
        'w_out': nrm((DEPTH, D_MODEL, D_MODEL), D_MODEL ** -0.5),
        'rwkv_mu': uni((DEPTH, 2, RWKV_COLS), 0.0, 0.5),
        'rwkv_w0': uni((DEPTH, 2, GROUP_W), -6.5, -1.5),
        'rwkv_w_up': nrm((DEPTH, 2, RWKV_DECAY_LORA, GROUP_W), 0.1 * RWKV_DECAY_LORA ** -0.5),
        'rwkv_a0': nrm((DEPTH, 2, GROUP_W), 0.1),
        'rwkv_a_up': nrm((DEPTH, 2, RWKV_AAA_LORA, GROUP_W), 0.5 * RWKV_AAA_LORA ** -0.5),
        'rwkv_g_up': nrm((DEPTH, RWKV_GATE_LORA, GROUP_W), RWKV_GATE_LORA ** -0.5),
        'rwkv_k_k': 0.85 + nrm((DEPTH, GROUP_W), 0.05),
        'rwkv_k_a': 1.0 + nrm((DEPTH, GROUP_W), 0.05),
        'rwkv_r_k': nrm((DEPTH, RWKV_HEADS, HEAD_DIM), 0.05),
        'rwkv_ln_g': 1.0 + nrm((DEPTH, GROUP_W), 0.05),
        'rwkv_ln_b': nrm((DEPTH, GROUP_W), 0.02),
        'ret_gn_g': 1.0 + nrm((DEPTH, GROUP_W), 0.05),
        'lru_conv_w': nrm((DEPTH, LRU_CONV_W, GROUP_W), LRU_CONV_W ** -0.5),
        'lru_conv_b': nrm((DEPTH, GROUP_W), 0.02),
        'lru_wa': nrm((DEPTH, 2, LRU_BLOCKS, LRU_BLOCK_W, LRU_BLOCK_W), LRU_BLOCK_W ** -0.5),
        'lru_ba': nrm((DEPTH, 2, GROUP_W), 0.02),
        'lru_wx': nrm((DEPTH, 2, LRU_BLOCKS, LRU_BLOCK_W, LRU_BLOCK_W), LRU_BLOCK_W ** -0.5),
        'lru_bx': nrm((DEPTH, 2, GROUP_W), 0.02),
        'lru_lambda': jnp.log(lru_s) - jnp.log1p(-lru_s),
        's5_a_re': -0.5 + nrm(s5_shape, 0.01),
        's5_a_im': jnp.pi * jnp.arange(S5_STATE, dtype=f32) + nrm(s5_shape, 0.01),
        's5_log_dt': uni((DEPTH, 2, S5_GROUPS), math.log(S5_DT_MIN), math.log(S5_DT_MAX)),
        's5_b_re': nrm((DEPTH, 2, S5_GROUPS, S5_STATE, S5_CH), (2.0 * S5_CH) ** -0.5),
        's5_b_im': nrm((DEPTH, 2, S5_GROUPS, S5_STATE, S5_CH), (2.0 * S5_CH) ** -0.5),
        's5_c_re': nrm((DEPTH, 2, S5_GROUPS, S5_CH, S5_STATE), S5_STATE ** -0.5),
        's5_c_im': nrm((DEPTH, 2, S5_GROUPS, S5_CH, S5_STATE), S5_STATE ** -0.5),
        's5_d': nrm((DEPTH, GROUP_W), 0.5),
        's5_glu_w': nrm((DEPTH, GROUP_W, 2 * GROUP_W), GROUP_W ** -0.5),
        's5_glu_b': nrm((DEPTH, 2 * GROUP_W), 0.02),
        'peer_wq': nrm((DEPTH, D_MODEL, PEER_HEADS * 2 * PEER_DKEY), D_MODEL ** -0.5),
        'peer_keys': nrm((DEPTH, PEER_HEADS, 2, PEER_NKEYS, PEER_DKEY), PEER_DKEY ** -0.5),
        'peer_u': nrm((DEPTH, PEER_EXPERTS, D_MODEL), D_MODEL ** -0.5),
        'peer_v': nrm((DEPTH, PEER_EXPERTS, D_MODEL), 0.5),
        'final_norm_g': 1.0 + nrm((D_MODEL,), 0.05),
    }


def reference(x, c, ctx, c_ctx, norm1_g, norm2_g, ada_w, ada_b, w_in, w_out,
              rwkv_mu, rwkv_w0, rwkv_w_up, rwkv_a0, rwkv_a_up, rwkv_g_up, rwkv_k_k, rwkv_k_a,
              rwkv_r_k, rwkv_ln_g, rwkv_ln_b, ret_gn_g,
              lru_conv_w, lru_conv_b, lru_wa, lru_ba, lru_wx, lru_bx, lru_lambda,
              s5_a_re, s5_a_im, s5_log_dt, s5_b_re, s5_b_im, s5_c_re, s5_c_im, s5_d, s5_glu_w, s5_glu_b,
              peer_wq, peer_keys, peer_u, peer_v, final_norm_g):
    split_cols = [RWKV_COLS, RWKV_COLS + RET_COLS, RWKV_COLS + RET_COLS + LRU_COLS]
    for l in range(DEPTH):
        mod_lat = jnp.split((jax.nn.silu(c) @ ada_w[l] + ada_b[l])[:, None, :], N_MOD, axis=-1)
        mod_ctx = jnp.split(jax.nn.silu(c_ctx) @ ada_w[l] + ada_b[l], N_MOD, axis=-1)

        z_lat = modulate(rmsnorm(x, norm1_g[l]), mod_lat[0], mod_lat[1]) @ w_in[l]
        z_ctx = modulate(rmsnorm(ctx, norm1_g[l]), mod_ctx[0], mod_ctx[1]) @ w_in[l]
        zl = jnp.split(z_lat.astype(jnp.float32), split_cols, axis=-1)
        zc = jnp.split(z_ctx.astype(jnp.float32), split_cols, axis=-1)

        ya = rwkv7_mixer(zc[0], zl[0], rwkv_mu[l], rwkv_w0[l], rwkv_w_up[l], rwkv_a0[l], rwkv_a_up[l],
                         rwkv_g_up[l], rwkv_k_k[l], rwkv_k_a[l], rwkv_r_k[l], rwkv_ln_g[l], rwkv_ln_b[l])
        yb = retention_mixer(zc[1], zl[1], ret_gn_g[l])
        yc = rglru_mixer(zc[2], zl[2], lru_conv_w[l], lru_conv_b[l], lru_wa[l], lru_ba[l],
                         lru_wx[l], lru_bx[l], lru_lambda[l])
        yd = s5_mixer(zc[3], zl[3], s5_a_re[l], s5_a_im[l], s5_log_dt[l], s5_b_re[l], s5_b_im[l],
                      s5_c_re[l], s5_c_im[l], s5_d[l], s5_glu_w[l], s5_glu_b[l])

        mix_lat = jnp.concatenate([ya[1], yb[1], yc[1], yd[1]], axis=-1).astype(x.dtype) @ w_out[l]
        x = x + mod_lat[2] * mix_lat
        h2 = modulate(rmsnorm(x, norm2_g[l]), mod_lat[3], mod_lat[4])
        x = x + mod_lat[5] * peer_ffn(h2, peer_wq[l], peer_keys[l], peer_u[l], peer_v[l])

        if l < DEPTH - 1:
            mix_ctx = jnp.concatenate([ya[0], yb[0], yc[0], yd[0]], axis=-1).astype(ctx.dtype) @ w_out[l]
            ctx = ctx + mod_ctx[2] * mix_ctx
            h2c = modulate(rmsnorm(ctx, norm2_g[l]), mod_ctx[3], mod_ctx[4])
            ctx = ctx + mod_ctx[5] * peer_ffn(h2c, peer_wq[l], peer_keys[l], peer_u[l], peer_v[l])
    return rmsnorm(x, final_norm_g)
```

```python
import functools
import math

import jax
import jax.numpy as jnp
import numpy as np
from jax import lax
from jax.experimental import pallas as pl
from jax.experimental.pallas import tpu as pltpu

F32 = jnp.float32
BF16 = jnp.bfloat16

NORM_EPS = 1e-6
N_MOD = 6
HEAD_DIM = 64
N_HEADS = 4
GROUP_W = N_HEADS * HEAD_DIM
GRID_W = 64

RWKV_GATE_LORA = 64
RWKV_LORA = 32
RWKV_GN_EPS = 64e-5
RWKV_COLS = 3 * GROUP_W + RWKV_GATE_LORA + 4 * RWKV_LORA
ZA_W = 3 * GROUP_W + 3 * 128

RET_CHUNK = 128
RET_GN_EPS = 1e-5
RET_FWD_OFFSET = 5.0
RET_BWD_OFFSET = 5.5
ROPE_BASE = 10000.0
RET_COLS = 4 * GROUP_W

LRU_CONV_W = 4
LRU_C = 8.0
LRU_COLS = 2 * GROUP_W
LRU_CHUNK = 128

S5_CH = 16
S5_GROUPS = GROUP_W // S5_CH
S5_STATE = 64
S5_N = S5_GROUPS * S5_STATE
S5_CHUNK = 128

PEER_HEADS = 8
PEER_NKEYS = 128
PEER_DKEY = 128
PEER_TOPK = 16
PEER_TOK_TILE = 512
PEER_EXP_BLOCK = 256

ROW_TILE_T = 32
RWKV_CHUNK = 32

VMEM_LIMIT_BYTES = 56 * 1024 * 1024

NEG_INF = float("-inf")


def _cparams(n_axes):
    return pltpu.CompilerParams(dimension_semantics=("arbitrary",) * n_axes,
                                vmem_limit_bytes=VMEM_LIMIT_BYTES)


def _sigmoid(x):
    return 1.0 / (1.0 + jnp.exp(-x))


def _gelu_tanh(x):
    return 0.5 * x * (1.0 + jnp.tanh(math.sqrt(2.0 / math.pi) * (x + 0.044715 * (x * x * x))))


def _softplus(x):
    return jnp.maximum(x, 0.0) + jnp.log(1.0 + jnp.exp(-jnp.abs(x)))


def _dot(a, b):
    return jnp.dot(a, b, preferred_element_type=F32)


def _dot_exact(a, b):
    return jnp.dot(a, b, preferred_element_type=F32, precision=lax.Precision.HIGHEST)


def _head_sum_matrix():
    idx = np.arange(GROUP_W) // HEAD_DIM
    return jnp.asarray((idx[:, None] == idx[None, :]).astype(np.float32))


def _mod_body(c_ref, w_ref, b_ref, o_ref):
    c = c_ref[...]
    s = (c * _sigmoid(c)).astype(BF16)
    o_ref[...] = _dot(s, w_ref[...].astype(BF16)) + b_ref[...]


def _modulation(cvec, w, b):
    rows, d = cvec.shape
    n = w.shape[1]
    tn = 512
    return pl.pallas_call(
        _mod_body,
        grid=(n // tn,),
        in_specs=[pl.BlockSpec((rows, d), lambda j: (0, 0)),
                  pl.BlockSpec((d, tn), lambda j: (0, j)),
                  pl.BlockSpec((1, tn), lambda j: (0, j))],
        out_specs=pl.BlockSpec((rows, tn), lambda j: (0, j)),
        out_shape=jax.ShapeDtypeStruct((rows, n), F32),
        compiler_params=_cparams(1),
        name="adaln_mod",
    )(cvec, w, b.reshape(1, n))


def _in_body(x_ref, g_ref, mod_ref, w_ref, za_ref, zb_ref, zc_ref, zd_ref):
    tt, nb, d = x_ref.shape
    x = x_ref[...]
    ms = jnp.mean(x * x, axis=-1, keepdims=True)
    y = x * lax.rsqrt(ms + NORM_EPS) * g_ref[...]
    h = y * (1.0 + mod_ref[1]) + mod_ref[0]
    hb = h.reshape(tt * nb, d).astype(BF16)
    c0 = 0
    for ref in (za_ref, zb_ref, zc_ref, zd_ref):
        w = ref.shape[-1]
        ref[...] = _dot(hb, w_ref[:, c0:c0 + w]).reshape(tt, nb, w)
        c0 += w


def _in_proj(xs, g, modv, w_p, n_ctx_tiles):
    lt, nb, d = xs.shape
    tt = ROW_TILE_T
    widths = (ZA_W, RET_COLS, LRU_COLS, GROUP_W)
    seg = lambda i: jnp.where(i >= n_ctx_tiles, 1, 0)
    return pl.pallas_call(
        _in_body,
        grid=(lt // tt,),
        in_specs=[pl.BlockSpec((tt, nb, d), lambda i: (i, 0, 0)),
                  pl.BlockSpec((1, d), lambda i: (0, 0)),
                  pl.BlockSpec((None, N_MOD, nb, d), lambda i: (seg(i), 0, 0, 0)),
                  pl.BlockSpec(w_p.shape, lambda i: (0, 0))],
        out_specs=[pl.BlockSpec((tt, nb, w), lambda i: (i, 0, 0)) for w in widths],
        out_shape=[jax.ShapeDtypeStruct((lt, nb, w), F32) for w in widths],
        compiler_params=_cparams(1),
        name="in_proj",
    )(xs, g.reshape(1, d), modv, w_p)


def _rwkv_feat_body(n_ctx_tiles, n_tiles,
                    z_ref, zp_ref, zn_ref, mu_ref, gup_ref, wup_ref, aup_ref, vec_ref, hs_ref,
                    r_ref, k_ref, v_ref, g_ref, av_ref, w_ref, kd_ref, bv_ref):
    i = pl.program_id(0)
    tt, nb, zw = z_ref.shape
    z0 = z_ref[...]
    has_prev = jnp.logical_and(i != 0, i != n_ctx_tiles)
    has_next = jnp.logical_and(i != n_ctx_tiles - 1, i != n_tiles - 1)
    hp = jnp.where(has_prev, zp_ref[...], 0.0)
    hn = jnp.where(has_next, zn_ref[...], 0.0)
    zp = jnp.concatenate([hp, z0[:-1]], axis=0)
    zn = jnp.concatenate([z0[1:], hn], axis=0)
    z = z0 + mu_ref[0] * (zp - z0) + mu_ref[1] * (zn - z0)
    z = z.reshape(tt * nb, zw)
    gw = GROUP_W
    r = z[:, 0:gw]
    k = z[:, gw:2 * gw]
    v = z[:, 2 * gw:3 * gw]
    g = _dot(_sigmoid(z[:, 3 * gw:3 * gw + 128]).astype(BF16), gup_ref[...])
    w_low = jnp.tanh(z[:, 3 * gw + 128:3 * gw + 256]).astype(BF16)
    a_low = z[:, 3 * gw + 256:3 * gw + 384].astype(BF16)
    k_k = vec_ref[0:1, :]
    k_a = vec_ref[1:2, :]
    kk = k * k_k
    ss = _dot_exact(kk * kk, hs_ref[...])
    kk = kk / jnp.maximum(jnp.sqrt(ss), 1e-12)
    shp = (tt, nb, gw)
    r_ref[...] = r.reshape(shp)
    k_ref[...] = k.reshape(shp)
    v_ref[...] = v.reshape(shp)
    g_ref[...] = g.reshape(shp)
    av_ref[...] = (-kk).reshape(shp)
    for d in range(2):
        w0 = vec_ref[2 + d:3 + d, :]
        a0 = vec_ref[4 + d:5 + d, :]
        w_log = -_softplus(-(w0 + _dot(w_low, wup_ref[d]))) - 0.5
        w_ref[d] = jnp.exp(-jnp.exp(w_log)).reshape(shp)
        a = _sigmoid(a0 + _dot(a_low, aup_ref[d]))
        kd_ref[d] = (k * (1.0 + (a - 1.0) * k_a)).reshape(shp)
        bv_ref[d] = (kk * a).reshape(shp)


def _rwkv_features(za, mu_p, gup_p, wup_p, aup_p, vecs, hs, n_ctx_tiles):
    lt, nb, zw = za.shape
    tt = ROW_TILE_T
    n_tiles = lt // tt
    gw = GROUP_W
    one = lambda shape: jax.ShapeDtypeStruct(shape, F32)
    full = lambda a: pl.BlockSpec(a.shape, lambda i: (0,) * a.ndim)
    o1 = pl.BlockSpec((tt, nb, gw), lambda i: (i, 0, 0))
    o2 = pl.BlockSpec((2, tt, nb, gw), lambda i: (0, i, 0, 0))
    return pl.pallas_call(
        functools.partial(_rwkv_feat_body, n_ctx_tiles, n_tiles),
        grid=(n_tiles,),
        in_specs=[pl.BlockSpec((tt, nb, zw), lambda i: (i, 0, 0)),
                  pl.BlockSpec((1, nb, zw), lambda i: (jnp.maximum(i * tt - 1, 0), 0, 0)),
                  pl.BlockSpec((1, nb, zw), lambda i: (jnp.minimum((i + 1) * tt, lt - 1), 0, 0)),
                  full(mu_p), full(gup_p), full(wup_p), full(aup_p), full(vecs), full(hs)],
        out_specs=[o1, o1, o1, o1, o1, o2, o2, o2],
        out_shape=[one((lt, nb, gw))] * 5 + [one((2, lt, nb, gw))] * 3,
        compiler_params=_cparams(1),
        name="rwkv_features",
    )(za, za, za, mu_p, gup_p, wup_p, aup_p, vecs, hs)


def _rwkv_scan_body(r_ref, a_ref, w_ref, kd_ref, b_ref, v_ref, y_ref, s_ref):
    d = pl.program_id(0)
    j = pl.program_id(1)
    tc = r_ref.shape[0]
    n_k = HEAD_DIM
    n_acc = 4

    @pl.when(j == 0)
    def _():
        s_ref[...] = jnp.zeros_like(s_ref)

    def step(i, carry):
        t = jnp.where(d == 0, i, tc - 1 - i)
        vv = v_ref[t]
        acc = [None] * n_acc
        for kk in range(n_k):
            term = s_ref[kk] * a_ref[t, pl.ds(kk, 1), :]
            acc[kk % n_acc] = term if acc[kk % n_acc] is None else acc[kk % n_acc] + term
        sa = (acc[0] + acc[1]) + (acc[2] + acc[3])
        acc = [None] * n_acc
        for kk in range(n_k):
            s_new = (s_ref[kk] * w_ref[t, pl.ds(kk, 1), :] + sa * b_ref[t, pl.ds(kk, 1), :]
                     + vv * kd_ref[t, pl.ds(kk, 1), :])
            s_ref[kk] = s_new
            term = s_new * r_ref[t, pl.ds(kk, 1), :]
            acc[kk % n_acc] = term if acc[kk % n_acc] is None else acc[kk % n_acc] + term
        y_ref[t] = (acc[0] + acc[1]) + (acc[2] + acc[3])
        return carry

    lax.fori_loop(0, tc, step, 0)


def _dir_chunk(d, j, n_ctx, n_all):
    bwd = jnp.where(j < n_ctx, n_ctx - 1 - j, n_all - 1 - (j - n_ctx))
    return jnp.where(d == 0, j, bwd)


def _rwkv_scan(r_e, a_e, w_e, kd_e, b_e, v_e, l_ctx):
    lt, n_k, lanes = r_e.shape
    v_hi = v_e.shape[1]
    tc = RWKV_CHUNK
    n_all, n_ctx = lt // tc, l_ctx // tc
    cm = lambda d, j: _dir_chunk(d, j, n_ctx, n_all)
    shared = pl.BlockSpec((tc, n_k, lanes), lambda d, j: (cm(d, j), 0, 0))
    per_dir = pl.BlockSpec((None, tc, n_k, lanes), lambda d, j: (d, cm(d, j), 0, 0))
    return pl.pallas_call(
        _rwkv_scan_body,
        grid=(2, n_all),
        in_specs=[shared, shared, per_dir, per_dir, per_dir,
                  pl.BlockSpec((tc, v_hi, lanes), lambda d, j: (cm(d, j), 0, 0))],
        out_specs=pl.BlockSpec((None, tc, v_hi, lanes), lambda d, j: (d, cm(d, j), 0, 0)),
        out_shape=jax.ShapeDtypeStruct((2, lt, v_hi, lanes), F32),
        scratch_shapes=[pltpu.VMEM((n_k, v_hi, lanes), F32)],
        compiler_params=_cparams(2),
        name="rwkv_scan",
    )(r_e, a_e, w_e, kd_e, b_e, v_e)


def _rwkv_expand_k(x, v_lo):
    lead = x.shape[:-2]
    nb = x.shape[-2]
    x = x.reshape(lead + (nb, N_HEADS, HEAD_DIM))
    x = jnp.moveaxis(x, -1, -3).reshape(lead + (HEAD_DIM, nb * N_HEADS))
    return jnp.tile(x, (1,) * (x.ndim - 1) + (v_lo,))


def _rwkv_expand_v(x, v_lo):
    lt, nb, _ = x.shape
    v_hi = HEAD_DIM // v_lo
    x = x.reshape(lt, nb, N_HEADS, v_hi, v_lo)
    return jnp.transpose(x, (0, 3, 4, 1, 2)).reshape(lt, v_hi, v_lo * nb * N_HEADS)


def _rwkv_collapse_v(y, nb, v_lo):
    _, lt, v_hi, _ = y.shape
    y = y.reshape(2, lt, v_hi, v_lo, nb, N_HEADS)
    return jnp.transpose(y, (0, 1, 4, 5, 2, 3)).reshape(2, lt, nb, GROUP_W)


def _ret_body(z_ref, cos_ref, sin_ref, dec_ref, rd_ref, wd_ref, cd_ref, hm_ref, o_ref, s_ref):
    j = pl.program_id(2)
    gw = GROUP_W
    half = gw // 2

    @pl.when(j == 0)
    def _():
        s_ref[...] = jnp.zeros_like(s_ref)

    cos = cos_ref[...]
    sin = sin_ref[...]

    def rope(x):
        x1, x2 = x[:, :half], x[:, half:]
        return jnp.concatenate([x1 * cos - x2 * sin, x1 * sin + x2 * cos], axis=-1)

    q = rope(z_ref[:, 0:gw]) * (HEAD_DIM ** -0.5)
    k = rope(z_ref[:, gw:2 * gw])
    v = z_ref[:, 2 * gw:3 * gw]
    vb = v.astype(BF16)
    k_t = k.T
    k_tb = k_t.astype(BF16)
    inter = _dot(q.astype(BF16), s_ref[...].astype(BF16)) * rd_ref[...]
    intra = jnp.zeros_like(inter)
    for h in range(N_HEADS):
        qm = (q * hm_ref[h:h + 1, :]).astype(BF16)
        att = _dot(qm, k_tb) * dec_ref[h]
        vm = (v * hm_ref[N_HEADS + h:N_HEADS + h + 1, :]).astype(BF16)
        intra = intra + _dot(att.astype(BF16), vm)
    o_ref[...] = intra + inter
    kv = _dot((k_t * wd_ref[...]).astype(BF16), vb)
    s_ref[...] = s_ref[...] * cd_ref[0] + kv * cd_ref[1]


def _retention(zb_bm, tabs, l_ctx):
    nb, lt, _ = zb_bm.shape
    c = RET_CHUNK
    n_all, n_ctx = lt // c, l_ctx // c
    gw = GROUP_W
    cos, sin, dec, rd, wd, cd, hm = tabs
    cm = lambda d, j: _dir_chunk(d, j, n_ctx, n_all)
    return pl.pallas_call(
        _ret_body,
        grid=(nb, 2, n_all),
        in_specs=[pl.BlockSpec((None, c, RET_COLS), lambda b, d, j: (b, cm(d, j), 0)),
                  pl.BlockSpec((c, gw // 2), lambda b, d, j: (cm(d, j), 0)),
                  pl.BlockSpec((c, gw // 2), lambda b, d, j: (cm(d, j), 0)),
                  pl.BlockSpec((None, N_HEADS, c, c), lambda b, d, j: (d, 0, 0, 0)),
                  pl.BlockSpec((None, c, gw), lambda b, d, j: (d, 0, 0)),
                  pl.BlockSpec((None, gw, c), lambda b, d, j: (d, 0, 0)),
                  pl.BlockSpec((None, 2, gw, gw), lambda b, d, j: (d, 0, 0, 0)),
                  pl.BlockSpec((2 * N_HEADS, gw), lambda b, d, j: (0, 0))],
        out_specs=pl.BlockSpec((None, None, c, gw), lambda b, d, j: (d, b, cm(d, j), 0)),
        out_shape=jax.ShapeDtypeStruct((2, nb, lt, gw), F32),
        scratch_shapes=[pltpu.VMEM((gw, gw), F32)],
        compiler_params=_cparams(3),
        name="retention",
    )(zb_bm, cos, sin, dec, rd, wd, cd, hm)


def _ret_perm():
    new = np.arange(GROUP_W)
    half_id, rem = new // 128, new % 128
    h, i = rem // 32, rem % 32
    return h * HEAD_DIM + half_id * 32 + i


def _ret_tables(lt):
    c = RET_CHUNK
    half = HEAD_DIM // 2
    freqs = ROPE_BASE ** (-jnp.arange(half, dtype=F32) / half)
    ang = jnp.arange(lt, dtype=F32)[:, None] * freqs[None]
    cos = jnp.tile(jnp.cos(ang), (1, N_HEADS))
    sin = jnp.tile(jnp.sin(ang), (1, N_HEADS))
    idx = jnp.arange(c, dtype=F32)
    perm = _ret_perm()
    head_of_qk = jnp.asarray(perm // HEAD_DIM)
    head_of_v = jnp.arange(GROUP_W) // HEAD_DIM
    dec, rd, wd, cd = [], [], [], []
    for offset, rev in ((RET_FWD_OFFSET, False), (RET_BWD_OFFSET, True)):
        lg = jnp.log1p(-jnp.exp2(-(offset + jnp.arange(N_HEADS, dtype=F32))))
        diff = idx[:, None] - idx[None, :]
        if rev:
            mask, dist = diff < 0, -diff
            read_pow, write_pow = c - idx, idx
        else:
            mask, dist = diff >= 0, diff
            read_pow, write_pow = idx + 1.0, c - 1.0 - idx
        dec.append(jnp.where(mask[None], jnp.exp(lg[:, None, None] * jnp.where(mask, dist, 0.0)[None]), 0.0))
        rd.append(jnp.exp(lg[head_of_v][None, :] * read_pow[:, None]))
        wd.append(jnp.exp(lg[head_of_qk][:, None] * write_pow[None, :]))
        same = (head_of_qk[:, None] == head_of_v[None, :]).astype(F32)
        cd.append(jnp.stack([same * jnp.exp(lg * c)[head_of_v][None, :], same]))
    hm_q = (head_of_qk[None, :] == jnp.arange(N_HEADS)[:, None]).astype(F32)
    hm_v = (head_of_v[None, :] == jnp.arange(N_HEADS)[:, None]).astype(F32)
    return cos, sin, jnp.stack(dec), jnp.stack(rd), jnp.stack(wd), jnp.stack(cd), jnp.concatenate([hm_q, hm_v])


def _lru_body(n_ctx, n_all, x_ref, xp_ref, xn_ref, cw_ref, wa_ref, wx_ref, vec_ref, o_ref,
              a_s, b_s, h_s):
    d = pl.program_id(0)
    j = pl.program_id(1)
    tc, nb, gw = x_ref.shape
    ch = _dir_chunk(d, j, n_ctx, n_all)

    @pl.when(j == 0)
    def _():
        h_s[...] = jnp.zeros_like(h_s)

    has_prev = jnp.logical_and(ch != 0, ch != n_ctx)
    has_next = jnp.logical_and(ch != n_ctx - 1, ch != n_all - 1)
    xe = jnp.concatenate([jnp.where(has_prev, xp_ref[...], 0.0), x_ref[...],
                          jnp.where(has_next, xn_ref[...], 0.0)], axis=0)
    xc = cw_ref[LRU_CONV_W:LRU_CONV_W + 1, :]
    for tap in range(LRU_CONV_W):
        xc = xc + cw_ref[tap:tap + 1, :] * xe[tap:tap + tc]
    xc = xc.reshape(tc * nb, gw)
    xb = xc.astype(BF16)
    r = _sigmoid(_dot(xb, wa_ref[...]) + vec_ref[0:1, :])
    gi = _sigmoid(_dot(xb, wx_ref[...]) + vec_ref[1:2, :])
    log_a = -LRU_C * r * _softplus(-vec_ref[2:3, :])
    th = jnp.tanh(log_a)
    one_minus_a2 = 2.0 * th / (th - 1.0)
    a_s[...] = jnp.exp(log_a).reshape(tc, nb, gw)
    b_s[...] = (jnp.sqrt(one_minus_a2) * (gi * xc)).reshape(tc, nb, gw)

    def step(i, h):
        t = jnp.where(d == 0, i, tc - 1 - i)
        h = a_s[t] * h + b_s[t]
        o_ref[t] = h
        return h

    h_s[...] = lax.fori_loop(0, tc, step, h_s[...])


def _lru(zc, cw, wa_bd, wx_bd, vecs, l_ctx):
    lt, nb, _ = zc.shape
    gw = GROUP_W
    tc = LRU_CHUNK
    n_all, n_ctx = lt // tc, l_ctx // tc
    cm = lambda d, j: _dir_chunk(d, j, n_ctx, n_all)
    return pl.pallas_call(
        functools.partial(_lru_body, n_ctx, n_all),
        grid=(2, n_all),
        in_specs=[pl.BlockSpec((tc, nb, gw), lambda d, j: (cm(d, j), 0, 0)),
                  pl.BlockSpec((2, nb, gw), lambda d, j: (jnp.maximum(cm(d, j) * (tc // 2) - 1, 0), 0, 0)),
                  pl.BlockSpec((1, nb, gw), lambda d, j: (jnp.minimum((cm(d, j) + 1) * tc, lt - 1), 0, 0)),
                  pl.BlockSpec(cw.shape, lambda d, j: (0, 0)),
                  pl.BlockSpec((None, gw, gw), lambda d, j: (d, 0, 0)),
                  pl.BlockSpec((None, gw, gw), lambda d, j: (d, 0, 0)),
                  pl.BlockSpec((None, 3, gw), lambda d, j: (d, 0, 0))],
        out_specs=pl.BlockSpec((None, tc, nb, gw), lambda d, j: (d, cm(d, j), 0, 0)),
        out_shape=jax.ShapeDtypeStruct((2, lt, nb, gw), F32),
        scratch_shapes=[pltpu.VMEM((tc, nb, gw), F32), pltpu.VMEM((tc, nb, gw), F32),
                        pltpu.VMEM((nb, gw), F32)],
        compiler_params=_cparams(2),
        name="rglru",
    )(zc, zc, zc, cw, wa_bd, wx_bd, vecs)


def _block_diag(w):
    nblk, n = w.shape[-3], w.shape[-1]
    eye = jnp.eye(nblk, dtype=w.dtype)
    out = w[..., :, :, None, :] * eye[:, None, :, None]
    return out.reshape(w.shape[:-3] + (nblk * n, nblk * n))


def _s5_body(u_ref, bm_ref, lam_ref, cm_ref, y_ref, h_s, st_s):
    d = pl.program_id(0)
    j = pl.program_id(1)
    tc, nb, gw = u_ref.shape
    n = S5_N

    @pl.when(j == 0)
    def _():
        st_s[...] = jnp.zeros_like(st_s)

    u = u_ref[...].reshape(tc * nb, gw).astype(BF16)
    h_s[...] = _dot(u, bm_ref[...]).reshape(tc, nb, 2 * n)
    lr = lam_ref[0:1, :]
    li = lam_ref[1:2, :]

    def step(i, carry):
        hr, hi = carry
        t = jnp.where(d == 0, i, tc - 1 - i)
        bu = h_s[t]
        nr = lr * hr - li * hi + bu[:, :n]
        ni = lr * hi + li * hr + bu[:, n:]
        h_s[t] = jnp.concatenate([nr, ni], axis=-1)
        return nr, ni

    hr, hi = lax.fori_loop(0, tc, step, (st_s[0], st_s[1]))
    st_s[0] = hr
    st_s[1] = hi
    hh = h_s[...].reshape(tc * nb, 2 * n).astype(BF16)
    y_ref[...] = _dot(hh, cm_ref[...]).reshape(tc, nb, gw)


def _s5(u, bmat, lam, cmat, l_ctx):
    lt, nb, gw = u.shape
    tc = S5_CHUNK
    n_all, n_ctx = lt // tc, l_ctx // tc
    cm = lambda d, j: _dir_chunk(d, j, n_ctx, n_all)
    return pl.pallas_call(
        _s5_body,
        grid=(2, n_all),
        in_specs=[pl.BlockSpec((tc, nb, gw), lambda d, j: (cm(d, j), 0, 0)),
                  pl.BlockSpec((None, gw, 2 * S5_N), lambda d, j: (d, 0, 0)),
                  pl.BlockSpec((None, 2, S5_N), lambda d, j: (d, 0, 0)),
                  pl.BlockSpec((None, 2 * S5_N, gw), lambda d, j: (d, 0, 0))],
        out_specs=pl.BlockSpec((None, tc, nb, gw), lambda d, j: (d, cm(d, j), 0, 0)),
        out_shape=jax.ShapeDtypeStruct((2, lt, nb, gw), F32),
        scratch_shapes=[pltpu.VMEM((tc, nb, 2 * S5_N), F32), pltpu.VMEM((2, nb, S5_N), F32)],
        compiler_params=_cparams(2),
        name="s5",
    )(u, bmat, lam, cmat)


def _s5_params(a_re, a_im, log_dt, b_re, b_im, c_re, c_im):
    dt = jnp.exp(log_dt)[..., None]
    er = jnp.exp(a_re * dt)
    lbr, lbi = er * jnp.cos(a_im * dt), er * jnp.sin(a_im * dt)
    den = a_re * a_re + a_im * a_im
    nr, ni = lbr - 1.0, lbi
    fr = (nr * a_re + ni * a_im) / den
    fi = (ni * a_re - nr * a_im) / den
    bbr = fr[..., None] * b_re - fi[..., None] * b_im
    bbi = fr[..., None] * b_im + fi[..., None] * b_re
    eye = jnp.eye(S5_GROUPS, dtype=F32)

    def in_map(bb):
        m = jnp.einsum('dgpc,gh->dgchp', bb, eye)
        return m.reshape(2, GROUP_W, S5_N)

    def out_map(cc):
        m = jnp.einsum('dgcp,gh->dgphc', cc, eye)
        return m.reshape(2, S5_N, GROUP_W)

    bmat = jnp.concatenate([in_map(bbr), in_map(bbi)], axis=-1).astype(BF16)
    cmat = jnp.concatenate([out_map(c_re), -out_map(c_im)], axis=-2).astype(BF16)
    lam = jnp.stack([lbr.reshape(2, S5_N), lbi.reshape(2, S5_N)], axis=1)
    return bmat, lam, cmat


def _to_colmajor(z, l_ctx):
    lt, nb, ch = z.shape
    rows = (lt - l_ctx) // GRID_W
    lat = z[l_ctx:].reshape(rows, GRID_W, nb, ch).transpose(1, 0, 2, 3).reshape(lt - l_ctx, nb, ch)
    return jnp.concatenate([z[:l_ctx], lat], axis=0)


def _from_colmajor(y, l_ctx):
    lt = y.shape[-3]
    nb, ch = y.shape[-2:]
    rows = (lt - l_ctx) // GRID_W
    lead = y.shape[:-3]
    lat = y[..., l_ctx:, :, :].reshape(lead + (GRID_W, rows, nb, ch))
    lat = jnp.swapaxes(lat, -4, -3).reshape(lead + (lt - l_ctx, nb, ch))
    return jnp.concatenate([y[..., :l_ctx, :, :], lat], axis=-3)


def _out_body(x_ref, mod_ref, ya_ref, r_ref, k_ref, v_ref, g_ref, ob_ref, gb_ref, hc_ref, gc_ref,
              yd_ref, ud_ref, vec_ref, hs_ref, gluw_ref, glub_ref, wo_ref, n2_ref, x1_ref, h2_ref):
    tt, nb, d = x_ref.shape
    gw = GROUP_W
    rows = tt * nb
    hs = hs_ref[...]
    two = lambda ref: (ref[0] + ref[1]).reshape(rows, gw)
    flat = lambda ref: ref[...].reshape(rows, gw)

    def head_norm(y, eps):
        mu = _dot_exact(y, hs) * (1.0 / HEAD_DIM)
        dlt = y - mu
        var = _dot_exact(dlt * dlt, hs) * (1.0 / HEAD_DIM)
        return dlt * lax.rsqrt(var + eps)

    r_k, ln_g, ln_b = vec_ref[0:1, :], vec_ref[1:2, :], vec_ref[2:3, :]
    r, k, v = flat(r_ref), flat(k_ref), flat(v_ref)
    yn = head_norm(two(ya_ref), RWKV_GN_EPS) * ln_g + ln_b
    bonus = _dot_exact(r * k * r_k, hs) * v
    mix_a = (yn + bonus) * flat(g_ref)
    gb = flat(gb_ref)
    mix_b = gb * _sigmoid(gb) * (head_norm(two(ob_ref), RET_GN_EPS) * vec_ref[3:4, :])
    mix_c = _gelu_tanh(flat(gc_ref)) * two(hc_ref)
    yd = vec_ref[4:5, :] * flat(ud_ref) + two(yd_ref)
    glu = _dot(_gelu_tanh(yd).astype(BF16), gluw_ref[...]) + glub_ref[...]
    mix_d = glu[:, :gw] * _sigmoid(glu[:, gw:])

    mix = _dot(mix_a.astype(BF16), wo_ref[0:gw, :])
    mix = mix + _dot(mix_b.astype(BF16), wo_ref[gw:2 * gw, :])
    mix = mix + _dot(mix_c.astype(BF16), wo_ref[2 * gw:3 * gw, :])
    mix = mix + _dot(mix_d.astype(BF16), wo_ref[3 * gw:4 * gw, :])
    x1 = x_ref[...] + mod_ref[2] * mix.reshape(tt, nb, d)
    x1_ref[...] = x1
    ms = jnp.mean(x1 * x1, axis=-1, keepdims=True)
    h2 = (x1 * lax.rsqrt(ms + NORM_EPS) * n2_ref[...]) * (1.0 + mod_ref[4]) + mod_ref[3]
    h2_ref[...] = h2.reshape(rows, d).astype(BF16)


def _out_proj(xs, modv, ya, feats, ob, zb, hc, zc, yd, zd, vecs, hs, gluw, glub, wo, n2, n_ctx_tiles, t0_tile):
    lt, nb, d = xs.shape
    tt = ROW_TILE_T
    gw = GROUP_W
    n_tiles = lt // tt - t0_tile
    seg = lambda i: jnp.where(i + t0_tile >= n_ctx_tiles, 1, 0)
    col = lambda c: pl.BlockSpec((tt, nb, gw), lambda i: (i + t0_tile, 0, c))
    two = pl.BlockSpec((2, tt, nb, gw), lambda i: (0, i + t0_tile, 0, 0))
    full = lambda a: pl.BlockSpec(a.shape, lambda i: (0,) * a.ndim)
    r, k, v, g = feats
    return pl.pallas_call(
        _out_body,
        grid=(n_tiles,),
        in_specs=[pl.BlockSpec((tt, nb, d), lambda i: (i + t0_tile, 0, 0)),
                  pl.BlockSpec((None, N_MOD, nb, d), lambda i: (seg(i), 0, 0, 0)),
                  two, col(0), col(0), col(0), col(0),
                  two, col(3), two, col(1), two, col(0),
                  full(vecs), full(hs), full(gluw), full(glub), full(wo), full(n2)],
        out_specs=[pl.BlockSpec((tt, nb, d), lambda i: (i + t0_tile, 0, 0)),
                   pl.BlockSpec((tt * nb, d), lambda i: (i + t0_tile, 0))],
        out_shape=[jax.ShapeDtypeStruct((lt, nb, d), F32), jax.ShapeDtypeStruct((lt * nb, d), BF16)],
        compiler_params=_cparams(1),
        name="out_proj",
    )(xs, modv, ya, r, k, v, g, ob, zb, hc, zc, yd, zd, vecs, hs, gluw, glub, wo, n2)


def _top_values(s, n):
    vals, cur = [], s
    for _ in range(n):
        m = jnp.max(cur, axis=0, keepdims=True)
        vals.append(m)
        cur = jnp.where(cur == m, NEG_INF, cur)
    return vals


def _peer_route_body(h_ref, wq_ref, keys_ref, e1_ref, thr_ref, s2_ref, w2_ref):
    hb = h_ref[...]
    for h in range(PEER_HEADS):
        st = []
        for p in range(2):
            c0 = (2 * h + p) * PEER_DKEY
            q = _dot(hb, wq_ref[:, c0:c0 + PEER_DKEY]).astype(BF16)
            st.append(lax.dot_general(keys_ref[2 * h + p], q, (((1,), (1,)), ((), ())),
                                      preferred_element_type=F32))
        s1, s2 = st
        v1 = _top_values(s1, PEER_TOPK + 1)
        v2 = _top_values(s2, PEER_TOPK + 1)
        v2m = jnp.concatenate(v2[:PEER_TOPK], axis=0)
        pad = jnp.full((6, s1.shape[1]), NEG_INF, F32)
        cand = jnp.concatenate([v1[a] + v2m for a in range(PEER_TOPK)]
                               + [v1[PEER_TOPK] + v2[0], v1[0] + v2[PEER_TOPK], pad], axis=0)
        top = _top_values(cand, PEER_TOPK + 1)
        tau = 0.5 * (top[PEER_TOPK - 1] + top[PEER_TOPK])
        cmax = top[0]
        z = jnp.sum(jnp.where(cand >= tau, jnp.exp(cand - cmax), 0.0), axis=0, keepdims=True)
        e1_ref[:, h, :] = jnp.exp(s1 - v1[0])
        thr_ref[:, h, :] = tau - s1
        s2_ref[h] = s2
        w2_ref[h] = jnp.exp(s2 - v2[0]) / z


def _peer_route(h2, wq, keys, t0_tile):
    t_all, d = h2.shape
    tt = PEER_TOK_TILE
    nk = PEER_NKEYS
    by_i = pl.BlockSpec((nk, PEER_HEADS, tt), lambda i: (0, 0, i + t0_tile))
    by_h = pl.BlockSpec((PEER_HEADS, nk, tt), lambda i: (0, 0, i + t0_tile))
    return pl.pallas_call(
        _peer_route_body,
        grid=(t_all // tt - t0_tile,),
        in_specs=[pl.BlockSpec((tt, d), lambda i: (i + t0_tile, 0)),
                  pl.BlockSpec(wq.shape, lambda i: (0, 0)),
                  pl.BlockSpec(keys.shape, lambda i: (0, 0, 0))],
        out_specs=[by_i, by_i, by_h, by_h],
        out_shape=[jax.ShapeDtypeStruct((nk, PEER_HEADS, t_all), F32)] * 2
                  + [jax.ShapeDtypeStruct((PEER_HEADS, nk, t_all), F32)] * 2,
        compiler_params=_cparams(1),
        name="peer_route",
    )(h2, wq, keys)


def _peer_ffn_body(h_ref, u_ref, vt_ref, e1_ref, thr_ref, s2_ref, w2_ref, x_ref, mod_ref, o_ref, acc_ref):
    j = pl.program_id(1)
    nk = PEER_NKEYS

    @pl.when(j == 0)
    def _():
        acc_ref[...] = jnp.zeros_like(acc_ref)

    act = lax.dot_general(u_ref[...], h_ref[...], (((1,), (1,)), ((), ())), preferred_element_type=F32)
    act = _gelu_tanh(act)
    parts = []
    for ii in range(PEER_EXP_BLOCK // nk):
        gate = None
        for h in range(PEER_HEADS):
            sel = s2_ref[h] >= thr_ref[ii, h:h + 1, :]
            term = jnp.where(sel, w2_ref[h] * e1_ref[ii, h:h + 1, :], 0.0)
            gate = term if gate is None else gate + term
        parts.append((act[ii * nk:(ii + 1) * nk] * gate).astype(BF16))
    p = jnp.concatenate(parts, axis=0)
    acc_ref[...] += _dot(vt_ref[...], p)

    @pl.when(j == pl.num_programs(1) - 1)
    def _():
        tt, d = x_ref.shape
        nb = mod_ref.shape[1]
        y = acc_ref[...].T.reshape(tt // nb, nb, d) * mod_ref[5]
        o_ref[...] = x_ref[...] + y.reshape(tt, d)


def _peer_ffn(h2, u_b, vt_b, e1, thr, s2, w2, x1, modv, n_ctx_tok_tiles, t0_tile):
    t_all, d = h2.shape
    tt = PEER_TOK_TILE
    eb = PEER_EXP_BLOCK
    nk = PEER_NKEYS
    nb = modv.shape[2]
    n_e = u_b.shape[0] // eb
    n_tiles = t_all // tt - t0_tile
    seg = lambda i: jnp.where(i + t0_tile >= n_ctx_tok_tiles, 1, 0)
    by_i = pl.BlockSpec((eb // nk, PEER_HEADS, tt), lambda i, j: (j, 0, i + t0_tile))
    by_h = pl.BlockSpec((PEER_HEADS, nk, tt), lambda i, j: (0, 0, i + t0_tile))
    return pl.pallas_call(
        _peer_ffn_body,
        grid=(n_tiles, n_e),
        in_specs=[pl.BlockSpec((tt, d), lambda i, j: (i + t0_tile, 0)),
                  pl.BlockSpec((eb, d), lambda i, j: (j, 0)),
                  pl.BlockSpec((d, eb), lambda i, j: (0, j)),
                  by_i, by_i, by_h, by_h,
                  pl.BlockSpec((tt, d), lambda i, j: (i + t0_tile, 0)),
                  pl.BlockSpec((None, N_MOD, nb, d), lambda i, j: (seg(i), 0, 0, 0))],
        out_specs=pl.BlockSpec((tt, d), lambda i, j: (i + t0_tile, 0)),
        out_shape=jax.ShapeDtypeStruct((t_all, d), F32),
        scratch_shapes=[pltpu.VMEM((d, tt), F32)],
        compiler_params=_cparams(2),
        name="peer_ffn",
    )(h2, u_b, vt_b, e1, thr, s2, w2, x1, modv)


def _final_body(x_ref, g_ref, o_ref):
    x = x_ref[...]
    ms = jnp.mean(x * x, axis=-1, keepdims=True)
    o_ref[...] = x * lax.rsqrt(ms + NORM_EPS) * g_ref[...]


def _final_norm(x2d, g):
    t, d = x2d.shape
    tr = 512
    return pl.pallas_call(
        _final_body,
        grid=(t // tr,),
        in_specs=[pl.BlockSpec((tr, d), lambda i: (i, 0)), pl.BlockSpec((1, d), lambda i: (0, 0))],
        out_specs=pl.BlockSpec((tr, d), lambda i: (i, 0)),
        out_shape=jax.ShapeDtypeStruct((t, d), F32),
        compiler_params=_cparams(1),
        name="final_norm",
    )(x2d, g.reshape(1, d))


def _pad_cols(w, width):
    return jnp.pad(w, [(0, 0)] * (w.ndim - 1) + [(0, width - w.shape[-1])])


def _rwkv_slabs(w):
    gw3 = 3 * GROUP_W
    parts = [w[..., :gw3],
             _pad_cols(w[..., gw3:gw3 + 64], 128),
             _pad_cols(w[..., gw3 + 64:gw3 + 128], 128),
             _pad_cols(w[..., gw3 + 128:gw3 + 192], 128)]
    return jnp.concatenate(parts, axis=-1)


def _layer_weights(l, p):
    d = p['w_in'].shape[1]
    w_in = p['w_in'][l]
    wa_, wb_, wc_, wd_ = jnp.split(w_in, [RWKV_COLS, RWKV_COLS + RET_COLS, RWKV_COLS + RET_COLS + LRU_COLS], axis=-1)
    perm = _ret_perm()
    gw = GROUP_W
    wb_ = jnp.concatenate([wb_[:, 0:gw][:, perm], wb_[:, gw:2 * gw][:, perm], wb_[:, 2 * gw:]], axis=-1)
    out = {'w_in': jnp.concatenate([_rwkv_slabs(wa_), wb_, wc_, wd_], axis=-1).astype(BF16)}
    out['mu'] = _rwkv_slabs(p['rwkv_mu'][l])[:, None, :]
    out['gup'] = jnp.pad(p['rwkv_g_up'][l], ((0, 64), (0, 0))).astype(BF16)
    lora = RWKV_LORA
    out['wup'] = jnp.stack([jnp.pad(p['rwkv_w_up'][l, dd], ((dd * lora, 128 - (dd + 1) * lora), (0, 0)))
                            for dd in range(2)]).astype(BF16)
    out['aup'] = jnp.stack([jnp.pad(p['rwkv_a_up'][l, dd], ((dd * lora, 128 - (dd + 1) * lora), (0, 0)))
                            for dd in range(2)]).astype(BF16)
    out['rwkv_vecs'] = jnp.concatenate([p['rwkv_k_k'][l][None], p['rwkv_k_a'][l][None],
                                        p['rwkv_w0'][l], p['rwkv_a0'][l]], axis=0)
    out['cw'] = jnp.concatenate([p['lru_conv_w'][l], p['lru_conv_b'][l][None]], axis=0)
    out['wa'] = _block_diag(p['lru_wa'][l]).astype(BF16)
    out['wx'] = _block_diag(p['lru_wx'][l]).astype(BF16)
    out['lru_vecs'] = jnp.stack([p['lru_ba'][l], p['lru_bx'][l], p['lru_lambda'][l]], axis=1)
    out['s5'] = _s5_params(p['s5_a_re'][l], p['s5_a_im'][l], p['s5_log_dt'][l], p['s5_b_re'][l],
                           p['s5_b_im'][l], p['s5_c_re'][l], p['s5_c_im'][l])
    out['out_vecs'] = jnp.stack([p['rwkv_r_k'][l].reshape(-1), p['rwkv_ln_g'][l], p['rwkv_ln_b'][l],
                                 p['ret_gn_g'][l], p['s5_d'][l]], axis=0)
    out['gluw'] = p['s5_glu_w'][l].astype(BF16)
    out['glub'] = p['s5_glu_b'][l][None]
    out['wo'] = p['w_out'][l].astype(BF16)
    out['n2'] = p['norm2_g'][l][None]
    out['wq'] = p['peer_wq'][l].astype(BF16)
    out['keys'] = p['peer_keys'][l].reshape(PEER_HEADS * 2, PEER_NKEYS, PEER_DKEY).astype(BF16)
    out['u'] = p['peer_u'][l].astype(BF16)
    out['vt'] = p['peer_v'][l].T.astype(BF16)
    return out


def kernel(x, c, ctx, c_ctx, norm1_g, norm2_g, ada_w, ada_b, w_in, w_out, rwkv_mu, rwkv_w0, rwkv_w_up, rwkv_a0, rwkv_a_up, rwkv_g_up, rwkv_k_k, rwkv_k_a, rwkv_r_k, rwkv_ln_g, rwkv_ln_b, ret_gn_g, lru_conv_w, lru_conv_b, lru_wa, lru_ba, lru_wx, lru_bx, lru_lambda, s5_a_re, s5_a_im, s5_log_dt, s5_b_re, s5_b_im, s5_c_re, s5_c_im, s5_d, s5_glu_w, s5_glu_b, peer_wq, peer_keys, peer_u, peer_v, final_norm_g):
    p = dict(w_in=w_in, w_out=w_out, norm2_g=norm2_g, rwkv_mu=rwkv_mu, rwkv_w0=rwkv_w0, rwkv_w_up=rwkv_w_up,
             rwkv_a0=rwkv_a0, rwkv_a_up=rwkv_a_up, rwkv_g_up=rwkv_g_up, rwkv_k_k=rwkv_k_k, rwkv_k_a=rwkv_k_a,
             rwkv_r_k=rwkv_r_k, rwkv_ln_g=rwkv_ln_g, rwkv_ln_b=rwkv_ln_b, ret_gn_g=ret_gn_g,
             lru_conv_w=lru_conv_w, lru_conv_b=lru_conv_b, lru_wa=lru_wa, lru_ba=lru_ba, lru_wx=lru_wx,
             lru_bx=lru_bx, lru_lambda=lru_lambda, s5_a_re=s5_a_re, s5_a_im=s5_a_im, s5_log_dt=s5_log_dt,
             s5_b_re=s5_b_re, s5_b_im=s5_b_im, s5_c_re=s5_c_re, s5_c_im=s5_c_im, s5_d=s5_d,
             s5_glu_w=s5_glu_w, s5_glu_b=s5_glu_b, peer_wq=peer_wq, peer_keys=peer_keys, peer_u=peer_u,
             peer_v=peer_v)
    nb, l_lat, d = x.shape
    l_ctx = ctx.shape[1]
    lt = l_ctx + l_lat
    depth = w_in.shape[0]
    assert 128 % (nb * N_HEADS) == 0 and l_ctx % 128 == 0 and l_lat % 128 == 0
    assert (l_ctx * nb) % PEER_TOK_TILE == 0 and (l_lat * nb) % PEER_TOK_TILE == 0
    v_lo = 128 // (nb * N_HEADS)
    n_ctx_tiles = l_ctx // ROW_TILE_T
    n_ctx_tok_tiles = l_ctx * nb // PEER_TOK_TILE
    hs = _head_sum_matrix()
    ret_tabs = _ret_tables(lt)

    xs = jnp.transpose(jnp.concatenate([ctx, x], axis=1), (1, 0, 2))
    mod_rows = 8 * ((nb + 1 + 7) // 8)
    cvec = jnp.zeros((mod_rows, d), F32).at[:nb].set(c).at[nb].set(c_ctx)

    for l in range(depth):
        wl = _layer_weights(l, p)
        last = l == depth - 1
        mod = _modulation(cvec, ada_w[l], ada_b[l])
        mod_lat = mod[:nb].reshape(nb, N_MOD, d).transpose(1, 0, 2)
        mod_ctx = jnp.broadcast_to(mod[nb].reshape(N_MOD, 1, d), (N_MOD, nb, d))
        modv = jnp.stack([mod_ctx, mod_lat])

        za, zb, zc, zd = _in_proj(xs, norm1_g[l], modv, wl['w_in'], n_ctx_tiles)

        r, k, v, g, av, w_dec, kd, bv = _rwkv_features(za, wl['mu'], wl['gup'], wl['wup'], wl['aup'],
                                                       wl['rwkv_vecs'], hs, n_ctx_tiles)
        ex = lambda t: _rwkv_expand_k(t, v_lo)
        ya = _rwkv_scan(ex(r), ex(av), ex(w_dec), ex(kd), ex(bv), _rwkv_expand_v(v, v_lo), l_ctx)
        ya = _rwkv_collapse_v(ya, nb, v_lo)
        ob = _retention(jnp.transpose(zb, (1, 0, 2)), ret_tabs, l_ctx)
        ob = jnp.transpose(ob, (0, 2, 1, 3))
        hc = _lru(zc, wl['cw'], wl['wa'], wl['wx'], wl['lru_vecs'], l_ctx)
        yd = _from_colmajor(_s5(_to_colmajor(zd, l_ctx), *wl['s5'], l_ctx), l_ctx)

        t0 = n_ctx_tiles if last else 0
        x1, h2 = _out_proj(xs, modv, ya, (r, k, v, g), ob, zb, hc, zc, yd, zd, wl['out_vecs'], hs,
                           wl['gluw'], wl['glub'], wl['wo'], wl['n2'], n_ctx_tiles, t0)
        t0p = n_ctx_tok_tiles if last else 0
        e1, thr, s2, w2 = _peer_route(h2, wl['wq'], wl['keys'], t0p)
        x2 = _peer_ffn(h2, wl['u'], wl['vt'], e1, thr, s2, w2, x1.reshape(lt * nb, d), modv,
                       n_ctx_tok_tiles, t0p)
        xs = x2.reshape(lt, nb, d)

    out = _final_norm(xs[l_ctx:].reshape(l_lat * nb, d), final_norm_g)
    return jnp.transpose(out.reshape(l_lat, nb, d), (1, 0, 2))
```

```python
import functools
import math

import jax
import jax.numpy as jnp
import numpy as np
from jax import lax
from jax.experimental import pallas as pl
from jax.experimental.pallas import tpu as pltpu

F32 = jnp.float32
BF16 = jnp.bfloat16

NORM_EPS = 1e-6
N_MOD = 6
HEAD_DIM = 64
N_HEADS = 4
GROUP_W = N_HEADS * HEAD_DIM
GRID_W = 64

RWKV_GATE_LORA = 64
RWKV_LORA = 32
RWKV_GN_EPS = 64e-5
RWKV_COLS = 3 * GROUP_W + RWKV_GATE_LORA + 4 * RWKV_LORA
ZA_W = 3 * GROUP_W + 3 * 128

RET_CHUNK = 128
RET_GN_EPS = 1e-5
RET_FWD_OFFSET = 5.0
RET_BWD_OFFSET = 5.5
ROPE_BASE = 10000.0
RET_COLS = 4 * GROUP_W

LRU_CONV_W = 4
LRU_C = 8.0
LRU_COLS = 2 * GROUP_W
LRU_CHUNK = 128

S5_CH = 16
S5_GROUPS = GROUP_W // S5_CH
S5_STATE = 64
S5_N = S5_GROUPS * S5_STATE
S5_CHUNK = 128

PEER_HEADS = 8
PEER_NKEYS = 128
PEER_DKEY = 128
PEER_TOPK = 16
PEER_TOK_TILE = 512
PEER_EXP_BLOCK = 1024

ROW_TILE_T = 32
RWKV_CHUNK = 32

VMEM_LIMIT_BYTES = 56 * 1024 * 1024

NEG_INF = float("-inf")


def _cparams(n_axes):
    return pltpu.CompilerParams(dimension_semantics=("arbitrary",) * n_axes,
                                vmem_limit_bytes=VMEM_LIMIT_BYTES)


def _sigmoid(x):
    return 1.0 / (1.0 + jnp.exp(-x))


def _gelu_tanh(x):
    return 0.5 * x * (1.0 + jnp.tanh(math.sqrt(2.0 / math.pi) * (x + 0.044715 * (x * x * x))))


def _softplus(x):
    return jnp.maximum(x, 0.0) + jnp.log(1.0 + jnp.exp(-jnp.abs(x)))


def _dot(a, b):
    return jnp.dot(a, b, preferred_element_type=F32)


def _dot_exact(a, b):
    return jnp.dot(a, b, preferred_element_type=F32, precision=lax.Precision.HIGHEST)


def _head_sum_matrix():
    idx = np.arange(GROUP_W) // HEAD_DIM
    return jnp.asarray((idx[:, None] == idx[None, :]).astype(np.float32))


def _mod_body(c_ref, w_ref, b_ref, o_ref):
    c = c_ref[...]
    s = (c * _sigmoid(c)).astype(BF16)
    o_ref[...] = _dot(s, w_ref[...].astype(BF16)) + b_ref[...]


def _modulation(cvec, w, b):
    rows, d = cvec.shape
    n = w.shape[1]
    tn = 512
    return pl.pallas_call(
        _mod_body,
        grid=(n // tn,),
        in_specs=[pl.BlockSpec((rows, d), lambda j: (0, 0)),
                  pl.BlockSpec((d, tn), lambda j: (0, j)),
                  pl.BlockSpec((1, tn), lambda j: (0, j))],
        out_specs=pl.BlockSpec((rows, tn), lambda j: (0, j)),
        out_shape=jax.ShapeDtypeStruct((rows, n), F32),
        compiler_params=_cparams(1),
        name="adaln_mod",
    )(cvec, w, b.reshape(1, n))


def _in_body(x_ref, g_ref, mod_ref, w_ref, za_ref, zb_ref, zc_ref, zd_ref):
    tt, nb, d = x_ref.shape
    x = x_ref[...]
    ms = jnp.mean(x * x, axis=-1, keepdims=True)
    y = x * lax.rsqrt(ms + NORM_EPS) * g_ref[...]
    h = y * (1.0 + mod_ref[1]) + mod_ref[0]
    hb = h.reshape(tt * nb, d).astype(BF16)
    c0 = 0
    for ref in (za_ref, zb_ref, zc_ref, zd_ref):
        w = ref.shape[-1]
        ref[...] = _dot(hb, w_ref[:, c0:c0 + w]).reshape(tt, nb, w)
        c0 += w


def _in_proj(xs, g, modv, w_p, n_ctx_tiles):
    lt, nb, d = xs.shape
    tt = ROW_TILE_T
    widths = (ZA_W, RET_COLS, LRU_COLS, GROUP_W)
    seg = lambda i: jnp.where(i >= n_ctx_tiles, 1, 0)
    return pl.pallas_call(
        _in_body,
        grid=(lt // tt,),
        in_specs=[pl.BlockSpec((tt, nb, d), lambda i: (i, 0, 0)),
                  pl.BlockSpec((1, d), lambda i: (0, 0)),
                  pl.BlockSpec((None, N_MOD, nb, d), lambda i: (seg(i), 0, 0, 0)),
                  pl.BlockSpec(w_p.shape, lambda i: (0, 0))],
        out_specs=[pl.BlockSpec((tt, nb, w), lambda i: (i, 0, 0)) for w in widths],
        out_shape=[jax.ShapeDtypeStruct((lt, nb, w), F32) for w in widths],
        compiler_params=_cparams(1),
        name="in_proj",
    )(xs, g.reshape(1, d), modv, w_p)


def _rwkv_feat_body(n_ctx_tiles, n_tiles,
                    z_ref, zp_ref, zn_ref, mu_ref, gup_ref, wup_ref, aup_ref, vec_ref, hs_ref,
                    r_ref, k_ref, v_ref, g_ref, av_ref, w_ref, kd_ref, bv_ref):
    i = pl.program_id(0)
    tt, nb, zw = z_ref.shape
    z0 = z_ref[...]
    has_prev = jnp.logical_and(i != 0, i != n_ctx_tiles)
    has_next = jnp.logical_and(i != n_ctx_tiles - 1, i != n_tiles - 1)
    hp = jnp.where(has_prev, zp_ref[...], 0.0)
    hn = jnp.where(has_next, zn_ref[...], 0.0)
    zp = jnp.concatenate([hp, z0[:-1]], axis=0)
    zn = jnp.concatenate([z0[1:], hn], axis=0)
    z = z0 + mu_ref[0] * (zp - z0) + mu_ref[1] * (zn - z0)
    z = z.reshape(tt * nb, zw)
    gw = GROUP_W
    r = z[:, 0:gw]
    k = z[:, gw:2 * gw]
    v = z[:, 2 * gw:3 * gw]
    g = _dot(_sigmoid(z[:, 3 * gw:3 * gw + 128]).astype(BF16), gup_ref[...])
    w_low = jnp.tanh(z[:, 3 * gw + 128:3 * gw + 256]).astype(BF16)
    a_low = z[:, 3 * gw + 256:3 * gw + 384].astype(BF16)
    k_k = vec_ref[0:1, :]
    k_a = vec_ref[1:2, :]
    kk = k * k_k
    ss = _dot_exact(kk * kk, hs_ref[...])
    kk = kk / jnp.maximum(jnp.sqrt(ss), 1e-12)
    shp = (tt, nb, gw)
    r_ref[...] = r.reshape(shp)
    k_ref[...] = k.reshape(shp)
    v_ref[...] = v.reshape(shp)
    g_ref[...] = g.reshape(shp)
    av_ref[...] = (-kk).reshape(shp)
    for d in range(2):
        w0 = vec_ref[2 + d:3 + d, :]
        a0 = vec_ref[4 + d:5 + d, :]
        w_log = -_softplus(-(w0 + _dot(w_low, wup_ref[d]))) - 0.5
        w_ref[d] = jnp.exp(-jnp.exp(w_log)).reshape(shp)
        a = _sigmoid(a0 + _dot(a_low, aup_ref[d]))
        kd_ref[d] = (k * (1.0 + (a - 1.0) * k_a)).reshape(shp)
        bv_ref[d] = (kk * a).reshape(shp)


def _rwkv_features(za, mu_p, gup_p, wup_p, aup_p, vecs, hs, n_ctx_tiles):
    lt, nb, zw = za.shape
    tt = ROW_TILE_T
    n_tiles = lt // tt
    gw = GROUP_W
    one = lambda shape: jax.ShapeDtypeStruct(shape, F32)
    full = lambda a: pl.BlockSpec(a.shape, lambda i: (0,) * a.ndim)
    o1 = pl.BlockSpec((tt, nb, gw), lambda i: (i, 0, 0))
    o2 = pl.BlockSpec((2, tt, nb, gw), lambda i: (0, i, 0, 0))
    return pl.pallas_call(
        functools.partial(_rwkv_feat_body, n_ctx_tiles, n_tiles),
        grid=(n_tiles,),
        in_specs=[pl.BlockSpec((tt, nb, zw), lambda i: (i, 0, 0)),
                  pl.BlockSpec((1, nb, zw), lambda i: (jnp.maximum(i * tt - 1, 0), 0, 0)),
                  pl.BlockSpec((1, nb, zw), lambda i: (jnp.minimum((i + 1) * tt, lt - 1), 0, 0)),
                  full(mu_p), full(gup_p), full(wup_p), full(aup_p), full(vecs), full(hs)],
        out_specs=[o1, o1, o1, o1, o1, o2, o2, o2],
        out_shape=[one((lt, nb, gw))] * 5 + [one((2, lt, nb, gw))] * 3,
        compiler_params=_cparams(1),
        name="rwkv_features",
    )(za, za, za, mu_p, gup_p, wup_p, aup_p, vecs, hs)


def _rwkv_scan_body(r_ref, a_ref, w_ref, kd_ref, b_ref, v_ref, y_ref, s_ref):
    d = pl.program_id(0)
    j = pl.program_id(1)
    tc = r_ref.shape[0]
    n_k = HEAD_DIM
    n_acc = 4

    @pl.when(j == 0)
    def _():
        s_ref[...] = jnp.zeros_like(s_ref)

    def step(i, carry):
        t = jnp.where(d == 0, i, tc - 1 - i)
        vv = v_ref[t]
        acc = [None] * n_acc
        for kk in range(n_k):
            term = s_ref[kk] * a_ref[t, pl.ds(kk, 1), :]
            acc[kk % n_acc] = term if acc[kk % n_acc] is None else acc[kk % n_acc] + term
        sa = (acc[0] + acc[1]) + (acc[2] + acc[3])
        acc = [None] * n_acc
        for kk in range(n_k):
            s_new = (s_ref[kk] * w_ref[t, pl.ds(kk, 1), :] + sa * b_ref[t, pl.ds(kk, 1), :]
                     + vv * kd_ref[t, pl.ds(kk, 1), :])
            s_ref[kk] = s_new
            term = s_new * r_ref[t, pl.ds(kk, 1), :]
            acc[kk % n_acc] = term if acc[kk % n_acc] is None else acc[kk % n_acc] + term
        y_ref[t] = (acc[0] + acc[1]) + (acc[2] + acc[3])
        return carry

    lax.fori_loop(0, tc, step, 0)


def _dir_chunk(d, j, n_ctx, n_all):
    bwd = jnp.where(j < n_ctx, n_ctx - 1 - j, n_all - 1 - (j - n_ctx))
    return jnp.where(d == 0, j, bwd)


def _rwkv_scan(r_e, a_e, w_e, kd_e, b_e, v_e, l_ctx):
    lt, n_k, lanes = r_e.shape
    v_hi = v_e.shape[1]
    tc = RWKV_CHUNK
    n_all, n_ctx = lt // tc, l_ctx // tc
    cm = lambda d, j: _dir_chunk(d, j, n_ctx, n_all)
    shared = pl.BlockSpec((tc, n_k, lanes), lambda d, j: (cm(d, j), 0, 0))
    per_dir = pl.BlockSpec((None, tc, n_k, lanes), lambda d, j: (d, cm(d, j), 0, 0))
    return pl.pallas_call(
        _rwkv_scan_body,
        grid=(2, n_all),
        in_specs=[shared, shared, per_dir, per_dir, per_dir,
                  pl.BlockSpec((tc, v_hi, lanes), lambda d, j: (cm(d, j), 0, 0))],
        out_specs=pl.BlockSpec((None, tc, v_hi, lanes), lambda d, j: (d, cm(d, j), 0, 0)),
        out_shape=jax.ShapeDtypeStruct((2, lt, v_hi, lanes), F32),
        scratch_shapes=[pltpu.VMEM((n_k, v_hi, lanes), F32)],
        compiler_params=_cparams(2),
        name="rwkv_scan",
    )(r_e, a_e, w_e, kd_e, b_e, v_e)


def _rwkv_expand_k(x, v_lo):
    lead = x.shape[:-2]
    nb = x.shape[-2]
    x = x.reshape(lead + (nb, N_HEADS, HEAD_DIM))
    x = jnp.moveaxis(x, -1, -3).reshape(lead + (HEAD_DIM, nb * N_HEADS))
    return jnp.tile(x, (1,) * (x.ndim - 1) + (v_lo,))


def _rwkv_expand_v(x, v_lo):
    lt, nb, _ = x.shape
    v_hi = HEAD_DIM // v_lo
    x = x.reshape(lt, nb, N_HEADS, v_hi, v_lo)
    return jnp.transpose(x, (0, 3, 4, 1, 2)).reshape(lt, v_hi, v_lo * nb * N_HEADS)


def _rwkv_collapse_v(y, nb, v_lo):
    _, lt, v_hi, _ = y.shape
    y = y.reshape(2, lt, v_hi, v_lo, nb, N_HEADS)
    return jnp.transpose(y, (0, 1, 4, 5, 2, 3)).reshape(2, lt, nb, GROUP_W)


def _ret_body(z_ref, cos_ref, sin_ref, dec_ref, rd_ref, wd_ref, cd_ref, hm_ref, o_ref, s_ref):
    j = pl.program_id(2)
    gw = GROUP_W
    half = gw // 2

    @pl.when(j == 0)
    def _():
        s_ref[...] = jnp.zeros_like(s_ref)

    cos = cos_ref[...]
    sin = sin_ref[...]

    def rope(x):
        x1, x2 = x[:, :half], x[:, half:]
        return jnp.concatenate([x1 * cos - x2 * sin, x1 * sin + x2 * cos], axis=-1)

    q = rope(z_ref[:, 0:gw]) * (HEAD_DIM ** -0.5)
    k = rope(z_ref[:, gw:2 * gw])
    v = z_ref[:, 2 * gw:3 * gw]
    vb = v.astype(BF16)
    k_t = k.T
    k_tb = k_t.astype(BF16)
    inter = _dot(q.astype(BF16), s_ref[...].astype(BF16)) * rd_ref[...]
    intra = jnp.zeros_like(inter)
    for h in range(N_HEADS):
        qm = (q * hm_ref[h:h + 1, :]).astype(BF16)
        att = _dot(qm, k_tb) * dec_ref[h]
        vm = (v * hm_ref[N_HEADS + h:N_HEADS + h + 1, :]).astype(BF16)
        intra = intra + _dot(att.astype(BF16), vm)
    o_ref[...] = intra + inter
    kv = _dot((k_t * wd_ref[...]).astype(BF16), vb)
    s_ref[...] = s_ref[...] * cd_ref[0] + kv * cd_ref[1]


def _retention(zb_bm, tabs, l_ctx):
    nb, lt, _ = zb_bm.shape
    c = RET_CHUNK
    n_all, n_ctx = lt // c, l_ctx // c
    gw = GROUP_W
    cos, sin, dec, rd, wd, cd, hm = tabs
    cm = lambda d, j: _dir_chunk(d, j, n_ctx, n_all)
    return pl.pallas_call(
        _ret_body,
        grid=(nb, 2, n_all),
        in_specs=[pl.BlockSpec((None, c, RET_COLS), lambda b, d, j: (b, cm(d, j), 0)),
                  pl.BlockSpec((c, gw // 2), lambda b, d, j: (cm(d, j), 0)),
                  pl.BlockSpec((c, gw // 2), lambda b, d, j: (cm(d, j), 0)),
                  pl.BlockSpec((None, N_HEADS, c, c), lambda b, d, j: (d, 0, 0, 0)),
                  pl.BlockSpec((None, c, gw), lambda b, d, j: (d, 0, 0)),
                  pl.BlockSpec((None, gw, c), lambda b, d, j: (d, 0, 0)),
                  pl.BlockSpec((None, 2, gw, gw), lambda b, d, j: (d, 0, 0, 0)),
                  pl.BlockSpec((2 * N_HEADS, gw), lambda b, d, j: (0, 0))],
        out_specs=pl.BlockSpec((None, None, c, gw), lambda b, d, j: (d, b, cm(d, j), 0)),
        out_shape=jax.ShapeDtypeStruct((2, nb, lt, gw), F32),
        scratch_shapes=[pltpu.VMEM((gw, gw), F32)],
        compiler_params=_cparams(3),
        name="retention",
    )(zb_bm, cos, sin, dec, rd, wd, cd, hm)


def _ret_perm():
    new = np.arange(GROUP_W)
    half_id, rem = new // 128, new % 128
    h, i = rem // 32, rem % 32
    return h * HEAD_DIM + half_id * 32 + i


def _ret_tables(lt):
    c = RET_CHUNK
    half = HEAD_DIM // 2
    freqs = ROPE_BASE ** (-jnp.arange(half, dtype=F32) / half)
    ang = jnp.arange(lt, dtype=F32)[:, None] * freqs[None]
    cos = jnp.tile(jnp.cos(ang), (1, N_HEADS))
    sin = jnp.tile(jnp.sin(ang), (1, N_HEADS))
    idx = jnp.arange(c, dtype=F32)
    perm = _ret_perm()
    head_of_qk = jnp.asarray(perm // HEAD_DIM)
    head_of_v = jnp.arange(GROUP_W) // HEAD_DIM
    dec, rd, wd, cd = [], [], [], []
    for offset, rev in ((RET_FWD_OFFSET, False), (RET_BWD_OFFSET, True)):
        lg = jnp.log1p(-jnp.exp2(-(offset + jnp.arange(N_HEADS, dtype=F32))))
        diff = idx[:, None] - idx[None, :]
        if rev:
            mask, dist = diff < 0, -diff
            read_pow, write_pow = c - idx, idx
        else:
            mask, dist = diff >= 0, diff
            read_pow, write_pow = idx + 1.0, c - 1.0 - idx
        dec.append(jnp.where(mask[None], jnp.exp(lg[:, None, None] * jnp.where(mask, dist, 0.0)[None]), 0.0))
        rd.append(jnp.exp(lg[head_of_v][None, :] * read_pow[:, None]))
        wd.append(jnp.exp(lg[head_of_qk][:, None] * write_pow[None, :]))
        same = (head_of_qk[:, None] == head_of_v[None, :]).astype(F32)
        cd.append(jnp.stack([same * jnp.exp(lg * c)[head_of_v][None, :], same]))
    hm_q = (head_of_qk[None, :] == jnp.arange(N_HEADS)[:, None]).astype(F32)
    hm_v = (head_of_v[None, :] == jnp.arange(N_HEADS)[:, None]).astype(F32)
    return cos, sin, jnp.stack(dec), jnp.stack(rd), jnp.stack(wd), jnp.stack(cd), jnp.concatenate([hm_q, hm_v])


def _lru_body(n_ctx, n_all, x_ref, xp_ref, xn_ref, cw_ref, wa_ref, wx_ref, vec_ref, o_ref,
              a_s, b_s, h_s):
    d = pl.program_id(0)
    j = pl.program_id(1)
    tc, nb, gw = x_ref.shape
    ch = _dir_chunk(d, j, n_ctx, n_all)

    @pl.when(j == 0)
    def _():
        h_s[...] = jnp.zeros_like(h_s)

    has_prev = jnp.logical_and(ch != 0, ch != n_ctx)
    has_next = jnp.logical_and(ch != n_ctx - 1, ch != n_all - 1)
    xe = jnp.concatenate([jnp.where(has_prev, xp_ref[...], 0.0), x_ref[...],
                          jnp.where(has_next, xn_ref[...], 0.0)], axis=0)
    xc = cw_ref[LRU_CONV_W:LRU_CONV_W + 1, :]
    for tap in range(LRU_CONV_W):
        xc = xc + cw_ref[tap:tap + 1, :] * xe[tap:tap + tc]
    xc = xc.reshape(tc * nb, gw)
    xb = xc.astype(BF16)
    r = _sigmoid(_dot(xb, wa_ref[...]) + vec_ref[0:1, :])
    gi = _sigmoid(_dot(xb, wx_ref[...]) + vec_ref[1:2, :])
    log_a = -LRU_C * r * _softplus(-vec_ref[2:3, :])
    th = jnp.tanh(log_a)
    one_minus_a2 = 2.0 * th / (th - 1.0)
    a_s[...] = jnp.exp(log_a).reshape(tc, nb, gw)
    b_s[...] = (jnp.sqrt(one_minus_a2) * (gi * xc)).reshape(tc, nb, gw)

    def step(i, h):
        t = jnp.where(d == 0, i, tc - 1 - i)
        h = a_s[t] * h + b_s[t]
        o_ref[t] = h
        return h

    h_s[...] = lax.fori_loop(0, tc, step, h_s[...])


def _lru(zc, cw, wa_bd, wx_bd, vecs, l_ctx):
    lt, nb, _ = zc.shape
    gw = GROUP_W
    tc = LRU_CHUNK
    n_all, n_ctx = lt // tc, l_ctx // tc
    cm = lambda d, j: _dir_chunk(d, j, n_ctx, n_all)
    return pl.pallas_call(
        functools.partial(_lru_body, n_ctx, n_all),
        grid=(2, n_all),
        in_specs=[pl.BlockSpec((tc, nb, gw), lambda d, j: (cm(d, j), 0, 0)),
                  pl.BlockSpec((2, nb, gw), lambda d, j: (jnp.maximum(cm(d, j) * (tc // 2) - 1, 0), 0, 0)),
                  pl.BlockSpec((1, nb, gw), lambda d, j: (jnp.minimum((cm(d, j) + 1) * tc, lt - 1), 0, 0)),
                  pl.BlockSpec(cw.shape, lambda d, j: (0, 0)),
                  pl.BlockSpec((None, gw, gw), lambda d, j: (d, 0, 0)),
                  pl.BlockSpec((None, gw, gw), lambda d, j: (d, 0, 0)),
                  pl.BlockSpec((None, 3, gw), lambda d, j: (d, 0, 0))],
        out_specs=pl.BlockSpec((None, tc, nb, gw), lambda d, j: (d, cm(d, j), 0, 0)),
        out_shape=jax.ShapeDtypeStruct((2, lt, nb, gw), F32),
        scratch_shapes=[pltpu.VMEM((tc, nb, gw), F32), pltpu.VMEM((tc, nb, gw), F32),
                        pltpu.VMEM((nb, gw), F32)],
        compiler_params=_cparams(2),
        name="rglru",
    )(zc, zc, zc, cw, wa_bd, wx_bd, vecs)


def _block_diag(w):
    nblk, n = w.shape[-3], w.shape[-1]
    eye = jnp.eye(nblk, dtype=w.dtype)
    out = w[..., :, :, None, :] * eye[:, None, :, None]
    return out.reshape(w.shape[:-3] + (nblk * n, nblk * n))


def _s5_body(u_ref, bm_ref, lam_ref, cm_ref, y_ref, h_s, st_s):
    d = pl.program_id(0)
    j = pl.program_id(1)
    tc, nb, gw = u_ref.shape
    n = S5_N

    @pl.when(j == 0)
    def _():
        st_s[...] = jnp.zeros_like(st_s)

    u = u_ref[...].reshape(tc * nb, gw).astype(BF16)
    h_s[...] = _dot(u, bm_ref[...]).reshape(tc, nb, 2 * n)
    lr = lam_ref[0:1, :]
    li = lam_ref[1:2, :]

    def step(i, carry):
        hr, hi = carry
        t = jnp.where(d == 0, i, tc - 1 - i)
        bu = h_s[t]
        nr = lr * hr - li * hi + bu[:, :n]
        ni = lr * hi + li * hr + bu[:, n:]
        h_s[t] = jnp.concatenate([nr, ni], axis=-1)
        return nr, ni

    hr, hi = lax.fori_loop(0, tc, step, (st_s[0], st_s[1]))
    st_s[0] = hr
    st_s[1] = hi
    hh = h_s[...].reshape(tc * nb, 2 * n).astype(BF16)
    y_ref[...] = _dot(hh, cm_ref[...]).reshape(tc, nb, gw)


def _s5(u, bmat, lam, cmat, l_ctx):
    lt, nb, gw = u.shape
    tc = S5_CHUNK
    n_all, n_ctx = lt // tc, l_ctx // tc
    cm = lambda d, j: _dir_chunk(d, j, n_ctx, n_all)
    return pl.pallas_call(
        _s5_body,
        grid=(2, n_all),
        in_specs=[pl.BlockSpec((tc, nb, gw), lambda d, j: (cm(d, j), 0, 0)),
                  pl.BlockSpec((None, gw, 2 * S5_N), lambda d, j: (d, 0, 0)),
                  pl.BlockSpec((None, 2, S5_N), lambda d, j: (d, 0, 0)),
                  pl.BlockSpec((None, 2 * S5_N, gw), lambda d, j: (d, 0, 0))],
        out_specs=pl.BlockSpec((None, tc, nb, gw), lambda d, j: (d, cm(d, j), 0, 0)),
        out_shape=jax.ShapeDtypeStruct((2, lt, nb, gw), F32),
        scratch_shapes=[pltpu.VMEM((tc, nb, 2 * S5_N), F32), pltpu.VMEM((2, nb, S5_N), F32)],
        compiler_params=_cparams(2),
        name="s5",
    )(u, bmat, lam, cmat)


def _s5_params(a_re, a_im, log_dt, b_re, b_im, c_re, c_im):
    dt = jnp.exp(log_dt)[..., None]
    er = jnp.exp(a_re * dt)
    lbr, lbi = er * jnp.cos(a_im * dt), er * jnp.sin(a_im * dt)
    den = a_re * a_re + a_im * a_im
    nr, ni = lbr - 1.0, lbi
    fr = (nr * a_re + ni * a_im) / den
    fi = (ni * a_re - nr * a_im) / den
    bbr = fr[..., None] * b_re - fi[..., None] * b_im
    bbi = fr[..., None] * b_im + fi[..., None] * b_re
    eye = jnp.eye(S5_GROUPS, dtype=F32)

    def in_map(bb):
        m = jnp.einsum('dgpc,gh->dgchp', bb, eye)
        return m.reshape(2, GROUP_W, S5_N)

    def out_map(cc):
        m = jnp.einsum('dgcp,gh->dgphc', cc, eye)
        return m.reshape(2, S5_N, GROUP_W)

    bmat = jnp.concatenate([in_map(bbr), in_map(bbi)], axis=-1).astype(BF16)
    cmat = jnp.concatenate([out_map(c_re), -out_map(c_im)], axis=-2).astype(BF16)
    lam = jnp.stack([lbr.reshape(2, S5_N), lbi.reshape(2, S5_N)], axis=1)
    return bmat, lam, cmat


def _to_colmajor(z, l_ctx):
    lt, nb, ch = z.shape
    rows = (lt - l_ctx) // GRID_W
    lat = z[l_ctx:].reshape(rows, GRID_W, nb, ch).transpose(1, 0, 2, 3).reshape(lt - l_ctx, nb, ch)
    return jnp.concatenate([z[:l_ctx], lat], axis=0)


def _from_colmajor(y, l_ctx):
    lt = y.shape[-3]
    nb, ch = y.shape[-2:]
    rows = (lt - l_ctx) // GRID_W
    lead = y.shape[:-3]
    lat = y[..., l_ctx:, :, :].reshape(lead + (GRID_W, rows, nb, ch))
    lat = jnp.swapaxes(lat, -4, -3).reshape(lead + (lt - l_ctx, nb, ch))
    return jnp.concatenate([y[..., :l_ctx, :, :], lat], axis=-3)


def _out_body(x_ref, mod_ref, ya_ref, r_ref, k_ref, v_ref, g_ref, ob_ref, gb_ref, hc_ref, gc_ref,
              yd_ref, ud_ref, vec_ref, hs_ref, gluw_ref, glub_ref, wo_ref, n2_ref, x1_ref, h2_ref):
    tt, nb, d = x_ref.shape
    gw = GROUP_W
    rows = tt * nb
    hs = hs_ref[...]
    two = lambda ref: (ref[0] + ref[1]).reshape(rows, gw)
    flat = lambda ref: ref[...].reshape(rows, gw)

    def head_norm(y, eps):
        mu = _dot_exact(y, hs) * (1.0 / HEAD_DIM)
        dlt = y - mu
        var = _dot_exact(dlt * dlt, hs) * (1.0 / HEAD_DIM)
        return dlt * lax.rsqrt(var + eps)

    r_k, ln_g, ln_b = vec_ref[0:1, :], vec_ref[1:2, :], vec_ref[2:3, :]
    r, k, v = flat(r_ref), flat(k_ref), flat(v_ref)
    yn = head_norm(two(ya_ref), RWKV_GN_EPS) * ln_g + ln_b
    bonus = _dot_exact(r * k * r_k, hs) * v
    mix_a = (yn + bonus) * flat(g_ref)
    gb = flat(gb_ref)
    mix_b = gb * _sigmoid(gb) * (head_norm(two(ob_ref), RET_GN_EPS) * vec_ref[3:4, :])
    mix_c = _gelu_tanh(flat(gc_ref)) * two(hc_ref)
    yd = vec_ref[4:5, :] * flat(ud_ref) + two(yd_ref)
    glu = _dot(_gelu_tanh(yd).astype(BF16), gluw_ref[...]) + glub_ref[...]
    mix_d = glu[:, :gw] * _sigmoid(glu[:, gw:])

    mix = _dot(mix_a.astype(BF16), wo_ref[0:gw, :])
    mix = mix + _dot(mix_b.astype(BF16), wo_ref[gw:2 * gw, :])
    mix = mix + _dot(mix_c.astype(BF16), wo_ref[2 * gw:3 * gw, :])
    mix = mix + _dot(mix_d.astype(BF16), wo_ref[3 * gw:4 * gw, :])
    x1 = x_ref[...] + mod_ref[2] * mix.reshape(tt, nb, d)
    x1_ref[...] = x1
    ms = jnp.mean(x1 * x1, axis=-1, keepdims=True)
    h2 = (x1 * lax.rsqrt(ms + NORM_EPS) * n2_ref[...]) * (1.0 + mod_ref[4]) + mod_ref[3]
    h2_ref[...] = h2.reshape(rows, d).astype(BF16)


def _out_proj(xs, modv, ya, feats, ob, zb, hc, zc, yd, zd, vecs, hs, gluw, glub, wo, n2, n_ctx_tiles, t0_tile):
    lt, nb, d = xs.shape
    tt = ROW_TILE_T
    gw = GROUP_W
    n_tiles = lt // tt - t0_tile
    seg = lambda i: jnp.where(i + t0_tile >= n_ctx_tiles, 1, 0)
    col = lambda c: pl.BlockSpec((tt, nb, gw), lambda i: (i + t0_tile, 0, c))
    two = pl.BlockSpec((2, tt, nb, gw), lambda i: (0, i + t0_tile, 0, 0))
    full = lambda a: pl.BlockSpec(a.shape, lambda i: (0,) * a.ndim)
    r, k, v, g = feats
    return pl.pallas_call(
        _out_body,
        grid=(n_tiles,),
        in_specs=[pl.BlockSpec((tt, nb, d), lambda i: (i + t0_tile, 0, 0)),
                  pl.BlockSpec((None, N_MOD, nb, d), lambda i: (seg(i), 0, 0, 0)),
                  two, col(0), col(0), col(0), col(0),
                  two, col(3), two, col(1), two, col(0),
                  full(vecs), full(hs), full(gluw), full(glub), full(wo), full(n2)],
        out_specs=[pl.BlockSpec((tt, nb, d), lambda i: (i + t0_tile, 0, 0)),
                   pl.BlockSpec((tt * nb, d), lambda i: (i + t0_tile, 0))],
        out_shape=[jax.ShapeDtypeStruct((lt, nb, d), F32), jax.ShapeDtypeStruct((lt * nb, d), BF16)],
        compiler_params=_cparams(1),
        name="out_proj",
    )(xs, modv, ya, r, k, v, g, ob, zb, hc, zc, yd, zd, vecs, hs, gluw, glub, wo, n2)


def _top_values(s, n):
    vals, cur = [], s
    for _ in range(n):
        m = jnp.max(cur, axis=0, keepdims=True)
        vals.append(m)
        cur = jnp.where(cur == m, NEG_INF, cur)
    return vals


def _peer_route_body(h_ref, wq_ref, keys_ref, e1_ref, thr_ref, s2_ref, w2_ref):
    hb = h_ref[...]
    for h in range(PEER_HEADS):
        st = []
        for p in range(2):
            c0 = (2 * h + p) * PEER_DKEY
            q = _dot(hb, wq_ref[:, c0:c0 + PEER_DKEY]).astype(BF16)
            st.append(lax.dot_general(keys_ref[2 * h + p], q, (((1,), (1,)), ((), ())),
                                      preferred_element_type=F32))
        s1, s2 = st
        v1 = _top_values(s1, PEER_TOPK + 1)
        v2 = _top_values(s2, PEER_TOPK + 1)
        v2m = jnp.concatenate(v2[:PEER_TOPK], axis=0)
        pad = jnp.full((6, s1.shape[1]), NEG_INF, F32)
        cand = jnp.concatenate([v1[a] + v2m for a in range(PEER_TOPK)]
                               + [v1[PEER_TOPK] + v2[0], v1[0] + v2[PEER_TOPK], pad], axis=0)
        top = _top_values(cand, PEER_TOPK + 1)
        tau = 0.5 * (top[PEER_TOPK - 1] + top[PEER_TOPK])
        cmax = top[0]
        z = jnp.sum(jnp.where(cand >= tau, jnp.exp(cand - cmax), 0.0), axis=0, keepdims=True)
        e1_ref[:, h, :] = jnp.exp(s1 - v1[0])
        thr_ref[:, h, :] = tau - s1
        s2_ref[h] = s2
        w2_ref[h] = jnp.exp(s2 - v2[0]) / z


def _peer_route(h2, wq, keys, t0_tile):
    t_all, d = h2.shape
    tt = PEER_TOK_TILE
    nk = PEER_NKEYS
    by_i = pl.BlockSpec((nk, PEER_HEADS, tt), lambda i: (0, 0, i + t0_tile))
    by_h = pl.BlockSpec((PEER_HEADS, nk, tt), lambda i: (0, 0, i + t0_tile))
    return pl.pallas_call(
        _peer_route_body,
        grid=(t_all // tt - t0_tile,),
        in_specs=[pl.BlockSpec((tt, d), lambda i: (i + t0_tile, 0)),
                  pl.BlockSpec(wq.shape, lambda i: (0, 0)),
                  pl.BlockSpec(keys.shape, lambda i: (0, 0, 0))],
        out_specs=[by_i, by_i, by_h, by_h],
        out_shape=[jax.ShapeDtypeStruct((nk, PEER_HEADS, t_all), F32)] * 2
                  + [jax.ShapeDtypeStruct((PEER_HEADS, nk, t_all), F32)] * 2,
        compiler_params=_cparams(1),
        name="peer_route",
    )(h2, wq, keys)


def _peer_gate_block(act_ref, p_ref, e1_ref, thr_ref, s2_ref, w2_ref, mxu_jobs=()):
    nk = PEER_NKEYS
    n_sub = act_ref.shape[0] // nk
    jobs = list(mxu_jobs)
    every = max(n_sub // max(len(jobs), 1), 1)
    for ii in range(n_sub):
        gate = None
        for h in range(PEER_HEADS):
            sel = s2_ref[h] >= thr_ref[ii, h:h + 1, :]
            term = jnp.where(sel, w2_ref[h] * e1_ref[ii, h:h + 1, :], 0.0)
            gate = term if gate is None else gate + term
        rows = slice(ii * nk, (ii + 1) * nk)
        p_ref[rows, :] = (_gelu_tanh(act_ref[rows, :]) * gate).astype(BF16)
        if jobs and ii % every == 0:
            jobs.pop(0)()
    for job in jobs:
        job()


def _peer_ffn_body(h_ref, u0_ref, ua_ref, ub_ref, vta_ref, vtb_ref, e1a_ref, thra_ref, e1b_ref, thrb_ref,
                   s2_ref, w2_ref, x_ref, mod_ref, o_ref, acc_ref, acta_ref, actb_ref, pa_ref, pb_ref):
    j = pl.program_id(1)
    last = pl.num_programs(1) - 1
    nt = (((1,), (1,)), ((), ()))

    half = h_ref.shape[0] // 2
    halves = (slice(0, half), slice(half, 2 * half))

    def pre_act(u_ref):
        return lax.dot_general(u_ref[...], h_ref[...], nt, preferred_element_type=F32)

    def mxu_jobs(u_ref, act_ref, vt_ref, p_ref):
        def pre_act_half(tok):
            def job():
                act_ref[:, tok] = lax.dot_general(u_ref[...], h_ref[tok, :], nt, preferred_element_type=F32)
            return job

        def fold_half(tok):
            def job():
                acc_ref[:, tok] += _dot(vt_ref[...], p_ref[:, tok])
            return job

        return [fold_half(halves[0]), pre_act_half(halves[0]), fold_half(halves[1]), pre_act_half(halves[1])]

    @pl.when(j == 0)
    def _():
        acc_ref[...] = jnp.zeros_like(acc_ref)
        pb_ref[...] = jnp.zeros_like(pb_ref)
        acta_ref[...] = pre_act(u0_ref)

    @pl.when(j < last)
    def _():
        _peer_gate_block(acta_ref, pa_ref, e1a_ref, thra_ref, s2_ref, w2_ref,
                         mxu_jobs(ua_ref, actb_ref, vta_ref, pb_ref))
        _peer_gate_block(actb_ref, pb_ref, e1b_ref, thrb_ref, s2_ref, w2_ref,
                         mxu_jobs(ub_ref, acta_ref, vtb_ref, pa_ref))

    @pl.when(j == last)
    def _():
        tt, d = x_ref.shape
        nb = mod_ref.shape[1]
        acc = acc_ref[...] + _dot(vta_ref[...], pb_ref[...])
        y = acc.T.reshape(tt // nb, nb, d) * mod_ref[5]
        o_ref[...] = x_ref[...] + y.reshape(tt, d)


def _peer_ffn(h2, u_b, vt_b, e1, thr, s2, w2, x1, modv, n_ctx_tok_tiles, t0_tile):
    t_all, d = h2.shape
    tt = PEER_TOK_TILE
    eb = PEER_EXP_BLOCK
    nk = PEER_NKEYS
    nb = modv.shape[2]
    n_e = u_b.shape[0] // eb
    assert n_e % 2 == 0
    n_steps = n_e // 2 + 1
    n_tiles = t_all // tt - t0_tile
    seg = lambda i: jnp.where(i + t0_tile >= n_ctx_tok_tiles, 1, 0)
    blk_a = lambda j: jnp.minimum(2 * j, n_e - 2)
    blk_b = lambda j: jnp.minimum(2 * j + 1, n_e - 1)
    blk_next = lambda j: jnp.minimum(2 * j + 2, n_e - 1)
    blk_prev = lambda j: jnp.maximum(2 * j - 1, 0)
    u_spec = lambda f: pl.BlockSpec((eb, d), lambda i, j: (f(j), 0))
    vt_spec = lambda f: pl.BlockSpec((d, eb), lambda i, j: (0, f(j)))
    by_i = lambda f: pl.BlockSpec((eb // nk, PEER_HEADS, tt), lambda i, j: (f(j), 0, i + t0_tile))
    by_h = pl.BlockSpec((PEER_HEADS, nk, tt), lambda i, j: (0, 0, i + t0_tile))
    return pl.pallas_call(
        _peer_ffn_body,
        grid=(n_tiles, n_steps),
        in_specs=[pl.BlockSpec((tt, d), lambda i, j: (i + t0_tile, 0)),
                  pl.BlockSpec((eb, d), lambda i, j: (0, 0), pipeline_mode=pl.Buffered(1)),
                  u_spec(blk_b), u_spec(blk_next), vt_spec(blk_prev), vt_spec(blk_a),
                  by_i(blk_a), by_i(blk_a), by_i(blk_b), by_i(blk_b), by_h, by_h,
                  pl.BlockSpec((tt, d), lambda i, j: (i + t0_tile, 0)),
                  pl.BlockSpec((None, N_MOD, nb, d), lambda i, j: (seg(i), 0, 0, 0))],
        out_specs=pl.BlockSpec((tt, d), lambda i, j: (i + t0_tile, 0)),
        out_shape=jax.ShapeDtypeStruct((t_all, d), F32),
        scratch_shapes=[pltpu.VMEM((d, tt), F32), pltpu.VMEM((eb, tt), F32), pltpu.VMEM((eb, tt), F32),
                        pltpu.VMEM((eb, tt), BF16), pltpu.VMEM((eb, tt), BF16)],
        compiler_params=_cparams(2),
        name="peer_ffn",
    )(h2, u_b, u_b, u_b, vt_b, vt_b, e1, thr, e1, thr, s2, w2, x1, modv)


def _final_body(x_ref, g_ref, o_ref):
    x = x_ref[...]
    ms = jnp.mean(x * x, axis=-1, keepdims=True)
    o_ref[...] = x * lax.rsqrt(ms + NORM_EPS) * g_ref[...]


def _final_norm(x2d, g):
    t, d = x2d.shape
    tr = 512
    return pl.pallas_call(
        _final_body,
        grid=(t // tr,),
        in_specs=[pl.BlockSpec((tr, d), lambda i: (i, 0)), pl.BlockSpec((1, d), lambda i: (0, 0))],
        out_specs=pl.BlockSpec((tr, d), lambda i: (i, 0)),
        out_shape=jax.ShapeDtypeStruct((t, d), F32),
        compiler_params=_cparams(1),
        name="final_norm",
    )(x2d, g.reshape(1, d))


def _pad_cols(w, width):
    return jnp.pad(w, [(0, 0)] * (w.ndim - 1) + [(0, width - w.shape[-1])])


def _rwkv_slabs(w):
    gw3 = 3 * GROUP_W
    parts = [w[..., :gw3],
             _pad_cols(w[..., gw3:gw3 + 64], 128),
             _pad_cols(w[..., gw3 + 64:gw3 + 128], 128),
             _pad_cols(w[..., gw3 + 128:gw3 + 192], 128)]
    return jnp.concatenate(parts, axis=-1)


def _layer_weights(l, p):
    d = p['w_in'].shape[1]
    w_in = p['w_in'][l]
    wa_, wb_, wc_, wd_ = jnp.split(w_in, [RWKV_COLS, RWKV_COLS + RET_COLS, RWKV_COLS + RET_COLS + LRU_COLS], axis=-1)
    perm = _ret_perm()
    gw = GROUP_W
    wb_ = jnp.concatenate([wb_[:, 0:gw][:, perm], wb_[:, gw:2 * gw][:, perm], wb_[:, 2 * gw:]], axis=-1)
    out = {'w_in': jnp.concatenate([_rwkv_slabs(wa_), wb_, wc_, wd_], axis=-1).astype(BF16)}
    out['mu'] = _rwkv_slabs(p['rwkv_mu'][l])[:, None, :]
    out['gup'] = jnp.pad(p['rwkv_g_up'][l], ((0, 64), (0, 0))).astype(BF16)
    lora = RWKV_LORA
    out['wup'] = jnp.stack([jnp.pad(p['rwkv_w_up'][l, dd], ((dd * lora, 128 - (dd + 1) * lora), (0, 0)))
                            for dd in range(2)]).astype(BF16)
    out['aup'] = jnp.stack([jnp.pad(p['rwkv_a_up'][l, dd], ((dd * lora, 128 - (dd + 1) * lora), (0, 0)))
                            for dd in range(2)]).astype(BF16)
    out['rwkv_vecs'] = jnp.concatenate([p['rwkv_k_k'][l][None], p['rwkv_k_a'][l][None],
                                        p['rwkv_w0'][l], p['rwkv_a0'][l]], axis=0)
    out['cw'] = jnp.concatenate([p['lru_conv_w'][l], p['lru_conv_b'][l][None]], axis=0)
    out['wa'] = _block_diag(p['lru_wa'][l]).astype(BF16)
    out['wx'] = _block_diag(p['lru_wx'][l]).astype(BF16)
    out['lru_vecs'] = jnp.stack([p['lru_ba'][l], p['lru_bx'][l], p['lru_lambda'][l]], axis=1)
    out['s5'] = _s5_params(p['s5_a_re'][l], p['s5_a_im'][l], p['s5_log_dt'][l], p['s5_b_re'][l],
                           p['s5_b_im'][l], p['s5_c_re'][l], p['s5_c_im'][l])
    out['out_vecs'] = jnp.stack([p['rwkv_r_k'][l].reshape(-1), p['rwkv_ln_g'][l], p['rwkv_ln_b'][l],
                                 p['ret_gn_g'][l], p['s5_d'][l]], axis=0)
    out['gluw'] = p['s5_glu_w'][l].astype(BF16)
    out['glub'] = p['s5_glu_b'][l][None]
    out['wo'] = p['w_out'][l].astype(BF16)
    out['n2'] = p['norm2_g'][l][None]
    out['wq'] = p['peer_wq'][l].astype(BF16)
    out['keys'] = p['peer_keys'][l].reshape(PEER_HEADS * 2, PEER_NKEYS, PEER_DKEY).astype(BF16)
    out['u'] = p['peer_u'][l].astype(BF16)
    out['vt'] = p['peer_v'][l].T.astype(BF16)
    return out


def kernel(x, c, ctx, c_ctx, norm1_g, norm2_g, ada_w, ada_b, w_in, w_out, rwkv_mu, rwkv_w0, rwkv_w_up, rwkv_a0, rwkv_a_up, rwkv_g_up, rwkv_k_k, rwkv_k_a, rwkv_r_k, rwkv_ln_g, rwkv_ln_b, ret_gn_g, lru_conv_w, lru_conv_b, lru_wa, lru_ba, lru_wx, lru_bx, lru_lambda, s5_a_re, s5_a_im, s5_log_dt, s5_b_re, s5_b_im, s5_c_re, s5_c_im, s5_d, s5_glu_w, s5_glu_b, peer_wq, peer_keys, peer_u, peer_v, final_norm_g):
    p = dict(w_in=w_in, w_out=w_out, norm2_g=norm2_g, rwkv_mu=rwkv_mu, rwkv_w0=rwkv_w0, rwkv_w_up=rwkv_w_up,
             rwkv_a0=rwkv_a0, rwkv_a_up=rwkv_a_up, rwkv_g_up=rwkv_g_up, rwkv_k_k=rwkv_k_k, rwkv_k_a=rwkv_k_a,
             rwkv_r_k=rwkv_r_k, rwkv_ln_g=rwkv_ln_g, rwkv_ln_b=rwkv_ln_b, ret_gn_g=ret_gn_g,
             lru_conv_w=lru_conv_w, lru_conv_b=lru_conv_b, lru_wa=lru_wa, lru_ba=lru_ba, lru_wx=lru_wx,
             lru_bx=lru_bx, lru_lambda=lru_lambda, s5_a_re=s5_a_re, s5_a_im=s5_a_im, s5_log_dt=s5_log_dt,
             s5_b_re=s5_b_re, s5_b_im=s5_b_im, s5_c_re=s5_c_re, s5_c_im=s5_c_im, s5_d=s5_d,
             s5_glu_w=s5_glu_w, s5_glu_b=s5_glu_b, peer_wq=peer_wq, peer_keys=peer_keys, peer_u=peer_u,
             peer_v=peer_v)
    nb, l_lat, d = x.shape
    l_ctx = ctx.shape[1]
    lt = l_ctx + l_lat
    depth = w_in.shape[0]
    assert 128 % (nb * N_HEADS) == 0 and l_ctx % 128 == 0 and l_lat % 128 == 0
    assert (l_ctx * nb) % PEER_TOK_TILE == 0 and (l_lat * nb) % PEER_TOK_TILE == 0
    v_lo = 128 // (nb * N_HEADS)
    n_ctx_tiles = l_ctx // ROW_TILE_T
    n_ctx_tok_tiles = l_ctx * nb // PEER_TOK_TILE
    hs = _head_sum_matrix()
    ret_tabs = _ret_tables(lt)

    xs = jnp.transpose(jnp.concatenate([ctx, x], axis=1), (1, 0, 2))
    mod_rows = 8 * ((nb + 1 + 7) // 8)
    cvec = jnp.zeros((mod_rows, d), F32).at[:nb].set(c).at[nb].set(c_ctx)

    for l in range(depth):
        wl = _layer_weights(l, p)
        last = l == depth - 1
        mod = _modulation(cvec, ada_w[l], ada_b[l])
        mod_lat = mod[:nb].reshape(nb, N_MOD, d).transpose(1, 0, 2)
        mod_ctx = jnp.broadcast_to(mod[nb].reshape(N_MOD, 1, d), (N_MOD, nb, d))
        modv = jnp.stack([mod_ctx, mod_lat])

        za, zb, zc, zd = _in_proj(xs, norm1_g[l], modv, wl['w_in'], n_ctx_tiles)

        r, k, v, g, av, w_dec, kd, bv = _rwkv_features(za, wl['mu'], wl['gup'], wl['wup'], wl['aup'],
                                                       wl['rwkv_vecs'], hs, n_ctx_tiles)
        ex = lambda t: _rwkv_expand_k(t, v_lo)
        ya = _rwkv_scan(ex(r), ex(av), ex(w_dec), ex(kd), ex(bv), _rwkv_expand_v(v, v_lo), l_ctx)
        ya = _rwkv_collapse_v(ya, nb, v_lo)
        ob = _retention(jnp.transpose(zb, (1, 0, 2)), ret_tabs, l_ctx)
        ob = jnp.transpose(ob, (0, 2, 1, 3))
        hc = _lru(zc, wl['cw'], wl['wa'], wl['wx'], wl['lru_vecs'], l_ctx)
        yd = _from_colmajor(_s5(_to_colmajor(zd, l_ctx), *wl['s5'], l_ctx), l_ctx)

        t0 = n_ctx_tiles if last else 0
        x1, h2 = _out_proj(xs, modv, ya, (r, k, v, g), ob, zb, hc, zc, yd, zd, wl['out_vecs'], hs,
                           wl['gluw'], wl['glub'], wl['wo'], wl['n2'], n_ctx_tiles, t0)
        t0p = n_ctx_tok_tiles if last else 0
        e1, thr, s2, w2 = _peer_route(h2, wl['wq'], wl['keys'], t0p)
        x2 = _peer_ffn(h2, wl['u'], wl['vt'], e1, thr, s2, w2, x1.reshape(lt * nb, d), modv,
                       n_ctx_tok_tiles, t0p)
        xs = x2.reshape(lt, nb, d)

    out = _final_norm(xs[l_ctx:].reshape(l_lat * nb, d), final_norm_g)
    return jnp.transpose(out.reshape(l_lat, nb, d), (1, 0, 2))
```

```python
import functools
import math

import jax
import jax.numpy as jnp
import numpy as np
from jax import lax
from jax.experimental import pallas as pl
from jax.experimental.pallas import tpu as pltpu

F32 = jnp.float32
BF16 = jnp.bfloat16

NORM_EPS = 1e-6
N_MOD = 6
HEAD_DIM = 64
N_HEADS = 4
GROUP_W = N_HEADS * HEAD_DIM
GRID_W = 64

RWKV_GATE_LORA = 64
RWKV_LORA = 32
RWKV_GN_EPS = 64e-5
RWKV_COLS = 3 * GROUP_W + RWKV_GATE_LORA + 4 * RWKV_LORA
ZA_W = 3 * GROUP_W + 3 * 128

RET_CHUNK = 128
RET_GN_EPS = 1e-5
RET_FWD_OFFSET = 5.0
RET_BWD_OFFSET = 5.5
ROPE_BASE = 10000.0
RET_COLS = 4 * GROUP_W

LRU_CONV_W = 4
LRU_C = 8.0
LRU_COLS = 2 * GROUP_W
LRU_CHUNK = 128

S5_CH = 16
S5_GROUPS = GROUP_W // S5_CH
S5_STATE = 64
S5_N = S5_GROUPS * S5_STATE
S5_CHUNK = 128

PEER_HEADS = 8
PEER_NKEYS = 128
PEER_DKEY = 128
PEER_TOPK = 16
PEER_TOK_TILE = 512
PEER_EXP_BLOCK = 1024

ROW_TILE_T = 32
RWKV_CHUNK = 32

VMEM_LIMIT_BYTES = 56 * 1024 * 1024

NEG_INF = float("-inf")


def _cparams(n_axes):
    return pltpu.CompilerParams(dimension_semantics=("arbitrary",) * n_axes,
                                vmem_limit_bytes=VMEM_LIMIT_BYTES)


def _sigmoid(x):
    return 1.0 / (1.0 + jnp.exp(-x))


def _gelu_tanh(x):
    return 0.5 * x * (1.0 + jnp.tanh(math.sqrt(2.0 / math.pi) * (x + 0.044715 * (x * x * x))))


def _softplus(x):
    return jnp.maximum(x, 0.0) + jnp.log(1.0 + jnp.exp(-jnp.abs(x)))


def _dot(a, b):
    return jnp.dot(a, b, preferred_element_type=F32)


def _dot_exact(a, b):
    return jnp.dot(a, b, preferred_element_type=F32, precision=lax.Precision.HIGHEST)


def _head_sum_matrix():
    idx = np.arange(GROUP_W) // HEAD_DIM
    return jnp.asarray((idx[:, None] == idx[None, :]).astype(np.float32))


def _mod_body(c_ref, w_ref, b_ref, o_ref):
    c = c_ref[...]
    s = (c * _sigmoid(c)).astype(BF16)
    o_ref[...] = _dot(s, w_ref[...].astype(BF16)) + b_ref[...]


def _modulation(cvec, w, b):
    rows, d = cvec.shape
    n = w.shape[1]
    tn = 512
    return pl.pallas_call(
        _mod_body,
        grid=(n // tn,),
        in_specs=[pl.BlockSpec((rows, d), lambda j: (0, 0)),
                  pl.BlockSpec((d, tn), lambda j: (0, j)),
                  pl.BlockSpec((1, tn), lambda j: (0, j))],
        out_specs=pl.BlockSpec((rows, tn), lambda j: (0, j)),
        out_shape=jax.ShapeDtypeStruct((rows, n), F32),
        compiler_params=_cparams(1),
        name="adaln_mod",
    )(cvec, w, b.reshape(1, n))


def _in_body(x_ref, g_ref, mod_ref, w_ref, za_ref, zb_ref, zc_ref, zd_ref):
    tt, nb, d = x_ref.shape
    x = x_ref[...]
    ms = jnp.mean(x * x, axis=-1, keepdims=True)
    y = x * lax.rsqrt(ms + NORM_EPS) * g_ref[...]
    h = y * (1.0 + mod_ref[1]) + mod_ref[0]
    hb = h.reshape(tt * nb, d).astype(BF16)
    c0 = 0
    for ref in (za_ref, zb_ref, zc_ref, zd_ref):
        w = ref.shape[-1]
        ref[...] = _dot(hb, w_ref[:, c0:c0 + w]).reshape(tt, nb, w)
        c0 += w


def _in_proj(xs, g, modv, w_p, n_ctx_tiles):
    lt, nb, d = xs.shape
    tt = ROW_TILE_T
    widths = (ZA_W, RET_COLS, LRU_COLS, GROUP_W)
    seg = lambda i: jnp.where(i >= n_ctx_tiles, 1, 0)
    return pl.pallas_call(
        _in_body,
        grid=(lt // tt,),
        in_specs=[pl.BlockSpec((tt, nb, d), lambda i: (i, 0, 0)),
                  pl.BlockSpec((1, d), lambda i: (0, 0)),
                  pl.BlockSpec((None, N_MOD, nb, d), lambda i: (seg(i), 0, 0, 0)),
                  pl.BlockSpec(w_p.shape, lambda i: (0, 0))],
        out_specs=[pl.BlockSpec((tt, nb, w), lambda i: (i, 0, 0)) for w in widths],
        out_shape=[jax.ShapeDtypeStruct((lt, nb, w), F32) for w in widths],
        compiler_params=_cparams(1),
        name="in_proj",
    )(xs, g.reshape(1, d), modv, w_p)


def _rwkv_feat_body(n_ctx_tiles, n_tiles,
                    z_ref, zp_ref, zn_ref, mu_ref, gup_ref, wup_ref, aup_ref, vec_ref, hs_ref,
                    r_ref, k_ref, v_ref, g_ref, av_ref, w_ref, kd_ref, bv_ref):
    i = pl.program_id(0)
    tt, nb, zw = z_ref.shape
    z0 = z_ref[...]
    has_prev = jnp.logical_and(i != 0, i != n_ctx_tiles)
    has_next = jnp.logical_and(i != n_ctx_tiles - 1, i != n_tiles - 1)
    hp = jnp.where(has_prev, zp_ref[...], 0.0)
    hn = jnp.where(has_next, zn_ref[...], 0.0)
    zp = jnp.concatenate([hp, z0[:-1]], axis=0)
    zn = jnp.concatenate([z0[1:], hn], axis=0)
    z = z0 + mu_ref[0] * (zp - z0) + mu_ref[1] * (zn - z0)
    z = z.reshape(tt * nb, zw)
    gw = GROUP_W
    r = z[:, 0:gw]
    k = z[:, gw:2 * gw]
    v = z[:, 2 * gw:3 * gw]
    g = _dot(_sigmoid(z[:, 3 * gw:3 * gw + 128]).astype(BF16), gup_ref[...])
    w_low = jnp.tanh(z[:, 3 * gw + 128:3 * gw + 256]).astype(BF16)
    a_low = z[:, 3 * gw + 256:3 * gw + 384].astype(BF16)
    k_k = vec_ref[0:1, :]
    k_a = vec_ref[1:2, :]
    kk = k * k_k
    ss = _dot_exact(kk * kk, hs_ref[...])
    kk = kk / jnp.maximum(jnp.sqrt(ss), 1e-12)
    shp = (tt, nb, gw)
    r_ref[...] = r.reshape(shp)
    k_ref[...] = k.reshape(shp)
    v_ref[...] = v.reshape(shp)
    g_ref[...] = g.reshape(shp)
    av_ref[...] = (-kk).reshape(shp)
    for d in range(2):
        w0 = vec_ref[2 + d:3 + d, :]
        a0 = vec_ref[4 + d:5 + d, :]
        w_log = -_softplus(-(w0 + _dot(w_low, wup_ref[d]))) - 0.5
        w_ref[d] = jnp.exp(-jnp.exp(w_log)).reshape(shp)
        a = _sigmoid(a0 + _dot(a_low, aup_ref[d]))
        kd_ref[d] = (k * (1.0 + (a - 1.0) * k_a)).reshape(shp)
        bv_ref[d] = (kk * a).reshape(shp)


def _rwkv_features(za, mu_p, gup_p, wup_p, aup_p, vecs, hs, n_ctx_tiles):
    lt, nb, zw = za.shape
    tt = ROW_TILE_T
    n_tiles = lt // tt
    gw = GROUP_W
    one = lambda shape: jax.ShapeDtypeStruct(shape, F32)
    full = lambda a: pl.BlockSpec(a.shape, lambda i: (0,) * a.ndim)
    o1 = pl.BlockSpec((tt, nb, gw), lambda i: (i, 0, 0))
    o2 = pl.BlockSpec((2, tt, nb, gw), lambda i: (0, i, 0, 0))
    return pl.pallas_call(
        functools.partial(_rwkv_feat_body, n_ctx_tiles, n_tiles),
        grid=(n_tiles,),
        in_specs=[pl.BlockSpec((tt, nb, zw), lambda i: (i, 0, 0)),
                  pl.BlockSpec((1, nb, zw), lambda i: (jnp.maximum(i * tt - 1, 0), 0, 0)),
                  pl.BlockSpec((1, nb, zw), lambda i: (jnp.minimum((i + 1) * tt, lt - 1), 0, 0)),
                  full(mu_p), full(gup_p), full(wup_p), full(aup_p), full(vecs), full(hs)],
        out_specs=[o1, o1, o1, o1, o1, o2, o2, o2],
        out_shape=[one((lt, nb, gw))] * 5 + [one((2, lt, nb, gw))] * 3,
        compiler_params=_cparams(1),
        name="rwkv_features",
    )(za, za, za, mu_p, gup_p, wup_p, aup_p, vecs, hs)


def _rwkv_scan_body(r_ref, a_ref, w_ref, kd_ref, b_ref, v_ref, y_ref, s_ref):
    d = pl.program_id(0)
    j = pl.program_id(1)
    tc = r_ref.shape[0]
    n_k = HEAD_DIM
    n_acc = 4

    @pl.when(j == 0)
    def _():
        s_ref[...] = jnp.zeros_like(s_ref)

    def step(i, carry):
        t = jnp.where(d == 0, i, tc - 1 - i)
        vv = v_ref[t]
        acc = [None] * n_acc
        for kk in range(n_k):
            term = s_ref[kk] * a_ref[t, pl.ds(kk, 1), :]
            acc[kk % n_acc] = term if acc[kk % n_acc] is None else acc[kk % n_acc] + term
        sa = (acc[0] + acc[1]) + (acc[2] + acc[3])
        acc = [None] * n_acc
        for kk in range(n_k):
            s_new = (s_ref[kk] * w_ref[t, pl.ds(kk, 1), :] + sa * b_ref[t, pl.ds(kk, 1), :]
                     + vv * kd_ref[t, pl.ds(kk, 1), :])
            s_ref[kk] = s_new
            term = s_new * r_ref[t, pl.ds(kk, 1), :]
            acc[kk % n_acc] = term if acc[kk % n_acc] is None else acc[kk % n_acc] + term
        y_ref[t] = (acc[0] + acc[1]) + (acc[2] + acc[3])
        return carry

    lax.fori_loop(0, tc, step, 0)


def _dir_chunk(d, j, n_ctx, n_all):
    bwd = jnp.where(j < n_ctx, n_ctx - 1 - j, n_all - 1 - (j - n_ctx))
    return jnp.where(d == 0, j, bwd)


def _rwkv_scan(r_e, a_e, w_e, kd_e, b_e, v_e, l_ctx):
    lt, n_k, lanes = r_e.shape
    v_hi = v_e.shape[1]
    tc = RWKV_CHUNK
    n_all, n_ctx = lt // tc, l_ctx // tc
    cm = lambda d, j: _dir_chunk(d, j, n_ctx, n_all)
    shared = pl.BlockSpec((tc, n_k, lanes), lambda d, j: (cm(d, j), 0, 0))
    per_dir = pl.BlockSpec((None, tc, n_k, lanes), lambda d, j: (d, cm(d, j), 0, 0))
    return pl.pallas_call(
        _rwkv_scan_body,
        grid=(2, n_all),
        in_specs=[shared, shared, per_dir, per_dir, per_dir,
                  pl.BlockSpec((tc, v_hi, lanes), lambda d, j: (cm(d, j), 0, 0))],
        out_specs=pl.BlockSpec((None, tc, v_hi, lanes), lambda d, j: (d, cm(d, j), 0, 0)),
        out_shape=jax.ShapeDtypeStruct((2, lt, v_hi, lanes), F32),
        scratch_shapes=[pltpu.VMEM((n_k, v_hi, lanes), F32)],
        compiler_params=_cparams(2),
        name="rwkv_scan",
    )(r_e, a_e, w_e, kd_e, b_e, v_e)


def _rwkv_expand_k(x, v_lo):
    lead = x.shape[:-2]
    nb = x.shape[-2]
    x = x.reshape(lead + (nb, N_HEADS, HEAD_DIM))
    x = jnp.moveaxis(x, -1, -3).reshape(lead + (HEAD_DIM, nb * N_HEADS))
    return jnp.tile(x, (1,) * (x.ndim - 1) + (v_lo,))


def _rwkv_expand_v(x, v_lo):
    lt, nb, _ = x.shape
    v_hi = HEAD_DIM // v_lo
    x = x.reshape(lt, nb, N_HEADS, v_hi, v_lo)
    return jnp.transpose(x, (0, 3, 4, 1, 2)).reshape(lt, v_hi, v_lo * nb * N_HEADS)


def _rwkv_collapse_v(y, nb, v_lo):
    _, lt, v_hi, _ = y.shape
    y = y.reshape(2, lt, v_hi, v_lo, nb, N_HEADS)
    return jnp.transpose(y, (0, 1, 4, 5, 2, 3)).reshape(2, lt, nb, GROUP_W)


def _ret_body(z_ref, cos_ref, sin_ref, dec_ref, rd_ref, wd_ref, cd_ref, hm_ref, o_ref, s_ref):
    j = pl.program_id(2)
    gw = GROUP_W
    half = gw // 2

    @pl.when(j == 0)
    def _():
        s_ref[...] = jnp.zeros_like(s_ref)

    cos = cos_ref[...]
    sin = sin_ref[...]

    def rope(x):
        x1, x2 = x[:, :half], x[:, half:]
        return jnp.concatenate([x1 * cos - x2 * sin, x1 * sin + x2 * cos], axis=-1)

    q = rope(z_ref[:, 0:gw]) * (HEAD_DIM ** -0.5)
    k = rope(z_ref[:, gw:2 * gw])
    v = z_ref[:, 2 * gw:3 * gw]
    vb = v.astype(BF16)
    k_t = k.T
    k_tb = k_t.astype(BF16)
    inter = _dot(q.astype(BF16), s_ref[...].astype(BF16)) * rd_ref[...]
    intra = jnp.zeros_like(inter)
    for h in range(N_HEADS):
        qm = (q * hm_ref[h:h + 1, :]).astype(BF16)
        att = _dot(qm, k_tb) * dec_ref[h]
        vm = (v * hm_ref[N_HEADS + h:N_HEADS + h + 1, :]).astype(BF16)
        intra = intra + _dot(att.astype(BF16), vm)
    o_ref[...] = intra + inter
    kv = _dot((k_t * wd_ref[...]).astype(BF16), vb)
    s_ref[...] = s_ref[...] * cd_ref[0] + kv * cd_ref[1]


def _retention(zb_bm, tabs, l_ctx):
    nb, lt, _ = zb_bm.shape
    c = RET_CHUNK
    n_all, n_ctx = lt // c, l_ctx // c
    gw = GROUP_W
    cos, sin, dec, rd, wd, cd, hm = tabs
    cm = lambda d, j: _dir_chunk(d, j, n_ctx, n_all)
    return pl.pallas_call(
        _ret_body,
        grid=(nb, 2, n_all),
        in_specs=[pl.BlockSpec((None, c, RET_COLS), lambda b, d, j: (b, cm(d, j), 0)),
                  pl.BlockSpec((c, gw // 2), lambda b, d, j: (cm(d, j), 0)),
                  pl.BlockSpec((c, gw // 2), lambda b, d, j: (cm(d, j), 0)),
                  pl.BlockSpec((None, N_HEADS, c, c), lambda b, d, j: (d, 0, 0, 0)),
                  pl.BlockSpec((None, c, gw), lambda b, d, j: (d, 0, 0)),
                  pl.BlockSpec((None, gw, c), lambda b, d, j: (d, 0, 0)),
                  pl.BlockSpec((None, 2, gw, gw), lambda b, d, j: (d, 0, 0, 0)),
                  pl.BlockSpec((2 * N_HEADS, gw), lambda b, d, j: (0, 0))],
        out_specs=pl.BlockSpec((None, None, c, gw), lambda b, d, j: (d, b, cm(d, j), 0)),
        out_shape=jax.ShapeDtypeStruct((2, nb, lt, gw), F32),
        scratch_shapes=[pltpu.VMEM((gw, gw), F32)],
        compiler_params=_cparams(3),
        name="retention",
    )(zb_bm, cos, sin, dec, rd, wd, cd, hm)


def _ret_perm():
    new = np.arange(GROUP_W)
    half_id, rem = new // 128, new % 128
    h, i = rem // 32, rem % 32
    return h * HEAD_DIM + half_id * 32 + i


def _ret_tables(lt):
    c = RET_CHUNK
    half = HEAD_DIM // 2
    freqs = ROPE_BASE ** (-jnp.arange(half, dtype=F32) / half)
    ang = jnp.arange(lt, dtype=F32)[:, None] * freqs[None]
    cos = jnp.tile(jnp.cos(ang), (1, N_HEADS))
    sin = jnp.tile(jnp.sin(ang), (1, N_HEADS))
    idx = jnp.arange(c, dtype=F32)
    perm = _ret_perm()
    head_of_qk = jnp.asarray(perm // HEAD_DIM)
    head_of_v = jnp.arange(GROUP_W) // HEAD_DIM
    dec, rd, wd, cd = [], [], [], []
    for offset, rev in ((RET_FWD_OFFSET, False), (RET_BWD_OFFSET, True)):
        lg = jnp.log1p(-jnp.exp2(-(offset + jnp.arange(N_HEADS, dtype=F32))))
        diff = idx[:, None] - idx[None, :]
        if rev:
            mask, dist = diff < 0, -diff
            read_pow, write_pow = c - idx, idx
        else:
            mask, dist = diff >= 0, diff
            read_pow, write_pow = idx + 1.0, c - 1.0 - idx
        dec.append(jnp.where(mask[None], jnp.exp(lg[:, None, None] * jnp.where(mask, dist, 0.0)[None]), 0.0))
        rd.append(jnp.exp(lg[head_of_v][None, :] * read_pow[:, None]))
        wd.append(jnp.exp(lg[head_of_qk][:, None] * write_pow[None, :]))
        same = (head_of_qk[:, None] == head_of_v[None, :]).astype(F32)
        cd.append(jnp.stack([same * jnp.exp(lg * c)[head_of_v][None, :], same]))
    hm_q = (head_of_qk[None, :] == jnp.arange(N_HEADS)[:, None]).astype(F32)
    hm_v = (head_of_v[None, :] == jnp.arange(N_HEADS)[:, None]).astype(F32)
    return cos, sin, jnp.stack(dec), jnp.stack(rd), jnp.stack(wd), jnp.stack(cd), jnp.concatenate([hm_q, hm_v])


def _lru_body(n_ctx, n_all, x_ref, xp_ref, xn_ref, cw_ref, wa_ref, wx_ref, vec_ref, o_ref,
              a_s, b_s, h_s):
    d = pl.program_id(0)
    j = pl.program_id(1)
    tc, nb, gw = x_ref.shape
    ch = _dir_chunk(d, j, n_ctx, n_all)

    @pl.when(j == 0)
    def _():
        h_s[...] = jnp.zeros_like(h_s)

    has_prev = jnp.logical_and(ch != 0, ch != n_ctx)
    has_next = jnp.logical_and(ch != n_ctx - 1, ch != n_all - 1)
    xe = jnp.concatenate([jnp.where(has_prev, xp_ref[...], 0.0), x_ref[...],
                          jnp.where(has_next, xn_ref[...], 0.0)], axis=0)
    xc = cw_ref[LRU_CONV_W:LRU_CONV_W + 1, :]
    for tap in range(LRU_CONV_W):
        xc = xc + cw_ref[tap:tap + 1, :] * xe[tap:tap + tc]
    xc = xc.reshape(tc * nb, gw)
    xb = xc.astype(BF16)
    r = _sigmoid(_dot(xb, wa_ref[...]) + vec_ref[0:1, :])
    gi = _sigmoid(_dot(xb, wx_ref[...]) + vec_ref[1:2, :])
    log_a = -LRU_C * r * _softplus(-vec_ref[2:3, :])
    th = jnp.tanh(log_a)
    one_minus_a2 = 2.0 * th / (th - 1.0)
    a_s[...] = jnp.exp(log_a).reshape(tc, nb, gw)
    b_s[...] = (jnp.sqrt(one_minus_a2) * (gi * xc)).reshape(tc, nb, gw)

    def step(i, h):
        t = jnp.where(d == 0, i, tc - 1 - i)
        h = a_s[t] * h + b_s[t]
        o_ref[t] = h
        return h

    h_s[...] = lax.fori_loop(0, tc, step, h_s[...])


def _lru(zc, cw, wa_bd, wx_bd, vecs, l_ctx):
    lt, nb, _ = zc.shape
    gw = GROUP_W
    tc = LRU_CHUNK
    n_all, n_ctx = lt // tc, l_ctx // tc
    cm = lambda d, j: _dir_chunk(d, j, n_ctx, n_all)
    return pl.pallas_call(
        functools.partial(_lru_body, n_ctx, n_all),
        grid=(2, n_all),
        in_specs=[pl.BlockSpec((tc, nb, gw), lambda d, j: (cm(d, j), 0, 0)),
                  pl.BlockSpec((2, nb, gw), lambda d, j: (jnp.maximum(cm(d, j) * (tc // 2) - 1, 0), 0, 0)),
                  pl.BlockSpec((1, nb, gw), lambda d, j: (jnp.minimum((cm(d, j) + 1) * tc, lt - 1), 0, 0)),
                  pl.BlockSpec(cw.shape, lambda d, j: (0, 0)),
                  pl.BlockSpec((None, gw, gw), lambda d, j: (d, 0, 0)),
                  pl.BlockSpec((None, gw, gw), lambda d, j: (d, 0, 0)),
                  pl.BlockSpec((None, 3, gw), lambda d, j: (d, 0, 0))],
        out_specs=pl.BlockSpec((None, tc, nb, gw), lambda d, j: (d, cm(d, j), 0, 0)),
        out_shape=jax.ShapeDtypeStruct((2, lt, nb, gw), F32),
        scratch_shapes=[pltpu.VMEM((tc, nb, gw), F32), pltpu.VMEM((tc, nb, gw), F32),
                        pltpu.VMEM((nb, gw), F32)],
        compiler_params=_cparams(2),
        name="rglru",
    )(zc, zc, zc, cw, wa_bd, wx_bd, vecs)


def _block_diag(w):
    nblk, n = w.shape[-3], w.shape[-1]
    eye = jnp.eye(nblk, dtype=w.dtype)
    out = w[..., :, :, None, :] * eye[:, None, :, None]
    return out.reshape(w.shape[:-3] + (nblk * n, nblk * n))


def _s5_body(u_ref, bm_ref, lam_ref, cm_ref, y_ref, h_s, st_s):
    d = pl.program_id(0)
    j = pl.program_id(1)
    tc, nb, gw = u_ref.shape
    n = S5_N

    @pl.when(j == 0)
    def _():
        st_s[...] = jnp.zeros_like(st_s)

    u = u_ref[...].reshape(tc * nb, gw).astype(BF16)
    h_s[...] = _dot(u, bm_ref[...]).reshape(tc, nb, 2 * n)
    lr = lam_ref[0:1, :]
    li = lam_ref[1:2, :]

    def step(i, carry):
        hr, hi = carry
        t = jnp.where(d == 0, i, tc - 1 - i)
        bu = h_s[t]
        nr = lr * hr - li * hi + bu[:, :n]
        ni = lr * hi + li * hr + bu[:, n:]
        h_s[t] = jnp.concatenate([nr, ni], axis=-1)
        return nr, ni

    hr, hi = lax.fori_loop(0, tc, step, (st_s[0], st_s[1]))
    st_s[0] = hr
    st_s[1] = hi
    hh = h_s[...].reshape(tc * nb, 2 * n).astype(BF16)
    y_ref[...] = _dot(hh, cm_ref[...]).reshape(tc, nb, gw)


def _s5(u, bmat, lam, cmat, l_ctx):
    lt, nb, gw = u.shape
    tc = S5_CHUNK
    n_all, n_ctx = lt // tc, l_ctx // tc
    cm = lambda d, j: _dir_chunk(d, j, n_ctx, n_all)
    return pl.pallas_call(
        _s5_body,
        grid=(2, n_all),
        in_specs=[pl.BlockSpec((tc, nb, gw), lambda d, j: (cm(d, j), 0, 0)),
                  pl.BlockSpec((None, gw, 2 * S5_N), lambda d, j: (d, 0, 0)),
                  pl.BlockSpec((None, 2, S5_N), lambda d, j: (d, 0, 0)),
                  pl.BlockSpec((None, 2 * S5_N, gw), lambda d, j: (d, 0, 0))],
        out_specs=pl.BlockSpec((None, tc, nb, gw), lambda d, j: (d, cm(d, j), 0, 0)),
        out_shape=jax.ShapeDtypeStruct((2, lt, nb, gw), F32),
        scratch_shapes=[pltpu.VMEM((tc, nb, 2 * S5_N), F32), pltpu.VMEM((2, nb, S5_N), F32)],
        compiler_params=_cparams(2),
        name="s5",
    )(u, bmat, lam, cmat)


def _s5_params(a_re, a_im, log_dt, b_re, b_im, c_re, c_im):
    dt = jnp.exp(log_dt)[..., None]
    er = jnp.exp(a_re * dt)
    lbr, lbi = er * jnp.cos(a_im * dt), er * jnp.sin(a_im * dt)
    den = a_re * a_re + a_im * a_im
    nr, ni = lbr - 1.0, lbi
    fr = (nr * a_re + ni * a_im) / den
    fi = (ni * a_re - nr * a_im) / den
    bbr = fr[..., None] * b_re - fi[..., None] * b_im
    bbi = fr[..., None] * b_im + fi[..., None] * b_re
    eye = jnp.eye(S5_GROUPS, dtype=F32)

    def in_map(bb):
        m = jnp.einsum('dgpc,gh->dgchp', bb, eye)
        return m.reshape(2, GROUP_W, S5_N)

    def out_map(cc):
        m = jnp.einsum('dgcp,gh->dgphc', cc, eye)
        return m.reshape(2, S5_N, GROUP_W)

    bmat = jnp.concatenate([in_map(bbr), in_map(bbi)], axis=-1).astype(BF16)
    cmat = jnp.concatenate([out_map(c_re), -out_map(c_im)], axis=-2).astype(BF16)
    lam = jnp.stack([lbr.reshape(2, S5_N), lbi.reshape(2, S5_N)], axis=1)
    return bmat, lam, cmat


def _to_colmajor(z, l_ctx):
    lt, nb, ch = z.shape
    rows = (lt - l_ctx) // GRID_W
    lat = z[l_ctx:].reshape(rows, GRID_W, nb, ch).transpose(1, 0, 2, 3).reshape(lt - l_ctx, nb, ch)
    return jnp.concatenate([z[:l_ctx], lat], axis=0)


def _from_colmajor(y, l_ctx):
    lt = y.shape[-3]
    nb, ch = y.shape[-2:]
    rows = (lt - l_ctx) // GRID_W
    lead = y.shape[:-3]
    lat = y[..., l_ctx:, :, :].reshape(lead + (GRID_W, rows, nb, ch))
    lat = jnp.swapaxes(lat, -4, -3).reshape(lead + (lt - l_ctx, nb, ch))
    return jnp.concatenate([y[..., :l_ctx, :, :], lat], axis=-3)


def _out_body(x_ref, mod_ref, ya_ref, r_ref, k_ref, v_ref, g_ref, ob_ref, gb_ref, hc_ref, gc_ref,
              yd_ref, ud_ref, vec_ref, hs_ref, gluw_ref, glub_ref, wo_ref, n2_ref, x1_ref, h2_ref):
    tt, nb, d = x_ref.shape
    gw = GROUP_W
    rows = tt * nb
    hs = hs_ref[...]
    two = lambda ref: (ref[0] + ref[1]).reshape(rows, gw)
    flat = lambda ref: ref[...].reshape(rows, gw)

    def head_norm(y, eps):
        mu = _dot_exact(y, hs) * (1.0 / HEAD_DIM)
        dlt = y - mu
        var = _dot_exact(dlt * dlt, hs) * (1.0 / HEAD_DIM)
        return dlt * lax.rsqrt(var + eps)

    r_k, ln_g, ln_b = vec_ref[0:1, :], vec_ref[1:2, :], vec_ref[2:3, :]
    r, k, v = flat(r_ref), flat(k_ref), flat(v_ref)
    yn = head_norm(two(ya_ref), RWKV_GN_EPS) * ln_g + ln_b
    bonus = _dot_exact(r * k * r_k, hs) * v
    mix_a = (yn + bonus) * flat(g_ref)
    gb = flat(gb_ref)
    mix_b = gb * _sigmoid(gb) * (head_norm(two(ob_ref), RET_GN_EPS) * vec_ref[3:4, :])
    mix_c = _gelu_tanh(flat(gc_ref)) * two(hc_ref)
    yd = vec_ref[4:5, :] * flat(ud_ref) + two(yd_ref)
    glu = _dot(_gelu_tanh(yd).astype(BF16), gluw_ref[...]) + glub_ref[...]
    mix_d = glu[:, :gw] * _sigmoid(glu[:, gw:])

    mix = _dot(mix_a.astype(BF16), wo_ref[0:gw, :])
    mix = mix + _dot(mix_b.astype(BF16), wo_ref[gw:2 * gw, :])
    mix = mix + _dot(mix_c.astype(BF16), wo_ref[2 * gw:3 * gw, :])
    mix = mix + _dot(mix_d.astype(BF16), wo_ref[3 * gw:4 * gw, :])
    x1 = x_ref[...] + mod_ref[2] * mix.reshape(tt, nb, d)
    x1_ref[...] = x1
    ms = jnp.mean(x1 * x1, axis=-1, keepdims=True)
    h2 = (x1 * lax.rsqrt(ms + NORM_EPS) * n2_ref[...]) * (1.0 + mod_ref[4]) + mod_ref[3]
    h2_ref[...] = h2.reshape(rows, d).astype(BF16)


def _out_proj(xs, modv, ya, feats, ob, zb, hc, zc, yd, zd, vecs, hs, gluw, glub, wo, n2, n_ctx_tiles, t0_tile):
    lt, nb, d = xs.shape
    tt = ROW_TILE_T
    gw = GROUP_W
    n_tiles = lt // tt - t0_tile
    seg = lambda i: jnp.where(i + t0_tile >= n_ctx_tiles, 1, 0)
    col = lambda c: pl.BlockSpec((tt, nb, gw), lambda i: (i + t0_tile, 0, c))
    two = pl.BlockSpec((2, tt, nb, gw), lambda i: (0, i + t0_tile, 0, 0))
    full = lambda a: pl.BlockSpec(a.shape, lambda i: (0,) * a.ndim)
    r, k, v, g = feats
    return pl.pallas_call(
        _out_body,
        grid=(n_tiles,),
        in_specs=[pl.BlockSpec((tt, nb, d), lambda i: (i + t0_tile, 0, 0)),
                  pl.BlockSpec((None, N_MOD, nb, d), lambda i: (seg(i), 0, 0, 0)),
                  two, col(0), col(0), col(0), col(0),
                  two, col(3), two, col(1), two, col(0),
                  full(vecs), full(hs), full(gluw), full(glub), full(wo), full(n2)],
        out_specs=[pl.BlockSpec((tt, nb, d), lambda i: (i + t0_tile, 0, 0)),
                   pl.BlockSpec((tt * nb, d), lambda i: (i + t0_tile, 0))],
        out_shape=[jax.ShapeDtypeStruct((lt, nb, d), F32), jax.ShapeDtypeStruct((lt * nb, d), BF16)],
        compiler_params=_cparams(1),
        name="out_proj",
    )(xs, modv, ya, r, k, v, g, ob, zb, hc, zc, yd, zd, vecs, hs, gluw, glub, wo, n2)


PEER_NO_RANK = 127.0


def _top_values(s, n, with_rank=False):
    vals, cur = [], s
    rank = jnp.full(s.shape, PEER_NO_RANK, F32) if with_rank else None
    for k in range(n):
        m = jnp.max(cur, axis=0, keepdims=True)
        vals.append(m)
        hit = cur == m
        if with_rank:
            rank = jnp.where(hit, float(k), rank)
        cur = jnp.where(hit, NEG_INF, cur)
    return (vals, rank) if with_rank else vals


def _peer_route_body(h_ref, wq_ref, keys_ref, e1_ref, cnt_ref, r2_ref, w2_ref, pack_ref):
    hb = h_ref[...]
    for h in range(PEER_HEADS):
        st = []
        for p in range(2):
            c0 = (2 * h + p) * PEER_DKEY
            q = _dot(hb, wq_ref[:, c0:c0 + PEER_DKEY]).astype(BF16)
            st.append(lax.dot_general(keys_ref[2 * h + p], q, (((1,), (1,)), ((), ())),
                                      preferred_element_type=F32))
        s1, s2 = st
        v1 = _top_values(s1, PEER_TOPK + 1)
        v2, rank2 = _top_values(s2, PEER_TOPK + 1, with_rank=True)
        v2m = jnp.concatenate(v2[:PEER_TOPK], axis=0)
        pad = jnp.full((6, s1.shape[1]), NEG_INF, F32)
        cand = jnp.concatenate([v1[a] + v2m for a in range(PEER_TOPK)]
                               + [v1[PEER_TOPK] + v2[0], v1[0] + v2[PEER_TOPK], pad], axis=0)
        top = _top_values(cand, PEER_TOPK + 1)
        tau = 0.5 * (top[PEER_TOPK - 1] + top[PEER_TOPK])
        cmax = top[0]
        z = jnp.sum(jnp.where(cand >= tau, jnp.exp(cand - cmax), 0.0), axis=0, keepdims=True)
        thr = tau - s1
        cnt = jnp.zeros_like(s1)
        for k in range(PEER_TOPK):
            cnt = cnt + jnp.where(v2[k] >= thr, 1.0, 0.0)
        e1_ref[:, h, :] = jnp.exp(s1 - v1[0])
        cnt_ref[:, h, :] = cnt
        pack_ref[0] = rank2.astype(BF16)
        pack_ref[1] = (jnp.exp(s2 - v2[0]) / z).astype(BF16)
        words = pack_ref.bitcast(jnp.uint32)
        r2_ref[h] = words[0]
        w2_ref[h] = words[1]


def _peer_route(h2, wq, keys, t0_tile):
    t_all, d = h2.shape
    tt = PEER_TOK_TILE
    nk = PEER_NKEYS
    by_i = pl.BlockSpec((nk, PEER_HEADS, tt), lambda i: (0, 0, i + t0_tile))
    by_h = pl.BlockSpec((PEER_HEADS, nk // 2, tt), lambda i: (0, 0, i + t0_tile))
    return pl.pallas_call(
        _peer_route_body,
        grid=(t_all // tt - t0_tile,),
        in_specs=[pl.BlockSpec((tt, d), lambda i: (i + t0_tile, 0)),
                  pl.BlockSpec(wq.shape, lambda i: (0, 0)),
                  pl.BlockSpec(keys.shape, lambda i: (0, 0, 0))],
        out_specs=[by_i, by_i, by_h, by_h],
        out_shape=[jax.ShapeDtypeStruct((nk, PEER_HEADS, t_all), F32)] * 2
                  + [jax.ShapeDtypeStruct((PEER_HEADS, nk // 2, t_all), jnp.uint32)] * 2,
        scratch_shapes=[pltpu.VMEM((2, nk, tt), BF16)],
        compiler_params=_cparams(1),
        name="peer_route",
    )(h2, wq, keys)


PEER_GATE_ROWS = 32
LANES = 128


BF16_ROWS = 16
SUBLANES = 8


def _peer_gate_block(p_ref, e1_ref, cnt_ref, r2_ref, w2_ref, bc_ref):
    nk = PEER_NKEYS
    n_sub = p_ref.shape[0] // nk
    rc = PEER_GATE_ROWS
    n_pv = rc // BF16_ROWS
    for c0 in range(0, p_ref.shape[1], LANES):
        cols = slice(c0, c0 + LANES)
        for ii in range(n_sub):
            for h in range(PEER_HEADS):
                bc_ref[0, ii, h] = jnp.broadcast_to(cnt_ref[ii, h:h + 1, cols], (BF16_ROWS, LANES)).astype(BF16)
                bc_ref[1, ii, h] = jnp.broadcast_to(e1_ref[ii, h:h + 1, cols], (BF16_ROWS, LANES)).astype(BF16)

        def chunk(q, carry, cols=cols):
            r0 = pl.multiple_of(q * rc, rc)
            gate = [[None] * n_pv for _ in range(n_sub)]
            for h in range(PEER_HEADS):
                rank = [pltpu.bitcast(r2_ref[h, q, SUBLANES * v:SUBLANES * (v + 1), cols], BF16)
                        for v in range(n_pv)]
                w2 = [pltpu.bitcast(w2_ref[h, q, SUBLANES * v:SUBLANES * (v + 1), cols], BF16)
                      for v in range(n_pv)]
                for ii in range(n_sub):
                    cnt = bc_ref[0, ii, h]
                    e1 = bc_ref[1, ii, h]
                    for v in range(n_pv):
                        term = jnp.where(rank[v] < cnt, w2[v] * e1, jnp.zeros_like(e1))
                        gate[ii][v] = term if gate[ii][v] is None else gate[ii][v] + term
            for ii in range(n_sub):
                for v in range(n_pv):
                    rows = pl.ds(ii * nk + r0 + BF16_ROWS * v, BF16_ROWS)
                    p_ref[rows, cols] = p_ref[rows, cols] * gate[ii][v]
            return carry

        lax.fori_loop(0, nk // rc, chunk, 0)


def _peer_ffn_body(h_ref, u_ref, vt_ref, e1_ref, cnt_ref, r2_ref, w2_ref, x_ref, mod_ref, o_ref,
                   acc_ref, p_ref, bc_ref):
    j = pl.program_id(1)

    @pl.when(j == 0)
    def _():
        acc_ref[...] = jnp.zeros_like(acc_ref)

    act = lax.dot_general(u_ref[...], h_ref[...], (((1,), (1,)), ((), ())),
                          preferred_element_type=F32)
    p_ref[...] = _gelu_tanh(act).astype(BF16)
    _peer_gate_block(p_ref, e1_ref, cnt_ref, r2_ref, w2_ref, bc_ref)
    acc_ref[...] += _dot(vt_ref[...], p_ref[...])

    @pl.when(j == pl.num_programs(1) - 1)
    def _():
        tt, d = x_ref.shape
        nb = mod_ref.shape[1]
        y = acc_ref[...].T.reshape(tt // nb, nb, d) * mod_ref[5]
        o_ref[...] = x_ref[...] + y.reshape(tt, d)


def _peer_ffn(h2, u_b, vt_b, e1, cnt, r2, w2, x1, modv, n_ctx_tok_tiles, t0_tile):
    t_all, d = h2.shape
    tt = PEER_TOK_TILE
    eb = PEER_EXP_BLOCK
    nk = PEER_NKEYS
    nb = modv.shape[2]
    n_e = u_b.shape[0] // eb
    n_tiles = t_all // tt - t0_tile
    seg = lambda i: jnp.where(i + t0_tile >= n_ctx_tok_tiles, 1, 0)
    by_i = pl.BlockSpec((eb // nk, PEER_HEADS, tt), lambda i, j: (j, 0, i + t0_tile))
    rc = PEER_GATE_ROWS
    by_h = pl.BlockSpec((PEER_HEADS, nk // rc, rc // 2, tt), lambda i, j: (0, 0, 0, i + t0_tile))
    r2, w2 = (a.reshape(PEER_HEADS, nk // rc, rc // 2, t_all) for a in (r2, w2))
    return pl.pallas_call(
        _peer_ffn_body,
        grid=(n_tiles, n_e),
        in_specs=[pl.BlockSpec((tt, d), lambda i, j: (i + t0_tile, 0)),
                  pl.BlockSpec((eb, d), lambda i, j: (j, 0)),
                  pl.BlockSpec((d, eb), lambda i, j: (0, j)),
                  by_i, by_i, by_h, by_h,
                  pl.BlockSpec((tt, d), lambda i, j: (i + t0_tile, 0)),
                  pl.BlockSpec((None, N_MOD, nb, d), lambda i, j: (seg(i), 0, 0, 0))],
        out_specs=pl.BlockSpec((tt, d), lambda i, j: (i + t0_tile, 0)),
        out_shape=jax.ShapeDtypeStruct((t_all, d), F32),
        scratch_shapes=[pltpu.VMEM((d, tt), F32), pltpu.VMEM((eb, tt), BF16),
                        pltpu.VMEM((2, eb // nk, PEER_HEADS, BF16_ROWS, LANES), BF16)],
        compiler_params=_cparams(2),
        name="peer_ffn",
    )(h2, u_b, vt_b, e1, cnt, r2, w2, x1, modv)


def _final_body(x_ref, g_ref, o_ref):
    x = x_ref[...]
    ms = jnp.mean(x * x, axis=-1, keepdims=True)
    o_ref[...] = x * lax.rsqrt(ms + NORM_EPS) * g_ref[...]


def _final_norm(x2d, g):
    t, d = x2d.shape
    tr = 512
    return pl.pallas_call(
        _final_body,
        grid=(t // tr,),
        in_specs=[pl.BlockSpec((tr, d), lambda i: (i, 0)), pl.BlockSpec((1, d), lambda i: (0, 0))],
        out_specs=pl.BlockSpec((tr, d), lambda i: (i, 0)),
        out_shape=jax.ShapeDtypeStruct((t, d), F32),
        compiler_params=_cparams(1),
        name="final_norm",
    )(x2d, g.reshape(1, d))


def _pad_cols(w, width):
    return jnp.pad(w, [(0, 0)] * (w.ndim - 1) + [(0, width - w.shape[-1])])


def _rwkv_slabs(w):
    gw3 = 3 * GROUP_W
    parts = [w[..., :gw3],
             _pad_cols(w[..., gw3:gw3 + 64], 128),
             _pad_cols(w[..., gw3 + 64:gw3 + 128], 128),
             _pad_cols(w[..., gw3 + 128:gw3 + 192], 128)]
    return jnp.concatenate(parts, axis=-1)


def _layer_weights(l, p):
    d = p['w_in'].shape[1]
    w_in = p['w_in'][l]
    wa_, wb_, wc_, wd_ = jnp.split(w_in, [RWKV_COLS, RWKV_COLS + RET_COLS, RWKV_COLS + RET_COLS + LRU_COLS], axis=-1)
    perm = _ret_perm()
    gw = GROUP_W
    wb_ = jnp.concatenate([wb_[:, 0:gw][:, perm], wb_[:, gw:2 * gw][:, perm], wb_[:, 2 * gw:]], axis=-1)
    out = {'w_in': jnp.concatenate([_rwkv_slabs(wa_), wb_, wc_, wd_], axis=-1).astype(BF16)}
    out['mu'] = _rwkv_slabs(p['rwkv_mu'][l])[:, None, :]
    out['gup'] = jnp.pad(p['rwkv_g_up'][l], ((0, 64), (0, 0))).astype(BF16)
    lora = RWKV_LORA
    out['wup'] = jnp.stack([jnp.pad(p['rwkv_w_up'][l, dd], ((dd * lora, 128 - (dd + 1) * lora), (0, 0)))
                            for dd in range(2)]).astype(BF16)
    out['aup'] = jnp.stack([jnp.pad(p['rwkv_a_up'][l, dd], ((dd * lora, 128 - (dd + 1) * lora), (0, 0)))
                            for dd in range(2)]).astype(BF16)
    out['rwkv_vecs'] = jnp.concatenate([p['rwkv_k_k'][l][None], p['rwkv_k_a'][l][None],
                                        p['rwkv_w0'][l], p['rwkv_a0'][l]], axis=0)
    out['cw'] = jnp.concatenate([p['lru_conv_w'][l], p['lru_conv_b'][l][None]], axis=0)
    out['wa'] = _block_diag(p['lru_wa'][l]).astype(BF16)
    out['wx'] = _block_diag(p['lru_wx'][l]).astype(BF16)
    out['lru_vecs'] = jnp.stack([p['lru_ba'][l], p['lru_bx'][l], p['lru_lambda'][l]], axis=1)
    out['s5'] = _s5_params(p['s5_a_re'][l], p['s5_a_im'][l], p['s5_log_dt'][l], p['s5_b_re'][l],
                           p['s5_b_im'][l], p['s5_c_re'][l], p['s5_c_im'][l])
    out['out_vecs'] = jnp.stack([p['rwkv_r_k'][l].reshape(-1), p['rwkv_ln_g'][l], p['rwkv_ln_b'][l],
                                 p['ret_gn_g'][l], p['s5_d'][l]], axis=0)
    out['gluw'] = p['s5_glu_w'][l].astype(BF16)
    out['glub'] = p['s5_glu_b'][l][None]
    out['wo'] = p['w_out'][l].astype(BF16)
    out['n2'] = p['norm2_g'][l][None]
    out['wq'] = p['peer_wq'][l].astype(BF16)
    out['keys'] = p['peer_keys'][l].reshape(PEER_HEADS * 2, PEER_NKEYS, PEER_DKEY).astype(BF16)
    out['u'] = p['peer_u'][l].astype(BF16)
    out['vt'] = p['peer_v'][l].T.astype(BF16)
    return out


def kernel(x, c, ctx, c_ctx, norm1_g, norm2_g, ada_w, ada_b, w_in, w_out, rwkv_mu, rwkv_w0, rwkv_w_up, rwkv_a0, rwkv_a_up, rwkv_g_up, rwkv_k_k, rwkv_k_a, rwkv_r_k, rwkv_ln_g, rwkv_ln_b, ret_gn_g, lru_conv_w, lru_conv_b, lru_wa, lru_ba, lru_wx, lru_bx, lru_lambda, s5_a_re, s5_a_im, s5_log_dt, s5_b_re, s5_b_im, s5_c_re, s5_c_im, s5_d, s5_glu_w, s5_glu_b, peer_wq, peer_keys, peer_u, peer_v, final_norm_g):
    p = dict(w_in=w_in, w_out=w_out, norm2_g=norm2_g, rwkv_mu=rwkv_mu, rwkv_w0=rwkv_w0, rwkv_w_up=rwkv_w_up,
             rwkv_a0=rwkv_a0, rwkv_a_up=rwkv_a_up, rwkv_g_up=rwkv_g_up, rwkv_k_k=rwkv_k_k, rwkv_k_a=rwkv_k_a,
             rwkv_r_k=rwkv_r_k, rwkv_ln_g=rwkv_ln_g, rwkv_ln_b=rwkv_ln_b, ret_gn_g=ret_gn_g,
             lru_conv_w=lru_conv_w, lru_conv_b=lru_conv_b, lru_wa=lru_wa, lru_ba=lru_ba, lru_wx=lru_wx,
             lru_bx=lru_bx, lru_lambda=lru_lambda, s5_a_re=s5_a_re, s5_a_im=s5_a_im, s5_log_dt=s5_log_dt,
             s5_b_re=s5_b_re, s5_b_im=s5_b_im, s5_c_re=s5_c_re, s5_c_im=s5_c_im, s5_d=s5_d,
             s5_glu_w=s5_glu_w, s5_glu_b=s5_glu_b, peer_wq=peer_wq, peer_keys=peer_keys, peer_u=peer_u,
             peer_v=peer_v)
    nb, l_lat, d = x.shape
    l_ctx = ctx.shape[1]
    lt = l_ctx + l_lat
    depth = w_in.shape[0]
    assert 128 % (nb * N_HEADS) == 0 and l_ctx % 128 == 0 and l_lat % 128 == 0
    assert (l_ctx * nb) % PEER_TOK_TILE == 0 and (l_lat * nb) % PEER_TOK_TILE == 0
    v_lo = 128 // (nb * N_HEADS)
    n_ctx_tiles = l_ctx // ROW_TILE_T
    n_ctx_tok_tiles = l_ctx * nb // PEER_TOK_TILE
    hs = _head_sum_matrix()
    ret_tabs = _ret_tables(lt)

    xs = jnp.transpose(jnp.concatenate([ctx, x], axis=1), (1, 0, 2))
    mod_rows = 8 * ((nb + 1 + 7) // 8)
    cvec = jnp.zeros((mod_rows, d), F32).at[:nb].set(c).at[nb].set(c_ctx)

    for l in range(depth):
        wl = _layer_weights(l, p)
        last = l == depth - 1
        mod = _modulation(cvec, ada_w[l], ada_b[l])
        mod_lat = mod[:nb].reshape(nb, N_MOD, d).transpose(1, 0, 2)
        mod_ctx = jnp.broadcast_to(mod[nb].reshape(N_MOD, 1, d), (N_MOD, nb, d))
        modv = jnp.stack([mod_ctx, mod_lat])

        za, zb, zc, zd = _in_proj(xs, norm1_g[l], modv, wl['w_in'], n_ctx_tiles)

        r, k, v, g, av, w_dec, kd, bv = _rwkv_features(za, wl['mu'], wl['gup'], wl['wup'], wl['aup'],
                                                       wl['rwkv_vecs'], hs, n_ctx_tiles)
        ex = lambda t: _rwkv_expand_k(t, v_lo)
        ya = _rwkv_scan(ex(r), ex(av), ex(w_dec), ex(kd), ex(bv), _rwkv_expand_v(v, v_lo), l_ctx)
        ya = _rwkv_collapse_v(ya, nb, v_lo)
        ob = _retention(jnp.transpose(zb, (1, 0, 2)), ret_tabs, l_ctx)
        ob = jnp.transpose(ob, (0, 2, 1, 3))
        hc = _lru(zc, wl['cw'], wl['wa'], wl['wx'], wl['lru_vecs'], l_ctx)
        yd = _from_colmajor(_s5(_to_colmajor(zd, l_ctx), *wl['s5'], l_ctx), l_ctx)

        t0 = n_ctx_tiles if last else 0
        x1, h2 = _out_proj(xs, modv, ya, (r, k, v, g), ob, zb, hc, zc, yd, zd, wl['out_vecs'], hs,
                           wl['gluw'], wl['glub'], wl['wo'], wl['n2'], n_ctx_tiles, t0)
        t0p = n_ctx_tok_tiles if last else 0
        e1, cnt, r2, w2 = _peer_route(h2, wl['wq'], wl['keys'], t0p)
        x2 = _peer_ffn(h2, wl['u'], wl['vt'], e1, cnt, r2, w2, x1.reshape(lt * nb, d), modv,
                       n_ctx_tok_tiles, t0p)
        xs = x2.reshape(lt, nb, d)

    out = _final_norm(xs[l_ctx:].reshape(l_lat * nb, d), final_norm_g)
    return jnp.transpose(out.reshape(l_lat, nb, d), (1, 0, 2))
```

```python
import functools
import math

import jax
import jax.numpy as jnp
import numpy as np
from jax import lax
from jax.experimental import pallas as pl
from jax.experimental.pallas import tpu as pltpu

F32 = jnp.float32
BF16 = jnp.bfloat16

NORM_EPS = 1e-6
N_MOD = 6
HEAD_DIM = 64
N_HEADS = 4
GROUP_W = N_HEADS * HEAD_DIM
GRID_W = 64

RWKV_GATE_LORA = 64
RWKV_LORA = 32
RWKV_GN_EPS = 64e-5
RWKV_COLS = 3 * GROUP_W + RWKV_GATE_LORA + 4 * RWKV_LORA
ZA_W = 3 * GROUP_W + 3 * 128

RET_CHUNK = 128
RET_GN_EPS = 1e-5
RET_FWD_OFFSET = 5.0
RET_BWD_OFFSET = 5.5
ROPE_BASE = 10000.0
RET_COLS = 4 * GROUP_W

LRU_CONV_W = 4
LRU_C = 8.0
LRU_COLS = 2 * GROUP_W
LRU_CHUNK = 128

S5_CH = 16
S5_GROUPS = GROUP_W // S5_CH
S5_STATE = 64
S5_N = S5_GROUPS * S5_STATE
S5_CHUNK = 128

PEER_HEADS = 8
PEER_NKEYS = 128
PEER_DKEY = 128
PEER_TOPK = 16
PEER_TOK_TILE = 512
PEER_EXP_BLOCK = 1024

ROW_TILE_T = 32
RWKV_CHUNK = 32

VMEM_LIMIT_BYTES = 56 * 1024 * 1024

NEG_INF = float("-inf")


def _cparams(n_axes):
    return pltpu.CompilerParams(dimension_semantics=("arbitrary",) * n_axes,
                                vmem_limit_bytes=VMEM_LIMIT_BYTES)


def _sigmoid(x):
    return 1.0 / (1.0 + jnp.exp(-x))


def _gelu_tanh(x):
    return 0.5 * x * (1.0 + jnp.tanh(math.sqrt(2.0 / math.pi) * (x + 0.044715 * (x * x * x))))


def _softplus(x):
    return jnp.maximum(x, 0.0) + jnp.log(1.0 + jnp.exp(-jnp.abs(x)))


def _dot(a, b):
    return jnp.dot(a, b, preferred_element_type=F32)


def _dot_exact(a, b):
    return jnp.dot(a, b, preferred_element_type=F32, precision=lax.Precision.HIGHEST)


def _head_sum_matrix():
    idx = np.arange(GROUP_W) // HEAD_DIM
    return jnp.asarray((idx[:, None] == idx[None, :]).astype(np.float32))


def _mod_body(c_ref, w_ref, b_ref, o_ref):
    c = c_ref[...]
    s = (c * _sigmoid(c)).astype(BF16)
    o_ref[...] = _dot(s, w_ref[...].astype(BF16)) + b_ref[...]


def _modulation(cvec, w, b):
    rows, d = cvec.shape
    n = w.shape[1]
    tn = 512
    return pl.pallas_call(
        _mod_body,
        grid=(n // tn,),
        in_specs=[pl.BlockSpec((rows, d), lambda j: (0, 0)),
                  pl.BlockSpec((d, tn), lambda j: (0, j)),
                  pl.BlockSpec((1, tn), lambda j: (0, j))],
        out_specs=pl.BlockSpec((rows, tn), lambda j: (0, j)),
        out_shape=jax.ShapeDtypeStruct((rows, n), F32),
        compiler_params=_cparams(1),
        name="adaln_mod",
    )(cvec, w, b.reshape(1, n))


def _in_body(x_ref, g_ref, mod_ref, w_ref, za_ref, zb_ref, zc_ref, zd_ref):
    tt, nb, d = x_ref.shape
    x = x_ref[...]
    ms = jnp.mean(x * x, axis=-1, keepdims=True)
    y = x * lax.rsqrt(ms + NORM_EPS) * g_ref[...]
    h = y * (1.0 + mod_ref[1]) + mod_ref[0]
    hb = h.reshape(tt * nb, d).astype(BF16)
    c0 = 0
    for ref in (za_ref, zb_ref, zc_ref, zd_ref):
        w = ref.shape[-1]
        ref[...] = _dot(hb, w_ref[:, c0:c0 + w]).reshape(tt, nb, w)
        c0 += w


def _in_proj(xs, g, modv, w_p, n_ctx_tiles):
    lt, nb, d = xs.shape
    tt = ROW_TILE_T
    widths = (ZA_W, RET_COLS, LRU_COLS, GROUP_W)
    seg = lambda i: jnp.where(i >= n_ctx_tiles, 1, 0)
    return pl.pallas_call(
        _in_body,
        grid=(lt // tt,),
        in_specs=[pl.BlockSpec((tt, nb, d), lambda i: (i, 0, 0)),
                  pl.BlockSpec((1, d), lambda i: (0, 0)),
                  pl.BlockSpec((None, N_MOD, nb, d), lambda i: (seg(i), 0, 0, 0)),
                  pl.BlockSpec(w_p.shape, lambda i: (0, 0))],
        out_specs=[pl.BlockSpec((tt, nb, w), lambda i: (i, 0, 0)) for w in widths],
        out_shape=[jax.ShapeDtypeStruct((lt, nb, w), F32) for w in widths],
        compiler_params=_cparams(1),
        name="in_proj",
    )(xs, g.reshape(1, d), modv, w_p)


def _rwkv_feat_body(n_ctx_tiles, n_tiles,
                    z_ref, zp_ref, zn_ref, mu_ref, gup_ref, wup_ref, aup_ref, vec_ref, hs_ref,
                    r_ref, k_ref, v_ref, g_ref, av_ref, w_ref, kd_ref, bv_ref):
    i = pl.program_id(0)
    tt, nb, zw = z_ref.shape
    z0 = z_ref[...]
    has_prev = jnp.logical_and(i != 0, i != n_ctx_tiles)
    has_next = jnp.logical_and(i != n_ctx_tiles - 1, i != n_tiles - 1)
    hp = jnp.where(has_prev, zp_ref[...], 0.0)
    hn = jnp.where(has_next, zn_ref[...], 0.0)
    zp = jnp.concatenate([hp, z0[:-1]], axis=0)
    zn = jnp.concatenate([z0[1:], hn], axis=0)
    z = z0 + mu_ref[0] * (zp - z0) + mu_ref[1] * (zn - z0)
    z = z.reshape(tt * nb, zw)
    gw = GROUP_W
    r = z[:, 0:gw]
    k = z[:, gw:2 * gw]
    v = z[:, 2 * gw:3 * gw]
    g = _dot(_sigmoid(z[:, 3 * gw:3 * gw + 128]).astype(BF16), gup_ref[...])
    w_low = jnp.tanh(z[:, 3 * gw + 128:3 * gw + 256]).astype(BF16)
    a_low = z[:, 3 * gw + 256:3 * gw + 384].astype(BF16)
    k_k = vec_ref[0:1, :]
    k_a = vec_ref[1:2, :]
    kk = k * k_k
    ss = _dot_exact(kk * kk, hs_ref[...])
    kk = kk / jnp.maximum(jnp.sqrt(ss), 1e-12)
    shp = (tt, nb, gw)
    r_ref[...] = r.reshape(shp)
    k_ref[...] = k.reshape(shp)
    v_ref[...] = v.reshape(shp)
    g_ref[...] = g.reshape(shp)
    av_ref[...] = (-kk).reshape(shp)
    for d in range(2):
        w0 = vec_ref[2 + d:3 + d, :]
        a0 = vec_ref[4 + d:5 + d, :]
        w_log = -_softplus(-(w0 + _dot(w_low, wup_ref[d]))) - 0.5
        w_ref[d] = jnp.exp(-jnp.exp(w_log)).reshape(shp)
        a = _sigmoid(a0 + _dot(a_low, aup_ref[d]))
        kd_ref[d] = (k * (1.0 + (a - 1.0) * k_a)).reshape(shp)
        bv_ref[d] = (kk * a).reshape(shp)


def _rwkv_features(za, mu_p, gup_p, wup_p, aup_p, vecs, hs, n_ctx_tiles):
    lt, nb, zw = za.shape
    tt = ROW_TILE_T
    n_tiles = lt // tt
    gw = GROUP_W
    one = lambda shape: jax.ShapeDtypeStruct(shape, F32)
    full = lambda a: pl.BlockSpec(a.shape, lambda i: (0,) * a.ndim)
    o1 = pl.BlockSpec((tt, nb, gw), lambda i: (i, 0, 0))
    o2 = pl.BlockSpec((2, tt, nb, gw), lambda i: (0, i, 0, 0))
    return pl.pallas_call(
        functools.partial(_rwkv_feat_body, n_ctx_tiles, n_tiles),
        grid=(n_tiles,),
        in_specs=[pl.BlockSpec((tt, nb, zw), lambda i: (i, 0, 0)),
                  pl.BlockSpec((1, nb, zw), lambda i: (jnp.maximum(i * tt - 1, 0), 0, 0)),
                  pl.BlockSpec((1, nb, zw), lambda i: (jnp.minimum((i + 1) * tt, lt - 1), 0, 0)),
                  full(mu_p), full(gup_p), full(wup_p), full(aup_p), full(vecs), full(hs)],
        out_specs=[o1, o1, o1, o1, o1, o2, o2, o2],
        out_shape=[one((lt, nb, gw))] * 5 + [one((2, lt, nb, gw))] * 3,
        compiler_params=_cparams(1),
        name="rwkv_features",
    )(za, za, za, mu_p, gup_p, wup_p, aup_p, vecs, hs)


def _rwkv_scan_body(r_ref, a_ref, w_ref, kd_ref, b_ref, v_ref, y_ref, s_ref, ex_ref):
    d = pl.program_id(0)
    j = pl.program_id(1)
    tc = r_ref.shape[0]
    n_k = HEAD_DIM
    n_acc = 4
    v_lo = s_ref.shape[-1] // r_ref.shape[-1]

    @pl.when(j == 0)
    def _():
        s_ref[...] = jnp.zeros_like(s_ref)

    for idx, ref in enumerate((r_ref, a_ref, w_ref, kd_ref, b_ref)):
        x = ref[...]
        reps = 1
        while reps < v_lo:
            x = jnp.concatenate([x, x], axis=-1)
            reps *= 2
        ex_ref[idx] = x
    r_x, a_x, w_x, kd_x, b_x = (ex_ref.at[idx] for idx in range(5))

    def step(i, carry):
        t = jnp.where(d == 0, i, tc - 1 - i)
        vv = v_ref[t]
        acc = [None] * n_acc
        for kk in range(n_k):
            term = s_ref[kk] * a_x[t, pl.ds(kk, 1), :]
            acc[kk % n_acc] = term if acc[kk % n_acc] is None else acc[kk % n_acc] + term
        sa = (acc[0] + acc[1]) + (acc[2] + acc[3])
        acc = [None] * n_acc
        for kk in range(n_k):
            s_new = (s_ref[kk] * w_x[t, pl.ds(kk, 1), :] + sa * b_x[t, pl.ds(kk, 1), :]
                     + vv * kd_x[t, pl.ds(kk, 1), :])
            s_ref[kk] = s_new
            term = s_new * r_x[t, pl.ds(kk, 1), :]
            acc[kk % n_acc] = term if acc[kk % n_acc] is None else acc[kk % n_acc] + term
        y_ref[t] = (acc[0] + acc[1]) + (acc[2] + acc[3])
        return carry

    lax.fori_loop(0, tc, step, 0)


def _dir_chunk(d, j, n_ctx, n_all):
    bwd = jnp.where(j < n_ctx, n_ctx - 1 - j, n_all - 1 - (j - n_ctx))
    return jnp.where(d == 0, j, bwd)


def _rwkv_scan(r_e, a_e, w_e, kd_e, b_e, v_e, l_ctx):
    lt, n_k, n_bh = r_e.shape
    _, v_hi, lanes = v_e.shape
    tc = RWKV_CHUNK
    n_all, n_ctx = lt // tc, l_ctx // tc
    cm = lambda d, j: _dir_chunk(d, j, n_ctx, n_all)
    shared = pl.BlockSpec((tc, n_k, n_bh), lambda d, j: (cm(d, j), 0, 0))
    per_dir = pl.BlockSpec((None, tc, n_k, n_bh), lambda d, j: (d, cm(d, j), 0, 0))
    return pl.pallas_call(
        _rwkv_scan_body,
        grid=(2, n_all),
        in_specs=[shared, shared, per_dir, per_dir, per_dir,
                  pl.BlockSpec((tc, v_hi, lanes), lambda d, j: (cm(d, j), 0, 0))],
        out_specs=pl.BlockSpec((None, tc, v_hi, lanes), lambda d, j: (d, cm(d, j), 0, 0)),
        out_shape=jax.ShapeDtypeStruct((2, lt, v_hi, lanes), F32),
        scratch_shapes=[pltpu.VMEM((n_k, v_hi, lanes), F32), pltpu.VMEM((5, tc, n_k, lanes), F32)],
        compiler_params=_cparams(2),
        name="rwkv_scan",
    )(r_e, a_e, w_e, kd_e, b_e, v_e)


def _rwkv_expand_k(x):
    lead = x.shape[:-2]
    nb = x.shape[-2]
    x = x.reshape(lead + (nb, N_HEADS, HEAD_DIM))
    return jnp.moveaxis(x, -1, -3).reshape(lead + (HEAD_DIM, nb * N_HEADS))


def _rwkv_expand_v(x, v_lo):
    lt, nb, _ = x.shape
    v_hi = HEAD_DIM // v_lo
    x = x.reshape(lt, nb, N_HEADS, v_hi, v_lo)
    return jnp.transpose(x, (0, 3, 4, 1, 2)).reshape(lt, v_hi, v_lo * nb * N_HEADS)


def _rwkv_collapse_v(y, nb, v_lo):
    _, lt, v_hi, _ = y.shape
    y = y.reshape(2, lt, v_hi, v_lo, nb, N_HEADS)
    return jnp.transpose(y, (0, 1, 4, 5, 2, 3)).reshape(2, lt, nb, GROUP_W)


def _ret_body(z_ref, cos_ref, sin_ref, dec_ref, rd_ref, wd_ref, cd_ref, hm_ref, o_ref, s_ref):
    j = pl.program_id(2)
    gw = GROUP_W
    half = gw // 2

    @pl.when(j == 0)
    def _():
        s_ref[...] = jnp.zeros_like(s_ref)

    cos = cos_ref[...]
    sin = sin_ref[...]

    def rope(x):
        x1, x2 = x[:, :half], x[:, half:]
        return jnp.concatenate([x1 * cos - x2 * sin, x1 * sin + x2 * cos], axis=-1)

    q = rope(z_ref[:, 0:gw]) * (HEAD_DIM ** -0.5)
    k = rope(z_ref[:, gw:2 * gw])
    v = z_ref[:, 2 * gw:3 * gw]
    vb = v.astype(BF16)
    k_t = k.T
    k_tb = k_t.astype(BF16)
    inter = _dot(q.astype(BF16), s_ref[...].astype(BF16)) * rd_ref[...]
    intra = jnp.zeros_like(inter)
    for h in range(N_HEADS):
        qm = (q * hm_ref[h:h + 1, :]).astype(BF16)
        att = _dot(qm, k_tb) * dec_ref[h]
        vm = (v * hm_ref[N_HEADS + h:N_HEADS + h + 1, :]).astype(BF16)
        intra = intra + _dot(att.astype(BF16), vm)
    o_ref[...] = intra + inter
    kv = _dot((k_t * wd_ref[...]).astype(BF16), vb)
    s_ref[...] = s_ref[...] * cd_ref[0] + kv * cd_ref[1]


def _retention(zb_bm, tabs, l_ctx):
    nb, lt, _ = zb_bm.shape
    c = RET_CHUNK
    n_all, n_ctx = lt // c, l_ctx // c
    gw = GROUP_W
    cos, sin, dec, rd, wd, cd, hm = tabs
    cm = lambda d, j: _dir_chunk(d, j, n_ctx, n_all)
    return pl.pallas_call(
        _ret_body,
        grid=(nb, 2, n_all),
        in_specs=[pl.BlockSpec((None, c, RET_COLS), lambda b, d, j: (b, cm(d, j), 0)),
                  pl.BlockSpec((c, gw // 2), lambda b, d, j: (cm(d, j), 0)),
                  pl.BlockSpec((c, gw // 2), lambda b, d, j: (cm(d, j), 0)),
                  pl.BlockSpec((None, N_HEADS, c, c), lambda b, d, j: (d, 0, 0, 0)),
                  pl.BlockSpec((None, c, gw), lambda b, d, j: (d, 0, 0)),
                  pl.BlockSpec((None, gw, c), lambda b, d, j: (d, 0, 0)),
                  pl.BlockSpec((None, 2, gw, gw), lambda b, d, j: (d, 0, 0, 0)),
                  pl.BlockSpec((2 * N_HEADS, gw), lambda b, d, j: (0, 0))],
        out_specs=pl.BlockSpec((None, None, c, gw), lambda b, d, j: (d, b, cm(d, j), 0)),
        out_shape=jax.ShapeDtypeStruct((2, nb, lt, gw), F32),
        scratch_shapes=[pltpu.VMEM((gw, gw), F32)],
        compiler_params=_cparams(3),
        name="retention",
    )(zb_bm, cos, sin, dec, rd, wd, cd, hm)


def _ret_perm():
    new = np.arange(GROUP_W)
    half_id, rem = new // 128, new % 128
    h, i = rem // 32, rem % 32
    return h * HEAD_DIM + half_id * 32 + i


def _ret_tables(lt):
    c = RET_CHUNK
    half = HEAD_DIM // 2
    freqs = ROPE_BASE ** (-jnp.arange(half, dtype=F32) / half)
    ang = jnp.arange(lt, dtype=F32)[:, None] * freqs[None]
    cos = jnp.tile(jnp.cos(ang), (1, N_HEADS))
    sin = jnp.tile(jnp.sin(ang), (1, N_HEADS))
    idx = jnp.arange(c, dtype=F32)
    perm = _ret_perm()
    head_of_qk = jnp.asarray(perm // HEAD_DIM)
    head_of_v = jnp.arange(GROUP_W) // HEAD_DIM
    dec, rd, wd, cd = [], [], [], []
    for offset, rev in ((RET_FWD_OFFSET, False), (RET_BWD_OFFSET, True)):
        lg = jnp.log1p(-jnp.exp2(-(offset + jnp.arange(N_HEADS, dtype=F32))))
        diff = idx[:, None] - idx[None, :]
        if rev:
            mask, dist = diff < 0, -diff
            read_pow, write_pow = c - idx, idx
        else:
            mask, dist = diff >= 0, diff
            read_pow, write_pow = idx + 1.0, c - 1.0 - idx
        dec.append(jnp.where(mask[None], jnp.exp(lg[:, None, None] * jnp.where(mask, dist, 0.0)[None]), 0.0))
        rd.append(jnp.exp(lg[head_of_v][None, :] * read_pow[:, None]))
        wd.append(jnp.exp(lg[head_of_qk][:, None] * write_pow[None, :]))
        same = (head_of_qk[:, None] == head_of_v[None, :]).astype(F32)
        cd.append(jnp.stack([same * jnp.exp(lg * c)[head_of_v][None, :], same]))
    hm_q = (head_of_qk[None, :] == jnp.arange(N_HEADS)[:, None]).astype(F32)
    hm_v = (head_of_v[None, :] == jnp.arange(N_HEADS)[:, None]).astype(F32)
    return cos, sin, jnp.stack(dec), jnp.stack(rd), jnp.stack(wd), jnp.stack(cd), jnp.concatenate([hm_q, hm_v])


def _lru_body(n_ctx, n_all, x_ref, xp_ref, xn_ref, cw_ref, wa_ref, wx_ref, vec_ref, o_ref,
              a_s, b_s, h_s):
    d = pl.program_id(0)
    j = pl.program_id(1)
    tc, nb, gw = x_ref.shape
    ch = _dir_chunk(d, j, n_ctx, n_all)

    @pl.when(j == 0)
    def _():
        h_s[...] = jnp.zeros_like(h_s)

    has_prev = jnp.logical_and(ch != 0, ch != n_ctx)
    has_next = jnp.logical_and(ch != n_ctx - 1, ch != n_all - 1)
    xe = jnp.concatenate([jnp.where(has_prev, xp_ref[...], 0.0), x_ref[...],
                          jnp.where(has_next, xn_ref[...], 0.0)], axis=0)
    xc = cw_ref[LRU_CONV_W:LRU_CONV_W + 1, :]
    for tap in range(LRU_CONV_W):
        xc = xc + cw_ref[tap:tap + 1, :] * xe[tap:tap + tc]
    xc = xc.reshape(tc * nb, gw)
    xb = xc.astype(BF16)
    r = _sigmoid(_dot(xb, wa_ref[...]) + vec_ref[0:1, :])
    gi = _sigmoid(_dot(xb, wx_ref[...]) + vec_ref[1:2, :])
    log_a = -LRU_C * r * _softplus(-vec_ref[2:3, :])
    th = jnp.tanh(log_a)
    one_minus_a2 = 2.0 * th / (th - 1.0)
    a_s[...] = jnp.exp(log_a).reshape(tc, nb, gw)
    b_s[...] = (jnp.sqrt(one_minus_a2) * (gi * xc)).reshape(tc, nb, gw)

    def step(i, h):
        t = jnp.where(d == 0, i, tc - 1 - i)
        h = a_s[t] * h + b_s[t]
        o_ref[t] = h
        return h

    h_s[...] = lax.fori_loop(0, tc, step, h_s[...])


def _lru(zc, cw, wa_bd, wx_bd, vecs, l_ctx):
    lt, nb, _ = zc.shape
    gw = GROUP_W
    tc = LRU_CHUNK
    n_all, n_ctx = lt // tc, l_ctx // tc
    cm = lambda d, j: _dir_chunk(d, j, n_ctx, n_all)
    return pl.pallas_call(
        functools.partial(_lru_body, n_ctx, n_all),
        grid=(2, n_all),
        in_specs=[pl.BlockSpec((tc, nb, gw), lambda d, j: (cm(d, j), 0, 0)),
                  pl.BlockSpec((2, nb, gw), lambda d, j: (jnp.maximum(cm(d, j) * (tc // 2) - 1, 0), 0, 0)),
                  pl.BlockSpec((1, nb, gw), lambda d, j: (jnp.minimum((cm(d, j) + 1) * tc, lt - 1), 0, 0)),
                  pl.BlockSpec(cw.shape, lambda d, j: (0, 0)),
                  pl.BlockSpec((None, gw, gw), lambda d, j: (d, 0, 0)),
                  pl.BlockSpec((None, gw, gw), lambda d, j: (d, 0, 0)),
                  pl.BlockSpec((None, 3, gw), lambda d, j: (d, 0, 0))],
        out_specs=pl.BlockSpec((None, tc, nb, gw), lambda d, j: (d, cm(d, j), 0, 0)),
        out_shape=jax.ShapeDtypeStruct((2, lt, nb, gw), F32),
        scratch_shapes=[pltpu.VMEM((tc, nb, gw), F32), pltpu.VMEM((tc, nb, gw), F32),
                        pltpu.VMEM((nb, gw), F32)],
        compiler_params=_cparams(2),
        name="rglru",
    )(zc, zc, zc, cw, wa_bd, wx_bd, vecs)


def _block_diag(w):
    nblk, n = w.shape[-3], w.shape[-1]
    eye = jnp.eye(nblk, dtype=w.dtype)
    out = w[..., :, :, None, :] * eye[:, None, :, None]
    return out.reshape(w.shape[:-3] + (nblk * n, nblk * n))


def _s5_body(u_ref, bm_ref, lam_ref, cm_ref, y_ref, h_s, st_s):
    d = pl.program_id(0)
    j = pl.program_id(1)
    tc, nb, gw = u_ref.shape
    n = S5_N

    @pl.when(j == 0)
    def _():
        st_s[...] = jnp.zeros_like(st_s)

    u = u_ref[...].reshape(tc * nb, gw).astype(BF16)
    h_s[...] = _dot(u, bm_ref[...]).reshape(tc, nb, 2 * n)
    lr = lam_ref[0:1, :]
    li = lam_ref[1:2, :]

    def step(i, carry):
        hr, hi = carry
        t = jnp.where(d == 0, i, tc - 1 - i)
        bu = h_s[t]
        nr = lr * hr - li * hi + bu[:, :n]
        ni = lr * hi + li * hr + bu[:, n:]
        h_s[t] = jnp.concatenate([nr, ni], axis=-1)
        return nr, ni

    hr, hi = lax.fori_loop(0, tc, step, (st_s[0], st_s[1]))
    st_s[0] = hr
    st_s[1] = hi
    hh = h_s[...].reshape(tc * nb, 2 * n).astype(BF16)
    y_ref[...] = _dot(hh, cm_ref[...]).reshape(tc, nb, gw)


def _s5(u, bmat, lam, cmat, l_ctx):
    lt, nb, gw = u.shape
    tc = S5_CHUNK
    n_all, n_ctx = lt // tc, l_ctx // tc
    cm = lambda d, j: _dir_chunk(d, j, n_ctx, n_all)
    return pl.pallas_call(
        _s5_body,
        grid=(2, n_all),
        in_specs=[pl.BlockSpec((tc, nb, gw), lambda d, j: (cm(d, j), 0, 0)),
                  pl.BlockSpec((None, gw, 2 * S5_N), lambda d, j: (d, 0, 0)),
                  pl.BlockSpec((None, 2, S5_N), lambda d, j: (d, 0, 0)),
                  pl.BlockSpec((None, 2 * S5_N, gw), lambda d, j: (d, 0, 0))],
        out_specs=pl.BlockSpec((None, tc, nb, gw), lambda d, j: (d, cm(d, j), 0, 0)),
        out_shape=jax.ShapeDtypeStruct((2, lt, nb, gw), F32),
        scratch_shapes=[pltpu.VMEM((tc, nb, 2 * S5_N), F32), pltpu.VMEM((2, nb, S5_N), F32)],
        compiler_params=_cparams(2),
        name="s5",
    )(u, bmat, lam, cmat)


def _s5_params(a_re, a_im, log_dt, b_re, b_im, c_re, c_im):
    dt = jnp.exp(log_dt)[..., None]
    er = jnp.exp(a_re * dt)
    lbr, lbi = er * jnp.cos(a_im * dt), er * jnp.sin(a_im * dt)
    den = a_re * a_re + a_im * a_im
    nr, ni = lbr - 1.0, lbi
    fr = (nr * a_re + ni * a_im) / den
    fi = (ni * a_re - nr * a_im) / den
    bbr = fr[..., None] * b_re - fi[..., None] * b_im
    bbi = fr[..., None] * b_im + fi[..., None] * b_re
    eye = jnp.eye(S5_GROUPS, dtype=F32)

    def in_map(bb):
        m = jnp.einsum('dgpc,gh->dgchp', bb, eye)
        return m.reshape(2, GROUP_W, S5_N)

    def out_map(cc):
        m = jnp.einsum('dgcp,gh->dgphc', cc, eye)
        return m.reshape(2, S5_N, GROUP_W)

    bmat = jnp.concatenate([in_map(bbr), in_map(bbi)], axis=-1).astype(BF16)
    cmat = jnp.concatenate([out_map(c_re), -out_map(c_im)], axis=-2).astype(BF16)
    lam = jnp.stack([lbr.reshape(2, S5_N), lbi.reshape(2, S5_N)], axis=1)
    return bmat, lam, cmat


def _to_colmajor(z, l_ctx):
    lt, nb, ch = z.shape
    rows = (lt - l_ctx) // GRID_W
    lat = z[l_ctx:].reshape(rows, GRID_W, nb, ch).transpose(1, 0, 2, 3).reshape(lt - l_ctx, nb, ch)
    return jnp.concatenate([z[:l_ctx], lat], axis=0)


def _from_colmajor(y, l_ctx):
    lt = y.shape[-3]
    nb, ch = y.shape[-2:]
    rows = (lt - l_ctx) // GRID_W
    lead = y.shape[:-3]
    lat = y[..., l_ctx:, :, :].reshape(lead + (GRID_W, rows, nb, ch))
    lat = jnp.swapaxes(lat, -4, -3).reshape(lead + (lt - l_ctx, nb, ch))
    return jnp.concatenate([y[..., :l_ctx, :, :], lat], axis=-3)


def _out_body(x_ref, mod_ref, ya_ref, r_ref, k_ref, v_ref, g_ref, ob_ref, gb_ref, hc_ref, gc_ref,
              yd_ref, ud_ref, vec_ref, hs_ref, gluw_ref, glub_ref, wo_ref, n2_ref, x1_ref, h2_ref):
    tt, nb, d = x_ref.shape
    gw = GROUP_W
    rows = tt * nb
    hs = hs_ref[...]
    two = lambda ref: (ref[0] + ref[1]).reshape(rows, gw)
    flat = lambda ref: ref[...].reshape(rows, gw)

    def head_norm(y, eps):
        mu = _dot_exact(y, hs) * (1.0 / HEAD_DIM)
        dlt = y - mu
        var = _dot_exact(dlt * dlt, hs) * (1.0 / HEAD_DIM)
        return dlt * lax.rsqrt(var + eps)

    r_k, ln_g, ln_b = vec_ref[0:1, :], vec_ref[1:2, :], vec_ref[2:3, :]
    r, k, v = flat(r_ref), flat(k_ref), flat(v_ref)
    yn = head_norm(two(ya_ref), RWKV_GN_EPS) * ln_g + ln_b
    bonus = _dot_exact(r * k * r_k, hs) * v
    mix_a = (yn + bonus) * flat(g_ref)
    gb = flat(gb_ref)
    mix_b = gb * _sigmoid(gb) * (head_norm(two(ob_ref), RET_GN_EPS) * vec_ref[3:4, :])
    mix_c = _gelu_tanh(flat(gc_ref)) * two(hc_ref)
    yd = vec_ref[4:5, :] * flat(ud_ref) + two(yd_ref)
    glu = _dot(_gelu_tanh(yd).astype(BF16), gluw_ref[...]) + glub_ref[...]
    mix_d = glu[:, :gw] * _sigmoid(glu[:, gw:])

    mix = _dot(mix_a.astype(BF16), wo_ref[0:gw, :])
    mix = mix + _dot(mix_b.astype(BF16), wo_ref[gw:2 * gw, :])
    mix = mix + _dot(mix_c.astype(BF16), wo_ref[2 * gw:3 * gw, :])
    mix = mix + _dot(mix_d.astype(BF16), wo_ref[3 * gw:4 * gw, :])
    x1 = x_ref[...] + mod_ref[2] * mix.reshape(tt, nb, d)
    x1_ref[...] = x1
    ms = jnp.mean(x1 * x1, axis=-1, keepdims=True)
    h2 = (x1 * lax.rsqrt(ms + NORM_EPS) * n2_ref[...]) * (1.0 + mod_ref[4]) + mod_ref[3]
    h2_ref[...] = h2.reshape(rows, d).astype(BF16)


def _out_proj(xs, modv, ya, feats, ob, zb, hc, zc, yd, zd, vecs, hs, gluw, glub, wo, n2, n_ctx_tiles, t0_tile):
    lt, nb, d = xs.shape
    tt = ROW_TILE_T
    gw = GROUP_W
    n_tiles = lt // tt - t0_tile
    seg = lambda i: jnp.where(i + t0_tile >= n_ctx_tiles, 1, 0)
    col = lambda c: pl.BlockSpec((tt, nb, gw), lambda i: (i + t0_tile, 0, c))
    two = pl.BlockSpec((2, tt, nb, gw), lambda i: (0, i + t0_tile, 0, 0))
    full = lambda a: pl.BlockSpec(a.shape, lambda i: (0,) * a.ndim)
    r, k, v, g = feats
    return pl.pallas_call(
        _out_body,
        grid=(n_tiles,),
        in_specs=[pl.BlockSpec((tt, nb, d), lambda i: (i + t0_tile, 0, 0)),
                  pl.BlockSpec((None, N_MOD, nb, d), lambda i: (seg(i), 0, 0, 0)),
                  two, col(0), col(0), col(0), col(0),
                  two, col(3), two, col(1), two, col(0),
                  full(vecs), full(hs), full(gluw), full(glub), full(wo), full(n2)],
        out_specs=[pl.BlockSpec((tt, nb, d), lambda i: (i + t0_tile, 0, 0)),
                   pl.BlockSpec((tt * nb, d), lambda i: (i + t0_tile, 0))],
        out_shape=[jax.ShapeDtypeStruct((lt, nb, d), F32), jax.ShapeDtypeStruct((lt * nb, d), BF16)],
        compiler_params=_cparams(1),
        name="out_proj",
    )(xs, modv, ya, r, k, v, g, ob, zb, hc, zc, yd, zd, vecs, hs, gluw, glub, wo, n2)


PEER_NO_RANK = 127.0


def _top_values(s, n, with_rank=False):
    vals, cur = [], s
    rank = jnp.full(s.shape, PEER_NO_RANK, F32) if with_rank else None
    for k in range(n):
        m = jnp.max(cur, axis=0, keepdims=True)
        vals.append(m)
        hit = cur == m
        if with_rank:
            rank = jnp.where(hit, float(k), rank)
        cur = jnp.where(hit, NEG_INF, cur)
    return (vals, rank) if with_rank else vals


def _peer_route_body(h_ref, wq_ref, keys_ref, e1_ref, cnt_ref, r2_ref, w2_ref, pack_ref):
    hb = h_ref[...]
    for h in range(PEER_HEADS):
        st = []
        for p in range(2):
            c0 = (2 * h + p) * PEER_DKEY
            q = _dot(hb, wq_ref[:, c0:c0 + PEER_DKEY]).astype(BF16)
            st.append(lax.dot_general(keys_ref[2 * h + p], q, (((1,), (1,)), ((), ())),
                                      preferred_element_type=F32))
        s1, s2 = st
        n_top = PEER_TOPK + 1
        v1 = _top_values(s1, n_top)
        v2, rank2 = _top_values(s2, n_top, with_rank=True)
        pad7 = [jnp.full((7, s1.shape[1]), NEG_INF, F32)]
        v2m = jnp.concatenate(v2 + pad7, axis=0)
        v1_tail = jnp.concatenate(v1[8:] + pad7, axis=0)
        cand = jnp.concatenate([v1[0] + v2m] + [v1[a] + v2m[0:8] for a in range(1, 8)]
                               + [v1_tail + v2[0]], axis=0)
        top = _top_values(cand, n_top)
        tau = 0.5 * (top[PEER_TOPK - 1] + top[PEER_TOPK])
        cmax = top[0]
        z = jnp.sum(jnp.where(cand >= tau, jnp.exp(cand - cmax), 0.0), axis=0, keepdims=True)
        thr = tau - s1
        cnt = jnp.zeros_like(s1)
        for k in range(PEER_TOPK):
            cnt = cnt + jnp.where(v2[k] >= thr, 1.0, 0.0)
        e1_ref[:, h, :] = jnp.exp(s1 - v1[0])
        cnt_ref[:, h, :] = cnt
        pack_ref[0] = rank2.astype(BF16)
        pack_ref[1] = (jnp.exp(s2 - v2[0]) / z).astype(BF16)
        words = pack_ref.bitcast(jnp.uint32)
        r2_ref[h] = words[0]
        w2_ref[h] = words[1]


def _peer_route(h2, wq, keys, t0_tile):
    t_all, d = h2.shape
    tt = PEER_TOK_TILE
    nk = PEER_NKEYS
    by_i = pl.BlockSpec((nk, PEER_HEADS, tt), lambda i: (0, 0, i + t0_tile))
    by_h = pl.BlockSpec((PEER_HEADS, nk // 2, tt), lambda i: (0, 0, i + t0_tile))
    return pl.pallas_call(
        _peer_route_body,
        grid=(t_all // tt - t0_tile,),
        in_specs=[pl.BlockSpec((tt, d), lambda i: (i + t0_tile, 0)),
                  pl.BlockSpec(wq.shape, lambda i: (0, 0)),
                  pl.BlockSpec(keys.shape, lambda i: (0, 0, 0))],
        out_specs=[by_i, by_i, by_h, by_h],
        out_shape=[jax.ShapeDtypeStruct((nk, PEER_HEADS, t_all), F32)] * 2
                  + [jax.ShapeDtypeStruct((PEER_HEADS, nk // 2, t_all), jnp.uint32)] * 2,
        scratch_shapes=[pltpu.VMEM((2, nk, tt), BF16)],
        compiler_params=_cparams(1),
        name="peer_route",
    )(h2, wq, keys)


PEER_GATE_ROWS = 32
LANES = 128


BF16_ROWS = 16
SUBLANES = 8


def _peer_gate_block(p_ref, e1_ref, cnt_ref, r2_ref, w2_ref, bc_ref):
    nk = PEER_NKEYS
    n_sub = p_ref.shape[0] // nk
    rc = PEER_GATE_ROWS
    n_pv = rc // BF16_ROWS
    for c0 in range(0, p_ref.shape[1], LANES):
        cols = slice(c0, c0 + LANES)
        for ii in range(n_sub):
            for h in range(PEER_HEADS):
                bc_ref[0, ii, h] = jnp.broadcast_to(cnt_ref[ii, h:h + 1, cols], (BF16_ROWS, LANES)).astype(BF16)
                bc_ref[1, ii, h] = jnp.broadcast_to(e1_ref[ii, h:h + 1, cols], (BF16_ROWS, LANES)).astype(BF16)

        def chunk(q, carry, cols=cols):
            r0 = pl.multiple_of(q * rc, rc)
            gate = [[None] * n_pv for _ in range(n_sub)]
            for h in range(PEER_HEADS):
                rank = [pltpu.bitcast(r2_ref[h, q, SUBLANES * v:SUBLANES * (v + 1), cols], BF16)
                        for v in range(n_pv)]
                w2 = [pltpu.bitcast(w2_ref[h, q, SUBLANES * v:SUBLANES * (v + 1), cols], BF16)
                      for v in range(n_pv)]
                for ii in range(n_sub):
                    cnt = bc_ref[0, ii, h]
                    e1 = bc_ref[1, ii, h]
                    for v in range(n_pv):
                        term = jnp.where(rank[v] < cnt, w2[v] * e1, jnp.zeros_like(e1))
                        gate[ii][v] = term if gate[ii][v] is None else gate[ii][v] + term
            for ii in range(n_sub):
                for v in range(n_pv):
                    rows = pl.ds(ii * nk + r0 + BF16_ROWS * v, BF16_ROWS)
                    p_ref[rows, cols] = p_ref[rows, cols] * gate[ii][v]
            return carry

        lax.fori_loop(0, nk // rc, chunk, 0)


def _peer_ffn_body(h_ref, u_ref, vt_ref, e1_ref, cnt_ref, r2_ref, w2_ref, x_ref, mod_ref, o_ref,
                   acc_ref, p_ref, bc_ref):
    j = pl.program_id(1)

    @pl.when(j == 0)
    def _():
        acc_ref[...] = jnp.zeros_like(acc_ref)

    act = lax.dot_general(u_ref[...], h_ref[...], (((1,), (1,)), ((), ())),
                          preferred_element_type=F32)
    p_ref[...] = _gelu_tanh(act).astype(BF16)
    _peer_gate_block(p_ref, e1_ref, cnt_ref, r2_ref, w2_ref, bc_ref)
    acc_ref[...] += _dot(vt_ref[...], p_ref[...])

    @pl.when(j == pl.num_programs(1) - 1)
    def _():
        tt, d = x_ref.shape
        nb = mod_ref.shape[1]
        y = acc_ref[...].T.reshape(tt // nb, nb, d) * mod_ref[5]
        o_ref[...] = x_ref[...] + y.reshape(tt, d)


def _peer_ffn(h2, u_b, vt_b, e1, cnt, r2, w2, x1, modv, n_ctx_tok_tiles, t0_tile):
    t_all, d = h2.shape
    tt = PEER_TOK_TILE
    eb = PEER_EXP_BLOCK
    nk = PEER_NKEYS
    nb = modv.shape[2]
    n_e = u_b.shape[0] // eb
    n_tiles = t_all // tt - t0_tile
    seg = lambda i: jnp.where(i + t0_tile >= n_ctx_tok_tiles, 1, 0)
    by_i = pl.BlockSpec((eb // nk, PEER_HEADS, tt), lambda i, j: (j, 0, i + t0_tile))
    rc = PEER_GATE_ROWS
    by_h = pl.BlockSpec((PEER_HEADS, nk // rc, rc // 2, tt), lambda i, j: (0, 0, 0, i + t0_tile))
    r2, w2 = (a.reshape(PEER_HEADS, nk // rc, rc // 2, t_all) for a in (r2, w2))
    return pl.pallas_call(
        _peer_ffn_body,
        grid=(n_tiles, n_e),
        in_specs=[pl.BlockSpec((tt, d), lambda i, j: (i + t0_tile, 0)),
                  pl.BlockSpec((eb, d), lambda i, j: (j, 0)),
                  pl.BlockSpec((d, eb), lambda i, j: (0, j)),
                  by_i, by_i, by_h, by_h,
                  pl.BlockSpec((tt, d), lambda i, j: (i + t0_tile, 0)),
                  pl.BlockSpec((None, N_MOD, nb, d), lambda i, j: (seg(i), 0, 0, 0))],
        out_specs=pl.BlockSpec((tt, d), lambda i, j: (i + t0_tile, 0)),
        out_shape=jax.ShapeDtypeStruct((t_all, d), F32),
        scratch_shapes=[pltpu.VMEM((d, tt), F32), pltpu.VMEM((eb, tt), BF16),
                        pltpu.VMEM((2, eb // nk, PEER_HEADS, BF16_ROWS, LANES), BF16)],
        compiler_params=_cparams(2),
        name="peer_ffn",
    )(h2, u_b, vt_b, e1, cnt, r2, w2, x1, modv)


def _final_body(x_ref, g_ref, o_ref):
    x = x_ref[...]
    ms = jnp.mean(x * x, axis=-1, keepdims=True)
    o_ref[...] = x * lax.rsqrt(ms + NORM_EPS) * g_ref[...]


def _final_norm(x2d, g):
    t, d = x2d.shape
    tr = 512
    return pl.pallas_call(
        _final_body,
        grid=(t // tr,),
        in_specs=[pl.BlockSpec((tr, d), lambda i: (i, 0)), pl.BlockSpec((1, d), lambda i: (0, 0))],
        out_specs=pl.BlockSpec((tr, d), lambda i: (i, 0)),
        out_shape=jax.ShapeDtypeStruct((t, d), F32),
        compiler_params=_cparams(1),
        name="final_norm",
    )(x2d, g.reshape(1, d))


def _pad_cols(w, width):
    return jnp.pad(w, [(0, 0)] * (w.ndim - 1) + [(0, width - w.shape[-1])])


def _rwkv_slabs(w):
    gw3 = 3 * GROUP_W
    parts = [w[..., :gw3],
             _pad_cols(w[..., gw3:gw3 + 64], 128),
             _pad_cols(w[..., gw3 + 64:gw3 + 128], 128),
             _pad_cols(w[..., gw3 + 128:gw3 + 192], 128)]
    return jnp.concatenate(parts, axis=-1)


def _layer_weights(l, p):
    d = p['w_in'].shape[1]
    w_in = p['w_in'][l]
    wa_, wb_, wc_, wd_ = jnp.split(w_in, [RWKV_COLS, RWKV_COLS + RET_COLS, RWKV_COLS + RET_COLS + LRU_COLS], axis=-1)
    perm = _ret_perm()
    gw = GROUP_W
    wb_ = jnp.concatenate([wb_[:, 0:gw][:, perm], wb_[:, gw:2 * gw][:, perm], wb_[:, 2 * gw:]], axis=-1)
    out = {'w_in': jnp.concatenate([_rwkv_slabs(wa_), wb_, wc_, wd_], axis=-1).astype(BF16)}
    out['mu'] = _rwkv_slabs(p['rwkv_mu'][l])[:, None, :]
    out['gup'] = jnp.pad(p['rwkv_g_up'][l], ((0, 64), (0, 0))).astype(BF16)
    lora = RWKV_LORA
    out['wup'] = jnp.stack([jnp.pad(p['rwkv_w_up'][l, dd], ((dd * lora, 128 - (dd + 1) * lora), (0, 0)))
                            for dd in range(2)]).astype(BF16)
    out['aup'] = jnp.stack([jnp.pad(p['rwkv_a_up'][l, dd], ((dd * lora, 128 - (dd + 1) * lora), (0, 0)))
                            for dd in range(2)]).astype(BF16)
    out['rwkv_vecs'] = jnp.concatenate([p['rwkv_k_k'][l][None], p['rwkv_k_a'][l][None],
                                        p['rwkv_w0'][l], p['rwkv_a0'][l]], axis=0)
    out['cw'] = jnp.concatenate([p['lru_conv_w'][l], p['lru_conv_b'][l][None]], axis=0)
    out['wa'] = _block_diag(p['lru_wa'][l]).astype(BF16)
    out['wx'] = _block_diag(p['lru_wx'][l]).astype(BF16)
    out['lru_vecs'] = jnp.stack([p['lru_ba'][l], p['lru_bx'][l], p['lru_lambda'][l]], axis=1)
    out['s5'] = _s5_params(p['s5_a_re'][l], p['s5_a_im'][l], p['s5_log_dt'][l], p['s5_b_re'][l],
                           p['s5_b_im'][l], p['s5_c_re'][l], p['s5_c_im'][l])
    out['out_vecs'] = jnp.stack([p['rwkv_r_k'][l].reshape(-1), p['rwkv_ln_g'][l], p['rwkv_ln_b'][l],
                                 p['ret_gn_g'][l], p['s5_d'][l]], axis=0)
    out['gluw'] = p['s5_glu_w'][l].astype(BF16)
    out['glub'] = p['s5_glu_b'][l][None]
    out['wo'] = p['w_out'][l].astype(BF16)
    out['n2'] = p['norm2_g'][l][None]
    out['wq'] = p['peer_wq'][l].astype(BF16)
    out['keys'] = p['peer_keys'][l].reshape(PEER_HEADS * 2, PEER_NKEYS, PEER_DKEY).astype(BF16)
    out['u'] = p['peer_u'][l].astype(BF16)
    out['vt'] = p['peer_v'][l].T.astype(BF16)
    return out


def kernel(x, c, ctx, c_ctx, norm1_g, norm2_g, ada_w, ada_b, w_in, w_out, rwkv_mu, rwkv_w0, rwkv_w_up, rwkv_a0, rwkv_a_up, rwkv_g_up, rwkv_k_k, rwkv_k_a, rwkv_r_k, rwkv_ln_g, rwkv_ln_b, ret_gn_g, lru_conv_w, lru_conv_b, lru_wa, lru_ba, lru_wx, lru_bx, lru_lambda, s5_a_re, s5_a_im, s5_log_dt, s5_b_re, s5_b_im, s5_c_re, s5_c_im, s5_d, s5_glu_w, s5_glu_b, peer_wq, peer_keys, peer_u, peer_v, final_norm_g):
    p = dict(w_in=w_in, w_out=w_out, norm2_g=norm2_g, rwkv_mu=rwkv_mu, rwkv_w0=rwkv_w0, rwkv_w_up=rwkv_w_up,
             rwkv_a0=rwkv_a0, rwkv_a_up=rwkv_a_up, rwkv_g_up=rwkv_g_up, rwkv_k_k=rwkv_k_k, rwkv_k_a=rwkv_k_a,
             rwkv_r_k=rwkv_r_k, rwkv_ln_g=rwkv_ln_g, rwkv_ln_b=rwkv_ln_b, ret_gn_g=ret_gn_g,
             lru_conv_w=lru_conv_w, lru_conv_b=lru_conv_b, lru_wa=lru_wa, lru_ba=lru_ba, lru_wx=lru_wx,
             lru_bx=lru_bx, lru_lambda=lru_lambda, s5_a_re=s5_a_re, s5_a_im=s5_a_im, s5_log_dt=s5_log_dt,
             s5_b_re=s5_b_re, s5_b_im=s5_b_im, s5_c_re=s5_c_re, s5_c_im=s5_c_im, s5_d=s5_d,
             s5_glu_w=s5_glu_w, s5_glu_b=s5_glu_b, peer_wq=peer_wq, peer_keys=peer_keys, peer_u=peer_u,
             peer_v=peer_v)
    nb, l_lat, d = x.shape
    l_ctx = ctx.shape[1]
    lt = l_ctx + l_lat
    depth = w_in.shape[0]
    assert 128 % (nb * N_HEADS) == 0 and l_ctx % 128 == 0 and l_lat % 128 == 0
    assert (l_ctx * nb) % PEER_TOK_TILE == 0 and (l_lat * nb) % PEER_TOK_TILE == 0
    v_lo = 128 // (nb * N_HEADS)
    n_ctx_tiles = l_ctx // ROW_TILE_T
    n_ctx_tok_tiles = l_ctx * nb // PEER_TOK_TILE
    hs = _head_sum_matrix()
    ret_tabs = _ret_tables(lt)

    xs = jnp.transpose(jnp.concatenate([ctx, x], axis=1), (1, 0, 2))
    mod_rows = 8 * ((nb + 1 + 7) // 8)
    cvec = jnp.zeros((mod_rows, d), F32).at[:nb].set(c).at[nb].set(c_ctx)

    for l in range(depth):
        wl = _layer_weights(l, p)
        last = l == depth - 1
        mod = _modulation(cvec, ada_w[l], ada_b[l])
        mod_lat = mod[:nb].reshape(nb, N_MOD, d).transpose(1, 0, 2)
        mod_ctx = jnp.broadcast_to(mod[nb].reshape(N_MOD, 1, d), (N_MOD, nb, d))
        modv = jnp.stack([mod_ctx, mod_lat])

        za, zb, zc, zd = _in_proj(xs, norm1_g[l], modv, wl['w_in'], n_ctx_tiles)

        r, k, v, g, av, w_dec, kd, bv = _rwkv_features(za, wl['mu'], wl['gup'], wl['wup'], wl['aup'],
                                                       wl['rwkv_vecs'], hs, n_ctx_tiles)
        ex = _rwkv_expand_k
        ya = _rwkv_scan(ex(r), ex(av), ex(w_dec), ex(kd), ex(bv), _rwkv_expand_v(v, v_lo), l_ctx)
        ya = _rwkv_collapse_v(ya, nb, v_lo)
        ob = _retention(jnp.transpose(zb, (1, 0, 2)), ret_tabs, l_ctx)
        ob = jnp.transpose(ob, (0, 2, 1, 3))
        hc = _lru(zc, wl['cw'], wl['wa'], wl['wx'], wl['lru_vecs'], l_ctx)
        yd = _from_colmajor(_s5(_to_colmajor(zd, l_ctx), *wl['s5'], l_ctx), l_ctx)

        t0 = n_ctx_tiles if last else 0
        x1, h2 = _out_proj(xs, modv, ya, (r, k, v, g), ob, zb, hc, zc, yd, zd, wl['out_vecs'], hs,
                           wl['gluw'], wl['glub'], wl['wo'], wl['n2'], n_ctx_tiles, t0)
        t0p = n_ctx_tok_tiles if last else 0
        e1, cnt, r2, w2 = _peer_route(h2, wl['wq'], wl['keys'], t0p)
        x2 = _peer_ffn(h2, wl['u'], wl['vt'], e1, cnt, r2, w2, x1.reshape(lt * nb, d), modv,
                       n_ctx_tok_tiles, t0p)
        xs = x2.reshape(lt, nb, d)

    out = _final_norm(xs[l_ctx:].reshape(l_lat * nb, d), final_norm_g)
    return jnp.transpose(out.reshape(l_lat, nb, d), (1, 0, 2))
```

```python
import functools
import math

import jax
import jax.numpy as jnp
import numpy as np
from jax import lax
from jax.experimental import pallas as pl
from jax.experimental.pallas import tpu as pltpu

F32 = jnp.float32
BF16 = jnp.bfloat16

NORM_EPS = 1e-6
N_MOD = 6
HEAD_DIM = 64
N_HEADS = 4
GROUP_W = N_HEADS * HEAD_DIM
GRID_W = 64

RWKV_GATE_LORA = 64
RWKV_LORA = 32
RWKV_GN_EPS = 64e-5
RWKV_COLS = 3 * GROUP_W + RWKV_GATE_LORA + 4 * RWKV_LORA
ZA_W = 3 * GROUP_W + 3 * 128

RET_CHUNK = 128
RET_GN_EPS = 1e-5
RET_FWD_OFFSET = 5.0
RET_BWD_OFFSET = 5.5
ROPE_BASE = 10000.0
RET_COLS = 4 * GROUP_W

LRU_CONV_W = 4
LRU_C = 8.0
LRU_COLS = 2 * GROUP_W
LRU_CHUNK = 128

S5_CH = 16
S5_GROUPS = GROUP_W // S5_CH
S5_STATE = 64
S5_N = S5_GROUPS * S5_STATE
S5_CHUNK = 128

PEER_HEADS = 8
PEER_NKEYS = 128
PEER_DKEY = 128
PEER_TOPK = 16
PEER_TOK_TILE = 512
PEER_EXP_BLOCK = 2048

ROW_TILE_T = 32
RWKV_CHUNK = 32

VMEM_LIMIT_BYTES = 56 * 1024 * 1024

NEG_INF = float("-inf")


def _cparams(n_axes):
    return pltpu.CompilerParams(dimension_semantics=("arbitrary",) * n_axes,
                                vmem_limit_bytes=VMEM_LIMIT_BYTES)


def _sigmoid(x):
    return 1.0 / (1.0 + jnp.exp(-x))


def _gelu_tanh(x):
    return 0.5 * x * (1.0 + jnp.tanh(math.sqrt(2.0 / math.pi) * (x + 0.044715 * (x * x * x))))


def _softplus(x):
    return jnp.maximum(x, 0.0) + jnp.log(1.0 + jnp.exp(-jnp.abs(x)))


def _dot(a, b):
    return jnp.dot(a, b, preferred_element_type=F32)


def _dot_exact(a, b):
    return jnp.dot(a, b, preferred_element_type=F32, precision=lax.Precision.HIGHEST)


def _head_sum_matrix():
    idx = np.arange(GROUP_W) // HEAD_DIM
    return jnp.asarray((idx[:, None] == idx[None, :]).astype(np.float32))


def _mod_body(c_ref, w_ref, b_ref, o_ref):
    c = c_ref[...]
    s = (c * _sigmoid(c)).astype(BF16)
    o_ref[...] = _dot(s, w_ref[...].astype(BF16)) + b_ref[...]


def _modulation(cvec, w, b):
    rows, d = cvec.shape
    n = w.shape[1]
    tn = 512
    return pl.pallas_call(
        _mod_body,
        grid=(n // tn,),
        in_specs=[pl.BlockSpec((rows, d), lambda j: (0, 0)),
                  pl.BlockSpec((d, tn), lambda j: (0, j)),
                  pl.BlockSpec((1, tn), lambda j: (0, j))],
        out_specs=pl.BlockSpec((rows, tn), lambda j: (0, j)),
        out_shape=jax.ShapeDtypeStruct((rows, n), F32),
        compiler_params=_cparams(1),
        name="adaln_mod",
    )(cvec, w, b.reshape(1, n))


def _in_body(x_ref, g_ref, mod_ref, w_ref, za_ref, zb_ref, zc_ref, zd_ref):
    tt, nb, d = x_ref.shape
    x = x_ref[...]
    ms = jnp.mean(x * x, axis=-1, keepdims=True)
    y = x * lax.rsqrt(ms + NORM_EPS) * g_ref[...]
    h = y * (1.0 + mod_ref[1]) + mod_ref[0]
    hb = h.reshape(tt * nb, d).astype(BF16)
    c0 = 0
    for ref in (za_ref, zb_ref, zc_ref, zd_ref):
        w = ref.shape[-1]
        ref[...] = _dot(hb, w_ref[:, c0:c0 + w]).reshape(tt, nb, w)
        c0 += w


def _in_proj(xs, g, modv, w_p, n_ctx_tiles):
    lt, nb, d = xs.shape
    tt = ROW_TILE_T
    widths = (ZA_W, RET_COLS, LRU_COLS, GROUP_W)
    seg = lambda i: jnp.where(i >= n_ctx_tiles, 1, 0)
    return pl.pallas_call(
        _in_body,
        grid=(lt // tt,),
        in_specs=[pl.BlockSpec((tt, nb, d), lambda i: (i, 0, 0)),
                  pl.BlockSpec((1, d), lambda i: (0, 0)),
                  pl.BlockSpec((None, N_MOD, nb, d), lambda i: (seg(i), 0, 0, 0)),
                  pl.BlockSpec(w_p.shape, lambda i: (0, 0))],
        out_specs=[pl.BlockSpec((tt, nb, w), lambda i: (i, 0, 0)) for w in widths],
        out_shape=[jax.ShapeDtypeStruct((lt, nb, w), F32) for w in widths],
        compiler_params=_cparams(1),
        name="in_proj",
    )(xs, g.reshape(1, d), modv, w_p)


def _rwkv_feat_body(n_ctx_tiles, n_tiles,
                    z_ref, zp_ref, zn_ref, mu_ref, gup_ref, wup_ref, aup_ref, vec_ref, hs_ref,
                    r_ref, k_ref, v_ref, g_ref, av_ref, w_ref, kd_ref, bv_ref):
    i = pl.program_id(0)
    tt, nb, zw = z_ref.shape
    z0 = z_ref[...]
    has_prev = jnp.logical_and(i != 0, i != n_ctx_tiles)
    has_next = jnp.logical_and(i != n_ctx_tiles - 1, i != n_tiles - 1)
    hp = jnp.where(has_prev, zp_ref[...], 0.0)
    hn = jnp.where(has_next, zn_ref[...], 0.0)
    zp = jnp.concatenate([hp, z0[:-1]], axis=0)
    zn = jnp.concatenate([z0[1:], hn], axis=0)
    z = z0 + mu_ref[0] * (zp - z0) + mu_ref[1] * (zn - z0)
    z = z.reshape(tt * nb, zw)
    gw = GROUP_W
    r = z[:, 0:gw]
    k = z[:, gw:2 * gw]
    v = z[:, 2 * gw:3 * gw]
    g = _dot(_sigmoid(z[:, 3 * gw:3 * gw + 128]).astype(BF16), gup_ref[...])
    w_low = jnp.tanh(z[:, 3 * gw + 128:3 * gw + 256]).astype(BF16)
    a_low = z[:, 3 * gw + 256:3 * gw + 384].astype(BF16)
    k_k = vec_ref[0:1, :]
    k_a = vec_ref[1:2, :]
    kk = k * k_k
    ss = _dot_exact(kk * kk, hs_ref[...])
    kk = kk / jnp.maximum(jnp.sqrt(ss), 1e-12)
    shp = (tt, nb, gw)
    r_ref[...] = r.reshape(shp)
    k_ref[...] = k.reshape(shp)
    v_ref[...] = v.reshape(shp)
    g_ref[...] = g.reshape(shp)
    av_ref[...] = (-kk).reshape(shp)
    for d in range(2):
        w0 = vec_ref[2 + d:3 + d, :]
        a0 = vec_ref[4 + d:5 + d, :]
        w_log = -_softplus(-(w0 + _dot(w_low, wup_ref[d]))) - 0.5
        w_ref[d] = jnp.exp(-jnp.exp(w_log)).reshape(shp)
        a = _sigmoid(a0 + _dot(a_low, aup_ref[d]))
        kd_ref[d] = (k * (1.0 + (a - 1.0) * k_a)).reshape(shp)
        bv_ref[d] = (kk * a).reshape(shp)


def _rwkv_features(za, mu_p, gup_p, wup_p, aup_p, vecs, hs, n_ctx_tiles):
    lt, nb, zw = za.shape
    tt = ROW_TILE_T
    n_tiles = lt // tt
    gw = GROUP_W
    one = lambda shape: jax.ShapeDtypeStruct(shape, F32)
    full = lambda a: pl.BlockSpec(a.shape, lambda i: (0,) * a.ndim)
    o1 = pl.BlockSpec((tt, nb, gw), lambda i: (i, 0, 0))
    o2 = pl.BlockSpec((2, tt, nb, gw), lambda i: (0, i, 0, 0))
    return pl.pallas_call(
        functools.partial(_rwkv_feat_body, n_ctx_tiles, n_tiles),
        grid=(n_tiles,),
        in_specs=[pl.BlockSpec((tt, nb, zw), lambda i: (i, 0, 0)),
                  pl.BlockSpec((1, nb, zw), lambda i: (jnp.maximum(i * tt - 1, 0), 0, 0)),
                  pl.BlockSpec((1, nb, zw), lambda i: (jnp.minimum((i + 1) * tt, lt - 1), 0, 0)),
                  full(mu_p), full(gup_p), full(wup_p), full(aup_p), full(vecs), full(hs)],
        out_specs=[o1, o1, o1, o1, o1, o2, o2, o2],
        out_shape=[one((lt, nb, gw))] * 5 + [one((2, lt, nb, gw))] * 3,
        compiler_params=_cparams(1),
        name="rwkv_features",
    )(za, za, za, mu_p, gup_p, wup_p, aup_p, vecs, hs)


def _rwkv_scan_body(r_ref, a_ref, w_ref, kd_ref, b_ref, v_ref, y_ref, s_ref, ex_ref):
    d = pl.program_id(0)
    j = pl.program_id(1)
    tc = r_ref.shape[0]
    n_k = HEAD_DIM
    n_acc = 4
    v_lo = s_ref.shape[-1] // r_ref.shape[-1]

    @pl.when(j == 0)
    def _():
        s_ref[...] = jnp.zeros_like(s_ref)

    for idx, ref in enumerate((r_ref, a_ref, w_ref, kd_ref, b_ref)):
        x = ref[...]
        reps = 1
        while reps < v_lo:
            x = jnp.concatenate([x, x], axis=-1)
            reps *= 2
        ex_ref[idx] = x
    r_x, a_x, w_x, kd_x, b_x = (ex_ref.at[idx] for idx in range(5))

    def step(i, carry):
        t = jnp.where(d == 0, i, tc - 1 - i)
        vv = v_ref[t]
        acc = [None] * n_acc
        for kk in range(n_k):
            term = s_ref[kk] * a_x[t, pl.ds(kk, 1), :]
            acc[kk % n_acc] = term if acc[kk % n_acc] is None else acc[kk % n_acc] + term
        sa = (acc[0] + acc[1]) + (acc[2] + acc[3])
        acc = [None] * n_acc
        for kk in range(n_k):
            s_new = (s_ref[kk] * w_x[t, pl.ds(kk, 1), :] + sa * b_x[t, pl.ds(kk, 1), :]
                     + vv * kd_x[t, pl.ds(kk, 1), :])
            s_ref[kk] = s_new
            term = s_new * r_x[t, pl.ds(kk, 1), :]
            acc[kk % n_acc] = term if acc[kk % n_acc] is None else acc[kk % n_acc] + term
        y_ref[t] = (acc[0] + acc[1]) + (acc[2] + acc[3])
        return carry

    lax.fori_loop(0, tc, step, 0)


def _dir_chunk(d, j, n_ctx, n_all):
    bwd = jnp.where(j < n_ctx, n_ctx - 1 - j, n_all - 1 - (j - n_ctx))
    return jnp.where(d == 0, j, bwd)


def _rwkv_scan(r_e, a_e, w_e, kd_e, b_e, v_e, l_ctx):
    lt, n_k, n_bh = r_e.shape
    _, v_hi, lanes = v_e.shape
    tc = RWKV_CHUNK
    n_all, n_ctx = lt // tc, l_ctx // tc
    cm = lambda d, j: _dir_chunk(d, j, n_ctx, n_all)
    shared = pl.BlockSpec((tc, n_k, n_bh), lambda d, j: (cm(d, j), 0, 0))
    per_dir = pl.BlockSpec((None, tc, n_k, n_bh), lambda d, j: (d, cm(d, j), 0, 0))
    return pl.pallas_call(
        _rwkv_scan_body,
        grid=(2, n_all),
        in_specs=[shared, shared, per_dir, per_dir, per_dir,
                  pl.BlockSpec((tc, v_hi, lanes), lambda d, j: (cm(d, j), 0, 0))],
        out_specs=pl.BlockSpec((None, tc, v_hi, lanes), lambda d, j: (d, cm(d, j), 0, 0)),
        out_shape=jax.ShapeDtypeStruct((2, lt, v_hi, lanes), F32),
        scratch_shapes=[pltpu.VMEM((n_k, v_hi, lanes), F32), pltpu.VMEM((5, tc, n_k, lanes), F32)],
        compiler_params=_cparams(2),
        name="rwkv_scan",
    )(r_e, a_e, w_e, kd_e, b_e, v_e)


def _rwkv_expand_k(x):
    lead = x.shape[:-2]
    nb = x.shape[-2]
    x = x.reshape(lead + (nb, N_HEADS, HEAD_DIM))
    return jnp.moveaxis(x, -1, -3).reshape(lead + (HEAD_DIM, nb * N_HEADS))


def _rwkv_expand_v(x, v_lo):
    lt, nb, _ = x.shape
    v_hi = HEAD_DIM // v_lo
    x = x.reshape(lt, nb, N_HEADS, v_hi, v_lo)
    return jnp.transpose(x, (0, 3, 4, 1, 2)).reshape(lt, v_hi, v_lo * nb * N_HEADS)


def _rwkv_collapse_v(y, nb, v_lo):
    _, lt, v_hi, _ = y.shape
    y = y.reshape(2, lt, v_hi, v_lo, nb, N_HEADS)
    return jnp.transpose(y, (0, 1, 4, 5, 2, 3)).reshape(2, lt, nb, GROUP_W)


def _ret_body(z_ref, cos_ref, sin_ref, dec_ref, rd_ref, wd_ref, cd_ref, hm_ref, o_ref, s_ref):
    j = pl.program_id(2)
    gw = GROUP_W
    half = gw // 2

    @pl.when(j == 0)
    def _():
        s_ref[...] = jnp.zeros_like(s_ref)

    cos = cos_ref[...]
    sin = sin_ref[...]

    def rope(x):
        x1, x2 = x[:, :half], x[:, half:]
        return jnp.concatenate([x1 * cos - x2 * sin, x1 * sin + x2 * cos], axis=-1)

    q = rope(z_ref[:, 0:gw]) * (HEAD_DIM ** -0.5)
    k = rope(z_ref[:, gw:2 * gw])
    v = z_ref[:, 2 * gw:3 * gw]
    vb = v.astype(BF16)
    k_t = k.T
    k_tb = k_t.astype(BF16)
    inter = _dot(q.astype(BF16), s_ref[...].astype(BF16)) * rd_ref[...]
    intra = jnp.zeros_like(inter)
    for h in range(N_HEADS):
        qm = (q * hm_ref[h:h + 1, :]).astype(BF16)
        att = _dot(qm, k_tb) * dec_ref[h]
        vm = (v * hm_ref[N_HEADS + h:N_HEADS + h + 1, :]).astype(BF16)
        intra = intra + _dot(att.astype(BF16), vm)
    o_ref[...] = intra + inter
    kv = _dot((k_t * wd_ref[...]).astype(BF16), vb)
    s_ref[...] = s_ref[...] * cd_ref[0] + kv * cd_ref[1]


def _retention(zb_bm, tabs, l_ctx):
    nb, lt, _ = zb_bm.shape
    c = RET_CHUNK
    n_all, n_ctx = lt // c, l_ctx // c
    gw = GROUP_W
    cos, sin, dec, rd, wd, cd, hm = tabs
    cm = lambda d, j: _dir_chunk(d, j, n_ctx, n_all)
    return pl.pallas_call(
        _ret_body,
        grid=(nb, 2, n_all),
        in_specs=[pl.BlockSpec((None, c, RET_COLS), lambda b, d, j: (b, cm(d, j), 0)),
                  pl.BlockSpec((c, gw // 2), lambda b, d, j: (cm(d, j), 0)),
                  pl.BlockSpec((c, gw // 2), lambda b, d, j: (cm(d, j), 0)),
                  pl.BlockSpec((None, N_HEADS, c, c), lambda b, d, j: (d, 0, 0, 0)),
                  pl.BlockSpec((None, c, gw), lambda b, d, j: (d, 0, 0)),
                  pl.BlockSpec((None, gw, c), lambda b, d, j: (d, 0, 0)),
                  pl.BlockSpec((None, 2, gw, gw), lambda b, d, j: (d, 0, 0, 0)),
                  pl.BlockSpec((2 * N_HEADS, gw), lambda b, d, j: (0, 0))],
        out_specs=pl.BlockSpec((None, None, c, gw), lambda b, d, j: (d, b, cm(d, j), 0)),
        out_shape=jax.ShapeDtypeStruct((2, nb, lt, gw), F32),
        scratch_shapes=[pltpu.VMEM((gw, gw), F32)],
        compiler_params=_cparams(3),
        name="retention",
    )(zb_bm, cos, sin, dec, rd, wd, cd, hm)


def _ret_perm():
    new = np.arange(GROUP_W)
    half_id, rem = new // 128, new % 128
    h, i = rem // 32, rem % 32
    return h * HEAD_DIM + half_id * 32 + i


def _ret_tables(lt):
    c = RET_CHUNK
    half = HEAD_DIM // 2
    freqs = ROPE_BASE ** (-jnp.arange(half, dtype=F32) / half)
    ang = jnp.arange(lt, dtype=F32)[:, None] * freqs[None]
    cos = jnp.tile(jnp.cos(ang), (1, N_HEADS))
    sin = jnp.tile(jnp.sin(ang), (1, N_HEADS))
    idx = jnp.arange(c, dtype=F32)
    perm = _ret_perm()
    head_of_qk = jnp.asarray(perm // HEAD_DIM)
    head_of_v = jnp.arange(GROUP_W) // HEAD_DIM
    dec, rd, wd, cd = [], [], [], []
    for offset, rev in ((RET_FWD_OFFSET, False), (RET_BWD_OFFSET, True)):
        lg = jnp.log1p(-jnp.exp2(-(offset + jnp.arange(N_HEADS, dtype=F32))))
        diff = idx[:, None] - idx[None, :]
        if rev:
            mask, dist = diff < 0, -diff
            read_pow, write_pow = c - idx, idx
        else:
            mask, dist = diff >= 0, diff
            read_pow, write_pow = idx + 1.0, c - 1.0 - idx
        dec.append(jnp.where(mask[None], jnp.exp(lg[:, None, None] * jnp.where(mask, dist, 0.0)[None]), 0.0))
        rd.append(jnp.exp(lg[head_of_v][None, :] * read_pow[:, None]))
        wd.append(jnp.exp(lg[head_of_qk][:, None] * write_pow[None, :]))
        same = (head_of_qk[:, None] == head_of_v[None, :]).astype(F32)
        cd.append(jnp.stack([same * jnp.exp(lg * c)[head_of_v][None, :], same]))
    hm_q = (head_of_qk[None, :] == jnp.arange(N_HEADS)[:, None]).astype(F32)
    hm_v = (head_of_v[None, :] == jnp.arange(N_HEADS)[:, None]).astype(F32)
    return cos, sin, jnp.stack(dec), jnp.stack(rd), jnp.stack(wd), jnp.stack(cd), jnp.concatenate([hm_q, hm_v])


def _lru_body(n_ctx, n_all, x_ref, xp_ref, xn_ref, cw_ref, wa_ref, wx_ref, vec_ref, o_ref,
              a_s, b_s, h_s):
    d = pl.program_id(0)
    j = pl.program_id(1)
    tc, nb, gw = x_ref.shape
    ch = _dir_chunk(d, j, n_ctx, n_all)

    @pl.when(j == 0)
    def _():
        h_s[...] = jnp.zeros_like(h_s)

    has_prev = jnp.logical_and(ch != 0, ch != n_ctx)
    has_next = jnp.logical_and(ch != n_ctx - 1, ch != n_all - 1)
    xe = jnp.concatenate([jnp.where(has_prev, xp_ref[...], 0.0), x_ref[...],
                          jnp.where(has_next, xn_ref[...], 0.0)], axis=0)
    xc = cw_ref[LRU_CONV_W:LRU_CONV_W + 1, :]
    for tap in range(LRU_CONV_W):
        xc = xc + cw_ref[tap:tap + 1, :] * xe[tap:tap + tc]
    xc = xc.reshape(tc * nb, gw)
    xb = xc.astype(BF16)
    r = _sigmoid(_dot(xb, wa_ref[...]) + vec_ref[0:1, :])
    gi = _sigmoid(_dot(xb, wx_ref[...]) + vec_ref[1:2, :])
    log_a = -LRU_C * r * _softplus(-vec_ref[2:3, :])
    th = jnp.tanh(log_a)
    one_minus_a2 = 2.0 * th / (th - 1.0)
    a_s[...] = jnp.exp(log_a).reshape(tc, nb, gw)
    b_s[...] = (jnp.sqrt(one_minus_a2) * (gi * xc)).reshape(tc, nb, gw)

    def step(i, h):
        t = jnp.where(d == 0, i, tc - 1 - i)
        h = a_s[t] * h + b_s[t]
        o_ref[t] = h
        return h

    h_s[...] = lax.fori_loop(0, tc, step, h_s[...])


def _lru(zc, cw, wa_bd, wx_bd, vecs, l_ctx):
    lt, nb, _ = zc.shape
    gw = GROUP_W
    tc = LRU_CHUNK
    n_all, n_ctx = lt // tc, l_ctx // tc
    cm = lambda d, j: _dir_chunk(d, j, n_ctx, n_all)
    return pl.pallas_call(
        functools.partial(_lru_body, n_ctx, n_all),
        grid=(2, n_all),
        in_specs=[pl.BlockSpec((tc, nb, gw), lambda d, j: (cm(d, j), 0, 0)),
                  pl.BlockSpec((2, nb, gw), lambda d, j: (jnp.maximum(cm(d, j) * (tc // 2) - 1, 0), 0, 0)),
                  pl.BlockSpec((1, nb, gw), lambda d, j: (jnp.minimum((cm(d, j) + 1) * tc, lt - 1), 0, 0)),
                  pl.BlockSpec(cw.shape, lambda d, j: (0, 0)),
                  pl.BlockSpec((None, gw, gw), lambda d, j: (d, 0, 0)),
                  pl.BlockSpec((None, gw, gw), lambda d, j: (d, 0, 0)),
                  pl.BlockSpec((None, 3, gw), lambda d, j: (d, 0, 0))],
        out_specs=pl.BlockSpec((None, tc, nb, gw), lambda d, j: (d, cm(d, j), 0, 0)),
        out_shape=jax.ShapeDtypeStruct((2, lt, nb, gw), F32),
        scratch_shapes=[pltpu.VMEM((tc, nb, gw), F32), pltpu.VMEM((tc, nb, gw), F32),
                        pltpu.VMEM((nb, gw), F32)],
        compiler_params=_cparams(2),
        name="rglru",
    )(zc, zc, zc, cw, wa_bd, wx_bd, vecs)


def _block_diag(w):
    nblk, n = w.shape[-3], w.shape[-1]
    eye = jnp.eye(nblk, dtype=w.dtype)
    out = w[..., :, :, None, :] * eye[:, None, :, None]
    return out.reshape(w.shape[:-3] + (nblk * n, nblk * n))


def _s5_body(u_ref, bm_ref, lam_ref, cm_ref, y_ref, h_s, st_s):
    d = pl.program_id(0)
    j = pl.program_id(1)
    tc, nb, gw = u_ref.shape
    n = S5_N

    @pl.when(j == 0)
    def _():
        st_s[...] = jnp.zeros_like(st_s)

    u = u_ref[...].reshape(tc * nb, gw).astype(BF16)
    h_s[...] = _dot(u, bm_ref[...]).reshape(tc, nb, 2 * n)
    lr = lam_ref[0:1, :]
    li = lam_ref[1:2, :]

    def step(i, carry):
        hr, hi = carry
        t = jnp.where(d == 0, i, tc - 1 - i)
        bu = h_s[t]
        nr = lr * hr - li * hi + bu[:, :n]
        ni = lr * hi + li * hr + bu[:, n:]
        h_s[t] = jnp.concatenate([nr, ni], axis=-1)
        return nr, ni

    hr, hi = lax.fori_loop(0, tc, step, (st_s[0], st_s[1]))
    st_s[0] = hr
    st_s[1] = hi
    hh = h_s[...].reshape(tc * nb, 2 * n).astype(BF16)
    y_ref[...] = _dot(hh, cm_ref[...]).reshape(tc, nb, gw)


def _s5(u, bmat, lam, cmat, l_ctx):
    lt, nb, gw = u.shape
    tc = S5_CHUNK
    n_all, n_ctx = lt // tc, l_ctx // tc
    cm = lambda d, j: _dir_chunk(d, j, n_ctx, n_all)
    return pl.pallas_call(
        _s5_body,
        grid=(2, n_all),
        in_specs=[pl.BlockSpec((tc, nb, gw), lambda d, j: (cm(d, j), 0, 0)),
                  pl.BlockSpec((None, gw, 2 * S5_N), lambda d, j: (d, 0, 0)),
                  pl.BlockSpec((None, 2, S5_N), lambda d, j: (d, 0, 0)),
                  pl.BlockSpec((None, 2 * S5_N, gw), lambda d, j: (d, 0, 0))],
        out_specs=pl.BlockSpec((None, tc, nb, gw), lambda d, j: (d, cm(d, j), 0, 0)),
        out_shape=jax.ShapeDtypeStruct((2, lt, nb, gw), F32),
        scratch_shapes=[pltpu.VMEM((tc, nb, 2 * S5_N), F32), pltpu.VMEM((2, nb, S5_N), F32)],
        compiler_params=_cparams(2),
        name="s5",
    )(u, bmat, lam, cmat)


def _s5_params(a_re, a_im, log_dt, b_re, b_im, c_re, c_im):
    dt = jnp.exp(log_dt)[..., None]
    er = jnp.exp(a_re * dt)
    lbr, lbi = er * jnp.cos(a_im * dt), er * jnp.sin(a_im * dt)
    den = a_re * a_re + a_im * a_im
    nr, ni = lbr - 1.0, lbi
    fr = (nr * a_re + ni * a_im) / den
    fi = (ni * a_re - nr * a_im) / den
    bbr = fr[..., None] * b_re - fi[..., None] * b_im
    bbi = fr[..., None] * b_im + fi[..., None] * b_re
    eye = jnp.eye(S5_GROUPS, dtype=F32)

    def in_map(bb):
        m = jnp.einsum('dgpc,gh->dgchp', bb, eye)
        return m.reshape(2, GROUP_W, S5_N)

    def out_map(cc):
        m = jnp.einsum('dgcp,gh->dgphc', cc, eye)
        return m.reshape(2, S5_N, GROUP_W)

    bmat = jnp.concatenate([in_map(bbr), in_map(bbi)], axis=-1).astype(BF16)
    cmat = jnp.concatenate([out_map(c_re), -out_map(c_im)], axis=-2).astype(BF16)
    lam = jnp.stack([lbr.reshape(2, S5_N), lbi.reshape(2, S5_N)], axis=1)
    return bmat, lam, cmat


def _to_colmajor(z, l_ctx):
    lt, nb, ch = z.shape
    rows = (lt - l_ctx) // GRID_W
    lat = z[l_ctx:].reshape(rows, GRID_W, nb, ch).transpose(1, 0, 2, 3).reshape(lt - l_ctx, nb, ch)
    return jnp.concatenate([z[:l_ctx], lat], axis=0)


def _from_colmajor(y, l_ctx):
    lt = y.shape[-3]
    nb, ch = y.shape[-2:]
    rows = (lt - l_ctx) // GRID_W
    lead = y.shape[:-3]
    lat = y[..., l_ctx:, :, :].reshape(lead + (GRID_W, rows, nb, ch))
    lat = jnp.swapaxes(lat, -4, -3).reshape(lead + (lt - l_ctx, nb, ch))
    return jnp.concatenate([y[..., :l_ctx, :, :], lat], axis=-3)


def _out_body(x_ref, mod_ref, ya_ref, r_ref, k_ref, v_ref, g_ref, ob_ref, gb_ref, hc_ref, gc_ref,
              yd_ref, ud_ref, vec_ref, hs_ref, gluw_ref, glub_ref, wo_ref, n2_ref, x1_ref, h2_ref):
    tt, nb, d = x_ref.shape
    gw = GROUP_W
    rows = tt * nb
    hs = hs_ref[...]
    two = lambda ref: (ref[0] + ref[1]).reshape(rows, gw)
    flat = lambda ref: ref[...].reshape(rows, gw)

    def head_norm(y, eps):
        mu = _dot_exact(y, hs) * (1.0 / HEAD_DIM)
        dlt = y - mu
        var = _dot_exact(dlt * dlt, hs) * (1.0 / HEAD_DIM)
        return dlt * lax.rsqrt(var + eps)

    r_k, ln_g, ln_b = vec_ref[0:1, :], vec_ref[1:2, :], vec_ref[2:3, :]
    r, k, v = flat(r_ref), flat(k_ref), flat(v_ref)
    yn = head_norm(two(ya_ref), RWKV_GN_EPS) * ln_g + ln_b
    bonus = _dot_exact(r * k * r_k, hs) * v
    mix_a = (yn + bonus) * flat(g_ref)
    gb = flat(gb_ref)
    mix_b = gb * _sigmoid(gb) * (head_norm(two(ob_ref), RET_GN_EPS) * vec_ref[3:4, :])
    mix_c = _gelu_tanh(flat(gc_ref)) * two(hc_ref)
    yd = vec_ref[4:5, :] * flat(ud_ref) + two(yd_ref)
    glu = _dot(_gelu_tanh(yd).astype(BF16), gluw_ref[...]) + glub_ref[...]
    mix_d = glu[:, :gw] * _sigmoid(glu[:, gw:])

    mix = _dot(mix_a.astype(BF16), wo_ref[0:gw, :])
    mix = mix + _dot(mix_b.astype(BF16), wo_ref[gw:2 * gw, :])
    mix = mix + _dot(mix_c.astype(BF16), wo_ref[2 * gw:3 * gw, :])
    mix = mix + _dot(mix_d.astype(BF16), wo_ref[3 * gw:4 * gw, :])
    x1 = x_ref[...] + mod_ref[2] * mix.reshape(tt, nb, d)
    x1_ref[...] = x1
    ms = jnp.mean(x1 * x1, axis=-1, keepdims=True)
    h2 = (x1 * lax.rsqrt(ms + NORM_EPS) * n2_ref[...]) * (1.0 + mod_ref[4]) + mod_ref[3]
    h2_ref[...] = h2.reshape(rows, d).T.astype(BF16)


def _out_proj(xs, modv, ya, feats, ob, zb, hc, zc, yd, zd, vecs, hs, gluw, glub, wo, n2, n_ctx_tiles, t0_tile):
    lt, nb, d = xs.shape
    tt = ROW_TILE_T
    gw = GROUP_W
    n_tiles = lt // tt - t0_tile
    seg = lambda i: jnp.where(i + t0_tile >= n_ctx_tiles, 1, 0)
    col = lambda c: pl.BlockSpec((tt, nb, gw), lambda i: (i + t0_tile, 0, c))
    two = pl.BlockSpec((2, tt, nb, gw), lambda i: (0, i + t0_tile, 0, 0))
    full = lambda a: pl.BlockSpec(a.shape, lambda i: (0,) * a.ndim)
    r, k, v, g = feats
    return pl.pallas_call(
        _out_body,
        grid=(n_tiles,),
        in_specs=[pl.BlockSpec((tt, nb, d), lambda i: (i + t0_tile, 0, 0)),
                  pl.BlockSpec((None, N_MOD, nb, d), lambda i: (seg(i), 0, 0, 0)),
                  two, col(0), col(0), col(0), col(0),
                  two, col(3), two, col(1), two, col(0),
                  full(vecs), full(hs), full(gluw), full(glub), full(wo), full(n2)],
        out_specs=[pl.BlockSpec((tt, nb, d), lambda i: (i + t0_tile, 0, 0)),
                   pl.BlockSpec((d, tt * nb), lambda i: (0, i + t0_tile))],
        out_shape=[jax.ShapeDtypeStruct((lt, nb, d), F32), jax.ShapeDtypeStruct((d, lt * nb), BF16)],
        compiler_params=_cparams(1),
        name="out_proj",
    )(xs, modv, ya, r, k, v, g, ob, zb, hc, zc, yd, zd, vecs, hs, gluw, glub, wo, n2)


PEER_NO_RANK = 127.0


def _top_values(s, n, with_rank=False):
    vals, cur = [], s
    rank = jnp.full(s.shape, PEER_NO_RANK, F32) if with_rank else None
    for k in range(n):
        m = jnp.max(cur, axis=0, keepdims=True)
        vals.append(m)
        hit = cur == m
        if with_rank:
            rank = jnp.where(hit, float(k), rank)
        cur = jnp.where(hit, NEG_INF, cur)
    return (vals, rank) if with_rank else vals


def _peer_route_body(h_ref, wq_ref, keys_ref, e1_ref, cnt_ref, r2_ref, w2_ref, pack_ref):
    hb = h_ref[...]
    for h in range(PEER_HEADS):
        st = []
        for p in range(2):
            c0 = (2 * h + p) * PEER_DKEY
            q = _dot(wq_ref[c0:c0 + PEER_DKEY, :], hb).astype(BF16)
            st.append(_dot(keys_ref[2 * h + p], q))
        s1, s2 = st
        n_top = PEER_TOPK + 1
        v1 = _top_values(s1, n_top)
        v2, rank2 = _top_values(s2, n_top, with_rank=True)
        pad7 = [jnp.full((7, s1.shape[1]), NEG_INF, F32)]
        v2m = jnp.concatenate(v2 + pad7, axis=0)
        v1_tail = jnp.concatenate(v1[8:] + pad7, axis=0)
        cand = jnp.concatenate([v1[0] + v2m] + [v1[a] + v2m[0:8] for a in range(1, 8)]
                               + [v1_tail + v2[0]], axis=0)
        top = _top_values(cand, n_top)
        tau = 0.5 * (top[PEER_TOPK - 1] + top[PEER_TOPK])
        cmax = top[0]
        z = jnp.sum(jnp.where(cand >= tau, jnp.exp(cand - cmax), 0.0), axis=0, keepdims=True)
        thr = tau - s1
        cnt = jnp.zeros_like(s1)
        for k in range(PEER_TOPK):
            cnt = cnt + jnp.where(v2[k] >= thr, 1.0, 0.0)
        e1_ref[:, h, :] = jnp.exp(s1 - v1[0])
        cnt_ref[:, h, :] = cnt
        pack_ref[0] = rank2.astype(BF16)
        pack_ref[1] = (jnp.exp(s2 - v2[0]) / z).astype(BF16)
        words = pack_ref.bitcast(jnp.uint32)
        r2_ref[h] = words[0]
        w2_ref[h] = words[1]


def _peer_route(h2t, wq, keys, t0_tile):
    d, t_all = h2t.shape
    tt = PEER_TOK_TILE
    nk = PEER_NKEYS
    by_i = pl.BlockSpec((nk, PEER_HEADS, tt), lambda i: (0, 0, i + t0_tile))
    by_h = pl.BlockSpec((PEER_HEADS, nk // 2, tt), lambda i: (0, 0, i + t0_tile))
    return pl.pallas_call(
        _peer_route_body,
        grid=(t_all // tt - t0_tile,),
        in_specs=[pl.BlockSpec((d, tt), lambda i: (0, i + t0_tile)),
                  pl.BlockSpec(wq.shape, lambda i: (0, 0)),
                  pl.BlockSpec(keys.shape, lambda i: (0, 0, 0))],
        out_specs=[by_i, by_i, by_h, by_h],
        out_shape=[jax.ShapeDtypeStruct((nk, PEER_HEADS, t_all), F32)] * 2
                  + [jax.ShapeDtypeStruct((PEER_HEADS, nk // 2, t_all), jnp.uint32)] * 2,
        scratch_shapes=[pltpu.VMEM((2, nk, tt), BF16)],
        compiler_params=_cparams(1),
        name="peer_route",
    )(h2t, wq, keys)


PEER_GATE_ROWS = 32
LANES = 128


BF16_ROWS = 16
SUBLANES = 8


def _peer_gate_block(p_ref, e1_ref, cnt_ref, r2_ref, w2_ref, bc_ref):
    nk = PEER_NKEYS
    n_sub = p_ref.shape[0] // nk
    rc = PEER_GATE_ROWS
    n_pv = rc // BF16_ROWS
    for c0 in range(0, p_ref.shape[1], LANES):
        cols = slice(c0, c0 + LANES)
        for ii in range(n_sub):
            for h in range(PEER_HEADS):
                bc_ref[0, ii, h] = jnp.broadcast_to(cnt_ref[ii, h:h + 1, cols], (BF16_ROWS, LANES)).astype(BF16)
                bc_ref[1, ii, h] = jnp.broadcast_to(e1_ref[ii, h:h + 1, cols], (BF16_ROWS, LANES)).astype(BF16)

        def chunk(q, carry, cols=cols):
            r0 = pl.multiple_of(q * rc, rc)
            gate = [[None] * n_pv for _ in range(n_sub)]
            for h in range(PEER_HEADS):
                rank = [pltpu.bitcast(r2_ref[h, q, SUBLANES * v:SUBLANES * (v + 1), cols], BF16)
                        for v in range(n_pv)]
                w2 = [pltpu.bitcast(w2_ref[h, q, SUBLANES * v:SUBLANES * (v + 1), cols], BF16)
                      for v in range(n_pv)]
                for ii in range(n_sub):
                    cnt = bc_ref[0, ii, h]
                    e1 = bc_ref[1, ii, h]
                    for v in range(n_pv):
                        term = jnp.where(rank[v] < cnt, w2[v] * e1, jnp.zeros_like(e1))
                        gate[ii][v] = term if gate[ii][v] is None else gate[ii][v] + term
            for ii in range(n_sub):
                for v in range(n_pv):
                    rows = pl.ds(ii * nk + r0 + BF16_ROWS * v, BF16_ROWS)
                    p_ref[rows, cols] = p_ref[rows, cols] * gate[ii][v]
            return carry

        lax.fori_loop(0, nk // rc, chunk, 0)


def _peer_ffn_body(h_ref, u_ref, vt_ref, e1_ref, cnt_ref, r2_ref, w2_ref, x_ref, mod_ref, o_ref,
                   acc_ref, p_ref, bc_ref):
    j = pl.program_id(1)

    @pl.when(j == 0)
    def _():
        acc_ref[...] = jnp.zeros_like(acc_ref)

    act = _dot(u_ref[...], h_ref[...])
    p_ref[...] = _gelu_tanh(act).astype(BF16)
    _peer_gate_block(p_ref, e1_ref, cnt_ref, r2_ref, w2_ref, bc_ref)
    acc_ref[...] += _dot(vt_ref[...], p_ref[...])

    @pl.when(j == pl.num_programs(1) - 1)
    def _():
        tt, d = x_ref.shape
        nb = mod_ref.shape[1]
        y = acc_ref[...].T.reshape(tt // nb, nb, d) * mod_ref[5]
        o_ref[...] = x_ref[...] + y.reshape(tt, d)


def _peer_ffn(h2t, u_b, vt_b, e1, cnt, r2, w2, x1, modv, n_ctx_tok_tiles, t0_tile):
    d, t_all = h2t.shape
    tt = PEER_TOK_TILE
    eb = PEER_EXP_BLOCK
    nk = PEER_NKEYS
    nb = modv.shape[2]
    n_e = u_b.shape[0] // eb
    n_tiles = t_all // tt - t0_tile
    seg = lambda i: jnp.where(i + t0_tile >= n_ctx_tok_tiles, 1, 0)
    by_i = pl.BlockSpec((eb // nk, PEER_HEADS, tt), lambda i, j: (j, 0, i + t0_tile))
    rc = PEER_GATE_ROWS
    by_h = pl.BlockSpec((PEER_HEADS, nk // rc, rc // 2, tt), lambda i, j: (0, 0, 0, i + t0_tile))
    r2, w2 = (a.reshape(PEER_HEADS, nk // rc, rc // 2, t_all) for a in (r2, w2))
    return pl.pallas_call(
        _peer_ffn_body,
        grid=(n_tiles, n_e),
        in_specs=[pl.BlockSpec((d, tt), lambda i, j: (0, i + t0_tile)),
                  pl.BlockSpec((eb, d), lambda i, j: (j, 0)),
                  pl.BlockSpec((d, eb), lambda i, j: (0, j)),
                  by_i, by_i, by_h, by_h,
                  pl.BlockSpec((tt, d), lambda i, j: (i + t0_tile, 0)),
                  pl.BlockSpec((None, N_MOD, nb, d), lambda i, j: (seg(i), 0, 0, 0))],
        out_specs=pl.BlockSpec((tt, d), lambda i, j: (i + t0_tile, 0)),
        out_shape=jax.ShapeDtypeStruct((t_all, d), F32),
        scratch_shapes=[pltpu.VMEM((d, tt), F32), pltpu.VMEM((eb, tt), BF16),
                        pltpu.VMEM((2, eb // nk, PEER_HEADS, BF16_ROWS, LANES), BF16)],
        compiler_params=_cparams(2),
        name="peer_ffn",
    )(h2t, u_b, vt_b, e1, cnt, r2, w2, x1, modv)


def _final_body(x_ref, g_ref, o_ref):
    x = x_ref[...]
    ms = jnp.mean(x * x, axis=-1, keepdims=True)
    o_ref[...] = x * lax.rsqrt(ms + NORM_EPS) * g_ref[...]


def _final_norm(x2d, g):
    t, d = x2d.shape
    tr = 512
    return pl.pallas_call(
        _final_body,
        grid=(t // tr,),
        in_specs=[pl.BlockSpec((tr, d), lambda i: (i, 0)), pl.BlockSpec((1, d), lambda i: (0, 0))],
        out_specs=pl.BlockSpec((tr, d), lambda i: (i, 0)),
        out_shape=jax.ShapeDtypeStruct((t, d), F32),
        compiler_params=_cparams(1),
        name="final_norm",
    )(x2d, g.reshape(1, d))


def _pad_cols(w, width):
    return jnp.pad(w, [(0, 0)] * (w.ndim - 1) + [(0, width - w.shape[-1])])


def _rwkv_slabs(w):
    gw3 = 3 * GROUP_W
    parts = [w[..., :gw3],
             _pad_cols(w[..., gw3:gw3 + 64], 128),
             _pad_cols(w[..., gw3 + 64:gw3 + 128], 128),
             _pad_cols(w[..., gw3 + 128:gw3 + 192], 128)]
    return jnp.concatenate(parts, axis=-1)


def _layer_weights(l, p):
    d = p['w_in'].shape[1]
    w_in = p['w_in'][l]
    wa_, wb_, wc_, wd_ = jnp.split(w_in, [RWKV_COLS, RWKV_COLS + RET_COLS, RWKV_COLS + RET_COLS + LRU_COLS], axis=-1)
    perm = _ret_perm()
    gw = GROUP_W
    wb_ = jnp.concatenate([wb_[:, 0:gw][:, perm], wb_[:, gw:2 * gw][:, perm], wb_[:, 2 * gw:]], axis=-1)
    out = {'w_in': jnp.concatenate([_rwkv_slabs(wa_), wb_, wc_, wd_], axis=-1).astype(BF16)}
    out['mu'] = _rwkv_slabs(p['rwkv_mu'][l])[:, None, :]
    out['gup'] = jnp.pad(p['rwkv_g_up'][l], ((0, 64), (0, 0))).astype(BF16)
    lora = RWKV_LORA
    out['wup'] = jnp.stack([jnp.pad(p['rwkv_w_up'][l, dd], ((dd * lora, 128 - (dd + 1) * lora), (0, 0)))
                            for dd in range(2)]).astype(BF16)
    out['aup'] = jnp.stack([jnp.pad(p['rwkv_a_up'][l, dd], ((dd * lora, 128 - (dd + 1) * lora), (0, 0)))
                            for dd in range(2)]).astype(BF16)
    out['rwkv_vecs'] = jnp.concatenate([p['rwkv_k_k'][l][None], p['rwkv_k_a'][l][None],
                                        p['rwkv_w0'][l], p['rwkv_a0'][l]], axis=0)
    out['cw'] = jnp.concatenate([p['lru_conv_w'][l], p['lru_conv_b'][l][None]], axis=0)
    out['wa'] = _block_diag(p['lru_wa'][l]).astype(BF16)
    out['wx'] = _block_diag(p['lru_wx'][l]).astype(BF16)
    out['lru_vecs'] = jnp.stack([p['lru_ba'][l], p['lru_bx'][l], p['lru_lambda'][l]], axis=1)
    out['s5'] = _s5_params(p['s5_a_re'][l], p['s5_a_im'][l], p['s5_log_dt'][l], p['s5_b_re'][l],
                           p['s5_b_im'][l], p['s5_c_re'][l], p['s5_c_im'][l])
    out['out_vecs'] = jnp.stack([p['rwkv_r_k'][l].reshape(-1), p['rwkv_ln_g'][l], p['rwkv_ln_b'][l],
                                 p['ret_gn_g'][l], p['s5_d'][l]], axis=0)
    out['gluw'] = p['s5_glu_w'][l].astype(BF16)
    out['glub'] = p['s5_glu_b'][l][None]
    out['wo'] = p['w_out'][l].astype(BF16)
    out['n2'] = p['norm2_g'][l][None]
    out['wq'] = p['peer_wq'][l].T.astype(BF16)
    out['keys'] = p['peer_keys'][l].reshape(PEER_HEADS * 2, PEER_NKEYS, PEER_DKEY).astype(BF16)
    out['u'] = p['peer_u'][l].astype(BF16)
    out['vt'] = p['peer_v'][l].T.astype(BF16)
    return out


def kernel(x, c, ctx, c_ctx, norm1_g, norm2_g, ada_w, ada_b, w_in, w_out, rwkv_mu, rwkv_w0, rwkv_w_up, rwkv_a0, rwkv_a_up, rwkv_g_up, rwkv_k_k, rwkv_k_a, rwkv_r_k, rwkv_ln_g, rwkv_ln_b, ret_gn_g, lru_conv_w, lru_conv_b, lru_wa, lru_ba, lru_wx, lru_bx, lru_lambda, s5_a_re, s5_a_im, s5_log_dt, s5_b_re, s5_b_im, s5_c_re, s5_c_im, s5_d, s5_glu_w, s5_glu_b, peer_wq, peer_keys, peer_u, peer_v, final_norm_g):
    p = dict(w_in=w_in, w_out=w_out, norm2_g=norm2_g, rwkv_mu=rwkv_mu, rwkv_w0=rwkv_w0, rwkv_w_up=rwkv_w_up,
             rwkv_a0=rwkv_a0, rwkv_a_up=rwkv_a_up, rwkv_g_up=rwkv_g_up, rwkv_k_k=rwkv_k_k, rwkv_k_a=rwkv_k_a,
             rwkv_r_k=rwkv_r_k, rwkv_ln_g=rwkv_ln_g, rwkv_ln_b=rwkv_ln_b, ret_gn_g=ret_gn_g,
             lru_conv_w=lru_conv_w, lru_conv_b=lru_conv_b, lru_wa=lru_wa, lru_ba=lru_ba, lru_wx=lru_wx,
             lru_bx=lru_bx, lru_lambda=lru_lambda, s5_a_re=s5_a_re, s5_a_im=s5_a_im, s5_log_dt=s5_log_dt,
             s5_b_re=s5_b_re, s5_b_im=s5_b_im, s5_c_re=s5_c_re, s5_c_im=s5_c_im, s5_d=s5_d,
             s5_glu_w=s5_glu_w, s5_glu_b=s5_glu_b, peer_wq=peer_wq, peer_keys=peer_keys, peer_u=peer_u,
             peer_v=peer_v)
    nb, l_lat, d = x.shape
    l_ctx = ctx.shape[1]
    lt = l_ctx + l_lat
    depth = w_in.shape[0]
    assert 128 % (nb * N_HEADS) == 0 and l_ctx % 128 == 0 and l_lat % 128 == 0
    assert (l_ctx * nb) % PEER_TOK_TILE == 0 and (l_lat * nb) % PEER_TOK_TILE == 0
    v_lo = 128 // (nb * N_HEADS)
    n_ctx_tiles = l_ctx // ROW_TILE_T
    n_ctx_tok_tiles = l_ctx * nb // PEER_TOK_TILE
    hs = _head_sum_matrix()
    ret_tabs = _ret_tables(lt)

    xs = jnp.transpose(jnp.concatenate([ctx, x], axis=1), (1, 0, 2))
    mod_rows = 8 * ((nb + 1 + 7) // 8)
    cvec = jnp.zeros((mod_rows, d), F32).at[:nb].set(c).at[nb].set(c_ctx)

    for l in range(depth):
        wl = _layer_weights(l, p)
        last = l == depth - 1
        mod = _modulation(cvec, ada_w[l], ada_b[l])
        mod_lat = mod[:nb].reshape(nb, N_MOD, d).transpose(1, 0, 2)
        mod_ctx = jnp.broadcast_to(mod[nb].reshape(N_MOD, 1, d), (N_MOD, nb, d))
        modv = jnp.stack([mod_ctx, mod_lat])

        za, zb, zc, zd = _in_proj(xs, norm1_g[l], modv, wl['w_in'], n_ctx_tiles)

        r, k, v, g, av, w_dec, kd, bv = _rwkv_features(za, wl['mu'], wl['gup'], wl['wup'], wl['aup'],
                                                       wl['rwkv_vecs'], hs, n_ctx_tiles)
        ex = _rwkv_expand_k
        ya = _rwkv_scan(ex(r), ex(av), ex(w_dec), ex(kd), ex(bv), _rwkv_expand_v(v, v_lo), l_ctx)
        ya = _rwkv_collapse_v(ya, nb, v_lo)
        ob = _retention(jnp.transpose(zb, (1, 0, 2)), ret_tabs, l_ctx)
        ob = jnp.transpose(ob, (0, 2, 1, 3))
        hc = _lru(zc, wl['cw'], wl['wa'], wl['wx'], wl['lru_vecs'], l_ctx)
        yd = _from_colmajor(_s5(_to_colmajor(zd, l_ctx), *wl['s5'], l_ctx), l_ctx)

        t0 = n_ctx_tiles if last else 0
        x1, h2 = _out_proj(xs, modv, ya, (r, k, v, g), ob, zb, hc, zc, yd, zd, wl['out_vecs'], hs,
                           wl['gluw'], wl['glub'], wl['wo'], wl['n2'], n_ctx_tiles, t0)
        t0p = n_ctx_tok_tiles if last else 0
        e1, cnt, r2, w2 = _peer_route(h2, wl['wq'], wl['keys'], t0p)
        x2 = _peer_ffn(h2, wl['u'], wl['vt'], e1, cnt, r2, w2, x1.reshape(lt * nb, d), modv,
                       n_ctx_tok_tiles, t0p)
        xs = x2.reshape(lt, nb, d)

    out = _final_norm(xs[l_ctx:].reshape(l_lat * nb, d), final_norm_g)
    return jnp.transpose(out.reshape(l_lat, nb, d), (1, 0, 2))
```

```python
import functools
import math

import jax
import jax.numpy as jnp
import numpy as np
from jax import lax
from jax.experimental import pallas as pl
from jax.experimental.pallas import tpu as pltpu

F32 = jnp.float32
BF16 = jnp.bfloat16

NORM_EPS = 1e-6
N_MOD = 6
HEAD_DIM = 64
N_HEADS = 4
GROUP_W = N_HEADS * HEAD_DIM
GRID_W = 64

RWKV_GATE_LORA = 64
RWKV_LORA = 32
RWKV_GN_EPS = 64e-5
RWKV_COLS = 3 * GROUP_W + RWKV_GATE_LORA + 4 * RWKV_LORA
ZA_W = 3 * GROUP_W + 3 * 128

RET_CHUNK = 128
RET_GN_EPS = 1e-5
RET_FWD_OFFSET = 5.0
RET_BWD_OFFSET = 5.5
ROPE_BASE = 10000.0
RET_COLS = 4 * GROUP_W

LRU_CONV_W = 4
LRU_C = 8.0
LRU_COLS = 2 * GROUP_W
LRU_CHUNK = 128

S5_CH = 16
S5_GROUPS = GROUP_W // S5_CH
S5_STATE = 64
S5_N = S5_GROUPS * S5_STATE
S5_CHUNK = 128

PEER_HEADS = 8
PEER_NKEYS = 128
PEER_DKEY = 128
PEER_TOPK = 16
PEER_TOK_TILE = 512
PEER_EXP_BLOCK = 2048

ROW_TILE_T = 32
RWKV_CHUNK = 64

VMEM_LIMIT_BYTES = 56 * 1024 * 1024

NEG_INF = float("-inf")


def _cparams(n_axes):
    return pltpu.CompilerParams(dimension_semantics=("arbitrary",) * n_axes,
                                vmem_limit_bytes=VMEM_LIMIT_BYTES)


def _sigmoid(x):
    return 1.0 / (1.0 + jnp.exp(-x))


def _gelu_tanh(x):
    return 0.5 * x * (1.0 + jnp.tanh(math.sqrt(2.0 / math.pi) * (x + 0.044715 * (x * x * x))))


def _softplus(x):
    return jnp.maximum(x, 0.0) + jnp.log(1.0 + jnp.exp(-jnp.abs(x)))


def _dot(a, b):
    return jnp.dot(a, b, preferred_element_type=F32)


def _dot_exact(a, b):
    return jnp.dot(a, b, preferred_element_type=F32, precision=lax.Precision.HIGHEST)


def _head_sum_matrix():
    idx = np.arange(GROUP_W) // HEAD_DIM
    return jnp.asarray((idx[:, None] == idx[None, :]).astype(np.float32))


def _mod_body(c_ref, w_ref, b_ref, o_ref):
    c = c_ref[...]
    s = (c * _sigmoid(c)).astype(BF16)
    o_ref[...] = _dot(s, w_ref[...].astype(BF16)) + b_ref[...]


def _modulation(cvec, w, b):
    rows, d = cvec.shape
    n = w.shape[1]
    tn = 512
    return pl.pallas_call(
        _mod_body,
        grid=(n // tn,),
        in_specs=[pl.BlockSpec((rows, d), lambda j: (0, 0)),
                  pl.BlockSpec((d, tn), lambda j: (0, j)),
                  pl.BlockSpec((1, tn), lambda j: (0, j))],
        out_specs=pl.BlockSpec((rows, tn), lambda j: (0, j)),
        out_shape=jax.ShapeDtypeStruct((rows, n), F32),
        compiler_params=_cparams(1),
        name="adaln_mod",
    )(cvec, w, b.reshape(1, n))


def _in_body(x_ref, g_ref, mod_ref, w_ref, za_ref, zb_ref, zc_ref, zd_ref):
    tt, nb, d = x_ref.shape
    x = x_ref[...]
    ms = jnp.mean(x * x, axis=-1, keepdims=True)
    y = x * lax.rsqrt(ms + NORM_EPS) * g_ref[...]
    h = y * (1.0 + mod_ref[1]) + mod_ref[0]
    hb = h.reshape(tt * nb, d).astype(BF16)
    c0 = 0
    for ref in (za_ref, zb_ref, zc_ref, zd_ref):
        w = ref.shape[-1]
        ref[...] = _dot(hb, w_ref[:, c0:c0 + w]).reshape(tt, nb, w)
        c0 += w


def _in_proj(xs, g, modv, w_p, n_ctx_tiles):
    lt, nb, d = xs.shape
    tt = ROW_TILE_T
    widths = (ZA_W, RET_COLS, LRU_COLS, GROUP_W)
    seg = lambda i: jnp.where(i >= n_ctx_tiles, 1, 0)
    return pl.pallas_call(
        _in_body,
        grid=(lt // tt,),
        in_specs=[pl.BlockSpec((tt, nb, d), lambda i: (i, 0, 0)),
                  pl.BlockSpec((1, d), lambda i: (0, 0)),
                  pl.BlockSpec((None, N_MOD, nb, d), lambda i: (seg(i), 0, 0, 0)),
                  pl.BlockSpec(w_p.shape, lambda i: (0, 0))],
        out_specs=[pl.BlockSpec((tt, nb, w), lambda i: (i, 0, 0)) for w in widths],
        out_shape=[jax.ShapeDtypeStruct((lt, nb, w), F32) for w in widths],
        compiler_params=_cparams(1),
        name="in_proj",
    )(xs, g.reshape(1, d), modv, w_p)


def _rwkv_feat_body(n_ctx_tiles, n_tiles,
                    z_ref, zp_ref, zn_ref, mu_ref, gup_ref, wup_ref, aup_ref, vec_ref, hs_ref,
                    r_ref, k_ref, v_ref, g_ref, av_ref, w_ref, kd_ref, bv_ref):
    i = pl.program_id(0)
    tt, nb, zw = z_ref.shape
    z0 = z_ref[...]
    has_prev = jnp.logical_and(i != 0, i != n_ctx_tiles)
    has_next = jnp.logical_and(i != n_ctx_tiles - 1, i != n_tiles - 1)
    hp = jnp.where(has_prev, zp_ref[...], 0.0)
    hn = jnp.where(has_next, zn_ref[...], 0.0)
    zp = jnp.concatenate([hp, z0[:-1]], axis=0)
    zn = jnp.concatenate([z0[1:], hn], axis=0)
    z = z0 + mu_ref[0] * (zp - z0) + mu_ref[1] * (zn - z0)
    z = z.reshape(tt * nb, zw)
    gw = GROUP_W
    r = z[:, 0:gw]
    k = z[:, gw:2 * gw]
    v = z[:, 2 * gw:3 * gw]
    g = _dot(_sigmoid(z[:, 3 * gw:3 * gw + 128]).astype(BF16), gup_ref[...])
    w_low = jnp.tanh(z[:, 3 * gw + 128:3 * gw + 256]).astype(BF16)
    a_low = z[:, 3 * gw + 256:3 * gw + 384].astype(BF16)
    k_k = vec_ref[0:1, :]
    k_a = vec_ref[1:2, :]
    kk = k * k_k
    ss = _dot_exact(kk * kk, hs_ref[...])
    kk = kk / jnp.maximum(jnp.sqrt(ss), 1e-12)
    shp = (tt, nb, gw)
    r_ref[...] = r.reshape(shp)
    k_ref[...] = k.reshape(shp)
    v_ref[...] = v.reshape(shp)
    g_ref[...] = g.reshape(shp)
    av_ref[...] = (-kk).reshape(shp)
    for d in range(2):
        w0 = vec_ref[2 + d:3 + d, :]
        a0 = vec_ref[4 + d:5 + d, :]
        w_log = -_softplus(-(w0 + _dot(w_low, wup_ref[d]))) - 0.5
        w_ref[d] = jnp.exp(-jnp.exp(w_log)).reshape(shp)
        a = _sigmoid(a0 + _dot(a_low, aup_ref[d]))
        kd_ref[d] = (k * (1.0 + (a - 1.0) * k_a)).reshape(shp)
        bv_ref[d] = (kk * a).reshape(shp)


def _rwkv_features(za, mu_p, gup_p, wup_p, aup_p, vecs, hs, n_ctx_tiles):
    lt, nb, zw = za.shape
    tt = ROW_TILE_T
    n_tiles = lt // tt
    gw = GROUP_W
    one = lambda shape: jax.ShapeDtypeStruct(shape, F32)
    full = lambda a: pl.BlockSpec(a.shape, lambda i: (0,) * a.ndim)
    o1 = pl.BlockSpec((tt, nb, gw), lambda i: (i, 0, 0))
    o2 = pl.BlockSpec((2, tt, nb, gw), lambda i: (0, i, 0, 0))
    return pl.pallas_call(
        functools.partial(_rwkv_feat_body, n_ctx_tiles, n_tiles),
        grid=(n_tiles,),
        in_specs=[pl.BlockSpec((tt, nb, zw), lambda i: (i, 0, 0)),
                  pl.BlockSpec((1, nb, zw), lambda i: (jnp.maximum(i * tt - 1, 0), 0, 0)),
                  pl.BlockSpec((1, nb, zw), lambda i: (jnp.minimum((i + 1) * tt, lt - 1), 0, 0)),
                  full(mu_p), full(gup_p), full(wup_p), full(aup_p), full(vecs), full(hs)],
        out_specs=[o1, o1, o1, o1, o1, o2, o2, o2],
        out_shape=[one((lt, nb, gw))] * 5 + [one((2, lt, nb, gw))] * 3,
        compiler_params=_cparams(1),
        name="rwkv_features",
    )(za, za, za, mu_p, gup_p, wup_p, aup_p, vecs, hs)


def _rwkv_scan_body(rf_ref, af_ref, wf_ref, kdf_ref, bf_ref, vf_ref,
                    rb_ref, ab_ref, wb_ref, kdb_ref, bb_ref, vb_ref,
                    yf_ref, yb_ref, s_ref, vx_ref):
    j = pl.program_id(0)
    tc, n_khi, lanes = rf_ref.shape
    n_v, n_bh = vf_ref.shape[1:]
    k_lo = lanes // n_bh
    n_half = 2
    hv = n_v // n_half

    @pl.when(j == 0)
    def _():
        s_ref[...] = jnp.zeros_like(s_ref)

    for d, ref in enumerate((vf_ref, vb_ref)):
        x = ref[...]
        reps = 1
        while reps < k_lo:
            x = jnp.concatenate([x, x], axis=-1)
            reps *= 2
        vx_ref[d] = x

    def group_sum(x):
        shift = n_bh
        while shift < lanes:
            x = x + pltpu.roll(x, shift, axis=1)
            shift *= 2
        return x

    dirs = ((rf_ref, af_ref, wf_ref, kdf_ref, bf_ref, yf_ref), (rb_ref, ab_ref, wb_ref, kdb_ref, bb_ref, yb_ref))

    segs = [(d, slice(half * hv, (half + 1) * hv)) for d in range(2) for half in range(n_half)]

    def step(i, carry):
        t = (i, tc - 1 - i)
        row = lambda ref, d, kh: ref[t[d], pl.ds(kh, 1), :]
        acc = [None] * len(segs)
        for kh in range(n_khi):
            a_row = [row(dirs[d][1], d, kh) for d in range(2)]
            for n, (d, rows) in enumerate(segs):
                term = s_ref[d, kh, rows, :] * a_row[d]
                acc[n] = term if acc[n] is None else acc[n] + term
        sa = [group_sum(a) for a in acc]
        ip = jnp.maximum(i - 1, 0)
        tp = (ip, tc - 1 - ip)
        for n, (d, rows) in enumerate(segs):
            dirs[d][5][tp[d], rows, :] = group_sum(carry[n])[:, :n_bh]
        for kh in range(n_khi):
            w_row, kd_row = ([row(dirs[d][c], d, kh) for d in range(2)] for c in (2, 3))
            for n, (d, rows) in enumerate(segs):
                s_ref[d, kh, rows, :] = s_ref[d, kh, rows, :] * w_row[d] + vx_ref[d, t[d], rows, :] * kd_row[d]
        acc = [None] * len(segs)
        for kh in range(n_khi):
            b_row, r_row = ([row(dirs[d][c], d, kh) for d in range(2)] for c in (4, 0))
            for n, (d, rows) in enumerate(segs):
                s_new = s_ref[d, kh, rows, :] + sa[n] * b_row[d]
                s_ref[d, kh, rows, :] = s_new
                term = s_new * r_row[d]
                acc[n] = term if acc[n] is None else acc[n] + term
        return tuple(acc)

    zero = jnp.zeros((hv, lanes), F32)
    last = lax.fori_loop(0, tc, step, (zero,) * len(segs))
    t_last = (tc - 1, 0)
    for n, (d, rows) in enumerate(segs):
        dirs[d][5][t_last[d], rows, :] = group_sum(last[n])[:, :n_bh]


def _dir_chunk(d, j, n_ctx, n_all):
    bwd = jnp.where(j < n_ctx, n_ctx - 1 - j, n_all - 1 - (j - n_ctx))
    return jnp.where(d == 0, j, bwd)


def _rwkv_scan(r_e, a_e, w_e, kd_e, b_e, v_e, l_ctx):
    lt, n_khi, lanes = r_e.shape
    _, n_v, n_bh = v_e.shape
    tc = RWKV_CHUNK
    n_all, n_ctx = lt // tc, l_ctx // tc
    specs = []
    for d in range(2):
        cm = functools.partial(_dir_chunk, d, n_ctx=n_ctx, n_all=n_all)
        shared = pl.BlockSpec((tc, n_khi, lanes), lambda j, cm=cm: (cm(j), 0, 0))
        per_dir = pl.BlockSpec((None, tc, n_khi, lanes), lambda j, cm=cm, d=d: (d, cm(j), 0, 0))
        tok = pl.BlockSpec((tc, n_v, n_bh), lambda j, cm=cm: (cm(j), 0, 0))
        specs.append(([shared, shared, per_dir, per_dir, per_dir, tok], tok))
    return pl.pallas_call(
        _rwkv_scan_body,
        grid=(n_all,),
        in_specs=specs[0][0] + specs[1][0],
        out_specs=[specs[0][1], specs[1][1]],
        out_shape=[jax.ShapeDtypeStruct((lt, n_v, n_bh), F32)] * 2,
        scratch_shapes=[pltpu.VMEM((2, n_khi, n_v, lanes), F32), pltpu.VMEM((2, tc, n_v, lanes), F32)],
        compiler_params=_cparams(1),
        name="rwkv_scan",
    )(r_e, a_e, w_e, kd_e, b_e, v_e, r_e, a_e, w_e, kd_e, b_e, v_e)


def _rwkv_expand_k(x, k_lo):
    lead = x.shape[:-2]
    nb = x.shape[-2]
    nd = len(lead)
    x = x.reshape(lead + (nb, N_HEADS, k_lo, HEAD_DIM // k_lo))
    x = jnp.transpose(x, tuple(range(nd)) + (nd + 3, nd + 2, nd, nd + 1))
    return x.reshape(lead + (HEAD_DIM // k_lo, k_lo * nb * N_HEADS))


def _rwkv_expand_v(x):
    lt, nb, _ = x.shape
    return jnp.transpose(x.reshape(lt, nb, N_HEADS, HEAD_DIM), (0, 3, 1, 2)).reshape(lt, HEAD_DIM, nb * N_HEADS)


def _rwkv_collapse_v(y, nb):
    lt = y.shape[0]
    return jnp.transpose(y.reshape(lt, HEAD_DIM, nb, N_HEADS), (0, 2, 3, 1)).reshape(lt, nb, GROUP_W)


def _ret_body(z_ref, cos_ref, sin_ref, dec_ref, rd_ref, wd_ref, cd_ref, hm_ref, o_ref, s_ref):
    j = pl.program_id(2)
    gw = GROUP_W
    half = gw // 2

    @pl.when(j == 0)
    def _():
        s_ref[...] = jnp.zeros_like(s_ref)

    cos = cos_ref[...]
    sin = sin_ref[...]

    def rope(x):
        x1, x2 = x[:, :half], x[:, half:]
        return jnp.concatenate([x1 * cos - x2 * sin, x1 * sin + x2 * cos], axis=-1)

    q = rope(z_ref[:, 0:gw]) * (HEAD_DIM ** -0.5)
    k = rope(z_ref[:, gw:2 * gw])
    v = z_ref[:, 2 * gw:3 * gw]
    vb = v.astype(BF16)
    k_t = k.T
    k_tb = k_t.astype(BF16)
    inter = _dot(q.astype(BF16), s_ref[...].astype(BF16)) * rd_ref[...]
    intra = jnp.zeros_like(inter)
    for h in range(N_HEADS):
        qm = (q * hm_ref[h:h + 1, :]).astype(BF16)
        att = _dot(qm, k_tb) * dec_ref[h]
        vm = (v * hm_ref[N_HEADS + h:N_HEADS + h + 1, :]).astype(BF16)
        intra = intra + _dot(att.astype(BF16), vm)
    o_ref[...] = intra + inter
    kv = _dot((k_t * wd_ref[...]).astype(BF16), vb)
    s_ref[...] = s_ref[...] * cd_ref[0] + kv * cd_ref[1]


def _retention(zb_bm, tabs, l_ctx):
    nb, lt, _ = zb_bm.shape
    c = RET_CHUNK
    n_all, n_ctx = lt // c, l_ctx // c
    gw = GROUP_W
    cos, sin, dec, rd, wd, cd, hm = tabs
    cm = lambda d, j: _dir_chunk(d, j, n_ctx, n_all)
    return pl.pallas_call(
        _ret_body,
        grid=(nb, 2, n_all),
        in_specs=[pl.BlockSpec((None, c, RET_COLS), lambda b, d, j: (b, cm(d, j), 0)),
                  pl.BlockSpec((c, gw // 2), lambda b, d, j: (cm(d, j), 0)),
                  pl.BlockSpec((c, gw // 2), lambda b, d, j: (cm(d, j), 0)),
                  pl.BlockSpec((None, N_HEADS, c, c), lambda b, d, j: (d, 0, 0, 0)),
                  pl.BlockSpec((None, c, gw), lambda b, d, j: (d, 0, 0)),
                  pl.BlockSpec((None, gw, c), lambda b, d, j: (d, 0, 0)),
                  pl.BlockSpec((None, 2, gw, gw), lambda b, d, j: (d, 0, 0, 0)),
                  pl.BlockSpec((2 * N_HEADS, gw), lambda b, d, j: (0, 0))],
        out_specs=pl.BlockSpec((None, None, c, gw), lambda b, d, j: (d, b, cm(d, j), 0)),
        out_shape=jax.ShapeDtypeStruct((2, nb, lt, gw), F32),
        scratch_shapes=[pltpu.VMEM((gw, gw), F32)],
        compiler_params=_cparams(3),
        name="retention",
    )(zb_bm, cos, sin, dec, rd, wd, cd, hm)


def _ret_perm():
    new = np.arange(GROUP_W)
    half_id, rem = new // 128, new % 128
    h, i = rem // 32, rem % 32
    return h * HEAD_DIM + half_id * 32 + i


def _ret_tables(lt):
    c = RET_CHUNK
    half = HEAD_DIM // 2
    freqs = ROPE_BASE ** (-jnp.arange(half, dtype=F32) / half)
    ang = jnp.arange(lt, dtype=F32)[:, None] * freqs[None]
    cos = jnp.tile(jnp.cos(ang), (1, N_HEADS))
    sin = jnp.tile(jnp.sin(ang), (1, N_HEADS))
    idx = jnp.arange(c, dtype=F32)
    perm = _ret_perm()
    head_of_qk = jnp.asarray(perm // HEAD_DIM)
    head_of_v = jnp.arange(GROUP_W) // HEAD_DIM
    dec, rd, wd, cd = [], [], [], []
    for offset, rev in ((RET_FWD_OFFSET, False), (RET_BWD_OFFSET, True)):
        lg = jnp.log1p(-jnp.exp2(-(offset + jnp.arange(N_HEADS, dtype=F32))))
        diff = idx[:, None] - idx[None, :]
        if rev:
            mask, dist = diff < 0, -diff
            read_pow, write_pow = c - idx, idx
        else:
            mask, dist = diff >= 0, diff
            read_pow, write_pow = idx + 1.0, c - 1.0 - idx
        dec.append(jnp.where(mask[None], jnp.exp(lg[:, None, None] * jnp.where(mask, dist, 0.0)[None]), 0.0))
        rd.append(jnp.exp(lg[head_of_v][None, :] * read_pow[:, None]))
        wd.append(jnp.exp(lg[head_of_qk][:, None] * write_pow[None, :]))
        same = (head_of_qk[:, None] == head_of_v[None, :]).astype(F32)
        cd.append(jnp.stack([same * jnp.exp(lg * c)[head_of_v][None, :], same]))
    hm_q = (head_of_qk[None, :] == jnp.arange(N_HEADS)[:, None]).astype(F32)
    hm_v = (head_of_v[None, :] == jnp.arange(N_HEADS)[:, None]).astype(F32)
    return cos, sin, jnp.stack(dec), jnp.stack(rd), jnp.stack(wd), jnp.stack(cd), jnp.concatenate([hm_q, hm_v])


def _lru_body(n_ctx, n_all, x_ref, xp_ref, xn_ref, cw_ref, wa_ref, wx_ref, vec_ref, o_ref,
              a_s, b_s, h_s):
    d = pl.program_id(0)
    j = pl.program_id(1)
    tc, nb, gw = x_ref.shape
    ch = _dir_chunk(d, j, n_ctx, n_all)

    @pl.when(j == 0)
    def _():
        h_s[...] = jnp.zeros_like(h_s)

    has_prev = jnp.logical_and(ch != 0, ch != n_ctx)
    has_next = jnp.logical_and(ch != n_ctx - 1, ch != n_all - 1)
    xe = jnp.concatenate([jnp.where(has_prev, xp_ref[...], 0.0), x_ref[...],
                          jnp.where(has_next, xn_ref[...], 0.0)], axis=0)
    xc = cw_ref[LRU_CONV_W:LRU_CONV_W + 1, :]
    for tap in range(LRU_CONV_W):
        xc = xc + cw_ref[tap:tap + 1, :] * xe[tap:tap + tc]
    xc = xc.reshape(tc * nb, gw)
    xb = xc.astype(BF16)
    r = _sigmoid(_dot(xb, wa_ref[...]) + vec_ref[0:1, :])
    gi = _sigmoid(_dot(xb, wx_ref[...]) + vec_ref[1:2, :])
    log_a = -LRU_C * r * _softplus(-vec_ref[2:3, :])
    th = jnp.tanh(log_a)
    one_minus_a2 = 2.0 * th / (th - 1.0)
    a_s[...] = jnp.exp(log_a).reshape(tc, nb, gw)
    b_s[...] = (jnp.sqrt(one_minus_a2) * (gi * xc)).reshape(tc, nb, gw)

    def step(i, h):
        t = jnp.where(d == 0, i, tc - 1 - i)
        h = a_s[t] * h + b_s[t]
        o_ref[t] = h
        return h

    h_s[...] = lax.fori_loop(0, tc, step, h_s[...])


def _lru(zc, cw, wa_bd, wx_bd, vecs, l_ctx):
    lt, nb, _ = zc.shape
    gw = GROUP_W
    tc = LRU_CHUNK
    n_all, n_ctx = lt // tc, l_ctx // tc
    cm = lambda d, j: _dir_chunk(d, j, n_ctx, n_all)
    return pl.pallas_call(
        functools.partial(_lru_body, n_ctx, n_all),
        grid=(2, n_all),
        in_specs=[pl.BlockSpec((tc, nb, gw), lambda d, j: (cm(d, j), 0, 0)),
                  pl.BlockSpec((2, nb, gw), lambda d, j: (jnp.maximum(cm(d, j) * (tc // 2) - 1, 0), 0, 0)),
                  pl.BlockSpec((1, nb, gw), lambda d, j: (jnp.minimum((cm(d, j) + 1) * tc, lt - 1), 0, 0)),
                  pl.BlockSpec(cw.shape, lambda d, j: (0, 0)),
                  pl.BlockSpec((None, gw, gw), lambda d, j: (d, 0, 0)),
                  pl.BlockSpec((None, gw, gw), lambda d, j: (d, 0, 0)),
                  pl.BlockSpec((None, 3, gw), lambda d, j: (d, 0, 0))],
        out_specs=pl.BlockSpec((None, tc, nb, gw), lambda d, j: (d, cm(d, j), 0, 0)),
        out_shape=jax.ShapeDtypeStruct((2, lt, nb, gw), F32),
        scratch_shapes=[pltpu.VMEM((tc, nb, gw), F32), pltpu.VMEM((tc, nb, gw), F32),
                        pltpu.VMEM((nb, gw), F32)],
        compiler_params=_cparams(2),
        name="rglru",
    )(zc, zc, zc, cw, wa_bd, wx_bd, vecs)


def _block_diag(w):
    nblk, n = w.shape[-3], w.shape[-1]
    eye = jnp.eye(nblk, dtype=w.dtype)
    out = w[..., :, :, None, :] * eye[:, None, :, None]
    return out.reshape(w.shape[:-3] + (nblk * n, nblk * n))


def _s5_body(u_ref, bm_ref, lam_ref, cm_ref, y_ref, h_s, st_s):
    d = pl.program_id(0)
    j = pl.program_id(1)
    tc, nb, gw = u_ref.shape
    n = S5_N

    @pl.when(j == 0)
    def _():
        st_s[...] = jnp.zeros_like(st_s)

    u = u_ref[...].reshape(tc * nb, gw).astype(BF16)
    h_s[...] = _dot(u, bm_ref[...]).reshape(tc, nb, 2 * n)
    lr = lam_ref[0:1, :]
    li = lam_ref[1:2, :]

    def step(i, carry):
        hr, hi = carry
        t = jnp.where(d == 0, i, tc - 1 - i)
        bu = h_s[t]
        nr = lr * hr - li * hi + bu[:, :n]
        ni = lr * hi + li * hr + bu[:, n:]
        h_s[t] = jnp.concatenate([nr, ni], axis=-1)
        return nr, ni

    hr, hi = lax.fori_loop(0, tc, step, (st_s[0], st_s[1]))
    st_s[0] = hr
    st_s[1] = hi
    hh = h_s[...].reshape(tc * nb, 2 * n).astype(BF16)
    y_ref[...] = _dot(hh, cm_ref[...]).reshape(tc, nb, gw)


def _s5(u, bmat, lam, cmat, l_ctx):
    lt, nb, gw = u.shape
    tc = S5_CHUNK
    n_all, n_ctx = lt // tc, l_ctx // tc
    cm = lambda d, j: _dir_chunk(d, j, n_ctx, n_all)
    return pl.pallas_call(
        _s5_body,
        grid=(2, n_all),
        in_specs=[pl.BlockSpec((tc, nb, gw), lambda d, j: (cm(d, j), 0, 0)),
                  pl.BlockSpec((None, gw, 2 * S5_N), lambda d, j: (d, 0, 0)),
                  pl.BlockSpec((None, 2, S5_N), lambda d, j: (d, 0, 0)),
                  pl.BlockSpec((None, 2 * S5_N, gw), lambda d, j: (d, 0, 0))],
        out_specs=pl.BlockSpec((None, tc, nb, gw), lambda d, j: (d, cm(d, j), 0, 0)),
        out_shape=jax.ShapeDtypeStruct((2, lt, nb, gw), F32),
        scratch_shapes=[pltpu.VMEM((tc, nb, 2 * S5_N), F32), pltpu.VMEM((2, nb, S5_N), F32)],
        compiler_params=_cparams(2),
        name="s5",
    )(u, bmat, lam, cmat)


def _s5_params(a_re, a_im, log_dt, b_re, b_im, c_re, c_im):
    dt = jnp.exp(log_dt)[..., None]
    er = jnp.exp(a_re * dt)
    lbr, lbi = er * jnp.cos(a_im * dt), er * jnp.sin(a_im * dt)
    den = a_re * a_re + a_im * a_im
    nr, ni = lbr - 1.0, lbi
    fr = (nr * a_re + ni * a_im) / den
    fi = (ni * a_re - nr * a_im) / den
    bbr = fr[..., None] * b_re - fi[..., None] * b_im
    bbi = fr[..., None] * b_im + fi[..., None] * b_re
    eye = jnp.eye(S5_GROUPS, dtype=F32)

    def in_map(bb):
        m = jnp.einsum('dgpc,gh->dgchp', bb, eye)
        return m.reshape(2, GROUP_W, S5_N)

    def out_map(cc):
        m = jnp.einsum('dgcp,gh->dgphc', cc, eye)
        return m.reshape(2, S5_N, GROUP_W)

    bmat = jnp.concatenate([in_map(bbr), in_map(bbi)], axis=-1).astype(BF16)
    cmat = jnp.concatenate([out_map(c_re), -out_map(c_im)], axis=-2).astype(BF16)
    lam = jnp.stack([lbr.reshape(2, S5_N), lbi.reshape(2, S5_N)], axis=1)
    return bmat, lam, cmat


def _to_colmajor(z, l_ctx):
    lt, nb, ch = z.shape
    rows = (lt - l_ctx) // GRID_W
    lat = z[l_ctx:].reshape(rows, GRID_W, nb, ch).transpose(1, 0, 2, 3).reshape(lt - l_ctx, nb, ch)
    return jnp.concatenate([z[:l_ctx], lat], axis=0)


def _from_colmajor(y, l_ctx):
    lt = y.shape[-3]
    nb, ch = y.shape[-2:]
    rows = (lt - l_ctx) // GRID_W
    lead = y.shape[:-3]
    lat = y[..., l_ctx:, :, :].reshape(lead + (GRID_W, rows, nb, ch))
    lat = jnp.swapaxes(lat, -4, -3).reshape(lead + (lt - l_ctx, nb, ch))
    return jnp.concatenate([y[..., :l_ctx, :, :], lat], axis=-3)


def _out_body(x_ref, mod_ref, ya_ref, r_ref, k_ref, v_ref, g_ref, ob_ref, gb_ref, hc_ref, gc_ref,
              yd_ref, ud_ref, vec_ref, hs_ref, gluw_ref, glub_ref, wo_ref, n2_ref, x1_ref, h2_ref):
    tt, nb, d = x_ref.shape
    gw = GROUP_W
    rows = tt * nb
    hs = hs_ref[...]
    two = lambda ref: (ref[0] + ref[1]).reshape(rows, gw)
    flat = lambda ref: ref[...].reshape(rows, gw)

    def head_norm(y, eps):
        mu = _dot_exact(y, hs) * (1.0 / HEAD_DIM)
        dlt = y - mu
        var = _dot_exact(dlt * dlt, hs) * (1.0 / HEAD_DIM)
        return dlt * lax.rsqrt(var + eps)

    r_k, ln_g, ln_b = vec_ref[0:1, :], vec_ref[1:2, :], vec_ref[2:3, :]
    r, k, v = flat(r_ref), flat(k_ref), flat(v_ref)
    yn = head_norm(two(ya_ref), RWKV_GN_EPS) * ln_g + ln_b
    bonus = _dot_exact(r * k * r_k, hs) * v
    mix_a = (yn + bonus) * flat(g_ref)
    gb = flat(gb_ref)
    mix_b = gb * _sigmoid(gb) * (head_norm(two(ob_ref), RET_GN_EPS) * vec_ref[3:4, :])
    mix_c = _gelu_tanh(flat(gc_ref)) * two(hc_ref)
    yd = vec_ref[4:5, :] * flat(ud_ref) + two(yd_ref)
    glu = _dot(_gelu_tanh(yd).astype(BF16), gluw_ref[...]) + glub_ref[...]
    mix_d = glu[:, :gw] * _sigmoid(glu[:, gw:])

    mix = _dot(mix_a.astype(BF16), wo_ref[0:gw, :])
    mix = mix + _dot(mix_b.astype(BF16), wo_ref[gw:2 * gw, :])
    mix = mix + _dot(mix_c.astype(BF16), wo_ref[2 * gw:3 * gw, :])
    mix = mix + _dot(mix_d.astype(BF16), wo_ref[3 * gw:4 * gw, :])
    x1 = x_ref[...] + mod_ref[2] * mix.reshape(tt, nb, d)
    x1_ref[...] = x1
    ms = jnp.mean(x1 * x1, axis=-1, keepdims=True)
    h2 = (x1 * lax.rsqrt(ms + NORM_EPS) * n2_ref[...]) * (1.0 + mod_ref[4]) + mod_ref[3]
    h2_ref[...] = h2.reshape(rows, d).T.astype(BF16)


def _out_proj(xs, modv, ya, feats, ob, zb, hc, zc, yd, zd, vecs, hs, gluw, glub, wo, n2, n_ctx_tiles, t0_tile):
    lt, nb, d = xs.shape
    tt = ROW_TILE_T
    gw = GROUP_W
    n_tiles = lt // tt - t0_tile
    seg = lambda i: jnp.where(i + t0_tile >= n_ctx_tiles, 1, 0)
    col = lambda c: pl.BlockSpec((tt, nb, gw), lambda i: (i + t0_tile, 0, c))
    two = pl.BlockSpec((2, tt, nb, gw), lambda i: (0, i + t0_tile, 0, 0))
    full = lambda a: pl.BlockSpec(a.shape, lambda i: (0,) * a.ndim)
    r, k, v, g = feats
    return pl.pallas_call(
        _out_body,
        grid=(n_tiles,),
        in_specs=[pl.BlockSpec((tt, nb, d), lambda i: (i + t0_tile, 0, 0)),
                  pl.BlockSpec((None, N_MOD, nb, d), lambda i: (seg(i), 0, 0, 0)),
                  two, col(0), col(0), col(0), col(0),
                  two, col(3), two, col(1), two, col(0),
                  full(vecs), full(hs), full(gluw), full(glub), full(wo), full(n2)],
        out_specs=[pl.BlockSpec((tt, nb, d), lambda i: (i + t0_tile, 0, 0)),
                   pl.BlockSpec((d, tt * nb), lambda i: (0, i + t0_tile))],
        out_shape=[jax.ShapeDtypeStruct((lt, nb, d), F32), jax.ShapeDtypeStruct((d, lt * nb), BF16)],
        compiler_params=_cparams(1),
        name="out_proj",
    )(xs, modv, ya, r, k, v, g, ob, zb, hc, zc, yd, zd, vecs, hs, gluw, glub, wo, n2)


PEER_NO_RANK = 127.0


def _top_values(s, n, with_rank=False):
    vals, cur = [], s
    rank = jnp.full(s.shape, PEER_NO_RANK, F32) if with_rank else None
    for k in range(n):
        m = jnp.max(cur, axis=0, keepdims=True)
        vals.append(m)
        hit = cur == m
        if with_rank:
            rank = jnp.where(hit, float(k), rank)
        cur = jnp.where(hit, NEG_INF, cur)
    return (vals, rank) if with_rank else vals


def _peer_route_body(h_ref, wq_ref, keys_ref, e1_ref, cnt_ref, r2_ref, w2_ref, pack_ref):
    hb = h_ref[...]
    for h in range(PEER_HEADS):
        st = []
        for p in range(2):
            c0 = (2 * h + p) * PEER_DKEY
            q = _dot(wq_ref[c0:c0 + PEER_DKEY, :], hb).astype(BF16)
            st.append(_dot(keys_ref[2 * h + p], q))
        s1, s2 = st
        n_top = PEER_TOPK + 1
        v1 = _top_values(s1, n_top)
        v2, rank2 = _top_values(s2, n_top, with_rank=True)
        pad7 = [jnp.full((7, s1.shape[1]), NEG_INF, F32)]
        v2m = jnp.concatenate(v2 + pad7, axis=0)
        v1_tail = jnp.concatenate(v1[8:] + pad7, axis=0)
        cand = jnp.concatenate([v1[0] + v2m] + [v1[a] + v2m[0:8] for a in range(1, 8)]
                               + [v1_tail + v2[0]], axis=0)
        top = _top_values(cand, n_top)
        tau = 0.5 * (top[PEER_TOPK - 1] + top[PEER_TOPK])
        cmax = top[0]
        z = jnp.sum(jnp.where(cand >= tau, jnp.exp(cand - cmax), 0.0), axis=0, keepdims=True)
        thr = tau - s1
        cnt = jnp.zeros_like(s1)
        for k in range(PEER_TOPK):
            cnt = cnt + jnp.where(v2[k] >= thr, 1.0, 0.0)
        e1_ref[:, h, :] = jnp.exp(s1 - v1[0])
        cnt_ref[:, h, :] = cnt
        pack_ref[0] = rank2.astype(BF16)
        pack_ref[1] = (jnp.exp(s2 - v2[0]) / z).astype(BF16)
        words = pack_ref.bitcast(jnp.uint32)
        r2_ref[h] = words[0]
        w2_ref[h] = words[1]


def _peer_route(h2t, wq, keys, t0_tile):
    d, t_all = h2t.shape
    tt = PEER_TOK_TILE
    nk = PEER_NKEYS
    by_i = pl.BlockSpec((nk, PEER_HEADS, tt), lambda i: (0, 0, i + t0_tile))
    by_h = pl.BlockSpec((PEER_HEADS, nk // 2, tt), lambda i: (0, 0, i + t0_tile))
    return pl.pallas_call(
        _peer_route_body,
        grid=(t_all // tt - t0_tile,),
        in_specs=[pl.BlockSpec((d, tt), lambda i: (0, i + t0_tile)),
                  pl.BlockSpec(wq.shape, lambda i: (0, 0)),
                  pl.BlockSpec(keys.shape, lambda i: (0, 0, 0))],
        out_specs=[by_i, by_i, by_h, by_h],
        out_shape=[jax.ShapeDtypeStruct((nk, PEER_HEADS, t_all), F32)] * 2
                  + [jax.ShapeDtypeStruct((PEER_HEADS, nk // 2, t_all), jnp.uint32)] * 2,
        scratch_shapes=[pltpu.VMEM((2, nk, tt), BF16)],
        compiler_params=_cparams(1),
        name="peer_route",
    )(h2t, wq, keys)


PEER_GATE_ROWS = 32
LANES = 128


BF16_ROWS = 16
SUBLANES = 8


def _peer_gate_block(p_ref, e1_ref, cnt_ref, r2_ref, w2_ref, bc_ref):
    nk = PEER_NKEYS
    n_sub = p_ref.shape[0] // nk
    rc = PEER_GATE_ROWS
    n_pv = rc // BF16_ROWS
    for c0 in range(0, p_ref.shape[1], LANES):
        cols = slice(c0, c0 + LANES)
        for ii in range(n_sub):
            for h in range(PEER_HEADS):
                bc_ref[0, ii, h] = jnp.broadcast_to(cnt_ref[ii, h:h + 1, cols], (BF16_ROWS, LANES)).astype(BF16)
                bc_ref[1, ii, h] = jnp.broadcast_to(e1_ref[ii, h:h + 1, cols], (BF16_ROWS, LANES)).astype(BF16)

        def chunk(q, carry, cols=cols):
            r0 = pl.multiple_of(q * rc, rc)
            gate = [[None] * n_pv for _ in range(n_sub)]
            for h in range(PEER_HEADS):
                rank = [pltpu.bitcast(r2_ref[h, q, SUBLANES * v:SUBLANES * (v + 1), cols], BF16)
                        for v in range(n_pv)]
                w2 = [pltpu.bitcast(w2_ref[h, q, SUBLANES * v:SUBLANES * (v + 1), cols], BF16)
                      for v in range(n_pv)]
                for ii in range(n_sub):
                    cnt = bc_ref[0, ii, h]
                    e1 = bc_ref[1, ii, h]
                    for v in range(n_pv):
                        term = jnp.where(rank[v] < cnt, w2[v] * e1, jnp.zeros_like(e1))
                        gate[ii][v] = term if gate[ii][v] is None else gate[ii][v] + term
            for ii in range(n_sub):
                for v in range(n_pv):
                    rows = pl.ds(ii * nk + r0 + BF16_ROWS * v, BF16_ROWS)
                    p_ref[rows, cols] = p_ref[rows, cols] * gate[ii][v]
            return carry

        lax.fori_loop(0, nk // rc, chunk, 0)


def _peer_ffn_body(h_ref, u_ref, vt_ref, e1_ref, cnt_ref, r2_ref, w2_ref, x_ref, mod_ref, o_ref,
                   acc_ref, p_ref, bc_ref):
    j = pl.program_id(1)

    @pl.when(j == 0)
    def _():
        acc_ref[...] = jnp.zeros_like(acc_ref)

    act = _dot(u_ref[...], h_ref[...])
    p_ref[...] = _gelu_tanh(act).astype(BF16)
    _peer_gate_block(p_ref, e1_ref, cnt_ref, r2_ref, w2_ref, bc_ref)
    acc_ref[...] += _dot(vt_ref[...], p_ref[...])

    @pl.when(j == pl.num_programs(1) - 1)
    def _():
        tt, d = x_ref.shape
        nb = mod_ref.shape[1]
        y = acc_ref[...].T.reshape(tt // nb, nb, d) * mod_ref[5]
        o_ref[...] = x_ref[...] + y.reshape(tt, d)


def _peer_ffn(h2t, u_b, vt_b, e1, cnt, r2, w2, x1, modv, n_ctx_tok_tiles, t0_tile):
    d, t_all = h2t.shape
    tt = PEER_TOK_TILE
    eb = PEER_EXP_BLOCK
    nk = PEER_NKEYS
    nb = modv.shape[2]
    n_e = u_b.shape[0] // eb
    n_tiles = t_all // tt - t0_tile
    seg = lambda i: jnp.where(i + t0_tile >= n_ctx_tok_tiles, 1, 0)
    by_i = pl.BlockSpec((eb // nk, PEER_HEADS, tt), lambda i, j: (j, 0, i + t0_tile))
    rc = PEER_GATE_ROWS
    by_h = pl.BlockSpec((PEER_HEADS, nk // rc, rc // 2, tt), lambda i, j: (0, 0, 0, i + t0_tile))
    r2, w2 = (a.reshape(PEER_HEADS, nk // rc, rc // 2, t_all) for a in (r2, w2))
    return pl.pallas_call(
        _peer_ffn_body,
        grid=(n_tiles, n_e),
        in_specs=[pl.BlockSpec((d, tt), lambda i, j: (0, i + t0_tile)),
                  pl.BlockSpec((eb, d), lambda i, j: (j, 0)),
                  pl.BlockSpec((d, eb), lambda i, j: (0, j)),
                  by_i, by_i, by_h, by_h,
                  pl.BlockSpec((tt, d), lambda i, j: (i + t0_tile, 0)),
                  pl.BlockSpec((None, N_MOD, nb, d), lambda i, j: (seg(i), 0, 0, 0))],
        out_specs=pl.BlockSpec((tt, d), lambda i, j: (i + t0_tile, 0)),
        out_shape=jax.ShapeDtypeStruct((t_all, d), F32),
        scratch_shapes=[pltpu.VMEM((d, tt), F32), pltpu.VMEM((eb, tt), BF16),
                        pltpu.VMEM((2, eb // nk, PEER_HEADS, BF16_ROWS, LANES), BF16)],
        compiler_params=_cparams(2),
        name="peer_ffn",
    )(h2t, u_b, vt_b, e1, cnt, r2, w2, x1, modv)


def _final_body(x_ref, g_ref, o_ref):
    x = x_ref[...]
    ms = jnp.mean(x * x, axis=-1, keepdims=True)
    o_ref[...] = x * lax.rsqrt(ms + NORM_EPS) * g_ref[...]


def _final_norm(x2d, g):
    t, d = x2d.shape
    tr = 512
    return pl.pallas_call(
        _final_body,
        grid=(t // tr,),
        in_specs=[pl.BlockSpec((tr, d), lambda i: (i, 0)), pl.BlockSpec((1, d), lambda i: (0, 0))],
        out_specs=pl.BlockSpec((tr, d), lambda i: (i, 0)),
        out_shape=jax.ShapeDtypeStruct((t, d), F32),
        compiler_params=_cparams(1),
        name="final_norm",
    )(x2d, g.reshape(1, d))


def _pad_cols(w, width):
    return jnp.pad(w, [(0, 0)] * (w.ndim - 1) + [(0, width - w.shape[-1])])


def _rwkv_slabs(w):
    gw3 = 3 * GROUP_W
    parts = [w[..., :gw3],
             _pad_cols(w[..., gw3:gw3 + 64], 128),
             _pad_cols(w[..., gw3 + 64:gw3 + 128], 128),
             _pad_cols(w[..., gw3 + 128:gw3 + 192], 128)]
    return jnp.concatenate(parts, axis=-1)


def _layer_weights(l, p):
    d = p['w_in'].shape[1]
    w_in = p['w_in'][l]
    wa_, wb_, wc_, wd_ = jnp.split(w_in, [RWKV_COLS, RWKV_COLS + RET_COLS, RWKV_COLS + RET_COLS + LRU_COLS], axis=-1)
    perm = _ret_perm()
    gw = GROUP_W
    wb_ = jnp.concatenate([wb_[:, 0:gw][:, perm], wb_[:, gw:2 * gw][:, perm], wb_[:, 2 * gw:]], axis=-1)
    out = {'w_in': jnp.concatenate([_rwkv_slabs(wa_), wb_, wc_, wd_], axis=-1).astype(BF16)}
    out['mu'] = _rwkv_slabs(p['rwkv_mu'][l])[:, None, :]
    out['gup'] = jnp.pad(p['rwkv_g_up'][l], ((0, 64), (0, 0))).astype(BF16)
    lora = RWKV_LORA
    out['wup'] = jnp.stack([jnp.pad(p['rwkv_w_up'][l, dd], ((dd * lora, 128 - (dd + 1) * lora), (0, 0)))
                            for dd in range(2)]).astype(BF16)
    out['aup'] = jnp.stack([jnp.pad(p['rwkv_a_up'][l, dd], ((dd * lora, 128 - (dd + 1) * lora), (0, 0)))
                            for dd in range(2)]).astype(BF16)
    out['rwkv_vecs'] = jnp.concatenate([p['rwkv_k_k'][l][None], p['rwkv_k_a'][l][None],
                                        p['rwkv_w0'][l], p['rwkv_a0'][l]], axis=0)
    out['cw'] = jnp.concatenate([p['lru_conv_w'][l], p['lru_conv_b'][l][None]], axis=0)
    out['wa'] = _block_diag(p['lru_wa'][l]).astype(BF16)
    out['wx'] = _block_diag(p['lru_wx'][l]).astype(BF16)
    out['lru_vecs'] = jnp.stack([p['lru_ba'][l], p['lru_bx'][l], p['lru_lambda'][l]], axis=1)
    out['s5'] = _s5_params(p['s5_a_re'][l], p['s5_a_im'][l], p['s5_log_dt'][l], p['s5_b_re'][l],
                           p['s5_b_im'][l], p['s5_c_re'][l], p['s5_c_im'][l])
    out['out_vecs'] = jnp.stack([p['rwkv_r_k'][l].reshape(-1), p['rwkv_ln_g'][l], p['rwkv_ln_b'][l],
                                 p['ret_gn_g'][l], p['s5_d'][l]], axis=0)
    out['gluw'] = p['s5_glu_w'][l].astype(BF16)
    out['glub'] = p['s5_glu_b'][l][None]
    out['wo'] = p['w_out'][l].astype(BF16)
    out['n2'] = p['norm2_g'][l][None]
    out['wq'] = p['peer_wq'][l].T.astype(BF16)
    out['keys'] = p['peer_keys'][l].reshape(PEER_HEADS * 2, PEER_NKEYS, PEER_DKEY).astype(BF16)
    out['u'] = p['peer_u'][l].astype(BF16)
    out['vt'] = p['peer_v'][l].T.astype(BF16)
    return out


def kernel(x, c, ctx, c_ctx, norm1_g, norm2_g, ada_w, ada_b, w_in, w_out, rwkv_mu, rwkv_w0, rwkv_w_up, rwkv_a0, rwkv_a_up, rwkv_g_up, rwkv_k_k, rwkv_k_a, rwkv_r_k, rwkv_ln_g, rwkv_ln_b, ret_gn_g, lru_conv_w, lru_conv_b, lru_wa, lru_ba, lru_wx, lru_bx, lru_lambda, s5_a_re, s5_a_im, s5_log_dt, s5_b_re, s5_b_im, s5_c_re, s5_c_im, s5_d, s5_glu_w, s5_glu_b, peer_wq, peer_keys, peer_u, peer_v, final_norm_g):
    p = dict(w_in=w_in, w_out=w_out, norm2_g=norm2_g, rwkv_mu=rwkv_mu, rwkv_w0=rwkv_w0, rwkv_w_up=rwkv_w_up,
             rwkv_a0=rwkv_a0, rwkv_a_up=rwkv_a_up, rwkv_g_up=rwkv_g_up, rwkv_k_k=rwkv_k_k, rwkv_k_a=rwkv_k_a,
             rwkv_r_k=rwkv_r_k, rwkv_ln_g=rwkv_ln_g, rwkv_ln_b=rwkv_ln_b, ret_gn_g=ret_gn_g,
             lru_conv_w=lru_conv_w, lru_conv_b=lru_conv_b, lru_wa=lru_wa, lru_ba=lru_ba, lru_wx=lru_wx,
             lru_bx=lru_bx, lru_lambda=lru_lambda, s5_a_re=s5_a_re, s5_a_im=s5_a_im, s5_log_dt=s5_log_dt,
             s5_b_re=s5_b_re, s5_b_im=s5_b_im, s5_c_re=s5_c_re, s5_c_im=s5_c_im, s5_d=s5_d,
             s5_glu_w=s5_glu_w, s5_glu_b=s5_glu_b, peer_wq=peer_wq, peer_keys=peer_keys, peer_u=peer_u,
             peer_v=peer_v)
    nb, l_lat, d = x.shape
    l_ctx = ctx.shape[1]
    lt = l_ctx + l_lat
    depth = w_in.shape[0]
    assert 128 % (nb * N_HEADS) == 0 and l_ctx % 128 == 0 and l_lat % 128 == 0
    assert (l_ctx * nb) % PEER_TOK_TILE == 0 and (l_lat * nb) % PEER_TOK_TILE == 0
    k_lo = LANES // (nb * N_HEADS)
    n_ctx_tiles = l_ctx // ROW_TILE_T
    n_ctx_tok_tiles = l_ctx * nb // PEER_TOK_TILE
    hs = _head_sum_matrix()
    ret_tabs = _ret_tables(lt)

    xs = jnp.transpose(jnp.concatenate([ctx, x], axis=1), (1, 0, 2))
    mod_rows = 8 * ((nb + 1 + 7) // 8)
    cvec = jnp.zeros((mod_rows, d), F32).at[:nb].set(c).at[nb].set(c_ctx)

    for l in range(depth):
        wl = _layer_weights(l, p)
        last = l == depth - 1
        mod = _modulation(cvec, ada_w[l], ada_b[l])
        mod_lat = mod[:nb].reshape(nb, N_MOD, d).transpose(1, 0, 2)
        mod_ctx = jnp.broadcast_to(mod[nb].reshape(N_MOD, 1, d), (N_MOD, nb, d))
        modv = jnp.stack([mod_ctx, mod_lat])

        za, zb, zc, zd = _in_proj(xs, norm1_g[l], modv, wl['w_in'], n_ctx_tiles)

        r, k, v, g, av, w_dec, kd, bv = _rwkv_features(za, wl['mu'], wl['gup'], wl['wup'], wl['aup'],
                                                       wl['rwkv_vecs'], hs, n_ctx_tiles)
        ex = lambda t: _rwkv_expand_k(t, k_lo)
        ya_f, ya_b = _rwkv_scan(ex(r), ex(av), ex(w_dec), ex(kd), ex(bv), _rwkv_expand_v(v), l_ctx)
        ya = jnp.stack([_rwkv_collapse_v(ya_f, nb), _rwkv_collapse_v(ya_b, nb)])
        ob = _retention(jnp.transpose(zb, (1, 0, 2)), ret_tabs, l_ctx)
        ob = jnp.transpose(ob, (0, 2, 1, 3))
        hc = _lru(zc, wl['cw'], wl['wa'], wl['wx'], wl['lru_vecs'], l_ctx)
        yd = _from_colmajor(_s5(_to_colmajor(zd, l_ctx), *wl['s5'], l_ctx), l_ctx)

        t0 = n_ctx_tiles if last else 0
        x1, h2 = _out_proj(xs, modv, ya, (r, k, v, g), ob, zb, hc, zc, yd, zd, wl['out_vecs'], hs,
                           wl['gluw'], wl['glub'], wl['wo'], wl['n2'], n_ctx_tiles, t0)
        t0p = n_ctx_tok_tiles if last else 0
        e1, cnt, r2, w2 = _peer_route(h2, wl['wq'], wl['keys'], t0p)
        x2 = _peer_ffn(h2, wl['u'], wl['vt'], e1, cnt, r2, w2, x1.reshape(lt * nb, d), modv,
                       n_ctx_tok_tiles, t0p)
        xs = x2.reshape(lt, nb, d)

    out = _final_norm(xs[l_ctx:].reshape(l_lat * nb, d), final_norm_g)
    return jnp.transpose(out.reshape(l_lat, nb, d), (1, 0, 2))
```

```python
import functools
import math

import jax
import jax.numpy as jnp
import numpy as np
from jax import lax
from jax.experimental import pallas as pl
from jax.experimental.pallas import tpu as pltpu

F32 = jnp.float32
BF16 = jnp.bfloat16

NORM_EPS = 1e-6
N_MOD = 6
HEAD_DIM = 64
N_HEADS = 4
GROUP_W = N_HEADS * HEAD_DIM
GRID_W = 64

RWKV_GATE_LORA = 64
RWKV_LORA = 32
RWKV_GN_EPS = 64e-5
RWKV_COLS = 3 * GROUP_W + RWKV_GATE_LORA + 4 * RWKV_LORA
ZA_W = 3 * GROUP_W + 3 * 128

RET_CHUNK = 128
RET_GN_EPS = 1e-5
RET_FWD_OFFSET = 5.0
RET_BWD_OFFSET = 5.5
ROPE_BASE = 10000.0
RET_COLS = 4 * GROUP_W

LRU_CONV_W = 4
LRU_C = 8.0
LRU_COLS = 2 * GROUP_W
LRU_CHUNK = 128

S5_CH = 16
S5_GROUPS = GROUP_W // S5_CH
S5_STATE = 64
S5_N = S5_GROUPS * S5_STATE
S5_CHUNK = 128

PEER_HEADS = 8
PEER_NKEYS = 128
PEER_DKEY = 128
PEER_TOPK = 16
PEER_TOK_TILE = 512
PEER_EXP_BLOCK = 2048

ROW_TILE_T = 32
RWKV_CHUNK = 64

VMEM_LIMIT_BYTES = 56 * 1024 * 1024

NEG_INF = float("-inf")


def _cparams(n_axes):
    return pltpu.CompilerParams(dimension_semantics=("arbitrary",) * n_axes,
                                vmem_limit_bytes=VMEM_LIMIT_BYTES)


def _sigmoid(x):
    return 1.0 / (1.0 + jnp.exp(-x))


def _gelu_tanh(x):
    return 0.5 * x * (1.0 + jnp.tanh(math.sqrt(2.0 / math.pi) * (x + 0.044715 * (x * x * x))))


def _softplus(x):
    return jnp.maximum(x, 0.0) + jnp.log(1.0 + jnp.exp(-jnp.abs(x)))


def _dot(a, b):
    return jnp.dot(a, b, preferred_element_type=F32)


def _dot_exact(a, b):
    return jnp.dot(a, b, preferred_element_type=F32, precision=lax.Precision.HIGHEST)


def _head_sum_matrix():
    idx = np.arange(GROUP_W) // HEAD_DIM
    return jnp.asarray((idx[:, None] == idx[None, :]).astype(np.float32))


def _mod_body(c_ref, w_ref, b_ref, o_ref):
    c = c_ref[...]
    s = (c * _sigmoid(c)).astype(BF16)
    o_ref[...] = _dot(s, w_ref[...].astype(BF16)) + b_ref[...]


def _modulation(cvec, w, b):
    rows, d = cvec.shape
    n = w.shape[1]
    tn = 512
    return pl.pallas_call(
        _mod_body,
        grid=(n // tn,),
        in_specs=[pl.BlockSpec((rows, d), lambda j: (0, 0)),
                  pl.BlockSpec((d, tn), lambda j: (0, j)),
                  pl.BlockSpec((1, tn), lambda j: (0, j))],
        out_specs=pl.BlockSpec((rows, tn), lambda j: (0, j)),
        out_shape=jax.ShapeDtypeStruct((rows, n), F32),
        compiler_params=_cparams(1),
        name="adaln_mod",
    )(cvec, w, b.reshape(1, n))


def _in_body(x_ref, g_ref, mod_ref, w_ref, za_ref, zb_ref, zc_ref, zd_ref):
    tt, nb, d = x_ref.shape
    x = x_ref[...]
    ms = jnp.mean(x * x, axis=-1, keepdims=True)
    y = x * lax.rsqrt(ms + NORM_EPS) * g_ref[...]
    h = y * (1.0 + mod_ref[1]) + mod_ref[0]
    hb = h.reshape(tt * nb, d).astype(BF16)
    c0 = 0
    for ref in (za_ref, zb_ref, zc_ref, zd_ref):
        w = ref.shape[-1]
        ref[...] = _dot(hb, w_ref[:, c0:c0 + w]).reshape(tt, nb, w)
        c0 += w


def _in_proj(xs, g, modv, w_p, n_ctx_tiles):
    lt, nb, d = xs.shape
    tt = ROW_TILE_T
    widths = (ZA_W, RET_COLS, LRU_COLS, GROUP_W)
    seg = lambda i: jnp.where(i >= n_ctx_tiles, 1, 0)
    return pl.pallas_call(
        _in_body,
        grid=(lt // tt,),
        in_specs=[pl.BlockSpec((tt, nb, d), lambda i: (i, 0, 0)),
                  pl.BlockSpec((1, d), lambda i: (0, 0)),
                  pl.BlockSpec((None, N_MOD, nb, d), lambda i: (seg(i), 0, 0, 0)),
                  pl.BlockSpec(w_p.shape, lambda i: (0, 0))],
        out_specs=[pl.BlockSpec((tt, nb, w), lambda i: (i, 0, 0)) for w in widths],
        out_shape=[jax.ShapeDtypeStruct((lt, nb, w), F32) for w in widths],
        compiler_params=_cparams(1),
        name="in_proj",
    )(xs, g.reshape(1, d), modv, w_p)


def _rwkv_feat_body(n_ctx_tiles, n_tiles,
                    z_ref, zp_ref, zn_ref, mu_ref, gup_ref, wup_ref, aup_ref, vec_ref, hs_ref,
                    r_ref, k_ref, v_ref, g_ref, av_ref, w_ref, kd_ref, bv_ref):
    i = pl.program_id(0)
    tt, nb, zw = z_ref.shape
    z0 = z_ref[...]
    has_prev = jnp.logical_and(i != 0, i != n_ctx_tiles)
    has_next = jnp.logical_and(i != n_ctx_tiles - 1, i != n_tiles - 1)
    hp = jnp.where(has_prev, zp_ref[...], 0.0)
    hn = jnp.where(has_next, zn_ref[...], 0.0)
    zp = jnp.concatenate([hp, z0[:-1]], axis=0)
    zn = jnp.concatenate([z0[1:], hn], axis=0)
    z = z0 + mu_ref[0] * (zp - z0) + mu_ref[1] * (zn - z0)
    z = z.reshape(tt * nb, zw)
    gw = GROUP_W
    r = z[:, 0:gw]
    k = z[:, gw:2 * gw]
    v = z[:, 2 * gw:3 * gw]
    g = _dot(_sigmoid(z[:, 3 * gw:3 * gw + 128]).astype(BF16), gup_ref[...])
    w_low = jnp.tanh(z[:, 3 * gw + 128:3 * gw + 256]).astype(BF16)
    a_low = z[:, 3 * gw + 256:3 * gw + 384].astype(BF16)
    k_k = vec_ref[0:1, :]
    k_a = vec_ref[1:2, :]
    kk = k * k_k
    ss = _dot_exact(kk * kk, hs_ref[...])
    kk = kk / jnp.maximum(jnp.sqrt(ss), 1e-12)
    shp = (tt, nb, gw)
    r_ref[...] = r.reshape(shp)
    k_ref[...] = k.reshape(shp)
    v_ref[...] = v.reshape(shp)
    g_ref[...] = g.reshape(shp)
    av_ref[...] = (-kk).reshape(shp)
    for d in range(2):
        w0 = vec_ref[2 + d:3 + d, :]
        a0 = vec_ref[4 + d:5 + d, :]
        w_log = -_softplus(-(w0 + _dot(w_low, wup_ref[d]))) - 0.5
        w_ref[d] = jnp.exp(-jnp.exp(w_log)).reshape(shp)
        a = _sigmoid(a0 + _dot(a_low, aup_ref[d]))
        kd_ref[d] = (k * (1.0 + (a - 1.0) * k_a)).reshape(shp)
        bv_ref[d] = (kk * a).reshape(shp)


def _rwkv_features(za, mu_p, gup_p, wup_p, aup_p, vecs, hs, n_ctx_tiles):
    lt, nb, zw = za.shape
    tt = ROW_TILE_T
    n_tiles = lt // tt
    gw = GROUP_W
    one = lambda shape: jax.ShapeDtypeStruct(shape, F32)
    full = lambda a: pl.BlockSpec(a.shape, lambda i: (0,) * a.ndim)
    o1 = pl.BlockSpec((tt, nb, gw), lambda i: (i, 0, 0))
    o2 = pl.BlockSpec((2, tt, nb, gw), lambda i: (0, i, 0, 0))
    return pl.pallas_call(
        functools.partial(_rwkv_feat_body, n_ctx_tiles, n_tiles),
        grid=(n_tiles,),
        in_specs=[pl.BlockSpec((tt, nb, zw), lambda i: (i, 0, 0)),
                  pl.BlockSpec((1, nb, zw), lambda i: (jnp.maximum(i * tt - 1, 0), 0, 0)),
                  pl.BlockSpec((1, nb, zw), lambda i: (jnp.minimum((i + 1) * tt, lt - 1), 0, 0)),
                  full(mu_p), full(gup_p), full(wup_p), full(aup_p), full(vecs), full(hs)],
        out_specs=[o1, o1, o1, o1, o1, o2, o2, o2],
        out_shape=[one((lt, nb, gw))] * 5 + [one((2, lt, nb, gw))] * 3,
        compiler_params=_cparams(1),
        name="rwkv_features",
    )(za, za, za, mu_p, gup_p, wup_p, aup_p, vecs, hs)


def _rwkv_scan_body(rf_ref, af_ref, wf_ref, kdf_ref, bf_ref, vf_ref,
                    rb_ref, ab_ref, wb_ref, kdb_ref, bb_ref, vb_ref,
                    yf_ref, yb_ref, s_ref, vx_ref):
    j = pl.program_id(0)
    tc, n_khi, lanes = rf_ref.shape
    n_v, n_bh = vf_ref.shape[1:]
    k_lo = lanes // n_bh
    n_half = 2
    hv = n_v // n_half

    @pl.when(j == 0)
    def _():
        s_ref[...] = jnp.zeros_like(s_ref)

    for d, ref in enumerate((vf_ref, vb_ref)):
        x = ref[...]
        reps = 1
        while reps < k_lo:
            x = jnp.concatenate([x, x], axis=-1)
            reps *= 2
        vx_ref[d] = x

    def group_sum(x):
        shift = n_bh
        while shift < lanes:
            x = x + pltpu.roll(x, shift, axis=1)
            shift *= 2
        return x

    dirs = ((rf_ref, af_ref, wf_ref, kdf_ref, bf_ref, yf_ref), (rb_ref, ab_ref, wb_ref, kdb_ref, bb_ref, yb_ref))

    segs = [(d, slice(half * hv, (half + 1) * hv)) for d in range(2) for half in range(n_half)]

    def step(i, carry):
        t = (i, tc - 1 - i)
        row = lambda ref, d, kh: ref[t[d], pl.ds(kh, 1), :]
        ip = jnp.maximum(i - 1, 0)
        tp = (ip, tc - 1 - ip)
        for n, (d, rows) in enumerate(segs):
            dirs[d][5][tp[d], rows, :] = group_sum(carry[n])[:, :n_bh]
        acc = [None] * len(segs)
        for kh in range(n_khi):
            a_row = [row(dirs[d][1], d, kh) for d in range(2)]
            for n, (d, rows) in enumerate(segs):
                term = s_ref[d, kh, rows, :] * a_row[d]
                acc[n] = term if acc[n] is None else acc[n] + term
        sa = [group_sum(a) for a in acc]
        for kh in range(n_khi):
            w_row, kd_row = ([row(dirs[d][c], d, kh) for d in range(2)] for c in (2, 3))
            for n, (d, rows) in enumerate(segs):
                s_ref[d, kh, rows, :] = s_ref[d, kh, rows, :] * w_row[d] + vx_ref[d, t[d], rows, :] * kd_row[d]
        acc = [None] * len(segs)
        for kh in range(n_khi):
            b_row, r_row = ([row(dirs[d][c], d, kh) for d in range(2)] for c in (4, 0))
            for n, (d, rows) in enumerate(segs):
                s_new = s_ref[d, kh, rows, :] + sa[n] * b_row[d]
                s_ref[d, kh, rows, :] = s_new
                term = s_new * r_row[d]
                acc[n] = term if acc[n] is None else acc[n] + term
        return tuple(acc)

    zero = jnp.zeros((hv, lanes), F32)
    last = lax.fori_loop(0, tc, step, (zero,) * len(segs))
    t_last = (tc - 1, 0)
    for n, (d, rows) in enumerate(segs):
        dirs[d][5][t_last[d], rows, :] = group_sum(last[n])[:, :n_bh]


def _dir_chunk(d, j, n_ctx, n_all):
    bwd = jnp.where(j < n_ctx, n_ctx - 1 - j, n_all - 1 - (j - n_ctx))
    return jnp.where(d == 0, j, bwd)


def _rwkv_scan(r_e, a_e, w_e, kd_e, b_e, v_e, l_ctx):
    lt, n_khi, lanes = r_e.shape
    _, n_v, n_bh = v_e.shape
    tc = RWKV_CHUNK
    n_all, n_ctx = lt // tc, l_ctx // tc
    specs = []
    for d in range(2):
        cm = functools.partial(_dir_chunk, d, n_ctx=n_ctx, n_all=n_all)
        shared = pl.BlockSpec((tc, n_khi, lanes), lambda j, cm=cm: (cm(j), 0, 0))
        per_dir = pl.BlockSpec((None, tc, n_khi, lanes), lambda j, cm=cm, d=d: (d, cm(j), 0, 0))
        tok = pl.BlockSpec((tc, n_v, n_bh), lambda j, cm=cm: (cm(j), 0, 0))
        specs.append(([shared, shared, per_dir, per_dir, per_dir, tok], tok))
    return pl.pallas_call(
        _rwkv_scan_body,
        grid=(n_all,),
        in_specs=specs[0][0] + specs[1][0],
        out_specs=[specs[0][1], specs[1][1]],
        out_shape=[jax.ShapeDtypeStruct((lt, n_v, n_bh), F32)] * 2,
        scratch_shapes=[pltpu.VMEM((2, n_khi, n_v, lanes), F32), pltpu.VMEM((2, tc, n_v, lanes), F32)],
        compiler_params=_cparams(1),
        name="rwkv_scan",
    )(r_e, a_e, w_e, kd_e, b_e, v_e, r_e, a_e, w_e, kd_e, b_e, v_e)


def _rwkv_expand_k(x, k_lo):
    lead = x.shape[:-2]
    nb = x.shape[-2]
    nd = len(lead)
    x = x.reshape(lead + (nb, N_HEADS, k_lo, HEAD_DIM // k_lo))
    x = jnp.transpose(x, tuple(range(nd)) + (nd + 3, nd + 2, nd, nd + 1))
    return x.reshape(lead + (HEAD_DIM // k_lo, k_lo * nb * N_HEADS))


def _rwkv_expand_v(x):
    lt, nb, _ = x.shape
    return jnp.transpose(x.reshape(lt, nb, N_HEADS, HEAD_DIM), (0, 3, 1, 2)).reshape(lt, HEAD_DIM, nb * N_HEADS)


def _rwkv_collapse_v(y, nb):
    lt = y.shape[0]
    return jnp.transpose(y.reshape(lt, HEAD_DIM, nb, N_HEADS), (0, 2, 3, 1)).reshape(lt, nb, GROUP_W)


def _ret_body(z_ref, cos_ref, sin_ref, dec_ref, rd_ref, wd_ref, cd_ref, hm_ref, o_ref, s_ref):
    j = pl.program_id(2)
    gw = GROUP_W
    half = gw // 2

    @pl.when(j == 0)
    def _():
        s_ref[...] = jnp.zeros_like(s_ref)

    cos = cos_ref[...]
    sin = sin_ref[...]

    def rope(x):
        x1, x2 = x[:, :half], x[:, half:]
        return jnp.concatenate([x1 * cos - x2 * sin, x1 * sin + x2 * cos], axis=-1)

    q = rope(z_ref[:, 0:gw]) * (HEAD_DIM ** -0.5)
    k = rope(z_ref[:, gw:2 * gw])
    v = z_ref[:, 2 * gw:3 * gw]
    vb = v.astype(BF16)
    k_t = k.T
    k_tb = k_t.astype(BF16)
    inter = _dot(q.astype(BF16), s_ref[...].astype(BF16)) * rd_ref[...]
    intra = jnp.zeros_like(inter)
    for h in range(N_HEADS):
        qm = (q * hm_ref[h:h + 1, :]).astype(BF16)
        att = _dot(qm, k_tb) * dec_ref[h]
        vm = (v * hm_ref[N_HEADS + h:N_HEADS + h + 1, :]).astype(BF16)
        intra = intra + _dot(att.astype(BF16), vm)
    o_ref[...] = intra + inter
    kv = _dot((k_t * wd_ref[...]).astype(BF16), vb)
    s_ref[...] = s_ref[...] * cd_ref[0] + kv * cd_ref[1]


def _retention(zb_bm, tabs, l_ctx):
    nb, lt, _ = zb_bm.shape
    c = RET_CHUNK
    n_all, n_ctx = lt // c, l_ctx // c
    gw = GROUP_W
    cos, sin, dec, rd, wd, cd, hm = tabs
    cm = lambda d, j: _dir_chunk(d, j, n_ctx, n_all)
    return pl.pallas_call(
        _ret_body,
        grid=(nb, 2, n_all),
        in_specs=[pl.BlockSpec((None, c, RET_COLS), lambda b, d, j: (b, cm(d, j), 0)),
                  pl.BlockSpec((c, gw // 2), lambda b, d, j: (cm(d, j), 0)),
                  pl.BlockSpec((c, gw // 2), lambda b, d, j: (cm(d, j), 0)),
                  pl.BlockSpec((None, N_HEADS, c, c), lambda b, d, j: (d, 0, 0, 0)),
                  pl.BlockSpec((None, c, gw), lambda b, d, j: (d, 0, 0)),
                  pl.BlockSpec((None, gw, c), lambda b, d, j: (d, 0, 0)),
                  pl.BlockSpec((None, 2, gw, gw), lambda b, d, j: (d, 0, 0, 0)),
                  pl.BlockSpec((2 * N_HEADS, gw), lambda b, d, j: (0, 0))],
        out_specs=pl.BlockSpec((None, None, c, gw), lambda b, d, j: (d, b, cm(d, j), 0)),
        out_shape=jax.ShapeDtypeStruct((2, nb, lt, gw), F32),
        scratch_shapes=[pltpu.VMEM((gw, gw), F32)],
        compiler_params=_cparams(3),
        name="retention",
    )(zb_bm, cos, sin, dec, rd, wd, cd, hm)


def _ret_perm():
    new = np.arange(GROUP_W)
    half_id, rem = new // 128, new % 128
    h, i = rem // 32, rem % 32
    return h * HEAD_DIM + half_id * 32 + i


def _ret_tables(lt):
    c = RET_CHUNK
    half = HEAD_DIM // 2
    freqs = ROPE_BASE ** (-jnp.arange(half, dtype=F32) / half)
    ang = jnp.arange(lt, dtype=F32)[:, None] * freqs[None]
    cos = jnp.tile(jnp.cos(ang), (1, N_HEADS))
    sin = jnp.tile(jnp.sin(ang), (1, N_HEADS))
    idx = jnp.arange(c, dtype=F32)
    perm = _ret_perm()
    head_of_qk = jnp.asarray(perm // HEAD_DIM)
    head_of_v = jnp.arange(GROUP_W) // HEAD_DIM
    dec, rd, wd, cd = [], [], [], []
    for offset, rev in ((RET_FWD_OFFSET, False), (RET_BWD_OFFSET, True)):
        lg = jnp.log1p(-jnp.exp2(-(offset + jnp.arange(N_HEADS, dtype=F32))))
        diff = idx[:, None] - idx[None, :]
        if rev:
            mask, dist = diff < 0, -diff
            read_pow, write_pow = c - idx, idx
        else:
            mask, dist = diff >= 0, diff
            read_pow, write_pow = idx + 1.0, c - 1.0 - idx
        dec.append(jnp.where(mask[None], jnp.exp(lg[:, None, None] * jnp.where(mask, dist, 0.0)[None]), 0.0))
        rd.append(jnp.exp(lg[head_of_v][None, :] * read_pow[:, None]))
        wd.append(jnp.exp(lg[head_of_qk][:, None] * write_pow[None, :]))
        same = (head_of_qk[:, None] == head_of_v[None, :]).astype(F32)
        cd.append(jnp.stack([same * jnp.exp(lg * c)[head_of_v][None, :], same]))
    hm_q = (head_of_qk[None, :] == jnp.arange(N_HEADS)[:, None]).astype(F32)
    hm_v = (head_of_v[None, :] == jnp.arange(N_HEADS)[:, None]).astype(F32)
    return cos, sin, jnp.stack(dec), jnp.stack(rd), jnp.stack(wd), jnp.stack(cd), jnp.concatenate([hm_q, hm_v])


def _lru_body(n_ctx, n_all, x_ref, xp_ref, xn_ref, cw_ref, wa_ref, wx_ref, vec_ref, o_ref,
              a_s, b_s, h_s):
    d = pl.program_id(0)
    j = pl.program_id(1)
    tc, nb, gw = x_ref.shape
    ch = _dir_chunk(d, j, n_ctx, n_all)

    @pl.when(j == 0)
    def _():
        h_s[...] = jnp.zeros_like(h_s)

    has_prev = jnp.logical_and(ch != 0, ch != n_ctx)
    has_next = jnp.logical_and(ch != n_ctx - 1, ch != n_all - 1)
    xe = jnp.concatenate([jnp.where(has_prev, xp_ref[...], 0.0), x_ref[...],
                          jnp.where(has_next, xn_ref[...], 0.0)], axis=0)
    xc = cw_ref[LRU_CONV_W:LRU_CONV_W + 1, :]
    for tap in range(LRU_CONV_W):
        xc = xc + cw_ref[tap:tap + 1, :] * xe[tap:tap + tc]
    xc = xc.reshape(tc * nb, gw)
    xb = xc.astype(BF16)
    r = _sigmoid(_dot(xb, wa_ref[...]) + vec_ref[0:1, :])
    gi = _sigmoid(_dot(xb, wx_ref[...]) + vec_ref[1:2, :])
    log_a = -LRU_C * r * _softplus(-vec_ref[2:3, :])
    th = jnp.tanh(log_a)
    one_minus_a2 = 2.0 * th / (th - 1.0)
    a_s[...] = jnp.exp(log_a).reshape(tc, nb, gw)
    b_s[...] = (jnp.sqrt(one_minus_a2) * (gi * xc)).reshape(tc, nb, gw)

    def step(i, h):
        t = jnp.where(d == 0, i, tc - 1 - i)
        h = a_s[t] * h + b_s[t]
        o_ref[t] = h
        return h

    h_s[...] = lax.fori_loop(0, tc, step, h_s[...])


def _lru(zc, cw, wa_bd, wx_bd, vecs, l_ctx):
    lt, nb, _ = zc.shape
    gw = GROUP_W
    tc = LRU_CHUNK
    n_all, n_ctx = lt // tc, l_ctx // tc
    cm = lambda d, j: _dir_chunk(d, j, n_ctx, n_all)
    return pl.pallas_call(
        functools.partial(_lru_body, n_ctx, n_all),
        grid=(2, n_all),
        in_specs=[pl.BlockSpec((tc, nb, gw), lambda d, j: (cm(d, j), 0, 0)),
                  pl.BlockSpec((2, nb, gw), lambda d, j: (jnp.maximum(cm(d, j) * (tc // 2) - 1, 0), 0, 0)),
                  pl.BlockSpec((1, nb, gw), lambda d, j: (jnp.minimum((cm(d, j) + 1) * tc, lt - 1), 0, 0)),
                  pl.BlockSpec(cw.shape, lambda d, j: (0, 0)),
                  pl.BlockSpec((None, gw, gw), lambda d, j: (d, 0, 0)),
                  pl.BlockSpec((None, gw, gw), lambda d, j: (d, 0, 0)),
                  pl.BlockSpec((None, 3, gw), lambda d, j: (d, 0, 0))],
        out_specs=pl.BlockSpec((None, tc, nb, gw), lambda d, j: (d, cm(d, j), 0, 0)),
        out_shape=jax.ShapeDtypeStruct((2, lt, nb, gw), F32),
        scratch_shapes=[pltpu.VMEM((tc, nb, gw), F32), pltpu.VMEM((tc, nb, gw), F32),
                        pltpu.VMEM((nb, gw), F32)],
        compiler_params=_cparams(2),
        name="rglru",
    )(zc, zc, zc, cw, wa_bd, wx_bd, vecs)


def _block_diag(w):
    nblk, n = w.shape[-3], w.shape[-1]
    eye = jnp.eye(nblk, dtype=w.dtype)
    out = w[..., :, :, None, :] * eye[:, None, :, None]
    return out.reshape(w.shape[:-3] + (nblk * n, nblk * n))


def _s5_body(u_ref, bm_ref, lam_ref, cm_ref, y_ref, h_s, st_s):
    d = pl.program_id(0)
    j = pl.program_id(1)
    tc, nb, gw = u_ref.shape
    n = S5_N

    @pl.when(j == 0)
    def _():
        st_s[...] = jnp.zeros_like(st_s)

    u = u_ref[...].reshape(tc * nb, gw).astype(BF16)
    h_s[...] = _dot(u, bm_ref[...]).reshape(tc, nb, 2 * n)
    lr = lam_ref[0:1, :]
    li = lam_ref[1:2, :]

    def step(i, carry):
        hr, hi = carry
        t = jnp.where(d == 0, i, tc - 1 - i)
        bu = h_s[t]
        nr = lr * hr - li * hi + bu[:, :n]
        ni = lr * hi + li * hr + bu[:, n:]
        h_s[t] = jnp.concatenate([nr, ni], axis=-1)
        return nr, ni

    hr, hi = lax.fori_loop(0, tc, step, (st_s[0], st_s[1]))
    st_s[0] = hr
    st_s[1] = hi
    hh = h_s[...].reshape(tc * nb, 2 * n).astype(BF16)
    y_ref[...] = _dot(hh, cm_ref[...]).reshape(tc, nb, gw)


def _s5(u, bmat, lam, cmat, l_ctx):
    lt, nb, gw = u.shape
    tc = S5_CHUNK
    n_all, n_ctx = lt // tc, l_ctx // tc
    cm = lambda d, j: _dir_chunk(d, j, n_ctx, n_all)
    return pl.pallas_call(
        _s5_body,
        grid=(2, n_all),
        in_specs=[pl.BlockSpec((tc, nb, gw), lambda d, j: (cm(d, j), 0, 0)),
                  pl.BlockSpec((None, gw, 2 * S5_N), lambda d, j: (d, 0, 0)),
                  pl.BlockSpec((None, 2, S5_N), lambda d, j: (d, 0, 0)),
                  pl.BlockSpec((None, 2 * S5_N, gw), lambda d, j: (d, 0, 0))],
        out_specs=pl.BlockSpec((None, tc, nb, gw), lambda d, j: (d, cm(d, j), 0, 0)),
        out_shape=jax.ShapeDtypeStruct((2, lt, nb, gw), F32),
        scratch_shapes=[pltpu.VMEM((tc, nb, 2 * S5_N), F32), pltpu.VMEM((2, nb, S5_N), F32)],
        compiler_params=_cparams(2),
        name="s5",
    )(u, bmat, lam, cmat)


def _s5_params(a_re, a_im, log_dt, b_re, b_im, c_re, c_im):
    dt = jnp.exp(log_dt)[..., None]
    er = jnp.exp(a_re * dt)
    lbr, lbi = er * jnp.cos(a_im * dt), er * jnp.sin(a_im * dt)
    den = a_re * a_re + a_im * a_im
    nr, ni = lbr - 1.0, lbi
    fr = (nr * a_re + ni * a_im) / den
    fi = (ni * a_re - nr * a_im) / den
    bbr = fr[..., None] * b_re - fi[..., None] * b_im
    bbi = fr[..., None] * b_im + fi[..., None] * b_re
    eye = jnp.eye(S5_GROUPS, dtype=F32)

    def in_map(bb):
        m = jnp.einsum('dgpc,gh->dgchp', bb, eye)
        return m.reshape(2, GROUP_W, S5_N)

    def out_map(cc):
        m = jnp.einsum('dgcp,gh->dgphc', cc, eye)
        return m.reshape(2, S5_N, GROUP_W)

    bmat = jnp.concatenate([in_map(bbr), in_map(bbi)], axis=-1).astype(BF16)
    cmat = jnp.concatenate([out_map(c_re), -out_map(c_im)], axis=-2).astype(BF16)
    lam = jnp.stack([lbr.reshape(2, S5_N), lbi.reshape(2, S5_N)], axis=1)
    return bmat, lam, cmat


def _to_colmajor(z, l_ctx):
    lt, nb, ch = z.shape
    rows = (lt - l_ctx) // GRID_W
    lat = z[l_ctx:].reshape(rows, GRID_W, nb, ch).transpose(1, 0, 2, 3).reshape(lt - l_ctx, nb, ch)
    return jnp.concatenate([z[:l_ctx], lat], axis=0)


def _from_colmajor(y, l_ctx):
    lt = y.shape[-3]
    nb, ch = y.shape[-2:]
    rows = (lt - l_ctx) // GRID_W
    lead = y.shape[:-3]
    lat = y[..., l_ctx:, :, :].reshape(lead + (GRID_W, rows, nb, ch))
    lat = jnp.swapaxes(lat, -4, -3).reshape(lead + (lt - l_ctx, nb, ch))
    return jnp.concatenate([y[..., :l_ctx, :, :], lat], axis=-3)


def _out_body(x_ref, mod_ref, ya_ref, r_ref, k_ref, v_ref, g_ref, ob_ref, gb_ref, hc_ref, gc_ref,
              yd_ref, ud_ref, vec_ref, hs_ref, gluw_ref, glub_ref, wo_ref, n2_ref, x1_ref, h2_ref):
    tt, nb, d = x_ref.shape
    gw = GROUP_W
    rows = tt * nb
    hs = hs_ref[...]
    two = lambda ref: (ref[0] + ref[1]).reshape(rows, gw)
    flat = lambda ref: ref[...].reshape(rows, gw)

    def head_norm(y, eps):
        mu = _dot_exact(y, hs) * (1.0 / HEAD_DIM)
        dlt = y - mu
        var = _dot_exact(dlt * dlt, hs) * (1.0 / HEAD_DIM)
        return dlt * lax.rsqrt(var + eps)

    r_k, ln_g, ln_b = vec_ref[0:1, :], vec_ref[1:2, :], vec_ref[2:3, :]
    r, k, v = flat(r_ref), flat(k_ref), flat(v_ref)
    yn = head_norm(two(ya_ref), RWKV_GN_EPS) * ln_g + ln_b
    bonus = _dot_exact(r * k * r_k, hs) * v
    mix_a = (yn + bonus) * flat(g_ref)
    gb = flat(gb_ref)
    mix_b = gb * _sigmoid(gb) * (head_norm(two(ob_ref), RET_GN_EPS) * vec_ref[3:4, :])
    mix_c = _gelu_tanh(flat(gc_ref)) * two(hc_ref)
    yd = vec_ref[4:5, :] * flat(ud_ref) + two(yd_ref)
    glu = _dot(_gelu_tanh(yd).astype(BF16), gluw_ref[...]) + glub_ref[...]
    mix_d = glu[:, :gw] * _sigmoid(glu[:, gw:])

    mix = _dot(mix_a.astype(BF16), wo_ref[0:gw, :])
    mix = mix + _dot(mix_b.astype(BF16), wo_ref[gw:2 * gw, :])
    mix = mix + _dot(mix_c.astype(BF16), wo_ref[2 * gw:3 * gw, :])
    mix = mix + _dot(mix_d.astype(BF16), wo_ref[3 * gw:4 * gw, :])
    x1 = x_ref[...] + mod_ref[2] * mix.reshape(tt, nb, d)
    x1_ref[...] = x1
    ms = jnp.mean(x1 * x1, axis=-1, keepdims=True)
    h2 = (x1 * lax.rsqrt(ms + NORM_EPS) * n2_ref[...]) * (1.0 + mod_ref[4]) + mod_ref[3]
    h2_ref[...] = h2.reshape(rows, d).T.astype(BF16)


def _out_proj(xs, modv, ya, feats, ob, zb, hc, zc, yd, zd, vecs, hs, gluw, glub, wo, n2, n_ctx_tiles, t0_tile):
    lt, nb, d = xs.shape
    tt = ROW_TILE_T
    gw = GROUP_W
    n_tiles = lt // tt - t0_tile
    seg = lambda i: jnp.where(i + t0_tile >= n_ctx_tiles, 1, 0)
    col = lambda c: pl.BlockSpec((tt, nb, gw), lambda i: (i + t0_tile, 0, c))
    two = pl.BlockSpec((2, tt, nb, gw), lambda i: (0, i + t0_tile, 0, 0))
    full = lambda a: pl.BlockSpec(a.shape, lambda i: (0,) * a.ndim)
    r, k, v, g = feats
    return pl.pallas_call(
        _out_body,
        grid=(n_tiles,),
        in_specs=[pl.BlockSpec((tt, nb, d), lambda i: (i + t0_tile, 0, 0)),
                  pl.BlockSpec((None, N_MOD, nb, d), lambda i: (seg(i), 0, 0, 0)),
                  two, col(0), col(0), col(0), col(0),
                  two, col(3), two, col(1), two, col(0),
                  full(vecs), full(hs), full(gluw), full(glub), full(wo), full(n2)],
        out_specs=[pl.BlockSpec((tt, nb, d), lambda i: (i + t0_tile, 0, 0)),
                   pl.BlockSpec((d, tt * nb), lambda i: (0, i + t0_tile))],
        out_shape=[jax.ShapeDtypeStruct((lt, nb, d), F32), jax.ShapeDtypeStruct((d, lt * nb), BF16)],
        compiler_params=_cparams(1),
        name="out_proj",
    )(xs, modv, ya, r, k, v, g, ob, zb, hc, zc, yd, zd, vecs, hs, gluw, glub, wo, n2)


PEER_NO_RANK = 127.0


def _top_values(s, n, with_rank=False):
    vals, cur = [], s
    rank = jnp.full(s.shape, PEER_NO_RANK, F32) if with_rank else None
    for k in range(n):
        m = jnp.max(cur, axis=0, keepdims=True)
        vals.append(m)
        hit = cur == m
        if with_rank:
            rank = jnp.where(hit, float(k), rank)
        cur = jnp.where(hit, NEG_INF, cur)
    return (vals, rank) if with_rank else vals


def _peer_route_body(h_ref, wq_ref, keys_ref, e1_ref, cnt_ref, r2_ref, w2_ref, pack_ref):
    hb = h_ref[...]
    for h in range(PEER_HEADS):
        st = []
        for p in range(2):
            c0 = (2 * h + p) * PEER_DKEY
            q = _dot(wq_ref[c0:c0 + PEER_DKEY, :], hb).astype(BF16)
            st.append(_dot(keys_ref[2 * h + p], q))
        s1, s2 = st
        n_top = PEER_TOPK + 1
        v1 = _top_values(s1, n_top)
        v2, rank2 = _top_values(s2, n_top, with_rank=True)
        pad7 = [jnp.full((7, s1.shape[1]), NEG_INF, F32)]
        v2m = jnp.concatenate(v2 + pad7, axis=0)
        v1_tail = jnp.concatenate(v1[8:] + pad7, axis=0)
        cand = jnp.concatenate([v1[0] + v2m] + [v1[a] + v2m[0:8] for a in range(1, 8)]
                               + [v1_tail + v2[0]], axis=0)
        top = _top_values(cand, n_top)
        tau = 0.5 * (top[PEER_TOPK - 1] + top[PEER_TOPK])
        cmax = top[0]
        z = jnp.sum(jnp.where(cand >= tau, jnp.exp(cand - cmax), 0.0), axis=0, keepdims=True)
        thr = tau - s1
        cnt = jnp.zeros_like(s1)
        for k in range(PEER_TOPK):
            cnt = cnt + jnp.where(v2[k] >= thr, 1.0, 0.0)
        e1_ref[:, h, :] = jnp.exp(s1 - v1[0])
        cnt_ref[:, h, :] = cnt
        pack_ref[0] = rank2.astype(BF16)
        pack_ref[1] = (jnp.exp(s2 - v2[0]) / z).astype(BF16)
        words = pack_ref.bitcast(jnp.uint32)
        r2_ref[h] = words[0]
        w2_ref[h] = words[1]


def _peer_route(h2t, wq, keys, t0_tile):
    d, t_all = h2t.shape
    tt = PEER_TOK_TILE
    nk = PEER_NKEYS
    by_i = pl.BlockSpec((nk, PEER_HEADS, tt), lambda i: (0, 0, i + t0_tile))
    by_h = pl.BlockSpec((PEER_HEADS, nk // 2, tt), lambda i: (0, 0, i + t0_tile))
    return pl.pallas_call(
        _peer_route_body,
        grid=(t_all // tt - t0_tile,),
        in_specs=[pl.BlockSpec((d, tt), lambda i: (0, i + t0_tile)),
                  pl.BlockSpec(wq.shape, lambda i: (0, 0)),
                  pl.BlockSpec(keys.shape, lambda i: (0, 0, 0))],
        out_specs=[by_i, by_i, by_h, by_h],
        out_shape=[jax.ShapeDtypeStruct((nk, PEER_HEADS, t_all), F32)] * 2
                  + [jax.ShapeDtypeStruct((PEER_HEADS, nk // 2, t_all), jnp.uint32)] * 2,
        scratch_shapes=[pltpu.VMEM((2, nk, tt), BF16)],
        compiler_params=_cparams(1),
        name="peer_route",
    )(h2t, wq, keys)


PEER_GATE_ROWS = 32
LANES = 128


BF16_ROWS = 16
SUBLANES = 8


def _peer_gate_block(p_ref, e1_ref, cnt_ref, r2_ref, w2_ref, bc_ref):
    nk = PEER_NKEYS
    n_sub = p_ref.shape[0] // nk
    rc = PEER_GATE_ROWS
    n_pv = rc // BF16_ROWS
    for c0 in range(0, p_ref.shape[1], LANES):
        cols = slice(c0, c0 + LANES)
        for ii in range(n_sub):
            for h in range(PEER_HEADS):
                bc_ref[0, ii, h] = jnp.broadcast_to(cnt_ref[ii, h:h + 1, cols], (BF16_ROWS, LANES)).astype(BF16)
                bc_ref[1, ii, h] = jnp.broadcast_to(e1_ref[ii, h:h + 1, cols], (BF16_ROWS, LANES)).astype(BF16)

        def chunk(q, carry, cols=cols):
            r0 = pl.multiple_of(q * rc, rc)
            gate = [[None] * n_pv for _ in range(n_sub)]
            for h in range(PEER_HEADS):
                rank = [pltpu.bitcast(r2_ref[h, q, SUBLANES * v:SUBLANES * (v + 1), cols], BF16)
                        for v in range(n_pv)]
                w2 = [pltpu.bitcast(w2_ref[h, q, SUBLANES * v:SUBLANES * (v + 1), cols], BF16)
                      for v in range(n_pv)]
                for ii in range(n_sub):
                    cnt = bc_ref[0, ii, h]
                    e1 = bc_ref[1, ii, h]
                    for v in range(n_pv):
                        term = jnp.where(rank[v] < cnt, w2[v] * e1, jnp.zeros_like(e1))
                        gate[ii][v] = term if gate[ii][v] is None else gate[ii][v] + term
            for ii in range(n_sub):
                for v in range(n_pv):
                    rows = pl.ds(ii * nk + r0 + BF16_ROWS * v, BF16_ROWS)
                    p_ref[rows, cols] = p_ref[rows, cols] * gate[ii][v]
            return carry

        lax.fori_loop(0, nk // rc, chunk, 0)


def _peer_ffn_body(h_ref, u_ref, vt_ref, e1_ref, cnt_ref, r2_ref, w2_ref, x_ref, mod_ref, o_ref,
                   acc_ref, p_ref, bc_ref):
    j = pl.program_id(1)

    @pl.when(j == 0)
    def _():
        acc_ref[...] = jnp.zeros_like(acc_ref)

    act = _dot(u_ref[...], h_ref[...])
    p_ref[...] = _gelu_tanh(act.astype(BF16))
    _peer_gate_block(p_ref, e1_ref, cnt_ref, r2_ref, w2_ref, bc_ref)
    acc_ref[...] += _dot(vt_ref[...], p_ref[...])

    @pl.when(j == pl.num_programs(1) - 1)
    def _():
        tt, d = x_ref.shape
        nb = mod_ref.shape[1]
        y = acc_ref[...].T.reshape(tt // nb, nb, d) * mod_ref[5]
        o_ref[...] = x_ref[...] + y.reshape(tt, d)


def _peer_ffn(h2t, u_b, vt_b, e1, cnt, r2, w2, x1, modv, n_ctx_tok_tiles, t0_tile):
    d, t_all = h2t.shape
    tt = PEER_TOK_TILE
    eb = PEER_EXP_BLOCK
    nk = PEER_NKEYS
    nb = modv.shape[2]
    n_e = u_b.shape[0] // eb
    n_tiles = t_all // tt - t0_tile
    seg = lambda i: jnp.where(i + t0_tile >= n_ctx_tok_tiles, 1, 0)
    by_i = pl.BlockSpec((eb // nk, PEER_HEADS, tt), lambda i, j: (j, 0, i + t0_tile))
    rc = PEER_GATE_ROWS
    by_h = pl.BlockSpec((PEER_HEADS, nk // rc, rc // 2, tt), lambda i, j: (0, 0, 0, i + t0_tile))
    r2, w2 = (a.reshape(PEER_HEADS, nk // rc, rc // 2, t_all) for a in (r2, w2))
    return pl.pallas_call(
        _peer_ffn_body,
        grid=(n_tiles, n_e),
        in_specs=[pl.BlockSpec((d, tt), lambda i, j: (0, i + t0_tile)),
                  pl.BlockSpec((eb, d), lambda i, j: (j, 0)),
                  pl.BlockSpec((d, eb), lambda i, j: (0, j)),
                  by_i, by_i, by_h, by_h,
                  pl.BlockSpec((tt, d), lambda i, j: (i + t0_tile, 0)),
                  pl.BlockSpec((None, N_MOD, nb, d), lambda i, j: (seg(i), 0, 0, 0))],
        out_specs=pl.BlockSpec((tt, d), lambda i, j: (i + t0_tile, 0)),
        out_shape=jax.ShapeDtypeStruct((t_all, d), F32),
        scratch_shapes=[pltpu.VMEM((d, tt), F32), pltpu.VMEM((eb, tt), BF16),
                        pltpu.VMEM((2, eb // nk, PEER_HEADS, BF16_ROWS, LANES), BF16)],
        compiler_params=_cparams(2),
        name="peer_ffn",
    )(h2t, u_b, vt_b, e1, cnt, r2, w2, x1, modv)


def _final_body(x_ref, g_ref, o_ref):
    x = x_ref[...]
    ms = jnp.mean(x * x, axis=-1, keepdims=True)
    o_ref[...] = x * lax.rsqrt(ms + NORM_EPS) * g_ref[...]


def _final_norm(x2d, g):
    t, d = x2d.shape
    tr = 512
    return pl.pallas_call(
        _final_body,
        grid=(t // tr,),
        in_specs=[pl.BlockSpec((tr, d), lambda i: (i, 0)), pl.BlockSpec((1, d), lambda i: (0, 0))],
        out_specs=pl.BlockSpec((tr, d), lambda i: (i, 0)),
        out_shape=jax.ShapeDtypeStruct((t, d), F32),
        compiler_params=_cparams(1),
        name="final_norm",
    )(x2d, g.reshape(1, d))


def _pad_cols(w, width):
    return jnp.pad(w, [(0, 0)] * (w.ndim - 1) + [(0, width - w.shape[-1])])


def _rwkv_slabs(w):
    gw3 = 3 * GROUP_W
    parts = [w[..., :gw3],
             _pad_cols(w[..., gw3:gw3 + 64], 128),
             _pad_cols(w[..., gw3 + 64:gw3 + 128], 128),
             _pad_cols(w[..., gw3 + 128:gw3 + 192], 128)]
    return jnp.concatenate(parts, axis=-1)


def _layer_weights(l, p):
    d = p['w_in'].shape[1]
    w_in = p['w_in'][l]
    wa_, wb_, wc_, wd_ = jnp.split(w_in, [RWKV_COLS, RWKV_COLS + RET_COLS, RWKV_COLS + RET_COLS + LRU_COLS], axis=-1)
    perm = _ret_perm()
    gw = GROUP_W
    wb_ = jnp.concatenate([wb_[:, 0:gw][:, perm], wb_[:, gw:2 * gw][:, perm], wb_[:, 2 * gw:]], axis=-1)
    out = {'w_in': jnp.concatenate([_rwkv_slabs(wa_), wb_, wc_, wd_], axis=-1).astype(BF16)}
    out['mu'] = _rwkv_slabs(p['rwkv_mu'][l])[:, None, :]
    out['gup'] = jnp.pad(p['rwkv_g_up'][l], ((0, 64), (0, 0))).astype(BF16)
    lora = RWKV_LORA
    out['wup'] = jnp.stack([jnp.pad(p['rwkv_w_up'][l, dd], ((dd * lora, 128 - (dd + 1) * lora), (0, 0)))
                            for dd in range(2)]).astype(BF16)
    out['aup'] = jnp.stack([jnp.pad(p['rwkv_a_up'][l, dd], ((dd * lora, 128 - (dd + 1) * lora), (0, 0)))
                            for dd in range(2)]).astype(BF16)
    out['rwkv_vecs'] = jnp.concatenate([p['rwkv_k_k'][l][None], p['rwkv_k_a'][l][None],
                                        p['rwkv_w0'][l], p['rwkv_a0'][l]], axis=0)
    out['cw'] = jnp.concatenate([p['lru_conv_w'][l], p['lru_conv_b'][l][None]], axis=0)
    out['wa'] = _block_diag(p['lru_wa'][l]).astype(BF16)
    out['wx'] = _block_diag(p['lru_wx'][l]).astype(BF16)
    out['lru_vecs'] = jnp.stack([p['lru_ba'][l], p['lru_bx'][l], p['lru_lambda'][l]], axis=1)
    out['s5'] = _s5_params(p['s5_a_re'][l], p['s5_a_im'][l], p['s5_log_dt'][l], p['s5_b_re'][l],
                           p['s5_b_im'][l], p['s5_c_re'][l], p['s5_c_im'][l])
    out['out_vecs'] = jnp.stack([p['rwkv_r_k'][l].reshape(-1), p['rwkv_ln_g'][l], p['rwkv_ln_b'][l],
                                 p['ret_gn_g'][l], p['s5_d'][l]], axis=0)
    out['gluw'] = p['s5_glu_w'][l].astype(BF16)
    out['glub'] = p['s5_glu_b'][l][None]
    out['wo'] = p['w_out'][l].astype(BF16)
    out['n2'] = p['norm2_g'][l][None]
    out['wq'] = p['peer_wq'][l].T.astype(BF16)
    out['keys'] = p['peer_keys'][l].reshape(PEER_HEADS * 2, PEER_NKEYS, PEER_DKEY).astype(BF16)
    out['u'] = p['peer_u'][l].astype(BF16)
    out['vt'] = p['peer_v'][l].T.astype(BF16)
    return out


def kernel(x, c, ctx, c_ctx, norm1_g, norm2_g, ada_w, ada_b, w_in, w_out, rwkv_mu, rwkv_w0, rwkv_w_up, rwkv_a0, rwkv_a_up, rwkv_g_up, rwkv_k_k, rwkv_k_a, rwkv_r_k, rwkv_ln_g, rwkv_ln_b, ret_gn_g, lru_conv_w, lru_conv_b, lru_wa, lru_ba, lru_wx, lru_bx, lru_lambda, s5_a_re, s5_a_im, s5_log_dt, s5_b_re, s5_b_im, s5_c_re, s5_c_im, s5_d, s5_glu_w, s5_glu_b, peer_wq, peer_keys, peer_u, peer_v, final_norm_g):
    p = dict(w_in=w_in, w_out=w_out, norm2_g=norm2_g, rwkv_mu=rwkv_mu, rwkv_w0=rwkv_w0, rwkv_w_up=rwkv_w_up,
             rwkv_a0=rwkv_a0, rwkv_a_up=rwkv_a_up, rwkv_g_up=rwkv_g_up, rwkv_k_k=rwkv_k_k, rwkv_k_a=rwkv_k_a,
             rwkv_r_k=rwkv_r_k, rwkv_ln_g=rwkv_ln_g, rwkv_ln_b=rwkv_ln_b, ret_gn_g=ret_gn_g,
             lru_conv_w=lru_conv_w, lru_conv_b=lru_conv_b, lru_wa=lru_wa, lru_ba=lru_ba, lru_wx=lru_wx,
             lru_bx=lru_bx, lru_lambda=lru_lambda, s5_a_re=s5_a_re, s5_a_im=s5_a_im, s5_log_dt=s5_log_dt,
             s5_b_re=s5_b_re, s5_b_im=s5_b_im, s5_c_re=s5_c_re, s5_c_im=s5_c_im, s5_d=s5_d,
             s5_glu_w=s5_glu_w, s5_glu_b=s5_glu_b, peer_wq=peer_wq, peer_keys=peer_keys, peer_u=peer_u,
             peer_v=peer_v)
    nb, l_lat, d = x.shape
    l_ctx = ctx.shape[1]
    lt = l_ctx + l_lat
    depth = w_in.shape[0]
    assert 128 % (nb * N_HEADS) == 0 and l_ctx % 128 == 0 and l_lat % 128 == 0
    assert (l_ctx * nb) % PEER_TOK_TILE == 0 and (l_lat * nb) % PEER_TOK_TILE == 0
    k_lo = LANES // (nb * N_HEADS)
    n_ctx_tiles = l_ctx // ROW_TILE_T
    n_ctx_tok_tiles = l_ctx * nb // PEER_TOK_TILE
    hs = _head_sum_matrix()
    ret_tabs = _ret_tables(lt)

    xs = jnp.transpose(jnp.concatenate([ctx, x], axis=1), (1, 0, 2))
    mod_rows = 8 * ((nb + 1 + 7) // 8)
    cvec = jnp.zeros((mod_rows, d), F32).at[:nb].set(c).at[nb].set(c_ctx)

    for l in range(depth):
        wl = _layer_weights(l, p)
        last = l == depth - 1
        mod = _modulation(cvec, ada_w[l], ada_b[l])
        mod_lat = mod[:nb].reshape(nb, N_MOD, d).transpose(1, 0, 2)
        mod_ctx = jnp.broadcast_to(mod[nb].reshape(N_MOD, 1, d), (N_MOD, nb, d))
        modv = jnp.stack([mod_ctx, mod_lat])

        za, zb, zc, zd = _in_proj(xs, norm1_g[l], modv, wl['w_in'], n_ctx_tiles)

        r, k, v, g, av, w_dec, kd, bv = _rwkv_features(za, wl['mu'], wl['gup'], wl['wup'], wl['aup'],
                                                       wl['rwkv_vecs'], hs, n_ctx_tiles)
        ex = lambda t: _rwkv_expand_k(t, k_lo)
        ya_f, ya_b = _rwkv_scan(ex(r), ex(av), ex(w_dec), ex(kd), ex(bv), _rwkv_expand_v(v), l_ctx)
        ya = jnp.stack([_rwkv_collapse_v(ya_f, nb), _rwkv_collapse_v(ya_b, nb)])
        ob = _retention(jnp.transpose(zb, (1, 0, 2)), ret_tabs, l_ctx)
        ob = jnp.transpose(ob, (0, 2, 1, 3))
        hc = _lru(zc, wl['cw'], wl['wa'], wl['wx'], wl['lru_vecs'], l_ctx)
        yd = _from_colmajor(_s5(_to_colmajor(zd, l_ctx), *wl['s5'], l_ctx), l_ctx)

        t0 = n_ctx_tiles if last else 0
        x1, h2 = _out_proj(xs, modv, ya, (r, k, v, g), ob, zb, hc, zc, yd, zd, wl['out_vecs'], hs,
                           wl['gluw'], wl['glub'], wl['wo'], wl['n2'], n_ctx_tiles, t0)
        t0p = n_ctx_tok_tiles if last else 0
        e1, cnt, r2, w2 = _peer_route(h2, wl['wq'], wl['keys'], t0p)
        x2 = _peer_ffn(h2, wl['u'], wl['vt'], e1, cnt, r2, w2, x1.reshape(lt * nb, d), modv,
                       n_ctx_tok_tiles, t0p)
        xs = x2.reshape(lt, nb, d)

    out = _final_norm(xs[l_ctx:].reshape(l_lat * nb, d), final_norm_g)
    return jnp.transpose(out.reshape(l_lat, nb, d), (1, 0, 2))
```

```python
import functools
import math

import jax
import jax.numpy as jnp
import numpy as np
from jax import lax
from jax.experimental import pallas as pl
from jax.experimental.pallas import tpu as pltpu

F32 = jnp.float32
BF16 = jnp.bfloat16

NORM_EPS = 1e-6
N_MOD = 6
HEAD_DIM = 64
N_HEADS = 4
GROUP_W = N_HEADS * HEAD_DIM
GRID_W = 64

RWKV_GATE_LORA = 64
RWKV_LORA = 32
RWKV_GN_EPS = 64e-5
RWKV_COLS = 3 * GROUP_W + RWKV_GATE_LORA + 4 * RWKV_LORA
ZA_W = 3 * GROUP_W + 3 * 128

RET_CHUNK = 128
RET_GN_EPS = 1e-5
RET_FWD_OFFSET = 5.0
RET_BWD_OFFSET = 5.5
ROPE_BASE = 10000.0
RET_COLS = 4 * GROUP_W

LRU_CONV_W = 4
LRU_C = 8.0
LRU_COLS = 2 * GROUP_W
LRU_CHUNK = 128

S5_CH = 16
S5_GROUPS = GROUP_W // S5_CH
S5_STATE = 64
S5_N = S5_GROUPS * S5_STATE
S5_CHUNK = 128

PEER_HEADS = 8
PEER_NKEYS = 128
PEER_DKEY = 128
PEER_TOPK = 16
PEER_TOK_TILE = 512
PEER_EXP_BLOCK = 2048

ROW_TILE_T = 32
RWKV_CHUNK = 64

VMEM_LIMIT_BYTES = 56 * 1024 * 1024

NEG_INF = float("-inf")


def _cparams(n_axes):
    return pltpu.CompilerParams(dimension_semantics=("arbitrary",) * n_axes,
                                vmem_limit_bytes=VMEM_LIMIT_BYTES)


def _sigmoid(x):
    return 1.0 / (1.0 + jnp.exp(-x))


def _gelu_tanh(x):
    return 0.5 * x * (1.0 + jnp.tanh(math.sqrt(2.0 / math.pi) * (x + 0.044715 * (x * x * x))))


def _softplus(x):
    return jnp.maximum(x, 0.0) + jnp.log(1.0 + jnp.exp(-jnp.abs(x)))


def _dot(a, b):
    return jnp.dot(a, b, preferred_element_type=F32)


def _dot_exact(a, b):
    return jnp.dot(a, b, preferred_element_type=F32, precision=lax.Precision.HIGHEST)


def _head_sum_matrix():
    idx = np.arange(GROUP_W) // HEAD_DIM
    return jnp.asarray((idx[:, None] == idx[None, :]).astype(np.float32))


def _mod_body(c_ref, w_ref, b_ref, o_ref):
    c = c_ref[...]
    s = (c * _sigmoid(c)).astype(BF16)
    o_ref[...] = _dot(s, w_ref[...].astype(BF16)) + b_ref[...]


def _modulation(cvec, w, b):
    rows, d = cvec.shape
    n = w.shape[1]
    tn = 512
    return pl.pallas_call(
        _mod_body,
        grid=(n // tn,),
        in_specs=[pl.BlockSpec((rows, d), lambda j: (0, 0)),
                  pl.BlockSpec((d, tn), lambda j: (0, j)),
                  pl.BlockSpec((1, tn), lambda j: (0, j))],
        out_specs=pl.BlockSpec((rows, tn), lambda j: (0, j)),
        out_shape=jax.ShapeDtypeStruct((rows, n), F32),
        compiler_params=_cparams(1),
        name="adaln_mod",
    )(cvec, w, b.reshape(1, n))


def _in_body(x_ref, g_ref, mod_ref, w_ref, za_ref, zb_ref, zc_ref, zd_ref):
    tt, nb, d = x_ref.shape
    x = x_ref[...]
    ms = jnp.mean(x * x, axis=-1, keepdims=True)
    y = x * lax.rsqrt(ms + NORM_EPS) * g_ref[...]
    h = y * (1.0 + mod_ref[1]) + mod_ref[0]
    hb = h.reshape(tt * nb, d).astype(BF16)
    c0 = 0
    for ref in (za_ref, zb_ref, zc_ref, zd_ref):
        w = ref.shape[-1]
        ref[...] = _dot(hb, w_ref[:, c0:c0 + w]).reshape(tt, nb, w)
        c0 += w


def _in_proj(xs, g, modv, w_p, n_ctx_tiles):
    lt, nb, d = xs.shape
    tt = ROW_TILE_T
    widths = (ZA_W, RET_COLS, LRU_COLS, GROUP_W)
    seg = lambda i: jnp.where(i >= n_ctx_tiles, 1, 0)
    return pl.pallas_call(
        _in_body,
        grid=(lt // tt,),
        in_specs=[pl.BlockSpec((tt, nb, d), lambda i: (i, 0, 0)),
                  pl.BlockSpec((1, d), lambda i: (0, 0)),
                  pl.BlockSpec((None, N_MOD, nb, d), lambda i: (seg(i), 0, 0, 0)),
                  pl.BlockSpec(w_p.shape, lambda i: (0, 0))],
        out_specs=[pl.BlockSpec((tt, nb, w), lambda i: (i, 0, 0)) for w in widths],
        out_shape=[jax.ShapeDtypeStruct((lt, nb, w), F32) for w in widths],
        compiler_params=_cparams(1),
        name="in_proj",
    )(xs, g.reshape(1, d), modv, w_p)


def _rwkv_feat_body(n_ctx_tiles, n_tiles,
                    z_ref, zp_ref, zn_ref, mu_ref, gup_ref, wup_ref, aup_ref, vec_ref, hs_ref,
                    r_ref, k_ref, v_ref, g_ref, av_ref, w_ref, kd_ref, bv_ref):
    i = pl.program_id(0)
    tt, nb, zw = z_ref.shape
    z0 = z_ref[...]
    has_prev = jnp.logical_and(i != 0, i != n_ctx_tiles)
    has_next = jnp.logical_and(i != n_ctx_tiles - 1, i != n_tiles - 1)
    hp = jnp.where(has_prev, zp_ref[...], 0.0)
    hn = jnp.where(has_next, zn_ref[...], 0.0)
    zp = jnp.concatenate([hp, z0[:-1]], axis=0)
    zn = jnp.concatenate([z0[1:], hn], axis=0)
    z = z0 + mu_ref[0] * (zp - z0) + mu_ref[1] * (zn - z0)
    z = z.reshape(tt * nb, zw)
    gw = GROUP_W
    r = z[:, 0:gw]
    k = z[:, gw:2 * gw]
    v = z[:, 2 * gw:3 * gw]
    g = _dot(_sigmoid(z[:, 3 * gw:3 * gw + 128]).astype(BF16), gup_ref[...])
    w_low = jnp.tanh(z[:, 3 * gw + 128:3 * gw + 256]).astype(BF16)
    a_low = z[:, 3 * gw + 256:3 * gw + 384].astype(BF16)
    k_k = vec_ref[0:1, :]
    k_a = vec_ref[1:2, :]
    kk = k * k_k
    ss = _dot_exact(kk * kk, hs_ref[...])
    kk = kk / jnp.maximum(jnp.sqrt(ss), 1e-12)
    shp = (tt, nb, gw)
    r_ref[...] = r.reshape(shp)
    k_ref[...] = k.reshape(shp)
    v_ref[...] = v.reshape(shp)
    g_ref[...] = g.reshape(shp)
    av_ref[...] = (-kk).reshape(shp)
    for d in range(2):
        w0 = vec_ref[2 + d:3 + d, :]
        a0 = vec_ref[4 + d:5 + d, :]
        w_log = -_softplus(-(w0 + _dot(w_low, wup_ref[d]))) - 0.5
        w_ref[d] = jnp.exp(-jnp.exp(w_log)).reshape(shp)
        a = _sigmoid(a0 + _dot(a_low, aup_ref[d]))
        kd_ref[d] = (k * (1.0 + (a - 1.0) * k_a)).reshape(shp)
        bv_ref[d] = (kk * a).reshape(shp)


def _rwkv_features(za, mu_p, gup_p, wup_p, aup_p, vecs, hs, n_ctx_tiles):
    lt, nb, zw = za.shape
    tt = ROW_TILE_T
    n_tiles = lt // tt
    gw = GROUP_W
    one = lambda shape: jax.ShapeDtypeStruct(shape, F32)
    full = lambda a: pl.BlockSpec(a.shape, lambda i: (0,) * a.ndim)
    o1 = pl.BlockSpec((tt, nb, gw), lambda i: (i, 0, 0))
    o2 = pl.BlockSpec((2, tt, nb, gw), lambda i: (0, i, 0, 0))
    return pl.pallas_call(
        functools.partial(_rwkv_feat_body, n_ctx_tiles, n_tiles),
        grid=(n_tiles,),
        in_specs=[pl.BlockSpec((tt, nb, zw), lambda i: (i, 0, 0)),
                  pl.BlockSpec((1, nb, zw), lambda i: (jnp.maximum(i * tt - 1, 0), 0, 0)),
                  pl.BlockSpec((1, nb, zw), lambda i: (jnp.minimum((i + 1) * tt, lt - 1), 0, 0)),
                  full(mu_p), full(gup_p), full(wup_p), full(aup_p), full(vecs), full(hs)],
        out_specs=[o1, o1, o1, o1, o1, o2, o2, o2],
        out_shape=[one((lt, nb, gw))] * 5 + [one((2, lt, nb, gw))] * 3,
        compiler_params=_cparams(1),
        name="rwkv_features",
    )(za, za, za, mu_p, gup_p, wup_p, aup_p, vecs, hs)


def _rwkv_scan_body(rf_ref, af_ref, wf_ref, kdf_ref, bf_ref, vf_ref,
                    rb_ref, ab_ref, wb_ref, kdb_ref, bb_ref, vb_ref,
                    yf_ref, yb_ref, s_ref, vx_ref):
    j = pl.program_id(0)
    tc, n_khi, lanes = rf_ref.shape
    n_v, n_bh = vf_ref.shape[1:]
    k_lo = lanes // n_bh
    n_half = 2
    hv = n_v // n_half

    @pl.when(j == 0)
    def _():
        s_ref[...] = jnp.zeros_like(s_ref)

    for d, ref in enumerate((vf_ref, vb_ref)):
        x = ref[...]
        reps = 1
        while reps < k_lo:
            x = jnp.concatenate([x, x], axis=-1)
            reps *= 2
        vx_ref[d] = x

    def group_sum(x):
        shift = n_bh
        while shift < lanes:
            x = x + pltpu.roll(x, shift, axis=1)
            shift *= 2
        return x

    dirs = ((rf_ref, af_ref, wf_ref, kdf_ref, bf_ref, yf_ref), (rb_ref, ab_ref, wb_ref, kdb_ref, bb_ref, yb_ref))

    segs = [(d, slice(half * hv, (half + 1) * hv)) for d in range(2) for half in range(n_half)]

    def step(i, carry):
        t = (i, tc - 1 - i)
        row = lambda ref, d, kh: ref[t[d], pl.ds(kh, 1), :]
        ip = jnp.maximum(i - 1, 0)
        tp = (ip, tc - 1 - ip)
        for n, (d, rows) in enumerate(segs):
            dirs[d][5][tp[d], rows, :] = group_sum(carry[n])[:, :n_bh]
        acc = [None] * len(segs)
        for kh in range(n_khi):
            a_row = [row(dirs[d][1], d, kh) for d in range(2)]
            for n, (d, rows) in enumerate(segs):
                term = s_ref[d, kh, rows, :] * a_row[d]
                acc[n] = term if acc[n] is None else acc[n] + term
        sa = [group_sum(a) for a in acc]
        for kh in range(n_khi):
            w_row, kd_row = ([row(dirs[d][c], d, kh) for d in range(2)] for c in (2, 3))
            for n, (d, rows) in enumerate(segs):
                s_ref[d, kh, rows, :] = s_ref[d, kh, rows, :] * w_row[d] + vx_ref[d, t[d], rows, :] * kd_row[d]
        acc = [None] * len(segs)
        for kh in range(n_khi):
            b_row, r_row = ([row(dirs[d][c], d, kh) for d in range(2)] for c in (4, 0))
            for n, (d, rows) in enumerate(segs):
                s_new = s_ref[d, kh, rows, :] + sa[n] * b_row[d]
                s_ref[d, kh, rows, :] = s_new
                term = s_new * r_row[d]
                acc[n] = term if acc[n] is None else acc[n] + term
        return tuple(acc)

    zero = jnp.zeros((hv, lanes), F32)
    last = lax.fori_loop(0, tc, step, (zero,) * len(segs))
    t_last = (tc - 1, 0)
    for n, (d, rows) in enumerate(segs):
        dirs[d][5][t_last[d], rows, :] = group_sum(last[n])[:, :n_bh]


def _dir_chunk(d, j, n_ctx, n_all):
    bwd = jnp.where(j < n_ctx, n_ctx - 1 - j, n_all - 1 - (j - n_ctx))
    return jnp.where(d == 0, j, bwd)


def _rwkv_scan(r_e, a_e, w_e, kd_e, b_e, v_e, l_ctx):
    lt, n_khi, lanes = r_e.shape
    _, n_v, n_bh = v_e.shape
    tc = RWKV_CHUNK
    n_all, n_ctx = lt // tc, l_ctx // tc
    specs = []
    for d in range(2):
        cm = functools.partial(_dir_chunk, d, n_ctx=n_ctx, n_all=n_all)
        shared = pl.BlockSpec((tc, n_khi, lanes), lambda j, cm=cm: (cm(j), 0, 0))
        per_dir = pl.BlockSpec((None, tc, n_khi, lanes), lambda j, cm=cm, d=d: (d, cm(j), 0, 0))
        tok = pl.BlockSpec((tc, n_v, n_bh), lambda j, cm=cm: (cm(j), 0, 0))
        specs.append(([shared, shared, per_dir, per_dir, per_dir, tok], tok))
    return pl.pallas_call(
        _rwkv_scan_body,
        grid=(n_all,),
        in_specs=specs[0][0] + specs[1][0],
        out_specs=[specs[0][1], specs[1][1]],
        out_shape=[jax.ShapeDtypeStruct((lt, n_v, n_bh), F32)] * 2,
        scratch_shapes=[pltpu.VMEM((2, n_khi, n_v, lanes), F32), pltpu.VMEM((2, tc, n_v, lanes), F32)],
        compiler_params=_cparams(1),
        name="rwkv_scan",
    )(r_e, a_e, w_e, kd_e, b_e, v_e, r_e, a_e, w_e, kd_e, b_e, v_e)


def _rwkv_expand_k(x, k_lo):
    lead = x.shape[:-2]
    nb = x.shape[-2]
    nd = len(lead)
    x = x.reshape(lead + (nb, N_HEADS, k_lo, HEAD_DIM // k_lo))
    x = jnp.transpose(x, tuple(range(nd)) + (nd + 3, nd + 2, nd, nd + 1))
    return x.reshape(lead + (HEAD_DIM // k_lo, k_lo * nb * N_HEADS))


def _rwkv_expand_v(x):
    lt, nb, _ = x.shape
    return jnp.transpose(x.reshape(lt, nb, N_HEADS, HEAD_DIM), (0, 3, 1, 2)).reshape(lt, HEAD_DIM, nb * N_HEADS)


def _rwkv_collapse_v(y, nb):
    lt = y.shape[0]
    return jnp.transpose(y.reshape(lt, HEAD_DIM, nb, N_HEADS), (0, 2, 3, 1)).reshape(lt, nb, GROUP_W)


def _ret_body(nb, z_ref, cos_ref, sin_ref, dec_ref, rd_ref, wd_ref, cd_ref, hm_ref, o_ref, s_ref):
    j = pl.program_id(1)
    gw = GROUP_W
    half = gw // 2

    @pl.when(j == 0)
    def _():
        s_ref[...] = jnp.zeros_like(s_ref)

    cos = cos_ref[...]
    sin = sin_ref[...]

    def rope(x):
        x1, x2 = x[:, :half], x[:, half:]
        return jnp.concatenate([x1 * cos - x2 * sin, x1 * sin + x2 * cos], axis=-1)

    def one_batch(b, carry):
        q = rope(z_ref[b, :, 0:gw]) * (HEAD_DIM ** -0.5)
        k = rope(z_ref[b, :, gw:2 * gw])
        v = z_ref[b, :, 2 * gw:3 * gw]
        vb = v.astype(BF16)
        k_t = k.T
        k_tb = k_t.astype(BF16)
        state = s_ref[b]
        inter = _dot(q.astype(BF16), state.astype(BF16)) * rd_ref[...]
        intra = jnp.zeros_like(inter)
        for h in range(N_HEADS):
            qm = (q * hm_ref[h:h + 1, :]).astype(BF16)
            att = _dot(qm, k_tb) * dec_ref[h]
            vm = (v * hm_ref[N_HEADS + h:N_HEADS + h + 1, :]).astype(BF16)
            intra = intra + _dot(att.astype(BF16), vm)
        o_ref[b] = intra + inter
        kv = _dot((k_t * wd_ref[...]).astype(BF16), vb)
        s_ref[b] = state * cd_ref[0] + kv * cd_ref[1]
        return carry

    lax.fori_loop(0, nb, one_batch, 0)


def _retention(zb, tabs, l_ctx):
    nb, lt, _ = zb.shape
    c = RET_CHUNK
    n_all, n_ctx = lt // c, l_ctx // c
    gw = GROUP_W
    cos, sin, dec, rd, wd, cd, hm = tabs
    cm = lambda d, j: _dir_chunk(d, j, n_ctx, n_all)
    return pl.pallas_call(
        functools.partial(_ret_body, nb),
        grid=(2, n_all),
        in_specs=[pl.BlockSpec((nb, c, RET_COLS), lambda d, j: (0, cm(d, j), 0)),
                  pl.BlockSpec((c, gw // 2), lambda d, j: (cm(d, j), 0)),
                  pl.BlockSpec((c, gw // 2), lambda d, j: (cm(d, j), 0)),
                  pl.BlockSpec((None, N_HEADS, c, c), lambda d, j: (d, 0, 0, 0)),
                  pl.BlockSpec((None, c, gw), lambda d, j: (d, 0, 0)),
                  pl.BlockSpec((None, gw, c), lambda d, j: (d, 0, 0)),
                  pl.BlockSpec((None, 2, gw, gw), lambda d, j: (d, 0, 0, 0)),
                  pl.BlockSpec((2 * N_HEADS, gw), lambda d, j: (0, 0))],
        out_specs=pl.BlockSpec((None, nb, c, gw), lambda d, j: (d, 0, cm(d, j), 0)),
        out_shape=jax.ShapeDtypeStruct((2, nb, lt, gw), F32),
        scratch_shapes=[pltpu.VMEM((nb, gw, gw), F32)],
        compiler_params=_cparams(2),
        name="retention",
    )(zb, cos, sin, dec, rd, wd, cd, hm)


def _ret_perm():
    new = np.arange(GROUP_W)
    half_id, rem = new // 128, new % 128
    h, i = rem // 32, rem % 32
    return h * HEAD_DIM + half_id * 32 + i


def _ret_tables(lt):
    c = RET_CHUNK
    half = HEAD_DIM // 2
    freqs = ROPE_BASE ** (-jnp.arange(half, dtype=F32) / half)
    ang = jnp.arange(lt, dtype=F32)[:, None] * freqs[None]
    cos = jnp.tile(jnp.cos(ang), (1, N_HEADS))
    sin = jnp.tile(jnp.sin(ang), (1, N_HEADS))
    idx = jnp.arange(c, dtype=F32)
    perm = _ret_perm()
    head_of_qk = jnp.asarray(perm // HEAD_DIM)
    head_of_v = jnp.arange(GROUP_W) // HEAD_DIM
    dec, rd, wd, cd = [], [], [], []
    for offset, rev in ((RET_FWD_OFFSET, False), (RET_BWD_OFFSET, True)):
        lg = jnp.log1p(-jnp.exp2(-(offset + jnp.arange(N_HEADS, dtype=F32))))
        diff = idx[:, None] - idx[None, :]
        if rev:
            mask, dist = diff < 0, -diff
            read_pow, write_pow = c - idx, idx
        else:
            mask, dist = diff >= 0, diff
            read_pow, write_pow = idx + 1.0, c - 1.0 - idx
        dec.append(jnp.where(mask[None], jnp.exp(lg[:, None, None] * jnp.where(mask, dist, 0.0)[None]), 0.0))
        rd.append(jnp.exp(lg[head_of_v][None, :] * read_pow[:, None]))
        wd.append(jnp.exp(lg[head_of_qk][:, None] * write_pow[None, :]))
        same = (head_of_qk[:, None] == head_of_v[None, :]).astype(F32)
        cd.append(jnp.stack([same * jnp.exp(lg * c)[head_of_v][None, :], same]))
    hm_q = (head_of_qk[None, :] == jnp.arange(N_HEADS)[:, None]).astype(F32)
    hm_v = (head_of_v[None, :] == jnp.arange(N_HEADS)[:, None]).astype(F32)
    return cos, sin, jnp.stack(dec), jnp.stack(rd), jnp.stack(wd), jnp.stack(cd), jnp.concatenate([hm_q, hm_v])


def _lru_body(n_ctx, n_all, x_ref, xp_ref, xn_ref, cw_ref, wa_ref, wx_ref, vec_ref, o_ref,
              a_s, b_s, h_s):
    d = pl.program_id(0)
    j = pl.program_id(1)
    tc, nb, gw = x_ref.shape
    ch = _dir_chunk(d, j, n_ctx, n_all)

    @pl.when(j == 0)
    def _():
        h_s[...] = jnp.zeros_like(h_s)

    has_prev = jnp.logical_and(ch != 0, ch != n_ctx)
    has_next = jnp.logical_and(ch != n_ctx - 1, ch != n_all - 1)
    xe = jnp.concatenate([jnp.where(has_prev, xp_ref[...], 0.0), x_ref[...],
                          jnp.where(has_next, xn_ref[...], 0.0)], axis=0)
    xc = cw_ref[LRU_CONV_W:LRU_CONV_W + 1, :]
    for tap in range(LRU_CONV_W):
        xc = xc + cw_ref[tap:tap + 1, :] * xe[tap:tap + tc]
    xc = xc.reshape(tc * nb, gw)
    xb = xc.astype(BF16)
    r = _sigmoid(_dot(xb, wa_ref[...]) + vec_ref[0:1, :])
    gi = _sigmoid(_dot(xb, wx_ref[...]) + vec_ref[1:2, :])
    log_a = -LRU_C * r * _softplus(-vec_ref[2:3, :])
    th = jnp.tanh(log_a)
    one_minus_a2 = 2.0 * th / (th - 1.0)
    a_s[...] = jnp.exp(log_a).reshape(tc, nb, gw)
    b_s[...] = (jnp.sqrt(one_minus_a2) * (gi * xc)).reshape(tc, nb, gw)

    def step(i, h):
        t = jnp.where(d == 0, i, tc - 1 - i)
        h = a_s[t] * h + b_s[t]
        o_ref[t] = h
        return h

    h_s[...] = lax.fori_loop(0, tc, step, h_s[...])


def _lru(zc, cw, wa_bd, wx_bd, vecs, l_ctx):
    lt, nb, _ = zc.shape
    gw = GROUP_W
    tc = LRU_CHUNK
    n_all, n_ctx = lt // tc, l_ctx // tc
    cm = lambda d, j: _dir_chunk(d, j, n_ctx, n_all)
    return pl.pallas_call(
        functools.partial(_lru_body, n_ctx, n_all),
        grid=(2, n_all),
        in_specs=[pl.BlockSpec((tc, nb, gw), lambda d, j: (cm(d, j), 0, 0)),
                  pl.BlockSpec((2, nb, gw), lambda d, j: (jnp.maximum(cm(d, j) * (tc // 2) - 1, 0), 0, 0)),
                  pl.BlockSpec((1, nb, gw), lambda d, j: (jnp.minimum((cm(d, j) + 1) * tc, lt - 1), 0, 0)),
                  pl.BlockSpec(cw.shape, lambda d, j: (0, 0)),
                  pl.BlockSpec((None, gw, gw), lambda d, j: (d, 0, 0)),
                  pl.BlockSpec((None, gw, gw), lambda d, j: (d, 0, 0)),
                  pl.BlockSpec((None, 3, gw), lambda d, j: (d, 0, 0))],
        out_specs=pl.BlockSpec((None, tc, nb, gw), lambda d, j: (d, cm(d, j), 0, 0)),
        out_shape=jax.ShapeDtypeStruct((2, lt, nb, gw), F32),
        scratch_shapes=[pltpu.VMEM((tc, nb, gw), F32), pltpu.VMEM((tc, nb, gw), F32),
                        pltpu.VMEM((nb, gw), F32)],
        compiler_params=_cparams(2),
        name="rglru",
    )(zc, zc, zc, cw, wa_bd, wx_bd, vecs)


def _block_diag(w):
    nblk, n = w.shape[-3], w.shape[-1]
    eye = jnp.eye(nblk, dtype=w.dtype)
    out = w[..., :, :, None, :] * eye[:, None, :, None]
    return out.reshape(w.shape[:-3] + (nblk * n, nblk * n))


def _s5_body(u_ref, bm_ref, lam_ref, cm_ref, y_ref, h_s, st_s):
    d = pl.program_id(0)
    j = pl.program_id(1)
    tc, nb, gw = u_ref.shape
    n = S5_N

    @pl.when(j == 0)
    def _():
        st_s[...] = jnp.zeros_like(st_s)

    u = u_ref[...].reshape(tc * nb, gw).astype(BF16)
    h_s[...] = _dot(u, bm_ref[...]).reshape(tc, nb, 2 * n)
    lr = lam_ref[0:1, :]
    li = lam_ref[1:2, :]

    def step(i, carry):
        hr, hi = carry
        t = jnp.where(d == 0, i, tc - 1 - i)
        bu = h_s[t]
        nr = lr * hr - li * hi + bu[:, :n]
        ni = lr * hi + li * hr + bu[:, n:]
        h_s[t] = jnp.concatenate([nr, ni], axis=-1)
        return nr, ni

    hr, hi = lax.fori_loop(0, tc, step, (st_s[0], st_s[1]), unroll=4)
    st_s[0] = hr
    st_s[1] = hi
    hh = h_s[...].reshape(tc * nb, 2 * n).astype(BF16)
    y_ref[...] = _dot(hh, cm_ref[...]).reshape(tc, nb, gw)


def _s5(u, bmat, lam, cmat, l_ctx):
    lt, nb, gw = u.shape
    tc = S5_CHUNK
    n_all, n_ctx = lt // tc, l_ctx // tc
    cm = lambda d, j: _dir_chunk(d, j, n_ctx, n_all)
    return pl.pallas_call(
        _s5_body,
        grid=(2, n_all),
        in_specs=[pl.BlockSpec((tc, nb, gw), lambda d, j: (cm(d, j), 0, 0)),
                  pl.BlockSpec((None, gw, 2 * S5_N), lambda d, j: (d, 0, 0)),
                  pl.BlockSpec((None, 2, S5_N), lambda d, j: (d, 0, 0)),
                  pl.BlockSpec((None, 2 * S5_N, gw), lambda d, j: (d, 0, 0))],
        out_specs=pl.BlockSpec((None, tc, nb, gw), lambda d, j: (d, cm(d, j), 0, 0)),
        out_shape=jax.ShapeDtypeStruct((2, lt, nb, gw), F32),
        scratch_shapes=[pltpu.VMEM((tc, nb, 2 * S5_N), F32), pltpu.VMEM((2, nb, S5_N), F32)],
        compiler_params=_cparams(2),
        name="s5",
    )(u, bmat, lam, cmat)


def _s5_params(a_re, a_im, log_dt, b_re, b_im, c_re, c_im):
    dt = jnp.exp(log_dt)[..., None]
    er = jnp.exp(a_re * dt)
    lbr, lbi = er * jnp.cos(a_im * dt), er * jnp.sin(a_im * dt)
    den = a_re * a_re + a_im * a_im
    nr, ni = lbr - 1.0, lbi
    fr = (nr * a_re + ni * a_im) / den
    fi = (ni * a_re - nr * a_im) / den
    bbr = fr[..., None] * b_re - fi[..., None] * b_im
    bbi = fr[..., None] * b_im + fi[..., None] * b_re
    eye = jnp.eye(S5_GROUPS, dtype=F32)

    def in_map(bb):
        m = jnp.einsum('dgpc,gh->dgchp', bb, eye)
        return m.reshape(2, GROUP_W, S5_N)

    def out_map(cc):
        m = jnp.einsum('dgcp,gh->dgphc', cc, eye)
        return m.reshape(2, S5_N, GROUP_W)

    bmat = jnp.concatenate([in_map(bbr), in_map(bbi)], axis=-1).astype(BF16)
    cmat = jnp.concatenate([out_map(c_re), -out_map(c_im)], axis=-2).astype(BF16)
    lam = jnp.stack([lbr.reshape(2, S5_N), lbi.reshape(2, S5_N)], axis=1)
    return bmat, lam, cmat


def _to_colmajor(z, l_ctx):
    lt, nb, ch = z.shape
    rows = (lt - l_ctx) // GRID_W
    lat = z[l_ctx:].reshape(rows, GRID_W, nb, ch).transpose(1, 0, 2, 3).reshape(lt - l_ctx, nb, ch)
    return jnp.concatenate([z[:l_ctx], lat], axis=0)


def _from_colmajor(y, l_ctx):
    lt = y.shape[-3]
    nb, ch = y.shape[-2:]
    rows = (lt - l_ctx) // GRID_W
    lead = y.shape[:-3]
    lat = y[..., l_ctx:, :, :].reshape(lead + (GRID_W, rows, nb, ch))
    lat = jnp.swapaxes(lat, -4, -3).reshape(lead + (lt - l_ctx, nb, ch))
    return jnp.concatenate([y[..., :l_ctx, :, :], lat], axis=-3)


def _out_body(x_ref, mod_ref, ya_ref, r_ref, k_ref, v_ref, g_ref, ob_ref, gb_ref, hc_ref, gc_ref,
              yd_ref, ud_ref, vec_ref, hs_ref, gluw_ref, glub_ref, wo_ref, n2_ref, x1_ref, h2_ref):
    tt, nb, d = x_ref.shape
    gw = GROUP_W
    rows = tt * nb
    hs = hs_ref[...]
    two = lambda ref: (ref[0] + ref[1]).reshape(rows, gw)
    flat = lambda ref: ref[...].reshape(rows, gw)

    def head_norm(y, eps):
        mu = _dot_exact(y, hs) * (1.0 / HEAD_DIM)
        dlt = y - mu
        var = _dot_exact(dlt * dlt, hs) * (1.0 / HEAD_DIM)
        return dlt * lax.rsqrt(var + eps)

    r_k, ln_g, ln_b = vec_ref[0:1, :], vec_ref[1:2, :], vec_ref[2:3, :]
    r, k, v = flat(r_ref), flat(k_ref), flat(v_ref)
    yn = head_norm(two(ya_ref), RWKV_GN_EPS) * ln_g + ln_b
    bonus = _dot_exact(r * k * r_k, hs) * v
    mix_a = (yn + bonus) * flat(g_ref)
    gb = flat(gb_ref)
    mix_b = gb * _sigmoid(gb) * (head_norm(two(ob_ref), RET_GN_EPS) * vec_ref[3:4, :])
    mix_c = _gelu_tanh(flat(gc_ref)) * two(hc_ref)
    yd = vec_ref[4:5, :] * flat(ud_ref) + two(yd_ref)
    glu = _dot(_gelu_tanh(yd).astype(BF16), gluw_ref[...]) + glub_ref[...]
    mix_d = glu[:, :gw] * _sigmoid(glu[:, gw:])

    mix = _dot(mix_a.astype(BF16), wo_ref[0:gw, :])
    mix = mix + _dot(mix_b.astype(BF16), wo_ref[gw:2 * gw, :])
    mix = mix + _dot(mix_c.astype(BF16), wo_ref[2 * gw:3 * gw, :])
    mix = mix + _dot(mix_d.astype(BF16), wo_ref[3 * gw:4 * gw, :])
    x1 = x_ref[...] + mod_ref[2] * mix.reshape(tt, nb, d)
    x1_ref[...] = x1
    ms = jnp.mean(x1 * x1, axis=-1, keepdims=True)
    h2 = (x1 * lax.rsqrt(ms + NORM_EPS) * n2_ref[...]) * (1.0 + mod_ref[4]) + mod_ref[3]
    h2_ref[...] = h2.reshape(rows, d).T.astype(BF16)


def _out_proj(xs, modv, ya, feats, ob, zb, hc, zc, yd, zd, vecs, hs, gluw, glub, wo, n2, n_ctx_tiles, t0_tile):
    lt, nb, d = xs.shape
    tt = ROW_TILE_T
    gw = GROUP_W
    n_tiles = lt // tt - t0_tile
    seg = lambda i: jnp.where(i + t0_tile >= n_ctx_tiles, 1, 0)
    col = lambda c: pl.BlockSpec((tt, nb, gw), lambda i: (i + t0_tile, 0, c))
    two = pl.BlockSpec((2, tt, nb, gw), lambda i: (0, i + t0_tile, 0, 0))
    full = lambda a: pl.BlockSpec(a.shape, lambda i: (0,) * a.ndim)
    r, k, v, g = feats
    return pl.pallas_call(
        _out_body,
        grid=(n_tiles,),
        in_specs=[pl.BlockSpec((tt, nb, d), lambda i: (i + t0_tile, 0, 0)),
                  pl.BlockSpec((None, N_MOD, nb, d), lambda i: (seg(i), 0, 0, 0)),
                  two, col(0), col(0), col(0), col(0),
                  two, col(3), two, col(1), two, col(0),
                  full(vecs), full(hs), full(gluw), full(glub), full(wo), full(n2)],
        out_specs=[pl.BlockSpec((tt, nb, d), lambda i: (i + t0_tile, 0, 0)),
                   pl.BlockSpec((d, tt * nb), lambda i: (0, i + t0_tile))],
        out_shape=[jax.ShapeDtypeStruct((lt, nb, d), F32), jax.ShapeDtypeStruct((d, lt * nb), BF16)],
        compiler_params=_cparams(1),
        name="out_proj",
    )(xs, modv, ya, r, k, v, g, ob, zb, hc, zc, yd, zd, vecs, hs, gluw, glub, wo, n2)


PEER_NO_RANK = 127.0


def _top_values(s, n, with_rank=False):
    vals, cur = [], s
    rank = jnp.full(s.shape, PEER_NO_RANK, F32) if with_rank else None
    for k in range(n):
        m = jnp.max(cur, axis=0, keepdims=True)
        vals.append(m)
        hit = cur == m
        if with_rank:
            rank = jnp.where(hit, float(k), rank)
        cur = jnp.where(hit, NEG_INF, cur)
    return (vals, rank) if with_rank else vals


def _peer_route_body(h_ref, wq_ref, keys_ref, e1_ref, cnt_ref, r2_ref, w2_ref, pack_ref):
    hb = h_ref[...]
    for h in range(PEER_HEADS):
        st = []
        for p in range(2):
            c0 = (2 * h + p) * PEER_DKEY
            q = _dot(wq_ref[c0:c0 + PEER_DKEY, :], hb).astype(BF16)
            st.append(_dot(keys_ref[2 * h + p], q))
        s1, s2 = st
        n_top = PEER_TOPK + 1
        v1 = _top_values(s1, n_top)
        v2, rank2 = _top_values(s2, n_top, with_rank=True)
        pad7 = [jnp.full((7, s1.shape[1]), NEG_INF, F32)]
        v2m = jnp.concatenate(v2 + pad7, axis=0)
        v1_tail = jnp.concatenate(v1[8:] + pad7, axis=0)
        cand = jnp.concatenate([v1[0] + v2m] + [v1[a] + v2m[0:8] for a in range(1, 8)]
                               + [v1_tail + v2[0]], axis=0)
        top = _top_values(cand, n_top)
        tau = 0.5 * (top[PEER_TOPK - 1] + top[PEER_TOPK])
        cmax = top[0]
        z = jnp.sum(jnp.where(cand >= tau, jnp.exp(cand - cmax), 0.0), axis=0, keepdims=True)
        thr = tau - s1
        cnt = jnp.zeros_like(s1)
        for k in range(PEER_TOPK):
            cnt = cnt + jnp.where(v2[k] >= thr, 1.0, 0.0)
        e1_ref[:, h, :] = jnp.exp(s1 - v1[0])
        cnt_ref[:, h, :] = cnt
        pack_ref[0] = rank2.astype(BF16)
        pack_ref[1] = (jnp.exp(s2 - v2[0]) / z).astype(BF16)
        words = pack_ref.bitcast(jnp.uint32)
        r2_ref[h] = words[0]
        w2_ref[h] = words[1]


def _peer_route(h2t, wq, keys, t0_tile):
    d, t_all = h2t.shape
    tt = PEER_TOK_TILE
    nk = PEER_NKEYS
    by_i = pl.BlockSpec((nk, PEER_HEADS, tt), lambda i: (0, 0, i + t0_tile))
    by_h = pl.BlockSpec((PEER_HEADS, nk // 2, tt), lambda i: (0, 0, i + t0_tile))
    return pl.pallas_call(
        _peer_route_body,
        grid=(t_all // tt - t0_tile,),
        in_specs=[pl.BlockSpec((d, tt), lambda i: (0, i + t0_tile)),
                  pl.BlockSpec(wq.shape, lambda i: (0, 0)),
                  pl.BlockSpec(keys.shape, lambda i: (0, 0, 0))],
        out_specs=[by_i, by_i, by_h, by_h],
        out_shape=[jax.ShapeDtypeStruct((nk, PEER_HEADS, t_all), F32)] * 2
                  + [jax.ShapeDtypeStruct((PEER_HEADS, nk // 2, t_all), jnp.uint32)] * 2,
        scratch_shapes=[pltpu.VMEM((2, nk, tt), BF16)],
        compiler_params=_cparams(1),
        name="peer_route",
    )(h2t, wq, keys)


PEER_GATE_ROWS = 32
LANES = 128


BF16_ROWS = 16
SUBLANES = 8


def _peer_gate_block(p_ref, e1_ref, cnt_ref, r2_ref, w2_ref, bc_ref):
    nk = PEER_NKEYS
    n_sub = p_ref.shape[0] // nk
    rc = PEER_GATE_ROWS
    n_pv = rc // BF16_ROWS
    for c0 in range(0, p_ref.shape[1], LANES):
        cols = slice(c0, c0 + LANES)
        for ii in range(n_sub):
            for h in range(PEER_HEADS):
                bc_ref[0, ii, h] = jnp.broadcast_to(cnt_ref[ii, h:h + 1, cols], (BF16_ROWS, LANES)).astype(BF16)
                bc_ref[1, ii, h] = jnp.broadcast_to(e1_ref[ii, h:h + 1, cols], (BF16_ROWS, LANES)).astype(BF16)

        def chunk(q, carry, cols=cols):
            r0 = pl.multiple_of(q * rc, rc)
            gate = [[None] * n_pv for _ in range(n_sub)]
            for h in range(PEER_HEADS):
                rank = [pltpu.bitcast(r2_ref[h, q, SUBLANES * v:SUBLANES * (v + 1), cols], BF16)
                        for v in range(n_pv)]
                w2 = [pltpu.bitcast(w2_ref[h, q, SUBLANES * v:SUBLANES * (v + 1), cols], BF16)
                      for v in range(n_pv)]
                for ii in range(n_sub):
                    cnt = bc_ref[0, ii, h]
                    e1 = bc_ref[1, ii, h]
                    for v in range(n_pv):
                        term = jnp.where(rank[v] < cnt, w2[v] * e1, jnp.zeros_like(e1))
                        gate[ii][v] = term if gate[ii][v] is None else gate[ii][v] + term
            for ii in range(n_sub):
                for v in range(n_pv):
                    rows = pl.ds(ii * nk + r0 + BF16_ROWS * v, BF16_ROWS)
                    p_ref[rows, cols] = p_ref[rows, cols] * gate[ii][v]
            return carry

        lax.fori_loop(0, nk // rc, chunk, 0)


def _peer_ffn_body(h_ref, u_ref, vt_ref, e1_ref, cnt_ref, r2_ref, w2_ref, x_ref, mod_ref, o_ref,
                   acc_ref, p_ref, bc_ref, ht_ref):
    j = pl.program_id(1)

    @pl.when(j == 0)
    def _():
        acc_ref[...] = jnp.zeros_like(acc_ref)
        ht_ref[...] = h_ref[...]

    act = _dot(u_ref[...], ht_ref[...])
    p_ref[...] = _gelu_tanh(act.astype(BF16))
    _peer_gate_block(p_ref, e1_ref, cnt_ref, r2_ref, w2_ref, bc_ref)
    acc_ref[...] += _dot(vt_ref[...], p_ref[...])

    @pl.when(j == pl.num_programs(1) - 1)
    def _():
        tt, d = x_ref.shape
        nb = mod_ref.shape[1]
        y = acc_ref[...].T.reshape(tt // nb, nb, d) * mod_ref[5]
        o_ref[...] = x_ref[...] + y.reshape(tt, d)


def _peer_ffn(h2t, u_b, vt_b, e1, cnt, r2, w2, x1, modv, n_ctx_tok_tiles, t0_tile):
    d, t_all = h2t.shape
    tt = PEER_TOK_TILE
    eb = PEER_EXP_BLOCK
    nk = PEER_NKEYS
    nb = modv.shape[2]
    n_e = u_b.shape[0] // eb
    n_tiles = t_all // tt - t0_tile
    seg = lambda i: jnp.where(i + t0_tile >= n_ctx_tok_tiles, 1, 0)
    by_i = pl.BlockSpec((eb // nk, PEER_HEADS, tt), lambda i, j: (j, 0, i + t0_tile))
    rc = PEER_GATE_ROWS
    by_h = pl.BlockSpec((PEER_HEADS, nk // rc, rc // 2, tt), lambda i, j: (0, 0, 0, i + t0_tile))
    r2, w2 = (a.reshape(PEER_HEADS, nk // rc, rc // 2, t_all) for a in (r2, w2))
    return pl.pallas_call(
        _peer_ffn_body,
        grid=(n_tiles, n_e),
        in_specs=[pl.BlockSpec((d, tt), lambda i, j: (0, i + t0_tile)),
                  pl.BlockSpec((eb, d), lambda i, j: (j, 0)),
                  pl.BlockSpec((d, eb), lambda i, j: (0, j)),
                  by_i, by_i, by_h, by_h,
                  pl.BlockSpec((tt, d), lambda i, j: (i + t0_tile, 0)),
                  pl.BlockSpec((None, N_MOD, nb, d), lambda i, j: (seg(i), 0, 0, 0))],
        out_specs=pl.BlockSpec((tt, d), lambda i, j: (i + t0_tile, 0)),
        out_shape=jax.ShapeDtypeStruct((t_all, d), F32),
        scratch_shapes=[pltpu.VMEM((d, tt), F32), pltpu.VMEM((eb, tt), BF16),
                        pltpu.VMEM((2, eb // nk, PEER_HEADS, BF16_ROWS, LANES), BF16),
                        pltpu.VMEM((d, tt), BF16)],
        compiler_params=_cparams(2),
        name="peer_ffn",
    )(h2t, u_b, vt_b, e1, cnt, r2, w2, x1, modv)


def _final_body(x_ref, g_ref, o_ref):
    x = x_ref[...]
    ms = jnp.mean(x * x, axis=-1, keepdims=True)
    o_ref[...] = x * lax.rsqrt(ms + NORM_EPS) * g_ref[...]


def _final_norm(x2d, g):
    t, d = x2d.shape
    tr = 512
    return pl.pallas_call(
        _final_body,
        grid=(t // tr,),
        in_specs=[pl.BlockSpec((tr, d), lambda i: (i, 0)), pl.BlockSpec((1, d), lambda i: (0, 0))],
        out_specs=pl.BlockSpec((tr, d), lambda i: (i, 0)),
        out_shape=jax.ShapeDtypeStruct((t, d), F32),
        compiler_params=_cparams(1),
        name="final_norm",
    )(x2d, g.reshape(1, d))


def _pad_cols(w, width):
    return jnp.pad(w, [(0, 0)] * (w.ndim - 1) + [(0, width - w.shape[-1])])


def _rwkv_slabs(w):
    gw3 = 3 * GROUP_W
    parts = [w[..., :gw3],
             _pad_cols(w[..., gw3:gw3 + 64], 128),
             _pad_cols(w[..., gw3 + 64:gw3 + 128], 128),
             _pad_cols(w[..., gw3 + 128:gw3 + 192], 128)]
    return jnp.concatenate(parts, axis=-1)


def _layer_weights(l, p):
    d = p['w_in'].shape[1]
    w_in = p['w_in'][l]
    wa_, wb_, wc_, wd_ = jnp.split(w_in, [RWKV_COLS, RWKV_COLS + RET_COLS, RWKV_COLS + RET_COLS + LRU_COLS], axis=-1)
    perm = _ret_perm()
    gw = GROUP_W
    wb_ = jnp.concatenate([wb_[:, 0:gw][:, perm], wb_[:, gw:2 * gw][:, perm], wb_[:, 2 * gw:]], axis=-1)
    out = {'w_in': jnp.concatenate([_rwkv_slabs(wa_), wb_, wc_, wd_], axis=-1).astype(BF16)}
    out['mu'] = _rwkv_slabs(p['rwkv_mu'][l])[:, None, :]
    out['gup'] = jnp.pad(p['rwkv_g_up'][l], ((0, 64), (0, 0))).astype(BF16)
    lora = RWKV_LORA
    out['wup'] = jnp.stack([jnp.pad(p['rwkv_w_up'][l, dd], ((dd * lora, 128 - (dd + 1) * lora), (0, 0)))
                            for dd in range(2)]).astype(BF16)
    out['aup'] = jnp.stack([jnp.pad(p['rwkv_a_up'][l, dd], ((dd * lora, 128 - (dd + 1) * lora), (0, 0)))
                            for dd in range(2)]).astype(BF16)
    out['rwkv_vecs'] = jnp.concatenate([p['rwkv_k_k'][l][None], p['rwkv_k_a'][l][None],
                                        p['rwkv_w0'][l], p['rwkv_a0'][l]], axis=0)
    out['cw'] = jnp.concatenate([p['lru_conv_w'][l], p['lru_conv_b'][l][None]], axis=0)
    out['wa'] = _block_diag(p['lru_wa'][l]).astype(BF16)
    out['wx'] = _block_diag(p['lru_wx'][l]).astype(BF16)
    out['lru_vecs'] = jnp.stack([p['lru_ba'][l], p['lru_bx'][l], p['lru_lambda'][l]], axis=1)
    out['s5'] = _s5_params(p['s5_a_re'][l], p['s5_a_im'][l], p['s5_log_dt'][l], p['s5_b_re'][l],
                           p['s5_b_im'][l], p['s5_c_re'][l], p['s5_c_im'][l])
    out['out_vecs'] = jnp.stack([p['rwkv_r_k'][l].reshape(-1), p['rwkv_ln_g'][l], p['rwkv_ln_b'][l],
                                 p['ret_gn_g'][l], p['s5_d'][l]], axis=0)
    out['gluw'] = p['s5_glu_w'][l].astype(BF16)
    out['glub'] = p['s5_glu_b'][l][None]
    out['wo'] = p['w_out'][l].astype(BF16)
    out['n2'] = p['norm2_g'][l][None]
    out['wq'] = p['peer_wq'][l].T.astype(BF16)
    out['keys'] = p['peer_keys'][l].reshape(PEER_HEADS * 2, PEER_NKEYS, PEER_DKEY).astype(BF16)
    out['u'] = p['peer_u'][l].astype(BF16)
    out['vt'] = p['peer_v'][l].T.astype(BF16)
    return out


def kernel(x, c, ctx, c_ctx, norm1_g, norm2_g, ada_w, ada_b, w_in, w_out, rwkv_mu, rwkv_w0, rwkv_w_up, rwkv_a0, rwkv_a_up, rwkv_g_up, rwkv_k_k, rwkv_k_a, rwkv_r_k, rwkv_ln_g, rwkv_ln_b, ret_gn_g, lru_conv_w, lru_conv_b, lru_wa, lru_ba, lru_wx, lru_bx, lru_lambda, s5_a_re, s5_a_im, s5_log_dt, s5_b_re, s5_b_im, s5_c_re, s5_c_im, s5_d, s5_glu_w, s5_glu_b, peer_wq, peer_keys, peer_u, peer_v, final_norm_g):
    p = dict(w_in=w_in, w_out=w_out, norm2_g=norm2_g, rwkv_mu=rwkv_mu, rwkv_w0=rwkv_w0, rwkv_w_up=rwkv_w_up,
             rwkv_a0=rwkv_a0, rwkv_a_up=rwkv_a_up, rwkv_g_up=rwkv_g_up, rwkv_k_k=rwkv_k_k, rwkv_k_a=rwkv_k_a,
             rwkv_r_k=rwkv_r_k, rwkv_ln_g=rwkv_ln_g, rwkv_ln_b=rwkv_ln_b, ret_gn_g=ret_gn_g,
             lru_conv_w=lru_conv_w, lru_conv_b=lru_conv_b, lru_wa=lru_wa, lru_ba=lru_ba, lru_wx=lru_wx,
             lru_bx=lru_bx, lru_lambda=lru_lambda, s5_a_re=s5_a_re, s5_a_im=s5_a_im, s5_log_dt=s5_log_dt,
             s5_b_re=s5_b_re, s5_b_im=s5_b_im, s5_c_re=s5_c_re, s5_c_im=s5_c_im, s5_d=s5_d,
             s5_glu_w=s5_glu_w, s5_glu_b=s5_glu_b, peer_wq=peer_wq, peer_keys=peer_keys, peer_u=peer_u,
             peer_v=peer_v)
    nb, l_lat, d = x.shape
    l_ctx = ctx.shape[1]
    lt = l_ctx + l_lat
    depth = w_in.shape[0]
    assert 128 % (nb * N_HEADS) == 0 and l_ctx % 128 == 0 and l_lat % 128 == 0
    assert (l_ctx * nb) % PEER_TOK_TILE == 0 and (l_lat * nb) % PEER_TOK_TILE == 0
    k_lo = LANES // (nb * N_HEADS)
    n_ctx_tiles = l_ctx // ROW_TILE_T
    n_ctx_tok_tiles = l_ctx * nb // PEER_TOK_TILE
    hs = _head_sum_matrix()
    ret_tabs = _ret_tables(lt)

    xs = jnp.transpose(jnp.concatenate([ctx, x], axis=1), (1, 0, 2))
    mod_rows = 8 * ((nb + 1 + 7) // 8)
    cvec = jnp.zeros((mod_rows, d), F32).at[:nb].set(c).at[nb].set(c_ctx)

    for l in range(depth):
        wl = _layer_weights(l, p)
        last = l == depth - 1
        mod = _modulation(cvec, ada_w[l], ada_b[l])
        mod_lat = mod[:nb].reshape(nb, N_MOD, d).transpose(1, 0, 2)
        mod_ctx = jnp.broadcast_to(mod[nb].reshape(N_MOD, 1, d), (N_MOD, nb, d))
        modv = jnp.stack([mod_ctx, mod_lat])

        za, zb, zc, zd = _in_proj(xs, norm1_g[l], modv, wl['w_in'], n_ctx_tiles)

        r, k, v, g, av, w_dec, kd, bv = _rwkv_features(za, wl['mu'], wl['gup'], wl['wup'], wl['aup'],
                                                       wl['rwkv_vecs'], hs, n_ctx_tiles)
        ex = lambda t: _rwkv_expand_k(t, k_lo)
        ya_f, ya_b = _rwkv_scan(ex(r), ex(av), ex(w_dec), ex(kd), ex(bv), _rwkv_expand_v(v), l_ctx)
        ya = jnp.stack([_rwkv_collapse_v(ya_f, nb), _rwkv_collapse_v(ya_b, nb)])
        ob = _retention(jnp.transpose(zb, (1, 0, 2)), ret_tabs, l_ctx)
        ob = jnp.transpose(ob, (0, 2, 1, 3))
        hc = _lru(zc, wl['cw'], wl['wa'], wl['wx'], wl['lru_vecs'], l_ctx)
        yd = _from_colmajor(_s5(_to_colmajor(zd, l_ctx), *wl['s5'], l_ctx), l_ctx)

        t0 = n_ctx_tiles if last else 0
        x1, h2 = _out_proj(xs, modv, ya, (r, k, v, g), ob, zb, hc, zc, yd, zd, wl['out_vecs'], hs,
                           wl['gluw'], wl['glub'], wl['wo'], wl['n2'], n_ctx_tiles, t0)
        t0p = n_ctx_tok_tiles if last else 0
        e1, cnt, r2, w2 = _peer_route(h2, wl['wq'], wl['keys'], t0p)
        x2 = _peer_ffn(h2, wl['u'], wl['vt'], e1, cnt, r2, w2, x1.reshape(lt * nb, d), modv,
                       n_ctx_tok_tiles, t0p)
        xs = x2.reshape(lt, nb, d)

    out = _final_norm(xs[l_ctx:].reshape(l_lat * nb, d), final_norm_g)
    return jnp.transpose(out.reshape(l_lat, nb, d), (1, 0, 2))
```

```python
import functools
import math

import jax
import jax.numpy as jnp
import numpy as np
from jax import lax
from jax.experimental import pallas as pl
from jax.experimental.pallas import tpu as pltpu

F32 = jnp.float32
BF16 = jnp.bfloat16

NORM_EPS = 1e-6
N_MOD = 6
HEAD_DIM = 64
N_HEADS = 4
GROUP_W = N_HEADS * HEAD_DIM
GRID_W = 64

RWKV_GATE_LORA = 64
RWKV_LORA = 32
RWKV_GN_EPS = 64e-5
RWKV_COLS = 3 * GROUP_W + RWKV_GATE_LORA + 4 * RWKV_LORA
ZA_W = 3 * GROUP_W + 3 * 128

RET_CHUNK = 128
RET_GN_EPS = 1e-5
RET_FWD_OFFSET = 5.0
RET_BWD_OFFSET = 5.5
ROPE_BASE = 10000.0
RET_COLS = 4 * GROUP_W

LRU_CONV_W = 4
LRU_C = 8.0
LRU_COLS = 2 * GROUP_W
LRU_CHUNK = 128

S5_CH = 16
S5_GROUPS = GROUP_W // S5_CH
S5_STATE = 64
S5_N = S5_GROUPS * S5_STATE
S5_CHUNK = 128

PEER_HEADS = 8
PEER_NKEYS = 128
PEER_DKEY = 128
PEER_TOPK = 16
PEER_TOK_TILE = 512
PEER_EXP_BLOCK = 2048

ROW_TILE_T = 32
RWKV_CHUNK = 64

VMEM_LIMIT_BYTES = 56 * 1024 * 1024

NEG_INF = float("-inf")


def _cparams(n_axes):
    return pltpu.CompilerParams(dimension_semantics=("arbitrary",) * n_axes,
                                vmem_limit_bytes=VMEM_LIMIT_BYTES)


def _sigmoid(x):
    return 1.0 / (1.0 + jnp.exp(-x))


def _gelu_tanh(x):
    return 0.5 * x * (1.0 + jnp.tanh(math.sqrt(2.0 / math.pi) * (x + 0.044715 * (x * x * x))))


def _softplus(x):
    return jnp.maximum(x, 0.0) + jnp.log(1.0 + jnp.exp(-jnp.abs(x)))


def _dot(a, b):
    return jnp.dot(a, b, preferred_element_type=F32)


def _dot_exact(a, b):
    return jnp.dot(a, b, preferred_element_type=F32, precision=lax.Precision.HIGHEST)


def _head_sum_matrix():
    idx = np.arange(GROUP_W) // HEAD_DIM
    return jnp.asarray((idx[:, None] == idx[None, :]).astype(np.float32))


def _mod_body(c_ref, w_ref, b_ref, o_ref):
    c = c_ref[...]
    s = (c * _sigmoid(c)).astype(BF16)
    o_ref[...] = _dot(s, w_ref[...].astype(BF16)) + b_ref[...]


def _modulation(cvec, w, b):
    rows, d = cvec.shape
    n = w.shape[1]
    tn = 512
    return pl.pallas_call(
        _mod_body,
        grid=(n // tn,),
        in_specs=[pl.BlockSpec((rows, d), lambda j: (0, 0)),
                  pl.BlockSpec((d, tn), lambda j: (0, j)),
                  pl.BlockSpec((1, tn), lambda j: (0, j))],
        out_specs=pl.BlockSpec((rows, tn), lambda j: (0, j)),
        out_shape=jax.ShapeDtypeStruct((rows, n), F32),
        compiler_params=_cparams(1),
        name="adaln_mod",
    )(cvec, w, b.reshape(1, n))


def _in_body(x_ref, g_ref, mod_ref, w_ref, za_ref, zb_ref, zc_ref, zd_ref):
    tt, nb, d = x_ref.shape
    x = x_ref[...]
    ms = jnp.mean(x * x, axis=-1, keepdims=True)
    y = x * lax.rsqrt(ms + NORM_EPS) * g_ref[...]
    h = y * (1.0 + mod_ref[1]) + mod_ref[0]
    hb = h.reshape(tt * nb, d).astype(BF16)
    c0 = 0
    for ref in (za_ref, zb_ref, zc_ref, zd_ref):
        w = ref.shape[-1]
        ref[...] = _dot(hb, w_ref[:, c0:c0 + w]).reshape(tt, nb, w)
        c0 += w


def _in_proj(xs, g, modv, w_p, n_ctx_tiles):
    lt, nb, d = xs.shape
    tt = ROW_TILE_T
    widths = (ZA_W, RET_COLS, LRU_COLS, GROUP_W)
    seg = lambda i: jnp.where(i >= n_ctx_tiles, 1, 0)
    return pl.pallas_call(
        _in_body,
        grid=(lt // tt,),
        in_specs=[pl.BlockSpec((tt, nb, d), lambda i: (i, 0, 0)),
                  pl.BlockSpec((1, d), lambda i: (0, 0)),
                  pl.BlockSpec((None, N_MOD, nb, d), lambda i: (seg(i), 0, 0, 0)),
                  pl.BlockSpec(w_p.shape, lambda i: (0, 0))],
        out_specs=[pl.BlockSpec((tt, nb, w), lambda i: (i, 0, 0)) for w in widths],
        out_shape=[jax.ShapeDtypeStruct((lt, nb, w), F32) for w in widths],
        compiler_params=_cparams(1),
        name="in_proj",
    )(xs, g.reshape(1, d), modv, w_p)


def _rwkv_feat_body(n_ctx_tiles, n_tiles,
                    z_ref, zp_ref, zn_ref, mu_ref, gup_ref, wup_ref, aup_ref, vec_ref, hs_ref,
                    r_ref, k_ref, v_ref, g_ref, av_ref, w_ref, kd_ref, bv_ref):
    i = pl.program_id(0)
    tt, nb, zw = z_ref.shape
    z0 = z_ref[...]
    has_prev = jnp.logical_and(i != 0, i != n_ctx_tiles)
    has_next = jnp.logical_and(i != n_ctx_tiles - 1, i != n_tiles - 1)
    hp = jnp.where(has_prev, zp_ref[...], 0.0)
    hn = jnp.where(has_next, zn_ref[...], 0.0)
    zp = jnp.concatenate([hp, z0[:-1]], axis=0)
    zn = jnp.concatenate([z0[1:], hn], axis=0)
    z = z0 + mu_ref[0] * (zp - z0) + mu_ref[1] * (zn - z0)
    z = z.reshape(tt * nb, zw)
    gw = GROUP_W
    r = z[:, 0:gw]
    k = z[:, gw:2 * gw]
    v = z[:, 2 * gw:3 * gw]
    g = _dot(_sigmoid(z[:, 3 * gw:3 * gw + 128]).astype(BF16), gup_ref[...])
    w_low = jnp.tanh(z[:, 3 * gw + 128:3 * gw + 256]).astype(BF16)
    a_low = z[:, 3 * gw + 256:3 * gw + 384].astype(BF16)
    k_k = vec_ref[0:1, :]
    k_a = vec_ref[1:2, :]
    kk = k * k_k
    ss = _dot_exact(kk * kk, hs_ref[...])
    kk = kk / jnp.maximum(jnp.sqrt(ss), 1e-12)
    shp = (tt, nb, gw)
    r_ref[...] = r.reshape(shp)
    k_ref[...] = k.reshape(shp)
    v_ref[...] = v.reshape(shp)
    g_ref[...] = g.reshape(shp)
    av_ref[...] = (-kk).reshape(shp)
    for d in range(2):
        w0 = vec_ref[2 + d:3 + d, :]
        a0 = vec_ref[4 + d:5 + d, :]
        w_log = -_softplus(-(w0 + _dot(w_low, wup_ref[d]))) - 0.5
        w_ref[d] = jnp.exp(-jnp.exp(w_log)).reshape(shp)
        a = _sigmoid(a0 + _dot(a_low, aup_ref[d]))
        kd_ref[d] = (k * (1.0 + (a - 1.0) * k_a)).reshape(shp)
        bv_ref[d] = (kk * a).reshape(shp)


def _rwkv_features(za, mu_p, gup_p, wup_p, aup_p, vecs, hs, n_ctx_tiles):
    lt, nb, zw = za.shape
    tt = ROW_TILE_T
    n_tiles = lt // tt
    gw = GROUP_W
    one = lambda shape: jax.ShapeDtypeStruct(shape, F32)
    full = lambda a: pl.BlockSpec(a.shape, lambda i: (0,) * a.ndim)
    o1 = pl.BlockSpec((tt, nb, gw), lambda i: (i, 0, 0))
    o2 = pl.BlockSpec((2, tt, nb, gw), lambda i: (0, i, 0, 0))
    return pl.pallas_call(
        functools.partial(_rwkv_feat_body, n_ctx_tiles, n_tiles),
        grid=(n_tiles,),
        in_specs=[pl.BlockSpec((tt, nb, zw), lambda i: (i, 0, 0)),
                  pl.BlockSpec((1, nb, zw), lambda i: (jnp.maximum(i * tt - 1, 0), 0, 0)),
                  pl.BlockSpec((1, nb, zw), lambda i: (jnp.minimum((i + 1) * tt, lt - 1), 0, 0)),
                  full(mu_p), full(gup_p), full(wup_p), full(aup_p), full(vecs), full(hs)],
        out_specs=[o1, o1, o1, o1, o1, o2, o2, o2],
        out_shape=[one((lt, nb, gw))] * 5 + [one((2, lt, nb, gw))] * 3,
        compiler_params=_cparams(1),
        name="rwkv_features",
    )(za, za, za, mu_p, gup_p, wup_p, aup_p, vecs, hs)


def _rwkv_scan_body(rf_ref, af_ref, wf_ref, kdf_ref, bf_ref, vf_ref,
                    rb_ref, ab_ref, wb_ref, kdb_ref, bb_ref, vb_ref,
                    yf_ref, yb_ref, s_ref, vx_ref):
    j = pl.program_id(0)
    tc, n_khi, lanes = rf_ref.shape
    n_v, n_bh = vf_ref.shape[1:]
    k_lo = lanes // n_bh
    n_half = 2
    hv = n_v // n_half

    @pl.when(j == 0)
    def _():
        s_ref[...] = jnp.zeros_like(s_ref)

    for d, ref in enumerate((vf_ref, vb_ref)):
        x = ref[...]
        reps = 1
        while reps < k_lo:
            x = jnp.concatenate([x, x], axis=-1)
            reps *= 2
        vx_ref[d] = x

    def group_sum(x):
        parts = [x] + [pltpu.roll(x, g * n_bh, axis=1) for g in range(1, k_lo)]
        while len(parts) > 1:
            parts = [parts[n] + parts[n + 1] for n in range(0, len(parts) - 1, 2)] + parts[len(parts) & ~1:]
        return parts[0]

    dirs = ((rf_ref, af_ref, wf_ref, kdf_ref, bf_ref, yf_ref), (rb_ref, ab_ref, wb_ref, kdb_ref, bb_ref, yb_ref))

    segs = [(d, slice(half * hv, (half + 1) * hv)) for d in range(2) for half in range(n_half)]

    def step(i, carry):
        t = (i, tc - 1 - i)
        row = lambda ref, d, kh: ref[t[d], pl.ds(kh, 1), :]
        ip = jnp.maximum(i - 1, 0)
        tp = (ip, tc - 1 - ip)
        for n, (d, rows) in enumerate(segs):
            dirs[d][5][tp[d], rows, :] = group_sum(carry[n])[:, :n_bh]
        acc = [None] * len(segs)
        for kh in range(n_khi):
            a_row = [row(dirs[d][1], d, kh) for d in range(2)]
            for n, (d, rows) in enumerate(segs):
                term = s_ref[d, kh, rows, :] * a_row[d]
                acc[n] = term if acc[n] is None else acc[n] + term
        sa = [group_sum(a) for a in acc]
        for kh in range(n_khi):
            w_row, kd_row = ([row(dirs[d][c], d, kh) for d in range(2)] for c in (2, 3))
            for n, (d, rows) in enumerate(segs):
                s_ref[d, kh, rows, :] = s_ref[d, kh, rows, :] * w_row[d] + vx_ref[d, t[d], rows, :] * kd_row[d]
        acc = [None] * len(segs)
        for kh in range(n_khi):
            b_row, r_row = ([row(dirs[d][c], d, kh) for d in range(2)] for c in (4, 0))
            for n, (d, rows) in enumerate(segs):
                s_new = s_ref[d, kh, rows, :] + sa[n] * b_row[d]
                s_ref[d, kh, rows, :] = s_new
                term = s_new * r_row[d]
                acc[n] = term if acc[n] is None else acc[n] + term
        return tuple(acc)

    zero = jnp.zeros((hv, lanes), F32)
    last = lax.fori_loop(0, tc, step, (zero,) * len(segs))
    t_last = (tc - 1, 0)
    for n, (d, rows) in enumerate(segs):
        dirs[d][5][t_last[d], rows, :] = group_sum(last[n])[:, :n_bh]


def _dir_chunk(d, j, n_ctx, n_all):
    bwd = jnp.where(j < n_ctx, n_ctx - 1 - j, n_all - 1 - (j - n_ctx))
    return jnp.where(d == 0, j, bwd)


def _rwkv_scan(r_e, a_e, w_e, kd_e, b_e, v_e, l_ctx):
    lt, n_khi, lanes = r_e.shape
    _, n_v, n_bh = v_e.shape
    tc = RWKV_CHUNK
    n_all, n_ctx = lt // tc, l_ctx // tc
    specs = []
    for d in range(2):
        cm = functools.partial(_dir_chunk, d, n_ctx=n_ctx, n_all=n_all)
        shared = pl.BlockSpec((tc, n_khi, lanes), lambda j, cm=cm: (cm(j), 0, 0))
        per_dir = pl.BlockSpec((None, tc, n_khi, lanes), lambda j, cm=cm, d=d: (d, cm(j), 0, 0))
        tok = pl.BlockSpec((tc, n_v, n_bh), lambda j, cm=cm: (cm(j), 0, 0))
        specs.append(([shared, shared, per_dir, per_dir, per_dir, tok], tok))
    return pl.pallas_call(
        _rwkv_scan_body,
        grid=(n_all,),
        in_specs=specs[0][0] + specs[1][0],
        out_specs=[specs[0][1], specs[1][1]],
        out_shape=[jax.ShapeDtypeStruct((lt, n_v, n_bh), F32)] * 2,
        scratch_shapes=[pltpu.VMEM((2, n_khi, n_v, lanes), F32), pltpu.VMEM((2, tc, n_v, lanes), F32)],
        compiler_params=_cparams(1),
        name="rwkv_scan",
    )(r_e, a_e, w_e, kd_e, b_e, v_e, r_e, a_e, w_e, kd_e, b_e, v_e)


def _rwkv_expand_k(x, k_lo):
    lead = x.shape[:-2]
    nb = x.shape[-2]
    nd = len(lead)
    x = x.reshape(lead + (nb, N_HEADS, k_lo, HEAD_DIM // k_lo))
    x = jnp.transpose(x, tuple(range(nd)) + (nd + 3, nd + 2, nd, nd + 1))
    return x.reshape(lead + (HEAD_DIM // k_lo, k_lo * nb * N_HEADS))


def _rwkv_expand_v(x):
    lt, nb, _ = x.shape
    return jnp.transpose(x.reshape(lt, nb, N_HEADS, HEAD_DIM), (0, 3, 1, 2)).reshape(lt, HEAD_DIM, nb * N_HEADS)


def _rwkv_collapse_v(y, nb):
    lt = y.shape[0]
    return jnp.transpose(y.reshape(lt, HEAD_DIM, nb, N_HEADS), (0, 2, 3, 1)).reshape(lt, nb, GROUP_W)


def _ret_body(nb, z_ref, cos_ref, sin_ref, dec_ref, rd_ref, wd_ref, cd_ref, hm_ref, o_ref, s_ref):
    j = pl.program_id(1)
    gw = GROUP_W
    half = gw // 2

    @pl.when(j == 0)
    def _():
        s_ref[...] = jnp.zeros_like(s_ref)

    cos = cos_ref[...]
    sin = sin_ref[...]

    def rope(x):
        x1, x2 = x[:, :half], x[:, half:]
        return jnp.concatenate([x1 * cos - x2 * sin, x1 * sin + x2 * cos], axis=-1)

    def one_batch(b, carry):
        q = rope(z_ref[b, :, 0:gw]) * (HEAD_DIM ** -0.5)
        k = rope(z_ref[b, :, gw:2 * gw])
        v = z_ref[b, :, 2 * gw:3 * gw]
        vb = v.astype(BF16)
        k_t = k.T
        k_tb = k_t.astype(BF16)
        state = s_ref[b]
        inter = _dot(q.astype(BF16), state.astype(BF16)) * rd_ref[...]
        intra = jnp.zeros_like(inter)
        for h in range(N_HEADS):
            qm = (q * hm_ref[h:h + 1, :]).astype(BF16)
            att = _dot(qm, k_tb) * dec_ref[h]
            vm = (v * hm_ref[N_HEADS + h:N_HEADS + h + 1, :]).astype(BF16)
            intra = intra + _dot(att.astype(BF16), vm)
        o_ref[b] = intra + inter
        kv = _dot((k_t * wd_ref[...]).astype(BF16), vb)
        s_ref[b] = state * cd_ref[0] + kv * cd_ref[1]
        return carry

    lax.fori_loop(0, nb, one_batch, 0)


def _retention(zb, tabs, l_ctx):
    nb, lt, _ = zb.shape
    c = RET_CHUNK
    n_all, n_ctx = lt // c, l_ctx // c
    gw = GROUP_W
    cos, sin, dec, rd, wd, cd, hm = tabs
    cm = lambda d, j: _dir_chunk(d, j, n_ctx, n_all)
    return pl.pallas_call(
        functools.partial(_ret_body, nb),
        grid=(2, n_all),
        in_specs=[pl.BlockSpec((nb, c, RET_COLS), lambda d, j: (0, cm(d, j), 0)),
                  pl.BlockSpec((c, gw // 2), lambda d, j: (cm(d, j), 0)),
                  pl.BlockSpec((c, gw // 2), lambda d, j: (cm(d, j), 0)),
                  pl.BlockSpec((None, N_HEADS, c, c), lambda d, j: (d, 0, 0, 0)),
                  pl.BlockSpec((None, c, gw), lambda d, j: (d, 0, 0)),
                  pl.BlockSpec((None, gw, c), lambda d, j: (d, 0, 0)),
                  pl.BlockSpec((None, 2, gw, gw), lambda d, j: (d, 0, 0, 0)),
                  pl.BlockSpec((2 * N_HEADS, gw), lambda d, j: (0, 0))],
        out_specs=pl.BlockSpec((None, nb, c, gw), lambda d, j: (d, 0, cm(d, j), 0)),
        out_shape=jax.ShapeDtypeStruct((2, nb, lt, gw), F32),
        scratch_shapes=[pltpu.VMEM((nb, gw, gw), F32)],
        compiler_params=_cparams(2),
        name="retention",
    )(zb, cos, sin, dec, rd, wd, cd, hm)


def _ret_perm():
    new = np.arange(GROUP_W)
    half_id, rem = new // 128, new % 128
    h, i = rem // 32, rem % 32
    return h * HEAD_DIM + half_id * 32 + i


def _ret_tables(lt):
    c = RET_CHUNK
    half = HEAD_DIM // 2
    freqs = ROPE_BASE ** (-jnp.arange(half, dtype=F32) / half)
    ang = jnp.arange(lt, dtype=F32)[:, None] * freqs[None]
    cos = jnp.tile(jnp.cos(ang), (1, N_HEADS))
    sin = jnp.tile(jnp.sin(ang), (1, N_HEADS))
    idx = jnp.arange(c, dtype=F32)
    perm = _ret_perm()
    head_of_qk = jnp.asarray(perm // HEAD_DIM)
    head_of_v = jnp.arange(GROUP_W) // HEAD_DIM
    dec, rd, wd, cd = [], [], [], []
    for offset, rev in ((RET_FWD_OFFSET, False), (RET_BWD_OFFSET, True)):
        lg = jnp.log1p(-jnp.exp2(-(offset + jnp.arange(N_HEADS, dtype=F32))))
        diff = idx[:, None] - idx[None, :]
        if rev:
            mask, dist = diff < 0, -diff
            read_pow, write_pow = c - idx, idx
        else:
            mask, dist = diff >= 0, diff
            read_pow, write_pow = idx + 1.0, c - 1.0 - idx
        dec.append(jnp.where(mask[None], jnp.exp(lg[:, None, None] * jnp.where(mask, dist, 0.0)[None]), 0.0))
        rd.append(jnp.exp(lg[head_of_v][None, :] * read_pow[:, None]))
        wd.append(jnp.exp(lg[head_of_qk][:, None] * write_pow[None, :]))
        same = (head_of_qk[:, None] == head_of_v[None, :]).astype(F32)
        cd.append(jnp.stack([same * jnp.exp(lg * c)[head_of_v][None, :], same]))
    hm_q = (head_of_qk[None, :] == jnp.arange(N_HEADS)[:, None]).astype(F32)
    hm_v = (head_of_v[None, :] == jnp.arange(N_HEADS)[:, None]).astype(F32)
    return cos, sin, jnp.stack(dec), jnp.stack(rd), jnp.stack(wd), jnp.stack(cd), jnp.concatenate([hm_q, hm_v])


def _lru_body(n_ctx, n_all, x_ref, xp_ref, xn_ref, cw_ref, wa_ref, wx_ref, vec_ref, o_ref,
              a_s, b_s, h_s):
    d = pl.program_id(0)
    j = pl.program_id(1)
    tc, nb, gw = x_ref.shape
    ch = _dir_chunk(d, j, n_ctx, n_all)

    @pl.when(j == 0)
    def _():
        h_s[...] = jnp.zeros_like(h_s)

    has_prev = jnp.logical_and(ch != 0, ch != n_ctx)
    has_next = jnp.logical_and(ch != n_ctx - 1, ch != n_all - 1)
    xe = jnp.concatenate([jnp.where(has_prev, xp_ref[...], 0.0), x_ref[...],
                          jnp.where(has_next, xn_ref[...], 0.0)], axis=0)
    xc = cw_ref[LRU_CONV_W:LRU_CONV_W + 1, :]
    for tap in range(LRU_CONV_W):
        xc = xc + cw_ref[tap:tap + 1, :] * xe[tap:tap + tc]
    xc = xc.reshape(tc * nb, gw)
    xb = xc.astype(BF16)
    r = _sigmoid(_dot(xb, wa_ref[...]) + vec_ref[0:1, :])
    gi = _sigmoid(_dot(xb, wx_ref[...]) + vec_ref[1:2, :])
    log_a = -LRU_C * r * _softplus(-vec_ref[2:3, :])
    th = jnp.tanh(log_a)
    one_minus_a2 = 2.0 * th / (th - 1.0)
    a_s[...] = jnp.exp(log_a).reshape(tc, nb, gw)
    b_s[...] = (jnp.sqrt(one_minus_a2) * (gi * xc)).reshape(tc, nb, gw)

    def step(i, h):
        t = jnp.where(d == 0, i, tc - 1 - i)
        h = a_s[t] * h + b_s[t]
        o_ref[t] = h
        return h

    h_s[...] = lax.fori_loop(0, tc, step, h_s[...])


def _lru(zc, cw, wa_bd, wx_bd, vecs, l_ctx):
    lt, nb, _ = zc.shape
    gw = GROUP_W
    tc = LRU_CHUNK
    n_all, n_ctx = lt // tc, l_ctx // tc
    cm = lambda d, j: _dir_chunk(d, j, n_ctx, n_all)
    return pl.pallas_call(
        functools.partial(_lru_body, n_ctx, n_all),
        grid=(2, n_all),
        in_specs=[pl.BlockSpec((tc, nb, gw), lambda d, j: (cm(d, j), 0, 0)),
                  pl.BlockSpec((2, nb, gw), lambda d, j: (jnp.maximum(cm(d, j) * (tc // 2) - 1, 0), 0, 0)),
                  pl.BlockSpec((1, nb, gw), lambda d, j: (jnp.minimum((cm(d, j) + 1) * tc, lt - 1), 0, 0)),
                  pl.BlockSpec(cw.shape, lambda d, j: (0, 0)),
                  pl.BlockSpec((None, gw, gw), lambda d, j: (d, 0, 0)),
                  pl.BlockSpec((None, gw, gw), lambda d, j: (d, 0, 0)),
                  pl.BlockSpec((None, 3, gw), lambda d, j: (d, 0, 0))],
        out_specs=pl.BlockSpec((None, tc, nb, gw), lambda d, j: (d, cm(d, j), 0, 0)),
        out_shape=jax.ShapeDtypeStruct((2, lt, nb, gw), F32),
        scratch_shapes=[pltpu.VMEM((tc, nb, gw), F32), pltpu.VMEM((tc, nb, gw), F32),
                        pltpu.VMEM((nb, gw), F32)],
        compiler_params=_cparams(2),
        name="rglru",
    )(zc, zc, zc, cw, wa_bd, wx_bd, vecs)


def _block_diag(w):
    nblk, n = w.shape[-3], w.shape[-1]
    eye = jnp.eye(nblk, dtype=w.dtype)
    out = w[..., :, :, None, :] * eye[:, None, :, None]
    return out.reshape(w.shape[:-3] + (nblk * n, nblk * n))


def _s5_body(u_ref, bm_ref, lam_ref, cm_ref, y_ref, h_s, st_s):
    d = pl.program_id(0)
    j = pl.program_id(1)
    tc, nb, gw = u_ref.shape
    n = S5_N

    @pl.when(j == 0)
    def _():
        st_s[...] = jnp.zeros_like(st_s)

    u = u_ref[...].reshape(tc * nb, gw).astype(BF16)
    h_s[...] = _dot(u, bm_ref[...]).reshape(tc, nb, 2 * n)
    lr = lam_ref[0:1, :]
    li = lam_ref[1:2, :]

    def step(i, carry):
        hr, hi = carry
        t = jnp.where(d == 0, i, tc - 1 - i)
        bu = h_s[t]
        nr = lr * hr - li * hi + bu[:, :n]
        ni = lr * hi + li * hr + bu[:, n:]
        h_s[t] = jnp.concatenate([nr, ni], axis=-1)
        return nr, ni

    hr, hi = lax.fori_loop(0, tc, step, (st_s[0], st_s[1]), unroll=4)
    st_s[0] = hr
    st_s[1] = hi
    hh = h_s[...].reshape(tc * nb, 2 * n).astype(BF16)
    y_ref[...] = _dot(hh, cm_ref[...]).reshape(tc, nb, gw)


def _s5(u, bmat, lam, cmat, l_ctx):
    lt, nb, gw = u.shape
    tc = S5_CHUNK
    n_all, n_ctx = lt // tc, l_ctx // tc
    cm = lambda d, j: _dir_chunk(d, j, n_ctx, n_all)
    return pl.pallas_call(
        _s5_body,
        grid=(2, n_all),
        in_specs=[pl.BlockSpec((tc, nb, gw), lambda d, j: (cm(d, j), 0, 0)),
                  pl.BlockSpec((None, gw, 2 * S5_N), lambda d, j: (d, 0, 0)),
                  pl.BlockSpec((None, 2, S5_N), lambda d, j: (d, 0, 0)),
                  pl.BlockSpec((None, 2 * S5_N, gw), lambda d, j: (d, 0, 0))],
        out_specs=pl.BlockSpec((None, tc, nb, gw), lambda d, j: (d, cm(d, j), 0, 0)),
        out_shape=jax.ShapeDtypeStruct((2, lt, nb, gw), F32),
        scratch_shapes=[pltpu.VMEM((tc, nb, 2 * S5_N), F32), pltpu.VMEM((2, nb, S5_N), F32)],
        compiler_params=_cparams(2),
        name="s5",
    )(u, bmat, lam, cmat)


def _s5_params(a_re, a_im, log_dt, b_re, b_im, c_re, c_im):
    dt = jnp.exp(log_dt)[..., None]
    er = jnp.exp(a_re * dt)
    lbr, lbi = er * jnp.cos(a_im * dt), er * jnp.sin(a_im * dt)
    den = a_re * a_re + a_im * a_im
    nr, ni = lbr - 1.0, lbi
    fr = (nr * a_re + ni * a_im) / den
    fi = (ni * a_re - nr * a_im) / den
    bbr = fr[..., None] * b_re - fi[..., None] * b_im
    bbi = fr[..., None] * b_im + fi[..., None] * b_re
    eye = jnp.eye(S5_GROUPS, dtype=F32)

    def in_map(bb):
        m = jnp.einsum('dgpc,gh->dgchp', bb, eye)
        return m.reshape(2, GROUP_W, S5_N)

    def out_map(cc):
        m = jnp.einsum('dgcp,gh->dgphc', cc, eye)
        return m.reshape(2, S5_N, GROUP_W)

    bmat = jnp.concatenate([in_map(bbr), in_map(bbi)], axis=-1).astype(BF16)
    cmat = jnp.concatenate([out_map(c_re), -out_map(c_im)], axis=-2).astype(BF16)
    lam = jnp.stack([lbr.reshape(2, S5_N), lbi.reshape(2, S5_N)], axis=1)
    return bmat, lam, cmat


def _to_colmajor(z, l_ctx):
    lt, nb, ch = z.shape
    rows = (lt - l_ctx) // GRID_W
    lat = z[l_ctx:].reshape(rows, GRID_W, nb, ch).transpose(1, 0, 2, 3).reshape(lt - l_ctx, nb, ch)
    return jnp.concatenate([z[:l_ctx], lat], axis=0)


def _from_colmajor(y, l_ctx):
    lt = y.shape[-3]
    nb, ch = y.shape[-2:]
    rows = (lt - l_ctx) // GRID_W
    lead = y.shape[:-3]
    lat = y[..., l_ctx:, :, :].reshape(lead + (GRID_W, rows, nb, ch))
    lat = jnp.swapaxes(lat, -4, -3).reshape(lead + (lt - l_ctx, nb, ch))
    return jnp.concatenate([y[..., :l_ctx, :, :], lat], axis=-3)


def _out_body(x_ref, mod_ref, ya_ref, r_ref, k_ref, v_ref, g_ref, ob_ref, gb_ref, hc_ref, gc_ref,
              yd_ref, ud_ref, vec_ref, hs_ref, gluw_ref, glub_ref, wo_ref, n2_ref, x1_ref, h2_ref):
    tt, nb, d = x_ref.shape
    gw = GROUP_W
    rows = tt * nb
    hs = hs_ref[...]
    two = lambda ref: (ref[0] + ref[1]).reshape(rows, gw)
    flat = lambda ref: ref[...].reshape(rows, gw)

    def head_norm(y, eps):
        mu = _dot_exact(y, hs) * (1.0 / HEAD_DIM)
        dlt = y - mu
        var = _dot_exact(dlt * dlt, hs) * (1.0 / HEAD_DIM)
        return dlt * lax.rsqrt(var + eps)

    r_k, ln_g, ln_b = vec_ref[0:1, :], vec_ref[1:2, :], vec_ref[2:3, :]
    r, k, v = flat(r_ref), flat(k_ref), flat(v_ref)
    yn = head_norm(two(ya_ref), RWKV_GN_EPS) * ln_g + ln_b
    bonus = _dot_exact(r * k * r_k, hs) * v
    mix_a = (yn + bonus) * flat(g_ref)
    gb = flat(gb_ref)
    mix_b = gb * _sigmoid(gb) * (head_norm(two(ob_ref), RET_GN_EPS) * vec_ref[3:4, :])
    mix_c = _gelu_tanh(flat(gc_ref)) * two(hc_ref)
    yd = vec_ref[4:5, :] * flat(ud_ref) + two(yd_ref)
    glu = _dot(_gelu_tanh(yd).astype(BF16), gluw_ref[...]) + glub_ref[...]
    mix_d = glu[:, :gw] * _sigmoid(glu[:, gw:])

    mix = _dot(mix_a.astype(BF16), wo_ref[0:gw, :])
    mix = mix + _dot(mix_b.astype(BF16), wo_ref[gw:2 * gw, :])
    mix = mix + _dot(mix_c.astype(BF16), wo_ref[2 * gw:3 * gw, :])
    mix = mix + _dot(mix_d.astype(BF16), wo_ref[3 * gw:4 * gw, :])
    x1 = x_ref[...] + mod_ref[2] * mix.reshape(tt, nb, d)
    x1_ref[...] = x1
    ms = jnp.mean(x1 * x1, axis=-1, keepdims=True)
    h2 = (x1 * lax.rsqrt(ms + NORM_EPS) * n2_ref[...]) * (1.0 + mod_ref[4]) + mod_ref[3]
    h2_ref[...] = h2.reshape(rows, d).T.astype(BF16)


def _out_proj(xs, modv, ya, feats, ob, zb, hc, zc, yd, zd, vecs, hs, gluw, glub, wo, n2, n_ctx_tiles, t0_tile):
    lt, nb, d = xs.shape
    tt = ROW_TILE_T
    gw = GROUP_W
    n_tiles = lt // tt - t0_tile
    seg = lambda i: jnp.where(i + t0_tile >= n_ctx_tiles, 1, 0)
    col = lambda c: pl.BlockSpec((tt, nb, gw), lambda i: (i + t0_tile, 0, c))
    two = pl.BlockSpec((2, tt, nb, gw), lambda i: (0, i + t0_tile, 0, 0))
    full = lambda a: pl.BlockSpec(a.shape, lambda i: (0,) * a.ndim)
    r, k, v, g = feats
    return pl.pallas_call(
        _out_body,
        grid=(n_tiles,),
        in_specs=[pl.BlockSpec((tt, nb, d), lambda i: (i + t0_tile, 0, 0)),
                  pl.BlockSpec((None, N_MOD, nb, d), lambda i: (seg(i), 0, 0, 0)),
                  two, col(0), col(0), col(0), col(0),
                  two, col(3), two, col(1), two, col(0),
                  full(vecs), full(hs), full(gluw), full(glub), full(wo), full(n2)],
        out_specs=[pl.BlockSpec((tt, nb, d), lambda i: (i + t0_tile, 0, 0)),
                   pl.BlockSpec((d, tt * nb), lambda i: (0, i + t0_tile))],
        out_shape=[jax.ShapeDtypeStruct((lt, nb, d), F32), jax.ShapeDtypeStruct((d, lt * nb), BF16)],
        compiler_params=_cparams(1),
        name="out_proj",
    )(xs, modv, ya, r, k, v, g, ob, zb, hc, zc, yd, zd, vecs, hs, gluw, glub, wo, n2)


PEER_NO_RANK = 127.0


def _top_values(s, n, with_rank=False):
    vals, cur = [], s
    rank = jnp.full(s.shape, PEER_NO_RANK, F32) if with_rank else None
    for k in range(n):
        m = jnp.max(cur, axis=0, keepdims=True)
        vals.append(m)
        hit = cur == m
        if with_rank:
            rank = jnp.where(hit, float(k), rank)
        cur = jnp.where(hit, NEG_INF, cur)
    return (vals, rank) if with_rank else vals


def _peer_route_body(h_ref, wq_ref, keys_ref, e1_ref, cnt_ref, r2_ref, w2_ref, pack_ref):
    hb = h_ref[...]
    for h in range(PEER_HEADS):
        st = []
        for p in range(2):
            c0 = (2 * h + p) * PEER_DKEY
            q = _dot(wq_ref[c0:c0 + PEER_DKEY, :], hb).astype(BF16)
            st.append(_dot(keys_ref[2 * h + p], q))
        s1, s2 = st
        n_top = PEER_TOPK + 1
        v1 = _top_values(s1, n_top)
        v2, rank2 = _top_values(s2, n_top, with_rank=True)
        pad7 = [jnp.full((7, s1.shape[1]), NEG_INF, F32)]
        v2m = jnp.concatenate(v2 + pad7, axis=0)
        v1_tail = jnp.concatenate(v1[8:] + pad7, axis=0)
        cand = jnp.concatenate([v1[0] + v2m] + [v1[a] + v2m[0:8] for a in range(1, 8)]
                               + [v1_tail + v2[0]], axis=0)
        top = _top_values(cand, n_top)
        tau = 0.5 * (top[PEER_TOPK - 1] + top[PEER_TOPK])
        cmax = top[0]
        z = jnp.sum(jnp.where(cand >= tau, jnp.exp(cand - cmax), 0.0), axis=0, keepdims=True)
        thr = tau - s1
        cnt = jnp.zeros_like(s1)
        for k in range(PEER_TOPK):
            cnt = cnt + jnp.where(v2[k] >= thr, 1.0, 0.0)
        e1_ref[:, h, :] = jnp.exp(s1 - v1[0])
        cnt_ref[:, h, :] = cnt
        pack_ref[0] = rank2.astype(BF16)
        pack_ref[1] = (jnp.exp(s2 - v2[0]) / z).astype(BF16)
        words = pack_ref.bitcast(jnp.uint32)
        r2_ref[h] = words[0]
        w2_ref[h] = words[1]


def _peer_route(h2t, wq, keys, t0_tile):
    d, t_all = h2t.shape
    tt = PEER_TOK_TILE
    nk = PEER_NKEYS
    by_i = pl.BlockSpec((nk, PEER_HEADS, tt), lambda i: (0, 0, i + t0_tile))
    by_h = pl.BlockSpec((PEER_HEADS, nk // 2, tt), lambda i: (0, 0, i + t0_tile))
    return pl.pallas_call(
        _peer_route_body,
        grid=(t_all // tt - t0_tile,),
        in_specs=[pl.BlockSpec((d, tt), lambda i: (0, i + t0_tile)),
                  pl.BlockSpec(wq.shape, lambda i: (0, 0)),
                  pl.BlockSpec(keys.shape, lambda i: (0, 0, 0))],
        out_specs=[by_i, by_i, by_h, by_h],
        out_shape=[jax.ShapeDtypeStruct((nk, PEER_HEADS, t_all), F32)] * 2
                  + [jax.ShapeDtypeStruct((PEER_HEADS, nk // 2, t_all), jnp.uint32)] * 2,
        scratch_shapes=[pltpu.VMEM((2, nk, tt), BF16)],
        compiler_params=_cparams(1),
        name="peer_route",
    )(h2t, wq, keys)


PEER_GATE_ROWS = 32
LANES = 128


BF16_ROWS = 16
SUBLANES = 8


def _peer_gate_block(p_ref, e1_ref, cnt_ref, r2_ref, w2_ref, bc_ref):
    nk = PEER_NKEYS
    n_sub = p_ref.shape[0] // nk
    rc = PEER_GATE_ROWS
    n_pv = rc // BF16_ROWS
    for c0 in range(0, p_ref.shape[1], LANES):
        cols = slice(c0, c0 + LANES)
        for ii in range(n_sub):
            for h in range(PEER_HEADS):
                bc_ref[0, ii, h] = jnp.broadcast_to(cnt_ref[ii, h:h + 1, cols], (BF16_ROWS, LANES)).astype(BF16)
                bc_ref[1, ii, h] = jnp.broadcast_to(e1_ref[ii, h:h + 1, cols], (BF16_ROWS, LANES)).astype(BF16)

        def chunk(q, carry, cols=cols):
            r0 = pl.multiple_of(q * rc, rc)
            gate = [[None] * n_pv for _ in range(n_sub)]
            for h in range(PEER_HEADS):
                rank = [pltpu.bitcast(r2_ref[h, q, SUBLANES * v:SUBLANES * (v + 1), cols], BF16)
                        for v in range(n_pv)]
                w2 = [pltpu.bitcast(w2_ref[h, q, SUBLANES * v:SUBLANES * (v + 1), cols], BF16)
                      for v in range(n_pv)]
                for ii in range(n_sub):
                    cnt = bc_ref[0, ii, h]
                    e1 = bc_ref[1, ii, h]
                    for v in range(n_pv):
                        term = jnp.where(rank[v] < cnt, w2[v] * e1, jnp.zeros_like(e1))
                        gate[ii][v] = term if gate[ii][v] is None else gate[ii][v] + term
            for ii in range(n_sub):
                for v in range(n_pv):
                    rows = pl.ds(ii * nk + r0 + BF16_ROWS * v, BF16_ROWS)
                    p_ref[rows, cols] = p_ref[rows, cols] * gate[ii][v]
            return carry

        lax.fori_loop(0, nk // rc, chunk, 0)


def _peer_ffn_body(h_ref, u_ref, vt_ref, e1_ref, cnt_ref, r2_ref, w2_ref, x_ref, mod_ref, o_ref,
                   acc_ref, p_ref, bc_ref, ht_ref):
    j = pl.program_id(1)

    @pl.when(j == 0)
    def _():
        acc_ref[...] = jnp.zeros_like(acc_ref)
        ht_ref[...] = h_ref[...]

    act = _dot(u_ref[...], ht_ref[...])
    p_ref[...] = _gelu_tanh(act.astype(BF16))
    _peer_gate_block(p_ref, e1_ref, cnt_ref, r2_ref, w2_ref, bc_ref)
    acc_ref[...] += _dot(vt_ref[...], p_ref[...])

    @pl.when(j == pl.num_programs(1) - 1)
    def _():
        tt, d = x_ref.shape
        nb = mod_ref.shape[1]
        y = acc_ref[...].T.reshape(tt // nb, nb, d) * mod_ref[5]
        o_ref[...] = x_ref[...] + y.reshape(tt, d)


def _peer_ffn(h2t, u_b, vt_b, layer, e1, cnt, r2, w2, x1, modv, n_ctx_tok_tiles, t0_tile):
    d, t_all = h2t.shape
    tt = PEER_TOK_TILE
    eb = PEER_EXP_BLOCK
    nk = PEER_NKEYS
    nb = modv.shape[2]
    n_e = u_b.shape[1] // eb
    n_tiles = t_all // tt - t0_tile
    seg = lambda i: jnp.where(i + t0_tile >= n_ctx_tok_tiles, 1, 0)
    by_i = pl.BlockSpec((eb // nk, PEER_HEADS, tt), lambda i, j: (j, 0, i + t0_tile))
    rc = PEER_GATE_ROWS
    by_h = pl.BlockSpec((PEER_HEADS, nk // rc, rc // 2, tt), lambda i, j: (0, 0, 0, i + t0_tile))
    r2, w2 = (a.reshape(PEER_HEADS, nk // rc, rc // 2, t_all) for a in (r2, w2))
    return pl.pallas_call(
        _peer_ffn_body,
        grid=(n_tiles, n_e),
        in_specs=[pl.BlockSpec((d, tt), lambda i, j: (0, i + t0_tile)),
                  pl.BlockSpec((None, eb, d), lambda i, j: (layer, j, 0)),
                  pl.BlockSpec((None, d, eb), lambda i, j: (layer, 0, j)),
                  by_i, by_i, by_h, by_h,
                  pl.BlockSpec((tt, d), lambda i, j: (i + t0_tile, 0)),
                  pl.BlockSpec((None, N_MOD, nb, d), lambda i, j: (seg(i), 0, 0, 0))],
        out_specs=pl.BlockSpec((tt, d), lambda i, j: (i + t0_tile, 0)),
        out_shape=jax.ShapeDtypeStruct((t_all, d), F32),
        scratch_shapes=[pltpu.VMEM((d, tt), F32), pltpu.VMEM((eb, tt), BF16),
                        pltpu.VMEM((2, eb // nk, PEER_HEADS, BF16_ROWS, LANES), BF16),
                        pltpu.VMEM((d, tt), BF16)],
        compiler_params=_cparams(2),
        name="peer_ffn",
    )(h2t, u_b, vt_b, e1, cnt, r2, w2, x1, modv)


def _final_body(x_ref, g_ref, o_ref):
    x = x_ref[...]
    ms = jnp.mean(x * x, axis=-1, keepdims=True)
    o_ref[...] = x * lax.rsqrt(ms + NORM_EPS) * g_ref[...]


def _final_norm(x2d, g):
    t, d = x2d.shape
    tr = 512
    return pl.pallas_call(
        _final_body,
        grid=(t // tr,),
        in_specs=[pl.BlockSpec((tr, d), lambda i: (i, 0)), pl.BlockSpec((1, d), lambda i: (0, 0))],
        out_specs=pl.BlockSpec((tr, d), lambda i: (i, 0)),
        out_shape=jax.ShapeDtypeStruct((t, d), F32),
        compiler_params=_cparams(1),
        name="final_norm",
    )(x2d, g.reshape(1, d))


def _pad_cols(w, width):
    return jnp.pad(w, [(0, 0)] * (w.ndim - 1) + [(0, width - w.shape[-1])])


def _rwkv_slabs(w):
    gw3 = 3 * GROUP_W
    parts = [w[..., :gw3],
             _pad_cols(w[..., gw3:gw3 + 64], 128),
             _pad_cols(w[..., gw3 + 64:gw3 + 128], 128),
             _pad_cols(w[..., gw3 + 128:gw3 + 192], 128)]
    return jnp.concatenate(parts, axis=-1)


def _layer_weights(l, p):
    d = p['w_in'].shape[1]
    w_in = p['w_in'][l]
    wa_, wb_, wc_, wd_ = jnp.split(w_in, [RWKV_COLS, RWKV_COLS + RET_COLS, RWKV_COLS + RET_COLS + LRU_COLS], axis=-1)
    perm = _ret_perm()
    gw = GROUP_W
    wb_ = jnp.concatenate([wb_[:, 0:gw][:, perm], wb_[:, gw:2 * gw][:, perm], wb_[:, 2 * gw:]], axis=-1)
    out = {'w_in': jnp.concatenate([_rwkv_slabs(wa_), wb_, wc_, wd_], axis=-1).astype(BF16)}
    out['mu'] = _rwkv_slabs(p['rwkv_mu'][l])[:, None, :]
    out['gup'] = jnp.pad(p['rwkv_g_up'][l], ((0, 64), (0, 0))).astype(BF16)
    lora = RWKV_LORA
    out['wup'] = jnp.stack([jnp.pad(p['rwkv_w_up'][l, dd], ((dd * lora, 128 - (dd + 1) * lora), (0, 0)))
                            for dd in range(2)]).astype(BF16)
    out['aup'] = jnp.stack([jnp.pad(p['rwkv_a_up'][l, dd], ((dd * lora, 128 - (dd + 1) * lora), (0, 0)))
                            for dd in range(2)]).astype(BF16)
    out['rwkv_vecs'] = jnp.concatenate([p['rwkv_k_k'][l][None], p['rwkv_k_a'][l][None],
                                        p['rwkv_w0'][l], p['rwkv_a0'][l]], axis=0)
    out['cw'] = jnp.concatenate([p['lru_conv_w'][l], p['lru_conv_b'][l][None]], axis=0)
    out['wa'] = _block_diag(p['lru_wa'][l]).astype(BF16)
    out['wx'] = _block_diag(p['lru_wx'][l]).astype(BF16)
    out['lru_vecs'] = jnp.stack([p['lru_ba'][l], p['lru_bx'][l], p['lru_lambda'][l]], axis=1)
    out['s5'] = _s5_params(p['s5_a_re'][l], p['s5_a_im'][l], p['s5_log_dt'][l], p['s5_b_re'][l],
                           p['s5_b_im'][l], p['s5_c_re'][l], p['s5_c_im'][l])
    out['out_vecs'] = jnp.stack([p['rwkv_r_k'][l].reshape(-1), p['rwkv_ln_g'][l], p['rwkv_ln_b'][l],
                                 p['ret_gn_g'][l], p['s5_d'][l]], axis=0)
    out['gluw'] = p['s5_glu_w'][l].astype(BF16)
    out['glub'] = p['s5_glu_b'][l][None]
    out['wo'] = p['w_out'][l].astype(BF16)
    out['n2'] = p['norm2_g'][l][None]
    out['wq'] = p['peer_wq'][l].T.astype(BF16)
    out['keys'] = p['peer_keys'][l].reshape(PEER_HEADS * 2, PEER_NKEYS, PEER_DKEY).astype(BF16)
    return out


def kernel(x, c, ctx, c_ctx, norm1_g, norm2_g, ada_w, ada_b, w_in, w_out, rwkv_mu, rwkv_w0, rwkv_w_up, rwkv_a0, rwkv_a_up, rwkv_g_up, rwkv_k_k, rwkv_k_a, rwkv_r_k, rwkv_ln_g, rwkv_ln_b, ret_gn_g, lru_conv_w, lru_conv_b, lru_wa, lru_ba, lru_wx, lru_bx, lru_lambda, s5_a_re, s5_a_im, s5_log_dt, s5_b_re, s5_b_im, s5_c_re, s5_c_im, s5_d, s5_glu_w, s5_glu_b, peer_wq, peer_keys, peer_u, peer_v, final_norm_g):
    p = dict(w_in=w_in, w_out=w_out, norm2_g=norm2_g, rwkv_mu=rwkv_mu, rwkv_w0=rwkv_w0, rwkv_w_up=rwkv_w_up,
             rwkv_a0=rwkv_a0, rwkv_a_up=rwkv_a_up, rwkv_g_up=rwkv_g_up, rwkv_k_k=rwkv_k_k, rwkv_k_a=rwkv_k_a,
             rwkv_r_k=rwkv_r_k, rwkv_ln_g=rwkv_ln_g, rwkv_ln_b=rwkv_ln_b, ret_gn_g=ret_gn_g,
             lru_conv_w=lru_conv_w, lru_conv_b=lru_conv_b, lru_wa=lru_wa, lru_ba=lru_ba, lru_wx=lru_wx,
             lru_bx=lru_bx, lru_lambda=lru_lambda, s5_a_re=s5_a_re, s5_a_im=s5_a_im, s5_log_dt=s5_log_dt,
             s5_b_re=s5_b_re, s5_b_im=s5_b_im, s5_c_re=s5_c_re, s5_c_im=s5_c_im, s5_d=s5_d,
             s5_glu_w=s5_glu_w, s5_glu_b=s5_glu_b, peer_wq=peer_wq, peer_keys=peer_keys, peer_u=peer_u,
             peer_v=peer_v)
    nb, l_lat, d = x.shape
    l_ctx = ctx.shape[1]
    lt = l_ctx + l_lat
    depth = w_in.shape[0]
    assert 128 % (nb * N_HEADS) == 0 and l_ctx % 128 == 0 and l_lat % 128 == 0
    assert (l_ctx * nb) % PEER_TOK_TILE == 0 and (l_lat * nb) % PEER_TOK_TILE == 0
    k_lo = LANES // (nb * N_HEADS)
    n_ctx_tiles = l_ctx // ROW_TILE_T
    n_ctx_tok_tiles = l_ctx * nb // PEER_TOK_TILE
    hs = _head_sum_matrix()
    ret_tabs = _ret_tables(lt)
    u_all = peer_u.astype(BF16)
    vt_all = jnp.swapaxes(peer_v, 1, 2).astype(BF16)

    xs = jnp.transpose(jnp.concatenate([ctx, x], axis=1), (1, 0, 2))
    mod_rows = 8 * ((nb + 1 + 7) // 8)
    cvec = jnp.zeros((mod_rows, d), F32).at[:nb].set(c).at[nb].set(c_ctx)

    for l in range(depth):
        wl = _layer_weights(l, p)
        last = l == depth - 1
        mod = _modulation(cvec, ada_w[l], ada_b[l])
        mod_lat = mod[:nb].reshape(nb, N_MOD, d).transpose(1, 0, 2)
        mod_ctx = jnp.broadcast_to(mod[nb].reshape(N_MOD, 1, d), (N_MOD, nb, d))
        modv = jnp.stack([mod_ctx, mod_lat])

        za, zb, zc, zd = _in_proj(xs, norm1_g[l], modv, wl['w_in'], n_ctx_tiles)

        r, k, v, g, av, w_dec, kd, bv = _rwkv_features(za, wl['mu'], wl['gup'], wl['wup'], wl['aup'],
                                                       wl['rwkv_vecs'], hs, n_ctx_tiles)
        ex = lambda t: _rwkv_expand_k(t, k_lo)
        ya_f, ya_b = _rwkv_scan(ex(r), ex(av), ex(w_dec), ex(kd), ex(bv), _rwkv_expand_v(v), l_ctx)
        ya = jnp.stack([_rwkv_collapse_v(ya_f, nb), _rwkv_collapse_v(ya_b, nb)])
        ob = _retention(jnp.transpose(zb, (1, 0, 2)), ret_tabs, l_ctx)
        ob = jnp.transpose(ob, (0, 2, 1, 3))
        hc = _lru(zc, wl['cw'], wl['wa'], wl['wx'], wl['lru_vecs'], l_ctx)
        yd = _from_colmajor(_s5(_to_colmajor(zd, l_ctx), *wl['s5'], l_ctx), l_ctx)

        t0 = n_ctx_tiles if last else 0
        x1, h2 = _out_proj(xs, modv, ya, (r, k, v, g), ob, zb, hc, zc, yd, zd, wl['out_vecs'], hs,
                           wl['gluw'], wl['glub'], wl['wo'], wl['n2'], n_ctx_tiles, t0)
        t0p = n_ctx_tok_tiles if last else 0
        e1, cnt, r2, w2 = _peer_route(h2, wl['wq'], wl['keys'], t0p)
        x2 = _peer_ffn(h2, u_all, vt_all, l, e1, cnt, r2, w2, x1.reshape(lt * nb, d), modv,
                       n_ctx_tok_tiles, t0p)
        xs = x2.reshape(lt, nb, d)

    out = _final_norm(xs[l_ctx:].reshape(l_lat * nb, d), final_norm_g)
    return jnp.transpose(out.reshape(l_lat, nb, d), (1, 0, 2))
```

```python
import functools
import math

import jax
import jax.numpy as jnp
import numpy as np
from jax import lax
from jax.experimental import pallas as pl
from jax.experimental.pallas import tpu as pltpu

F32 = jnp.float32
BF16 = jnp.bfloat16

NORM_EPS = 1e-6
N_MOD = 6
HEAD_DIM = 64
N_HEADS = 4
GROUP_W = N_HEADS * HEAD_DIM
GRID_W = 64

RWKV_GATE_LORA = 64
RWKV_LORA = 32
RWKV_GN_EPS = 64e-5
RWKV_COLS = 3 * GROUP_W + RWKV_GATE_LORA + 4 * RWKV_LORA
ZA_W = 3 * GROUP_W + 3 * 128

RET_CHUNK = 128
RET_GN_EPS = 1e-5
RET_FWD_OFFSET = 5.0
RET_BWD_OFFSET = 5.5
ROPE_BASE = 10000.0
RET_COLS = 4 * GROUP_W

LRU_CONV_W = 4
LRU_C = 8.0
LRU_COLS = 2 * GROUP_W
LRU_CHUNK = 128

S5_CH = 16
S5_GROUPS = GROUP_W // S5_CH
S5_STATE = 64
S5_N = S5_GROUPS * S5_STATE
S5_CHUNK = 128

PEER_HEADS = 8
PEER_NKEYS = 128
PEER_DKEY = 128
PEER_TOPK = 16
PEER_TOK_TILE = 512
PEER_EXP_BLOCK = 2048

ROW_TILE_T = 32
RWKV_CHUNK = 64

VMEM_LIMIT_BYTES = 56 * 1024 * 1024

NEG_INF = float("-inf")


def _cparams(n_axes):
    return pltpu.CompilerParams(dimension_semantics=("arbitrary",) * n_axes,
                                vmem_limit_bytes=VMEM_LIMIT_BYTES)


def _sigmoid(x):
    return 1.0 / (1.0 + jnp.exp(-x))


def _gelu_tanh(x):
    return 0.5 * x * (1.0 + jnp.tanh(math.sqrt(2.0 / math.pi) * (x + 0.044715 * (x * x * x))))


def _softplus(x):
    return jnp.maximum(x, 0.0) + jnp.log(1.0 + jnp.exp(-jnp.abs(x)))


def _dot(a, b):
    return jnp.dot(a, b, preferred_element_type=F32)


def _dot_exact(a, b):
    return jnp.dot(a, b, preferred_element_type=F32, precision=lax.Precision.HIGHEST)


def _head_sum_matrix():
    idx = np.arange(GROUP_W) // HEAD_DIM
    return jnp.asarray((idx[:, None] == idx[None, :]).astype(np.float32))


def _mod_body(c_ref, w_ref, b_ref, o_ref):
    c = c_ref[...]
    s = (c * _sigmoid(c)).astype(BF16)
    o_ref[...] = _dot(s, w_ref[...].astype(BF16)) + b_ref[...]


def _modulation(cvec, w, b):
    rows, d = cvec.shape
    n = w.shape[1]
    tn = 512
    return pl.pallas_call(
        _mod_body,
        grid=(n // tn,),
        in_specs=[pl.BlockSpec((rows, d), lambda j: (0, 0)),
                  pl.BlockSpec((d, tn), lambda j: (0, j)),
                  pl.BlockSpec((1, tn), lambda j: (0, j))],
        out_specs=pl.BlockSpec((rows, tn), lambda j: (0, j)),
        out_shape=jax.ShapeDtypeStruct((rows, n), F32),
        compiler_params=_cparams(1),
        name="adaln_mod",
    )(cvec, w, b.reshape(1, n))


RET_QKV = 3 * GROUP_W


def _in_body(x_ref, g_ref, mod_ref, w_ref, za_ref, zqkv_ref, zg_ref, zc_ref, zd_ref):
    tt, nb, d = x_ref.shape
    x = x_ref[...]
    ms = jnp.mean(x * x, axis=-1, keepdims=True)
    y = x * lax.rsqrt(ms + NORM_EPS) * g_ref[...]
    h = y * (1.0 + mod_ref[1]) + mod_ref[0]
    hb = h.reshape(tt * nb, d).astype(BF16)
    c0 = 0
    for ref in (za_ref, zqkv_ref, zg_ref, zc_ref, zd_ref):
        w = ref.shape[-1]
        z = _dot(hb, w_ref[:, c0:c0 + w]).reshape(tt, nb, w)
        ref[...] = jnp.swapaxes(z, 0, 1) if ref is zqkv_ref else z
        c0 += w


def _in_proj(xs, g, modv, w_p, n_ctx_tiles):
    lt, nb, d = xs.shape
    tt = ROW_TILE_T
    widths = (ZA_W, RET_QKV, GROUP_W, LRU_COLS, GROUP_W)
    seg = lambda i: jnp.where(i >= n_ctx_tiles, 1, 0)
    tm = lambda w: (pl.BlockSpec((tt, nb, w), lambda i: (i, 0, 0)), jax.ShapeDtypeStruct((lt, nb, w), F32))
    bm = lambda w: (pl.BlockSpec((nb, tt, w), lambda i: (0, i, 0)), jax.ShapeDtypeStruct((nb, lt, w), F32))
    outs = [bm(w) if n == 1 else tm(w) for n, w in enumerate(widths)]
    return pl.pallas_call(
        _in_body,
        grid=(lt // tt,),
        in_specs=[pl.BlockSpec((tt, nb, d), lambda i: (i, 0, 0)),
                  pl.BlockSpec((1, d), lambda i: (0, 0)),
                  pl.BlockSpec((None, N_MOD, nb, d), lambda i: (seg(i), 0, 0, 0)),
                  pl.BlockSpec(w_p.shape, lambda i: (0, 0))],
        out_specs=[o[0] for o in outs],
        out_shape=[o[1] for o in outs],
        compiler_params=_cparams(1),
        name="in_proj",
    )(xs, g.reshape(1, d), modv, w_p)


def _rwkv_feat_body(n_ctx_tiles, n_tiles,
                    z_ref, zp_ref, zn_ref, mu_ref, gup_ref, wup_ref, aup_ref, vec_ref, hs_ref,
                    r_ref, k_ref, v_ref, g_ref, av_ref, w_ref, kd_ref, bv_ref):
    i = pl.program_id(0)
    tt, nb, zw = z_ref.shape
    z0 = z_ref[...]
    has_prev = jnp.logical_and(i != 0, i != n_ctx_tiles)
    has_next = jnp.logical_and(i != n_ctx_tiles - 1, i != n_tiles - 1)
    hp = jnp.where(has_prev, zp_ref[...], 0.0)
    hn = jnp.where(has_next, zn_ref[...], 0.0)
    zp = jnp.concatenate([hp, z0[:-1]], axis=0)
    zn = jnp.concatenate([z0[1:], hn], axis=0)
    z = z0 + mu_ref[0] * (zp - z0) + mu_ref[1] * (zn - z0)
    z = z.reshape(tt * nb, zw)
    gw = GROUP_W
    r = z[:, 0:gw]
    k = z[:, gw:2 * gw]
    v = z[:, 2 * gw:3 * gw]
    g = _dot(_sigmoid(z[:, 3 * gw:3 * gw + 128]).astype(BF16), gup_ref[...])
    w_low = jnp.tanh(z[:, 3 * gw + 128:3 * gw + 256]).astype(BF16)
    a_low = z[:, 3 * gw + 256:3 * gw + 384].astype(BF16)
    k_k = vec_ref[0:1, :]
    k_a = vec_ref[1:2, :]
    kk = k * k_k
    ss = _dot_exact(kk * kk, hs_ref[...])
    kk = kk / jnp.maximum(jnp.sqrt(ss), 1e-12)
    shp = (tt, nb, gw)
    r_ref[...] = r.reshape(shp)
    k_ref[...] = k.reshape(shp)
    v_ref[...] = v.reshape(shp)
    g_ref[...] = g.reshape(shp)
    av_ref[...] = (-kk).reshape(shp)
    for d in range(2):
        w0 = vec_ref[2 + d:3 + d, :]
        a0 = vec_ref[4 + d:5 + d, :]
        w_log = -_softplus(-(w0 + _dot(w_low, wup_ref[d]))) - 0.5
        w_ref[d] = jnp.exp(-jnp.exp(w_log)).reshape(shp)
        a = _sigmoid(a0 + _dot(a_low, aup_ref[d]))
        kd_ref[d] = (k * (1.0 + (a - 1.0) * k_a)).reshape(shp)
        bv_ref[d] = (kk * a).reshape(shp)


def _rwkv_features(za, mu_p, gup_p, wup_p, aup_p, vecs, hs, n_ctx_tiles):
    lt, nb, zw = za.shape
    tt = ROW_TILE_T
    n_tiles = lt // tt
    gw = GROUP_W
    one = lambda shape: jax.ShapeDtypeStruct(shape, F32)
    full = lambda a: pl.BlockSpec(a.shape, lambda i: (0,) * a.ndim)
    o1 = pl.BlockSpec((tt, nb, gw), lambda i: (i, 0, 0))
    o2 = pl.BlockSpec((2, tt, nb, gw), lambda i: (0, i, 0, 0))
    return pl.pallas_call(
        functools.partial(_rwkv_feat_body, n_ctx_tiles, n_tiles),
        grid=(n_tiles,),
        in_specs=[pl.BlockSpec((tt, nb, zw), lambda i: (i, 0, 0)),
                  pl.BlockSpec((1, nb, zw), lambda i: (jnp.maximum(i * tt - 1, 0), 0, 0)),
                  pl.BlockSpec((1, nb, zw), lambda i: (jnp.minimum((i + 1) * tt, lt - 1), 0, 0)),
                  full(mu_p), full(gup_p), full(wup_p), full(aup_p), full(vecs), full(hs)],
        out_specs=[o1, o1, o1, o1, o1, o2, o2, o2],
        out_shape=[one((lt, nb, gw))] * 5 + [one((2, lt, nb, gw))] * 3,
        compiler_params=_cparams(1),
        name="rwkv_features",
    )(za, za, za, mu_p, gup_p, wup_p, aup_p, vecs, hs)


def _rwkv_scan_body(rf_ref, af_ref, wf_ref, kdf_ref, bf_ref, vf_ref,
                    rb_ref, ab_ref, wb_ref, kdb_ref, bb_ref, vb_ref,
                    yf_ref, yb_ref, s_ref, vx_ref):
    j = pl.program_id(0)
    tc, n_khi, lanes = rf_ref.shape
    n_v, n_bh = vf_ref.shape[1:]
    k_lo = lanes // n_bh
    n_half = 2
    hv = n_v // n_half

    @pl.when(j == 0)
    def _():
        s_ref[...] = jnp.zeros_like(s_ref)

    for d, ref in enumerate((vf_ref, vb_ref)):
        x = ref[...]
        reps = 1
        while reps < k_lo:
            x = jnp.concatenate([x, x], axis=-1)
            reps *= 2
        vx_ref[d] = x

    def group_sum(x):
        parts = [x] + [pltpu.roll(x, g * n_bh, axis=1) for g in range(1, k_lo)]
        while len(parts) > 1:
            parts = [parts[n] + parts[n + 1] for n in range(0, len(parts) - 1, 2)] + parts[len(parts) & ~1:]
        return parts[0]

    dirs = ((rf_ref, af_ref, wf_ref, kdf_ref, bf_ref, yf_ref), (rb_ref, ab_ref, wb_ref, kdb_ref, bb_ref, yb_ref))

    segs = [(d, slice(half * hv, (half + 1) * hv)) for d in range(2) for half in range(n_half)]

    def step(i, carry):
        t = (i, tc - 1 - i)
        row = lambda ref, d, kh: ref[t[d], pl.ds(kh, 1), :]
        ip = jnp.maximum(i - 1, 0)
        tp = (ip, tc - 1 - ip)
        for n, (d, rows) in enumerate(segs):
            dirs[d][5][tp[d], rows, :] = group_sum(carry[n])[:, :n_bh]
        acc = [None] * len(segs)
        for kh in range(n_khi):
            a_row = [row(dirs[d][1], d, kh) for d in range(2)]
            for n, (d, rows) in enumerate(segs):
                term = s_ref[d, kh, rows, :] * a_row[d]
                acc[n] = term if acc[n] is None else acc[n] + term
        sa = [group_sum(a) for a in acc]
        for kh in range(n_khi):
            w_row, kd_row = ([row(dirs[d][c], d, kh) for d in range(2)] for c in (2, 3))
            for n, (d, rows) in enumerate(segs):
                s_ref[d, kh, rows, :] = s_ref[d, kh, rows, :] * w_row[d] + vx_ref[d, t[d], rows, :] * kd_row[d]
        acc = [None] * len(segs)
        for kh in range(n_khi):
            b_row, r_row = ([row(dirs[d][c], d, kh) for d in range(2)] for c in (4, 0))
            for n, (d, rows) in enumerate(segs):
                s_new = s_ref[d, kh, rows, :] + sa[n] * b_row[d]
                s_ref[d, kh, rows, :] = s_new
                term = s_new * r_row[d]
                acc[n] = term if acc[n] is None else acc[n] + term
        return tuple(acc)

    zero = jnp.zeros((hv, lanes), F32)
    last = lax.fori_loop(0, tc, step, (zero,) * len(segs))
    t_last = (tc - 1, 0)
    for n, (d, rows) in enumerate(segs):
        dirs[d][5][t_last[d], rows, :] = group_sum(last[n])[:, :n_bh]


def _dir_chunk(d, j, n_ctx, n_all):
    bwd = jnp.where(j < n_ctx, n_ctx - 1 - j, n_all - 1 - (j - n_ctx))
    return jnp.where(d == 0, j, bwd)


def _rwkv_scan(r_e, a_e, w_e, kd_e, b_e, v_e, l_ctx):
    lt, n_khi, lanes = r_e.shape
    _, n_v, n_bh = v_e.shape
    tc = RWKV_CHUNK
    n_all, n_ctx = lt // tc, l_ctx // tc
    specs = []
    for d in range(2):
        cm = functools.partial(_dir_chunk, d, n_ctx=n_ctx, n_all=n_all)
        shared = pl.BlockSpec((tc, n_khi, lanes), lambda j, cm=cm: (cm(j), 0, 0))
        per_dir = pl.BlockSpec((None, tc, n_khi, lanes), lambda j, cm=cm, d=d: (d, cm(j), 0, 0))
        tok = pl.BlockSpec((tc, n_v, n_bh), lambda j, cm=cm: (cm(j), 0, 0))
        specs.append(([shared, shared, per_dir, per_dir, per_dir, tok], tok))
    return pl.pallas_call(
        _rwkv_scan_body,
        grid=(n_all,),
        in_specs=specs[0][0] + specs[1][0],
        out_specs=[specs[0][1], specs[1][1]],
        out_shape=[jax.ShapeDtypeStruct((lt, n_v, n_bh), F32)] * 2,
        scratch_shapes=[pltpu.VMEM((2, n_khi, n_v, lanes), F32), pltpu.VMEM((2, tc, n_v, lanes), F32)],
        compiler_params=_cparams(1),
        name="rwkv_scan",
    )(r_e, a_e, w_e, kd_e, b_e, v_e, r_e, a_e, w_e, kd_e, b_e, v_e)


def _rwkv_expand_k(x, k_lo):
    lead = x.shape[:-2]
    nb = x.shape[-2]
    nd = len(lead)
    x = x.reshape(lead + (nb, N_HEADS, k_lo, HEAD_DIM // k_lo))
    x = jnp.transpose(x, tuple(range(nd)) + (nd + 3, nd + 2, nd, nd + 1))
    return x.reshape(lead + (HEAD_DIM // k_lo, k_lo * nb * N_HEADS))


def _rwkv_expand_v(x):
    lt, nb, _ = x.shape
    return jnp.transpose(x.reshape(lt, nb, N_HEADS, HEAD_DIM), (0, 3, 1, 2)).reshape(lt, HEAD_DIM, nb * N_HEADS)


def _rwkv_collapse_v(y, nb):
    lt = y.shape[0]
    return jnp.transpose(y.reshape(lt, HEAD_DIM, nb, N_HEADS), (0, 2, 3, 1)).reshape(lt, nb, GROUP_W)


def _ret_body(nb, z_ref, cos_ref, sin_ref, dec_ref, rd_ref, wd_ref, cd_ref, hm_ref, o_ref, s_ref, ob_ref):
    j = pl.program_id(1)
    gw = GROUP_W
    half = gw // 2

    @pl.when(j == 0)
    def _():
        s_ref[...] = jnp.zeros_like(s_ref)

    cos = cos_ref[...]
    sin = sin_ref[...]

    def rope(x):
        x1, x2 = x[:, :half], x[:, half:]
        return jnp.concatenate([x1 * cos - x2 * sin, x1 * sin + x2 * cos], axis=-1)

    def one_batch(b, carry):
        q = rope(z_ref[b, :, 0:gw]) * (HEAD_DIM ** -0.5)
        k = rope(z_ref[b, :, gw:2 * gw])
        v = z_ref[b, :, 2 * gw:3 * gw]
        vb = v.astype(BF16)
        k_t = k.T
        k_tb = k_t.astype(BF16)
        state = s_ref[b]
        inter = _dot(q.astype(BF16), state.astype(BF16)) * rd_ref[...]
        intra = jnp.zeros_like(inter)
        for h in range(N_HEADS):
            qm = (q * hm_ref[h:h + 1, :]).astype(BF16)
            att = _dot(qm, k_tb) * dec_ref[h]
            vm = (v * hm_ref[N_HEADS + h:N_HEADS + h + 1, :]).astype(BF16)
            intra = intra + _dot(att.astype(BF16), vm)
        ob_ref[b] = intra + inter
        kv = _dot((k_t * wd_ref[...]).astype(BF16), vb)
        s_ref[b] = state * cd_ref[0] + kv * cd_ref[1]
        return carry

    lax.fori_loop(0, nb, one_batch, 0)
    o_ref[...] = jnp.swapaxes(ob_ref[...], 0, 1)


def _retention(zb, tabs, l_ctx):
    nb, lt, _ = zb.shape
    c = RET_CHUNK
    n_all, n_ctx = lt // c, l_ctx // c
    gw = GROUP_W
    cos, sin, dec, rd, wd, cd, hm = tabs
    cm = lambda d, j: _dir_chunk(d, j, n_ctx, n_all)
    return pl.pallas_call(
        functools.partial(_ret_body, nb),
        grid=(2, n_all),
        in_specs=[pl.BlockSpec((nb, c, RET_QKV), lambda d, j: (0, cm(d, j), 0)),
                  pl.BlockSpec((c, gw // 2), lambda d, j: (cm(d, j), 0)),
                  pl.BlockSpec((c, gw // 2), lambda d, j: (cm(d, j), 0)),
                  pl.BlockSpec((None, N_HEADS, c, c), lambda d, j: (d, 0, 0, 0)),
                  pl.BlockSpec((None, c, gw), lambda d, j: (d, 0, 0)),
                  pl.BlockSpec((None, gw, c), lambda d, j: (d, 0, 0)),
                  pl.BlockSpec((None, 2, gw, gw), lambda d, j: (d, 0, 0, 0)),
                  pl.BlockSpec((2 * N_HEADS, gw), lambda d, j: (0, 0))],
        out_specs=pl.BlockSpec((None, c, nb, gw), lambda d, j: (d, cm(d, j), 0, 0)),
        out_shape=jax.ShapeDtypeStruct((2, lt, nb, gw), F32),
        scratch_shapes=[pltpu.VMEM((nb, gw, gw), F32), pltpu.VMEM((nb, c, gw), F32)],
        compiler_params=_cparams(2),
        name="retention",
    )(zb, cos, sin, dec, rd, wd, cd, hm)


def _ret_perm():
    new = np.arange(GROUP_W)
    half_id, rem = new // 128, new % 128
    h, i = rem // 32, rem % 32
    return h * HEAD_DIM + half_id * 32 + i


def _ret_tables(lt):
    c = RET_CHUNK
    half = HEAD_DIM // 2
    freqs = ROPE_BASE ** (-jnp.arange(half, dtype=F32) / half)
    ang = jnp.arange(lt, dtype=F32)[:, None] * freqs[None]
    cos = jnp.tile(jnp.cos(ang), (1, N_HEADS))
    sin = jnp.tile(jnp.sin(ang), (1, N_HEADS))
    idx = jnp.arange(c, dtype=F32)
    perm = _ret_perm()
    head_of_qk = jnp.asarray(perm // HEAD_DIM)
    head_of_v = jnp.arange(GROUP_W) // HEAD_DIM
    dec, rd, wd, cd = [], [], [], []
    for offset, rev in ((RET_FWD_OFFSET, False), (RET_BWD_OFFSET, True)):
        lg = jnp.log1p(-jnp.exp2(-(offset + jnp.arange(N_HEADS, dtype=F32))))
        diff = idx[:, None] - idx[None, :]
        if rev:
            mask, dist = diff < 0, -diff
            read_pow, write_pow = c - idx, idx
        else:
            mask, dist = diff >= 0, diff
            read_pow, write_pow = idx + 1.0, c - 1.0 - idx
        dec.append(jnp.where(mask[None], jnp.exp(lg[:, None, None] * jnp.where(mask, dist, 0.0)[None]), 0.0))
        rd.append(jnp.exp(lg[head_of_v][None, :] * read_pow[:, None]))
        wd.append(jnp.exp(lg[head_of_qk][:, None] * write_pow[None, :]))
        same = (head_of_qk[:, None] == head_of_v[None, :]).astype(F32)
        cd.append(jnp.stack([same * jnp.exp(lg * c)[head_of_v][None, :], same]))
    hm_q = (head_of_qk[None, :] == jnp.arange(N_HEADS)[:, None]).astype(F32)
    hm_v = (head_of_v[None, :] == jnp.arange(N_HEADS)[:, None]).astype(F32)
    return cos, sin, jnp.stack(dec), jnp.stack(rd), jnp.stack(wd), jnp.stack(cd), jnp.concatenate([hm_q, hm_v])


def _lru_body(n_ctx, n_all, x_ref, xp_ref, xn_ref, cw_ref, wa_ref, wx_ref, vec_ref, o_ref,
              a_s, b_s, h_s):
    d = pl.program_id(0)
    j = pl.program_id(1)
    tc, nb, gw = x_ref.shape
    ch = _dir_chunk(d, j, n_ctx, n_all)

    @pl.when(j == 0)
    def _():
        h_s[...] = jnp.zeros_like(h_s)

    has_prev = jnp.logical_and(ch != 0, ch != n_ctx)
    has_next = jnp.logical_and(ch != n_ctx - 1, ch != n_all - 1)
    xe = jnp.concatenate([jnp.where(has_prev, xp_ref[...], 0.0), x_ref[...],
                          jnp.where(has_next, xn_ref[...], 0.0)], axis=0)
    xc = cw_ref[LRU_CONV_W:LRU_CONV_W + 1, :]
    for tap in range(LRU_CONV_W):
        xc = xc + cw_ref[tap:tap + 1, :] * xe[tap:tap + tc]
    xc = xc.reshape(tc * nb, gw)
    xb = xc.astype(BF16)
    r = _sigmoid(_dot(xb, wa_ref[...]) + vec_ref[0:1, :])
    gi = _sigmoid(_dot(xb, wx_ref[...]) + vec_ref[1:2, :])
    log_a = -LRU_C * r * _softplus(-vec_ref[2:3, :])
    th = jnp.tanh(log_a)
    one_minus_a2 = 2.0 * th / (th - 1.0)
    a_s[...] = jnp.exp(log_a).reshape(tc, nb, gw)
    b_s[...] = (jnp.sqrt(one_minus_a2) * (gi * xc)).reshape(tc, nb, gw)

    def step(i, h):
        t = jnp.where(d == 0, i, tc - 1 - i)
        h = a_s[t] * h + b_s[t]
        o_ref[t] = h
        return h

    h_s[...] = lax.fori_loop(0, tc, step, h_s[...])


def _lru(zc, cw, wa_bd, wx_bd, vecs, l_ctx):
    lt, nb, _ = zc.shape
    gw = GROUP_W
    tc = LRU_CHUNK
    n_all, n_ctx = lt // tc, l_ctx // tc
    cm = lambda d, j: _dir_chunk(d, j, n_ctx, n_all)
    return pl.pallas_call(
        functools.partial(_lru_body, n_ctx, n_all),
        grid=(2, n_all),
        in_specs=[pl.BlockSpec((tc, nb, gw), lambda d, j: (cm(d, j), 0, 0)),
                  pl.BlockSpec((2, nb, gw), lambda d, j: (jnp.maximum(cm(d, j) * (tc // 2) - 1, 0), 0, 0)),
                  pl.BlockSpec((1, nb, gw), lambda d, j: (jnp.minimum((cm(d, j) + 1) * tc, lt - 1), 0, 0)),
                  pl.BlockSpec(cw.shape, lambda d, j: (0, 0)),
                  pl.BlockSpec((None, gw, gw), lambda d, j: (d, 0, 0)),
                  pl.BlockSpec((None, gw, gw), lambda d, j: (d, 0, 0)),
                  pl.BlockSpec((None, 3, gw), lambda d, j: (d, 0, 0))],
        out_specs=pl.BlockSpec((None, tc, nb, gw), lambda d, j: (d, cm(d, j), 0, 0)),
        out_shape=jax.ShapeDtypeStruct((2, lt, nb, gw), F32),
        scratch_shapes=[pltpu.VMEM((tc, nb, gw), F32), pltpu.VMEM((tc, nb, gw), F32),
                        pltpu.VMEM((nb, gw), F32)],
        compiler_params=_cparams(2),
        name="rglru",
    )(zc, zc, zc, cw, wa_bd, wx_bd, vecs)


def _block_diag(w):
    nblk, n = w.shape[-3], w.shape[-1]
    eye = jnp.eye(nblk, dtype=w.dtype)
    out = w[..., :, :, None, :] * eye[:, None, :, None]
    return out.reshape(w.shape[:-3] + (nblk * n, nblk * n))


def _s5_body(u_ref, bm_ref, lam_ref, cm_ref, y_ref, h_s, st_s):
    d = pl.program_id(0)
    j = pl.program_id(1)
    tc, nb, gw = u_ref.shape
    n = S5_N

    @pl.when(j == 0)
    def _():
        st_s[...] = jnp.zeros_like(st_s)

    u = u_ref[...].reshape(tc * nb, gw).astype(BF16)
    h_s[...] = _dot(u, bm_ref[...]).reshape(tc, nb, 2 * n)
    lr = lam_ref[0:1, :]
    li = lam_ref[1:2, :]

    def step(i, carry):
        hr, hi = carry
        t = jnp.where(d == 0, i, tc - 1 - i)
        bu = h_s[t]
        nr = lr * hr - li * hi + bu[:, :n]
        ni = lr * hi + li * hr + bu[:, n:]
        h_s[t] = jnp.concatenate([nr, ni], axis=-1)
        return nr, ni

    hr, hi = lax.fori_loop(0, tc, step, (st_s[0], st_s[1]), unroll=4)
    st_s[0] = hr
    st_s[1] = hi
    hh = h_s[...].reshape(tc * nb, 2 * n).astype(BF16)
    y_ref[...] = _dot(hh, cm_ref[...]).reshape(tc, nb, gw)


def _s5(u, bmat, lam, cmat, l_ctx):
    lt, nb, gw = u.shape
    tc = S5_CHUNK
    n_all, n_ctx = lt // tc, l_ctx // tc
    cm = lambda d, j: _dir_chunk(d, j, n_ctx, n_all)
    return pl.pallas_call(
        _s5_body,
        grid=(2, n_all),
        in_specs=[pl.BlockSpec((tc, nb, gw), lambda d, j: (cm(d, j), 0, 0)),
                  pl.BlockSpec((None, gw, 2 * S5_N), lambda d, j: (d, 0, 0)),
                  pl.BlockSpec((None, 2, S5_N), lambda d, j: (d, 0, 0)),
                  pl.BlockSpec((None, 2 * S5_N, gw), lambda d, j: (d, 0, 0))],
        out_specs=pl.BlockSpec((None, tc, nb, gw), lambda d, j: (d, cm(d, j), 0, 0)),
        out_shape=jax.ShapeDtypeStruct((2, lt, nb, gw), F32),
        scratch_shapes=[pltpu.VMEM((tc, nb, 2 * S5_N), F32), pltpu.VMEM((2, nb, S5_N), F32)],
        compiler_params=_cparams(2),
        name="s5",
    )(u, bmat, lam, cmat)


def _s5_params(a_re, a_im, log_dt, b_re, b_im, c_re, c_im):
    dt = jnp.exp(log_dt)[..., None]
    er = jnp.exp(a_re * dt)
    lbr, lbi = er * jnp.cos(a_im * dt), er * jnp.sin(a_im * dt)
    den = a_re * a_re + a_im * a_im
    nr, ni = lbr - 1.0, lbi
    fr = (nr * a_re + ni * a_im) / den
    fi = (ni * a_re - nr * a_im) / den
    bbr = fr[..., None] * b_re - fi[..., None] * b_im
    bbi = fr[..., None] * b_im + fi[..., None] * b_re
    eye = jnp.eye(S5_GROUPS, dtype=F32)

    def in_map(bb):
        m = jnp.einsum('dgpc,gh->dgchp', bb, eye)
        return m.reshape(2, GROUP_W, S5_N)

    def out_map(cc):
        m = jnp.einsum('dgcp,gh->dgphc', cc, eye)
        return m.reshape(2, S5_N, GROUP_W)

    bmat = jnp.concatenate([in_map(bbr), in_map(bbi)], axis=-1).astype(BF16)
    cmat = jnp.concatenate([out_map(c_re), -out_map(c_im)], axis=-2).astype(BF16)
    lam = jnp.stack([lbr.reshape(2, S5_N), lbi.reshape(2, S5_N)], axis=1)
    return bmat, lam, cmat


def _to_colmajor(z, l_ctx):
    lt, nb, ch = z.shape
    rows = (lt - l_ctx) // GRID_W
    lat = z[l_ctx:].reshape(rows, GRID_W, nb, ch).transpose(1, 0, 2, 3).reshape(lt - l_ctx, nb, ch)
    return jnp.concatenate([z[:l_ctx], lat], axis=0)


def _from_colmajor(y, l_ctx):
    lt = y.shape[-3]
    nb, ch = y.shape[-2:]
    rows = (lt - l_ctx) // GRID_W
    lead = y.shape[:-3]
    lat = y[..., l_ctx:, :, :].reshape(lead + (GRID_W, rows, nb, ch))
    lat = jnp.swapaxes(lat, -4, -3).reshape(lead + (lt - l_ctx, nb, ch))
    return jnp.concatenate([y[..., :l_ctx, :, :], lat], axis=-3)


def _out_body(x_ref, mod_ref, ya_ref, r_ref, k_ref, v_ref, g_ref, ob_ref, gb_ref, hc_ref, gc_ref,
              yd_ref, ud_ref, vec_ref, hs_ref, gluw_ref, glub_ref, wo_ref, n2_ref, x1_ref, h2_ref):
    tt, nb, d = x_ref.shape
    gw = GROUP_W
    rows = tt * nb
    hs = hs_ref[...]
    two = lambda ref: (ref[0] + ref[1]).reshape(rows, gw)
    flat = lambda ref: ref[...].reshape(rows, gw)

    def head_norm(y, eps):
        mu = _dot_exact(y, hs) * (1.0 / HEAD_DIM)
        dlt = y - mu
        var = _dot_exact(dlt * dlt, hs) * (1.0 / HEAD_DIM)
        return dlt * lax.rsqrt(var + eps)

    r_k, ln_g, ln_b = vec_ref[0:1, :], vec_ref[1:2, :], vec_ref[2:3, :]
    r, k, v = flat(r_ref), flat(k_ref), flat(v_ref)
    yn = head_norm(two(ya_ref), RWKV_GN_EPS) * ln_g + ln_b
    bonus = _dot_exact(r * k * r_k, hs) * v
    mix_a = (yn + bonus) * flat(g_ref)
    gb = flat(gb_ref)
    mix_b = gb * _sigmoid(gb) * (head_norm(two(ob_ref), RET_GN_EPS) * vec_ref[3:4, :])
    mix_c = _gelu_tanh(flat(gc_ref)) * two(hc_ref)
    yd = vec_ref[4:5, :] * flat(ud_ref) + two(yd_ref)
    glu = _dot(_gelu_tanh(yd).astype(BF16), gluw_ref[...]) + glub_ref[...]
    mix_d = glu[:, :gw] * _sigmoid(glu[:, gw:])

    mix = _dot(mix_a.astype(BF16), wo_ref[0:gw, :])
    mix = mix + _dot(mix_b.astype(BF16), wo_ref[gw:2 * gw, :])
    mix = mix + _dot(mix_c.astype(BF16), wo_ref[2 * gw:3 * gw, :])
    mix = mix + _dot(mix_d.astype(BF16), wo_ref[3 * gw:4 * gw, :])
    x1 = x_ref[...] + mod_ref[2] * mix.reshape(tt, nb, d)
    x1_ref[...] = x1
    ms = jnp.mean(x1 * x1, axis=-1, keepdims=True)
    h2 = (x1 * lax.rsqrt(ms + NORM_EPS) * n2_ref[...]) * (1.0 + mod_ref[4]) + mod_ref[3]
    h2_ref[...] = h2.reshape(rows, d).T.astype(BF16)


def _out_proj(xs, modv, ya, feats, ob, zb, hc, zc, yd, zd, vecs, hs, gluw, glub, wo, n2, n_ctx_tiles, t0_tile):
    lt, nb, d = xs.shape
    tt = ROW_TILE_T
    gw = GROUP_W
    n_tiles = lt // tt - t0_tile
    seg = lambda i: jnp.where(i + t0_tile >= n_ctx_tiles, 1, 0)
    col = lambda c: pl.BlockSpec((tt, nb, gw), lambda i: (i + t0_tile, 0, c))
    two = pl.BlockSpec((2, tt, nb, gw), lambda i: (0, i + t0_tile, 0, 0))
    full = lambda a: pl.BlockSpec(a.shape, lambda i: (0,) * a.ndim)
    r, k, v, g = feats
    return pl.pallas_call(
        _out_body,
        grid=(n_tiles,),
        in_specs=[pl.BlockSpec((tt, nb, d), lambda i: (i + t0_tile, 0, 0)),
                  pl.BlockSpec((None, N_MOD, nb, d), lambda i: (seg(i), 0, 0, 0)),
                  two, col(0), col(0), col(0), col(0),
                  two, col(0), two, col(1), two, col(0),
                  full(vecs), full(hs), full(gluw), full(glub), full(wo), full(n2)],
        out_specs=[pl.BlockSpec((tt, nb, d), lambda i: (i + t0_tile, 0, 0)),
                   pl.BlockSpec((d, tt * nb), lambda i: (0, i + t0_tile))],
        out_shape=[jax.ShapeDtypeStruct((lt, nb, d), F32), jax.ShapeDtypeStruct((d, lt * nb), BF16)],
        compiler_params=_cparams(1),
        name="out_proj",
    )(xs, modv, ya, r, k, v, g, ob, zb, hc, zc, yd, zd, vecs, hs, gluw, glub, wo, n2)


PEER_NO_RANK = 127.0


def _top_values(s, n, with_rank=False):
    vals, cur = [], s
    rank = jnp.full(s.shape, PEER_NO_RANK, F32) if with_rank else None
    for k in range(n):
        m = jnp.max(cur, axis=0, keepdims=True)
        vals.append(m)
        hit = cur == m
        if with_rank:
            rank = jnp.where(hit, float(k), rank)
        cur = jnp.where(hit, NEG_INF, cur)
    return (vals, rank) if with_rank else vals


def _peer_route_body(h_ref, wq_ref, keys_ref, e1_ref, cnt_ref, r2_ref, w2_ref, pack_ref):
    hb = h_ref[...]
    for h in range(PEER_HEADS):
        st = []
        for p in range(2):
            c0 = (2 * h + p) * PEER_DKEY
            q = _dot(wq_ref[c0:c0 + PEER_DKEY, :], hb).astype(BF16)
            st.append(_dot(keys_ref[2 * h + p], q))
        s1, s2 = st
        n_top = PEER_TOPK + 1
        v1 = _top_values(s1, n_top)
        v2, rank2 = _top_values(s2, n_top, with_rank=True)
        pad7 = [jnp.full((7, s1.shape[1]), NEG_INF, F32)]
        v2m = jnp.concatenate(v2 + pad7, axis=0)
        v1_tail = jnp.concatenate(v1[8:] + pad7, axis=0)
        cand = jnp.concatenate([v1[0] + v2m] + [v1[a] + v2m[0:8] for a in range(1, 8)]
                               + [v1_tail + v2[0]], axis=0)
        top = _top_values(cand, n_top)
        tau = 0.5 * (top[PEER_TOPK - 1] + top[PEER_TOPK])
        cmax = top[0]
        z = jnp.sum(jnp.where(cand >= tau, jnp.exp(cand - cmax), 0.0), axis=0, keepdims=True)
        thr = tau - s1
        cnt = jnp.zeros_like(s1)
        for k in range(PEER_TOPK):
            cnt = cnt + jnp.where(v2[k] >= thr, 1.0, 0.0)
        e1_ref[:, h, :] = jnp.exp(s1 - v1[0])
        cnt_ref[:, h, :] = cnt
        pack_ref[0] = rank2.astype(BF16)
        pack_ref[1] = (jnp.exp(s2 - v2[0]) / z).astype(BF16)
        words = pack_ref.bitcast(jnp.uint32)
        r2_ref[h] = words[0]
        w2_ref[h] = words[1]


def _peer_route(h2t, wq, keys, t0_tile):
    d, t_all = h2t.shape
    tt = PEER_TOK_TILE
    nk = PEER_NKEYS
    by_i = pl.BlockSpec((nk, PEER_HEADS, tt), lambda i: (0, 0, i + t0_tile))
    by_h = pl.BlockSpec((PEER_HEADS, nk // 2, tt), lambda i: (0, 0, i + t0_tile))
    return pl.pallas_call(
        _peer_route_body,
        grid=(t_all // tt - t0_tile,),
        in_specs=[pl.BlockSpec((d, tt), lambda i: (0, i + t0_tile)),
                  pl.BlockSpec(wq.shape, lambda i: (0, 0)),
                  pl.BlockSpec(keys.shape, lambda i: (0, 0, 0))],
        out_specs=[by_i, by_i, by_h, by_h],
        out_shape=[jax.ShapeDtypeStruct((nk, PEER_HEADS, t_all), F32)] * 2
                  + [jax.ShapeDtypeStruct((PEER_HEADS, nk // 2, t_all), jnp.uint32)] * 2,
        scratch_shapes=[pltpu.VMEM((2, nk, tt), BF16)],
        compiler_params=_cparams(1),
        name="peer_route",
    )(h2t, wq, keys)


PEER_GATE_ROWS = 32
LANES = 128


BF16_ROWS = 16
SUBLANES = 8


def _peer_gate_block(p_ref, e1_ref, cnt_ref, r2_ref, w2_ref, bc_ref):
    nk = PEER_NKEYS
    n_sub = p_ref.shape[0] // nk
    rc = PEER_GATE_ROWS
    n_pv = rc // BF16_ROWS
    for c0 in range(0, p_ref.shape[1], LANES):
        cols = slice(c0, c0 + LANES)
        for ii in range(n_sub):
            for h in range(PEER_HEADS):
                bc_ref[0, ii, h] = jnp.broadcast_to(cnt_ref[ii, h:h + 1, cols], (BF16_ROWS, LANES)).astype(BF16)
                bc_ref[1, ii, h] = jnp.broadcast_to(e1_ref[ii, h:h + 1, cols], (BF16_ROWS, LANES)).astype(BF16)

        def chunk(q, carry, cols=cols):
            r0 = pl.multiple_of(q * rc, rc)
            gate = [[None] * n_pv for _ in range(n_sub)]
            for h in range(PEER_HEADS):
                rank = [pltpu.bitcast(r2_ref[h, q, SUBLANES * v:SUBLANES * (v + 1), cols], BF16)
                        for v in range(n_pv)]
                w2 = [pltpu.bitcast(w2_ref[h, q, SUBLANES * v:SUBLANES * (v + 1), cols], BF16)
                      for v in range(n_pv)]
                for ii in range(n_sub):
                    cnt = bc_ref[0, ii, h]
                    e1 = bc_ref[1, ii, h]
                    for v in range(n_pv):
                        term = jnp.where(rank[v] < cnt, w2[v] * e1, jnp.zeros_like(e1))
                        gate[ii][v] = term if gate[ii][v] is None else gate[ii][v] + term
            for ii in range(n_sub):
                for v in range(n_pv):
                    rows = pl.ds(ii * nk + r0 + BF16_ROWS * v, BF16_ROWS)
                    p_ref[rows, cols] = p_ref[rows, cols] * gate[ii][v]
            return carry

        lax.fori_loop(0, nk // rc, chunk, 0)


def _peer_ffn_body(h_ref, u_ref, vt_ref, e1_ref, cnt_ref, r2_ref, w2_ref, x_ref, mod_ref, o_ref,
                   acc_ref, p_ref, bc_ref, ht_ref):
    j = pl.program_id(1)

    @pl.when(j == 0)
    def _():
        acc_ref[...] = jnp.zeros_like(acc_ref)
        ht_ref[...] = h_ref[...]

    act = _dot(u_ref[...], ht_ref[...])
    p_ref[...] = _gelu_tanh(act.astype(BF16))
    _peer_gate_block(p_ref, e1_ref, cnt_ref, r2_ref, w2_ref, bc_ref)
    acc_ref[...] += _dot(vt_ref[...], p_ref[...])

    @pl.when(j == pl.num_programs(1) - 1)
    def _():
        tt, d = x_ref.shape
        nb = mod_ref.shape[1]
        y = acc_ref[...].T.reshape(tt // nb, nb, d) * mod_ref[5]
        o_ref[...] = x_ref[...] + y.reshape(tt, d)


def _peer_ffn(h2t, u_b, vt_b, layer, e1, cnt, r2, w2, x1, modv, n_ctx_tok_tiles, t0_tile):
    d, t_all = h2t.shape
    tt = PEER_TOK_TILE
    eb = PEER_EXP_BLOCK
    nk = PEER_NKEYS
    nb = modv.shape[2]
    n_e = u_b.shape[1] // eb
    n_tiles = t_all // tt - t0_tile
    seg = lambda i: jnp.where(i + t0_tile >= n_ctx_tok_tiles, 1, 0)
    by_i = pl.BlockSpec((eb // nk, PEER_HEADS, tt), lambda i, j: (j, 0, i + t0_tile))
    rc = PEER_GATE_ROWS
    by_h = pl.BlockSpec((PEER_HEADS, nk // rc, rc // 2, tt), lambda i, j: (0, 0, 0, i + t0_tile))
    r2, w2 = (a.reshape(PEER_HEADS, nk // rc, rc // 2, t_all) for a in (r2, w2))
    return pl.pallas_call(
        _peer_ffn_body,
        grid=(n_tiles, n_e),
        in_specs=[pl.BlockSpec((d, tt), lambda i, j: (0, i + t0_tile)),
                  pl.BlockSpec((None, eb, d), lambda i, j: (layer, j, 0)),
                  pl.BlockSpec((None, d, eb), lambda i, j: (layer, 0, j)),
                  by_i, by_i, by_h, by_h,
                  pl.BlockSpec((tt, d), lambda i, j: (i + t0_tile, 0)),
                  pl.BlockSpec((None, N_MOD, nb, d), lambda i, j: (seg(i), 0, 0, 0))],
        out_specs=pl.BlockSpec((tt, d), lambda i, j: (i + t0_tile, 0)),
        out_shape=jax.ShapeDtypeStruct((t_all, d), F32),
        scratch_shapes=[pltpu.VMEM((d, tt), F32), pltpu.VMEM((eb, tt), BF16),
                        pltpu.VMEM((2, eb // nk, PEER_HEADS, BF16_ROWS, LANES), BF16),
                        pltpu.VMEM((d, tt), BF16)],
        compiler_params=_cparams(2),
        name="peer_ffn",
    )(h2t, u_b, vt_b, e1, cnt, r2, w2, x1, modv)


def _final_body(x_ref, g_ref, o_ref):
    x = x_ref[...]
    ms = jnp.mean(x * x, axis=-1, keepdims=True)
    o_ref[...] = jnp.swapaxes(x * lax.rsqrt(ms + NORM_EPS) * g_ref[...], 0, 1)


def _final_norm(xs, g, l_ctx):
    lt, nb, d = xs.shape
    tt = ROW_TILE_T
    t0 = l_ctx // tt
    return pl.pallas_call(
        _final_body,
        grid=((lt - l_ctx) // tt,),
        in_specs=[pl.BlockSpec((tt, nb, d), lambda i: (i + t0, 0, 0)), pl.BlockSpec((1, d), lambda i: (0, 0))],
        out_specs=pl.BlockSpec((nb, tt, d), lambda i: (0, i, 0)),
        out_shape=jax.ShapeDtypeStruct((nb, lt - l_ctx, d), F32),
        compiler_params=_cparams(1),
        name="final_norm",
    )(xs, g.reshape(1, d))


def _pad_cols(w, width):
    return jnp.pad(w, [(0, 0)] * (w.ndim - 1) + [(0, width - w.shape[-1])])


def _rwkv_slabs(w):
    gw3 = 3 * GROUP_W
    parts = [w[..., :gw3],
             _pad_cols(w[..., gw3:gw3 + 64], 128),
             _pad_cols(w[..., gw3 + 64:gw3 + 128], 128),
             _pad_cols(w[..., gw3 + 128:gw3 + 192], 128)]
    return jnp.concatenate(parts, axis=-1)


def _layer_weights(l, p):
    d = p['w_in'].shape[1]
    w_in = p['w_in'][l]
    wa_, wb_, wc_, wd_ = jnp.split(w_in, [RWKV_COLS, RWKV_COLS + RET_COLS, RWKV_COLS + RET_COLS + LRU_COLS], axis=-1)
    perm = _ret_perm()
    gw = GROUP_W
    wb_ = jnp.concatenate([wb_[:, 0:gw][:, perm], wb_[:, gw:2 * gw][:, perm], wb_[:, 2 * gw:]], axis=-1)
    out = {'w_in': jnp.concatenate([_rwkv_slabs(wa_), wb_, wc_, wd_], axis=-1).astype(BF16)}
    out['mu'] = _rwkv_slabs(p['rwkv_mu'][l])[:, None, :]
    out['gup'] = jnp.pad(p['rwkv_g_up'][l], ((0, 64), (0, 0))).astype(BF16)
    lora = RWKV_LORA
    out['wup'] = jnp.stack([jnp.pad(p['rwkv_w_up'][l, dd], ((dd * lora, 128 - (dd + 1) * lora), (0, 0)))
                            for dd in range(2)]).astype(BF16)
    out['aup'] = jnp.stack([jnp.pad(p['rwkv_a_up'][l, dd], ((dd * lora, 128 - (dd + 1) * lora), (0, 0)))
                            for dd in range(2)]).astype(BF16)
    out['rwkv_vecs'] = jnp.concatenate([p['rwkv_k_k'][l][None], p['rwkv_k_a'][l][None],
                                        p['rwkv_w0'][l], p['rwkv_a0'][l]], axis=0)
    out['cw'] = jnp.concatenate([p['lru_conv_w'][l], p['lru_conv_b'][l][None]], axis=0)
    out['wa'] = _block_diag(p['lru_wa'][l]).astype(BF16)
    out['wx'] = _block_diag(p['lru_wx'][l]).astype(BF16)
    out['lru_vecs'] = jnp.stack([p['lru_ba'][l], p['lru_bx'][l], p['lru_lambda'][l]], axis=1)
    out['s5'] = _s5_params(p['s5_a_re'][l], p['s5_a_im'][l], p['s5_log_dt'][l], p['s5_b_re'][l],
                           p['s5_b_im'][l], p['s5_c_re'][l], p['s5_c_im'][l])
    out['out_vecs'] = jnp.stack([p['rwkv_r_k'][l].reshape(-1), p['rwkv_ln_g'][l], p['rwkv_ln_b'][l],
                                 p['ret_gn_g'][l], p['s5_d'][l]], axis=0)
    out['gluw'] = p['s5_glu_w'][l].astype(BF16)
    out['glub'] = p['s5_glu_b'][l][None]
    out['wo'] = p['w_out'][l].astype(BF16)
    out['n2'] = p['norm2_g'][l][None]
    out['wq'] = p['peer_wq'][l].T.astype(BF16)
    out['keys'] = p['peer_keys'][l].reshape(PEER_HEADS * 2, PEER_NKEYS, PEER_DKEY).astype(BF16)
    return out


def kernel(x, c, ctx, c_ctx, norm1_g, norm2_g, ada_w, ada_b, w_in, w_out, rwkv_mu, rwkv_w0, rwkv_w_up, rwkv_a0, rwkv_a_up, rwkv_g_up, rwkv_k_k, rwkv_k_a, rwkv_r_k, rwkv_ln_g, rwkv_ln_b, ret_gn_g, lru_conv_w, lru_conv_b, lru_wa, lru_ba, lru_wx, lru_bx, lru_lambda, s5_a_re, s5_a_im, s5_log_dt, s5_b_re, s5_b_im, s5_c_re, s5_c_im, s5_d, s5_glu_w, s5_glu_b, peer_wq, peer_keys, peer_u, peer_v, final_norm_g):
    p = dict(w_in=w_in, w_out=w_out, norm2_g=norm2_g, rwkv_mu=rwkv_mu, rwkv_w0=rwkv_w0, rwkv_w_up=rwkv_w_up,
             rwkv_a0=rwkv_a0, rwkv_a_up=rwkv_a_up, rwkv_g_up=rwkv_g_up, rwkv_k_k=rwkv_k_k, rwkv_k_a=rwkv_k_a,
             rwkv_r_k=rwkv_r_k, rwkv_ln_g=rwkv_ln_g, rwkv_ln_b=rwkv_ln_b, ret_gn_g=ret_gn_g,
             lru_conv_w=lru_conv_w, lru_conv_b=lru_conv_b, lru_wa=lru_wa, lru_ba=lru_ba, lru_wx=lru_wx,
             lru_bx=lru_bx, lru_lambda=lru_lambda, s5_a_re=s5_a_re, s5_a_im=s5_a_im, s5_log_dt=s5_log_dt,
             s5_b_re=s5_b_re, s5_b_im=s5_b_im, s5_c_re=s5_c_re, s5_c_im=s5_c_im, s5_d=s5_d,
             s5_glu_w=s5_glu_w, s5_glu_b=s5_glu_b, peer_wq=peer_wq, peer_keys=peer_keys, peer_u=peer_u,
             peer_v=peer_v)
    nb, l_lat, d = x.shape
    l_ctx = ctx.shape[1]
    lt = l_ctx + l_lat
    depth = w_in.shape[0]
    assert 128 % (nb * N_HEADS) == 0 and l_ctx % 128 == 0 and l_lat % 128 == 0
    assert (l_ctx * nb) % PEER_TOK_TILE == 0 and (l_lat * nb) % PEER_TOK_TILE == 0
    k_lo = LANES // (nb * N_HEADS)
    n_ctx_tiles = l_ctx // ROW_TILE_T
    n_ctx_tok_tiles = l_ctx * nb // PEER_TOK_TILE
    hs = _head_sum_matrix()
    ret_tabs = _ret_tables(lt)
    u_all = peer_u.astype(BF16)
    vt_all = jnp.swapaxes(peer_v, 1, 2).astype(BF16)

    xs = jnp.transpose(jnp.concatenate([ctx, x], axis=1), (1, 0, 2))
    mod_rows = 8 * ((nb + 1 + 7) // 8)
    cvec = jnp.zeros((mod_rows, d), F32).at[:nb].set(c).at[nb].set(c_ctx)

    for l in range(depth):
        wl = _layer_weights(l, p)
        last = l == depth - 1
        mod = _modulation(cvec, ada_w[l], ada_b[l])
        mod_lat = mod[:nb].reshape(nb, N_MOD, d).transpose(1, 0, 2)
        mod_ctx = jnp.broadcast_to(mod[nb].reshape(N_MOD, 1, d), (N_MOD, nb, d))
        modv = jnp.stack([mod_ctx, mod_lat])

        za, zb_qkv, zb_gate, zc, zd = _in_proj(xs, norm1_g[l], modv, wl['w_in'], n_ctx_tiles)

        r, k, v, g, av, w_dec, kd, bv = _rwkv_features(za, wl['mu'], wl['gup'], wl['wup'], wl['aup'],
                                                       wl['rwkv_vecs'], hs, n_ctx_tiles)
        ex = lambda t: _rwkv_expand_k(t, k_lo)
        ya_f, ya_b = _rwkv_scan(ex(r), ex(av), ex(w_dec), ex(kd), ex(bv), _rwkv_expand_v(v), l_ctx)
        ya = jnp.stack([_rwkv_collapse_v(ya_f, nb), _rwkv_collapse_v(ya_b, nb)])
        ob = _retention(zb_qkv, ret_tabs, l_ctx)
        hc = _lru(zc, wl['cw'], wl['wa'], wl['wx'], wl['lru_vecs'], l_ctx)
        yd = _from_colmajor(_s5(_to_colmajor(zd, l_ctx), *wl['s5'], l_ctx), l_ctx)

        t0 = n_ctx_tiles if last else 0
        x1, h2 = _out_proj(xs, modv, ya, (r, k, v, g), ob, zb_gate, hc, zc, yd, zd, wl['out_vecs'], hs,
                           wl['gluw'], wl['glub'], wl['wo'], wl['n2'], n_ctx_tiles, t0)
        t0p = n_ctx_tok_tiles if last else 0
        e1, cnt, r2, w2 = _peer_route(h2, wl['wq'], wl['keys'], t0p)
        x2 = _peer_ffn(h2, u_all, vt_all, l, e1, cnt, r2, w2, x1.reshape(lt * nb, d), modv,
                       n_ctx_tok_tiles, t0p)
        xs = x2.reshape(lt, nb, d)

    return _final_norm(xs, final_norm_g, l_ctx)
```

```python
import functools
import math

import jax
import jax.numpy as jnp
import numpy as np
from jax import lax
from jax.experimental import pallas as pl
from jax.experimental.pallas import tpu as pltpu

F32 = jnp.float32
BF16 = jnp.bfloat16

NORM_EPS = 1e-6
N_MOD = 6
HEAD_DIM = 64
N_HEADS = 4
GROUP_W = N_HEADS * HEAD_DIM
GRID_W = 64

RWKV_GATE_LORA = 64
RWKV_LORA = 32
RWKV_GN_EPS = 64e-5
RWKV_COLS = 3 * GROUP_W + RWKV_GATE_LORA + 4 * RWKV_LORA
ZA_W = 3 * GROUP_W + 3 * 128

RET_CHUNK = 128
RET_GN_EPS = 1e-5
RET_FWD_OFFSET = 5.0
RET_BWD_OFFSET = 5.5
ROPE_BASE = 10000.0
RET_COLS = 4 * GROUP_W

LRU_CONV_W = 4
LRU_C = 8.0
LRU_COLS = 2 * GROUP_W
LRU_CHUNK = 128

S5_CH = 16
S5_GROUPS = GROUP_W // S5_CH
S5_STATE = 64
S5_N = S5_GROUPS * S5_STATE
S5_CHUNK = 128

PEER_HEADS = 8
PEER_NKEYS = 128
PEER_DKEY = 128
PEER_TOPK = 16
PEER_TOK_TILE = 512
PEER_EXP_BLOCK = 2048

ROW_TILE_T = 32
RWKV_CHUNK = 64

VMEM_LIMIT_BYTES = 56 * 1024 * 1024

NEG_INF = float("-inf")


def _cparams(n_axes):
    return pltpu.CompilerParams(dimension_semantics=("arbitrary",) * n_axes,
                                vmem_limit_bytes=VMEM_LIMIT_BYTES)


def _sigmoid(x):
    return 1.0 / (1.0 + jnp.exp(-x))


def _gelu_tanh(x):
    return 0.5 * x * (1.0 + jnp.tanh(math.sqrt(2.0 / math.pi) * (x + 0.044715 * (x * x * x))))


def _softplus(x):
    return jnp.maximum(x, 0.0) + jnp.log(1.0 + jnp.exp(-jnp.abs(x)))


def _dot(a, b):
    return jnp.dot(a, b, preferred_element_type=F32)


def _dot_exact(a, b):
    return jnp.dot(a, b, preferred_element_type=F32, precision=lax.Precision.HIGHEST)


def _head_sum_matrix():
    idx = np.arange(GROUP_W) // HEAD_DIM
    return jnp.asarray((idx[:, None] == idx[None, :]).astype(np.float32))


def _mod_body(c_ref, w_ref, b_ref, o_ref):
    c = c_ref[...]
    s = (c * _sigmoid(c)).astype(BF16)
    o_ref[...] = _dot(s, w_ref[...].astype(BF16)) + b_ref[...]


def _modulation(cvec, w, b):
    rows, d = cvec.shape
    n = w.shape[1]
    tn = 512
    return pl.pallas_call(
        _mod_body,
        grid=(n // tn,),
        in_specs=[pl.BlockSpec((rows, d), lambda j: (0, 0)),
                  pl.BlockSpec((d, tn), lambda j: (0, j)),
                  pl.BlockSpec((1, tn), lambda j: (0, j))],
        out_specs=pl.BlockSpec((rows, tn), lambda j: (0, j)),
        out_shape=jax.ShapeDtypeStruct((rows, n), F32),
        compiler_params=_cparams(1),
        name="adaln_mod",
    )(cvec, w, b.reshape(1, n))


RET_QKV = 3 * GROUP_W


def _in_body(x_ref, g_ref, mod_ref, w_ref, za_ref, zqkv_ref, zg_ref, zc_ref, zd_ref):
    tt, nb, d = x_ref.shape
    x = x_ref[...]
    ms = jnp.mean(x * x, axis=-1, keepdims=True)
    y = x * lax.rsqrt(ms + NORM_EPS) * g_ref[...]
    h = y * (1.0 + mod_ref[1]) + mod_ref[0]
    hb = h.reshape(tt * nb, d).astype(BF16)
    c0 = 0
    for ref in (za_ref, zqkv_ref, zg_ref, zc_ref, zd_ref):
        w = ref.shape[-1]
        z = _dot(hb, w_ref[:, c0:c0 + w]).reshape(tt, nb, w)
        ref[...] = jnp.swapaxes(z, 0, 1) if ref is zqkv_ref else z
        c0 += w


def _in_proj(xs, g, modv, w_p, n_ctx_tiles):
    lt, nb, d = xs.shape
    tt = ROW_TILE_T
    widths = (ZA_W, RET_QKV, GROUP_W, LRU_COLS, GROUP_W)
    seg = lambda i: jnp.where(i >= n_ctx_tiles, 1, 0)
    tm = lambda w: (pl.BlockSpec((tt, nb, w), lambda i: (i, 0, 0)), jax.ShapeDtypeStruct((lt, nb, w), F32))
    bm = lambda w: (pl.BlockSpec((nb, tt, w), lambda i: (0, i, 0)), jax.ShapeDtypeStruct((nb, lt, w), F32))
    outs = [bm(w) if n == 1 else tm(w) for n, w in enumerate(widths)]
    return pl.pallas_call(
        _in_body,
        grid=(lt // tt,),
        in_specs=[pl.BlockSpec((tt, nb, d), lambda i: (i, 0, 0)),
                  pl.BlockSpec((1, d), lambda i: (0, 0)),
                  pl.BlockSpec((None, N_MOD, nb, d), lambda i: (seg(i), 0, 0, 0)),
                  pl.BlockSpec(w_p.shape, lambda i: (0, 0))],
        out_specs=[o[0] for o in outs],
        out_shape=[o[1] for o in outs],
        compiler_params=_cparams(1),
        name="in_proj",
    )(xs, g.reshape(1, d), modv, w_p)


def _rwkv_feat_body(n_ctx_tiles, n_tiles,
                    z_ref, zp_ref, zn_ref, mu_ref, gup_ref, wup_ref, aup_ref, vec_ref, hs_ref,
                    r_ref, k_ref, v_ref, g_ref, av_ref, w_ref, kd_ref, bv_ref):
    i = pl.program_id(0)
    tt, nb, zw = z_ref.shape
    z0 = z_ref[...]
    has_prev = jnp.logical_and(i != 0, i != n_ctx_tiles)
    has_next = jnp.logical_and(i != n_ctx_tiles - 1, i != n_tiles - 1)
    hp = jnp.where(has_prev, zp_ref[...], 0.0)
    hn = jnp.where(has_next, zn_ref[...], 0.0)
    zp = jnp.concatenate([hp, z0[:-1]], axis=0)
    zn = jnp.concatenate([z0[1:], hn], axis=0)
    z = z0 + mu_ref[0] * (zp - z0) + mu_ref[1] * (zn - z0)
    z = z.reshape(tt * nb, zw)
    gw = GROUP_W
    r = z[:, 0:gw]
    k = z[:, gw:2 * gw]
    v = z[:, 2 * gw:3 * gw]
    g = _dot(_sigmoid(z[:, 3 * gw:3 * gw + 128]).astype(BF16), gup_ref[...])
    w_low = jnp.tanh(z[:, 3 * gw + 128:3 * gw + 256]).astype(BF16)
    a_low = z[:, 3 * gw + 256:3 * gw + 384].astype(BF16)
    k_k = vec_ref[0:1, :]
    k_a = vec_ref[1:2, :]
    kk = k * k_k
    ss = _dot_exact(kk * kk, hs_ref[...])
    kk = kk / jnp.maximum(jnp.sqrt(ss), 1e-12)
    shp = (tt, nb, gw)
    r_ref[...] = r.reshape(shp)
    k_ref[...] = k.reshape(shp)
    v_ref[...] = v.reshape(shp)
    g_ref[...] = g.reshape(shp)
    av_ref[...] = (-kk).reshape(shp)
    for d in range(2):
        w0 = vec_ref[2 + d:3 + d, :]
        a0 = vec_ref[4 + d:5 + d, :]
        w_log = -_softplus(-(w0 + _dot(w_low, wup_ref[d]))) - 0.5
        w_ref[d] = jnp.exp(-jnp.exp(w_log)).reshape(shp)
        a = _sigmoid(a0 + _dot(a_low, aup_ref[d]))
        kd_ref[d] = (k * (1.0 + (a - 1.0) * k_a)).reshape(shp)
        bv_ref[d] = (kk * a).reshape(shp)


def _rwkv_features(za, mu_p, gup_p, wup_p, aup_p, vecs, hs, n_ctx_tiles):
    lt, nb, zw = za.shape
    tt = ROW_TILE_T
    n_tiles = lt // tt
    gw = GROUP_W
    one = lambda shape: jax.ShapeDtypeStruct(shape, F32)
    full = lambda a: pl.BlockSpec(a.shape, lambda i: (0,) * a.ndim)
    o1 = pl.BlockSpec((tt, nb, gw), lambda i: (i, 0, 0))
    o2 = pl.BlockSpec((2, tt, nb, gw), lambda i: (0, i, 0, 0))
    return pl.pallas_call(
        functools.partial(_rwkv_feat_body, n_ctx_tiles, n_tiles),
        grid=(n_tiles,),
        in_specs=[pl.BlockSpec((tt, nb, zw), lambda i: (i, 0, 0)),
                  pl.BlockSpec((1, nb, zw), lambda i: (jnp.maximum(i * tt - 1, 0), 0, 0)),
                  pl.BlockSpec((1, nb, zw), lambda i: (jnp.minimum((i + 1) * tt, lt - 1), 0, 0)),
                  full(mu_p), full(gup_p), full(wup_p), full(aup_p), full(vecs), full(hs)],
        out_specs=[o1, o1, o1, o1, o1, o2, o2, o2],
        out_shape=[one((lt, nb, gw))] * 5 + [one((2, lt, nb, gw))] * 3,
        compiler_params=_cparams(1),
        name="rwkv_features",
    )(za, za, za, mu_p, gup_p, wup_p, aup_p, vecs, hs)


def _rwkv_scan_body(rf_ref, af_ref, wf_ref, kdf_ref, bf_ref, vf_ref,
                    rb_ref, ab_ref, wb_ref, kdb_ref, bb_ref, vb_ref,
                    yf_ref, yb_ref, s_ref, vx_ref):
    j = pl.program_id(0)
    tc, n_khi, lanes = rf_ref.shape
    n_v = s_ref.shape[2]
    k_lo = n_v // n_khi
    n_bh = lanes // k_lo

    @pl.when(j == 0)
    def _():
        s_ref[...] = jnp.zeros_like(s_ref)

    def group_sum(x):
        parts = [x] + [pltpu.roll(x, g * n_bh, axis=x.ndim - 1) for g in range(1, k_lo)]
        while len(parts) > 1:
            parts = [parts[n] + parts[n + 1] for n in range(0, len(parts) - 1, 2)] + parts[len(parts) & ~1:]
        return parts[0]

    grp3 = lax.broadcasted_iota(jnp.int32, (tc, n_khi, lanes), 2) // n_bh
    for d, ref in enumerate((vf_ref, vb_ref)):
        x = ref[...]
        for g in range(k_lo):
            vx_ref[d, :, g * n_khi:(g + 1) * n_khi, :] = group_sum(jnp.where(grp3 == g, x, 0.0))

    grp2 = lax.broadcasted_iota(jnp.int32, (n_khi, lanes), 1) // n_bh

    def pack_rows(y):
        out = y[0:n_khi]
        for g in range(1, k_lo):
            out = jnp.where(grp2 == g, y[g * n_khi:(g + 1) * n_khi], out)
        return out

    dirs = ((rf_ref, af_ref, wf_ref, kdf_ref, bf_ref, yf_ref), (rb_ref, ab_ref, wb_ref, kdb_ref, bb_ref, yb_ref))

    def step(i, carry):
        t = (i, tc - 1 - i)
        row = lambda ref, d, kh: ref[t[d], pl.ds(kh, 1), :]
        ip = jnp.maximum(i - 1, 0)
        tp = (ip, tc - 1 - ip)
        for d in range(2):
            dirs[d][5][tp[d]] = pack_rows(group_sum(carry[d]))
        acc = [None, None]
        for kh in range(n_khi):
            for d in range(2):
                term = s_ref[d, kh] * row(dirs[d][1], d, kh)
                acc[d] = term if acc[d] is None else acc[d] + term
        sa = [group_sum(a) for a in acc]
        for kh in range(n_khi):
            for d in range(2):
                s_ref[d, kh] = s_ref[d, kh] * row(dirs[d][2], d, kh) + vx_ref[d, t[d]] * row(dirs[d][3], d, kh)
        acc = [None, None]
        for kh in range(n_khi):
            for d in range(2):
                s_new = s_ref[d, kh] + sa[d] * row(dirs[d][4], d, kh)
                s_ref[d, kh] = s_new
                term = s_new * row(dirs[d][0], d, kh)
                acc[d] = term if acc[d] is None else acc[d] + term
        return tuple(acc)

    zero = jnp.zeros((n_v, lanes), F32)
    last = lax.fori_loop(0, tc, step, (zero, zero))
    t_last = (tc - 1, 0)
    for d in range(2):
        dirs[d][5][t_last[d]] = pack_rows(group_sum(last[d]))


def _dir_chunk(d, j, n_ctx, n_all):
    bwd = jnp.where(j < n_ctx, n_ctx - 1 - j, n_all - 1 - (j - n_ctx))
    return jnp.where(d == 0, j, bwd)


def _rwkv_scan(r_e, a_e, w_e, kd_e, b_e, v_e, l_ctx):
    lt, n_khi, lanes = r_e.shape
    tc = RWKV_CHUNK
    n_all, n_ctx = lt // tc, l_ctx // tc
    specs = []
    for d in range(2):
        cm = functools.partial(_dir_chunk, d, n_ctx=n_ctx, n_all=n_all)
        shared = pl.BlockSpec((tc, n_khi, lanes), lambda j, cm=cm: (cm(j), 0, 0))
        per_dir = pl.BlockSpec((None, tc, n_khi, lanes), lambda j, cm=cm, d=d: (d, cm(j), 0, 0))
        specs.append(([shared, shared, per_dir, per_dir, per_dir, shared], shared))
    return pl.pallas_call(
        _rwkv_scan_body,
        grid=(n_all,),
        in_specs=specs[0][0] + specs[1][0],
        out_specs=[specs[0][1], specs[1][1]],
        out_shape=[jax.ShapeDtypeStruct((lt, n_khi, lanes), F32)] * 2,
        scratch_shapes=[pltpu.VMEM((2, n_khi, HEAD_DIM, lanes), F32), pltpu.VMEM((2, tc, HEAD_DIM, lanes), F32)],
        compiler_params=_cparams(1),
        name="rwkv_scan",
    )(r_e, a_e, w_e, kd_e, b_e, v_e, r_e, a_e, w_e, kd_e, b_e, v_e)


def _rwkv_expand_k(x, k_lo):
    lead = x.shape[:-2]
    nb = x.shape[-2]
    nd = len(lead)
    x = x.reshape(lead + (nb, N_HEADS, k_lo, HEAD_DIM // k_lo))
    x = jnp.transpose(x, tuple(range(nd)) + (nd + 3, nd + 2, nd, nd + 1))
    return x.reshape(lead + (HEAD_DIM // k_lo, k_lo * nb * N_HEADS))


def _rwkv_collapse_k(y, nb, k_lo):
    lt, k_hi, _ = y.shape
    y = y.reshape(lt, k_hi, k_lo, nb, N_HEADS)
    return jnp.transpose(y, (0, 3, 4, 2, 1)).reshape(lt, nb, GROUP_W)


def _ret_body(nb, z_ref, cos_ref, sin_ref, dec_ref, rd_ref, wd_ref, cd_ref, hm_ref, o_ref, s_ref, ob_ref):
    j = pl.program_id(1)
    gw = GROUP_W
    half = gw // 2

    @pl.when(j == 0)
    def _():
        s_ref[...] = jnp.zeros_like(s_ref)

    cos = cos_ref[...]
    sin = sin_ref[...]

    def rope(x):
        x1, x2 = x[:, :half], x[:, half:]
        return jnp.concatenate([x1 * cos - x2 * sin, x1 * sin + x2 * cos], axis=-1)

    def one_batch(b, carry):
        q = rope(z_ref[b, :, 0:gw]) * (HEAD_DIM ** -0.5)
        k = rope(z_ref[b, :, gw:2 * gw])
        v = z_ref[b, :, 2 * gw:3 * gw]
        vb = v.astype(BF16)
        k_t = k.T
        k_tb = k_t.astype(BF16)
        state = s_ref[b]
        inter = _dot(q.astype(BF16), state.astype(BF16)) * rd_ref[...]
        intra = jnp.zeros_like(inter)
        for h in range(N_HEADS):
            qm = (q * hm_ref[h:h + 1, :]).astype(BF16)
            att = _dot(qm, k_tb) * dec_ref[h]
            vm = (v * hm_ref[N_HEADS + h:N_HEADS + h + 1, :]).astype(BF16)
            intra = intra + _dot(att.astype(BF16), vm)
        ob_ref[b] = intra + inter
        kv = _dot((k_t * wd_ref[...]).astype(BF16), vb)
        s_ref[b] = state * cd_ref[0] + kv * cd_ref[1]
        return carry

    lax.fori_loop(0, nb, one_batch, 0)
    o_ref[...] = jnp.swapaxes(ob_ref[...], 0, 1)


def _retention(zb, tabs, l_ctx):
    nb, lt, _ = zb.shape
    c = RET_CHUNK
    n_all, n_ctx = lt // c, l_ctx // c
    gw = GROUP_W
    cos, sin, dec, rd, wd, cd, hm = tabs
    cm = lambda d, j: _dir_chunk(d, j, n_ctx, n_all)
    return pl.pallas_call(
        functools.partial(_ret_body, nb),
        grid=(2, n_all),
        in_specs=[pl.BlockSpec((nb, c, RET_QKV), lambda d, j: (0, cm(d, j), 0)),
                  pl.BlockSpec((c, gw // 2), lambda d, j: (cm(d, j), 0)),
                  pl.BlockSpec((c, gw // 2), lambda d, j: (cm(d, j), 0)),
                  pl.BlockSpec((None, N_HEADS, c, c), lambda d, j: (d, 0, 0, 0)),
                  pl.BlockSpec((None, c, gw), lambda d, j: (d, 0, 0)),
                  pl.BlockSpec((None, gw, c), lambda d, j: (d, 0, 0)),
                  pl.BlockSpec((None, 2, gw, gw), lambda d, j: (d, 0, 0, 0)),
                  pl.BlockSpec((2 * N_HEADS, gw), lambda d, j: (0, 0))],
        out_specs=pl.BlockSpec((None, c, nb, gw), lambda d, j: (d, cm(d, j), 0, 0)),
        out_shape=jax.ShapeDtypeStruct((2, lt, nb, gw), F32),
        scratch_shapes=[pltpu.VMEM((nb, gw, gw), F32), pltpu.VMEM((nb, c, gw), F32)],
        compiler_params=_cparams(2),
        name="retention",
    )(zb, cos, sin, dec, rd, wd, cd, hm)


def _ret_perm():
    new = np.arange(GROUP_W)
    half_id, rem = new // 128, new % 128
    h, i = rem // 32, rem % 32
    return h * HEAD_DIM + half_id * 32 + i


def _ret_tables(lt):
    c = RET_CHUNK
    half = HEAD_DIM // 2
    freqs = ROPE_BASE ** (-jnp.arange(half, dtype=F32) / half)
    ang = jnp.arange(lt, dtype=F32)[:, None] * freqs[None]
    cos = jnp.tile(jnp.cos(ang), (1, N_HEADS))
    sin = jnp.tile(jnp.sin(ang), (1, N_HEADS))
    idx = jnp.arange(c, dtype=F32)
    perm = _ret_perm()
    head_of_qk = jnp.asarray(perm // HEAD_DIM)
    head_of_v = jnp.arange(GROUP_W) // HEAD_DIM
    dec, rd, wd, cd = [], [], [], []
    for offset, rev in ((RET_FWD_OFFSET, False), (RET_BWD_OFFSET, True)):
        lg = jnp.log1p(-jnp.exp2(-(offset + jnp.arange(N_HEADS, dtype=F32))))
        diff = idx[:, None] - idx[None, :]
        if rev:
            mask, dist = diff < 0, -diff
            read_pow, write_pow = c - idx, idx
        else:
            mask, dist = diff >= 0, diff
            read_pow, write_pow = idx + 1.0, c - 1.0 - idx
        dec.append(jnp.where(mask[None], jnp.exp(lg[:, None, None] * jnp.where(mask, dist, 0.0)[None]), 0.0))
        rd.append(jnp.exp(lg[head_of_v][None, :] * read_pow[:, None]))
        wd.append(jnp.exp(lg[head_of_qk][:, None] * write_pow[None, :]))
        same = (head_of_qk[:, None] == head_of_v[None, :]).astype(F32)
        cd.append(jnp.stack([same * jnp.exp(lg * c)[head_of_v][None, :], same]))
    hm_q = (head_of_qk[None, :] == jnp.arange(N_HEADS)[:, None]).astype(F32)
    hm_v = (head_of_v[None, :] == jnp.arange(N_HEADS)[:, None]).astype(F32)
    return cos, sin, jnp.stack(dec), jnp.stack(rd), jnp.stack(wd), jnp.stack(cd), jnp.concatenate([hm_q, hm_v])


def _lru_body(n_ctx, n_all, x_ref, xp_ref, xn_ref, cw_ref, wa_ref, wx_ref, vec_ref, o_ref,
              a_s, b_s, h_s):
    d = pl.program_id(0)
    j = pl.program_id(1)
    tc, nb, gw = x_ref.shape
    ch = _dir_chunk(d, j, n_ctx, n_all)

    @pl.when(j == 0)
    def _():
        h_s[...] = jnp.zeros_like(h_s)

    has_prev = jnp.logical_and(ch != 0, ch != n_ctx)
    has_next = jnp.logical_and(ch != n_ctx - 1, ch != n_all - 1)
    xe = jnp.concatenate([jnp.where(has_prev, xp_ref[...], 0.0), x_ref[...],
                          jnp.where(has_next, xn_ref[...], 0.0)], axis=0)
    xc = cw_ref[LRU_CONV_W:LRU_CONV_W + 1, :]
    for tap in range(LRU_CONV_W):
        xc = xc + cw_ref[tap:tap + 1, :] * xe[tap:tap + tc]
    xc = xc.reshape(tc * nb, gw)
    xb = xc.astype(BF16)
    r = _sigmoid(_dot(xb, wa_ref[...]) + vec_ref[0:1, :])
    gi = _sigmoid(_dot(xb, wx_ref[...]) + vec_ref[1:2, :])
    log_a = -LRU_C * r * _softplus(-vec_ref[2:3, :])
    th = jnp.tanh(log_a)
    one_minus_a2 = 2.0 * th / (th - 1.0)
    a_s[...] = jnp.exp(log_a).reshape(tc, nb, gw)
    b_s[...] = (jnp.sqrt(one_minus_a2) * (gi * xc)).reshape(tc, nb, gw)

    def step(i, h):
        t = jnp.where(d == 0, i, tc - 1 - i)
        h = a_s[t] * h + b_s[t]
        o_ref[t] = h
        return h

    h_s[...] = lax.fori_loop(0, tc, step, h_s[...])


def _lru(zc, cw, wa_bd, wx_bd, vecs, l_ctx):
    lt, nb, _ = zc.shape
    gw = GROUP_W
    tc = LRU_CHUNK
    n_all, n_ctx = lt // tc, l_ctx // tc
    cm = lambda d, j: _dir_chunk(d, j, n_ctx, n_all)
    return pl.pallas_call(
        functools.partial(_lru_body, n_ctx, n_all),
        grid=(2, n_all),
        in_specs=[pl.BlockSpec((tc, nb, gw), lambda d, j: (cm(d, j), 0, 0)),
                  pl.BlockSpec((2, nb, gw), lambda d, j: (jnp.maximum(cm(d, j) * (tc // 2) - 1, 0), 0, 0)),
                  pl.BlockSpec((1, nb, gw), lambda d, j: (jnp.minimum((cm(d, j) + 1) * tc, lt - 1), 0, 0)),
                  pl.BlockSpec(cw.shape, lambda d, j: (0, 0)),
                  pl.BlockSpec((None, gw, gw), lambda d, j: (d, 0, 0)),
                  pl.BlockSpec((None, gw, gw), lambda d, j: (d, 0, 0)),
                  pl.BlockSpec((None, 3, gw), lambda d, j: (d, 0, 0))],
        out_specs=pl.BlockSpec((None, tc, nb, gw), lambda d, j: (d, cm(d, j), 0, 0)),
        out_shape=jax.ShapeDtypeStruct((2, lt, nb, gw), F32),
        scratch_shapes=[pltpu.VMEM((tc, nb, gw), F32), pltpu.VMEM((tc, nb, gw), F32),
                        pltpu.VMEM((nb, gw), F32)],
        compiler_params=_cparams(2),
        name="rglru",
    )(zc, zc, zc, cw, wa_bd, wx_bd, vecs)


def _block_diag(w):
    nblk, n = w.shape[-3], w.shape[-1]
    eye = jnp.eye(nblk, dtype=w.dtype)
    out = w[..., :, :, None, :] * eye[:, None, :, None]
    return out.reshape(w.shape[:-3] + (nblk * n, nblk * n))


def _s5_body(u_ref, bm_ref, lam_ref, cm_ref, y_ref, h_s, st_s):
    d = pl.program_id(0)
    j = pl.program_id(1)
    tc, nb, gw = u_ref.shape
    n = S5_N

    @pl.when(j == 0)
    def _():
        st_s[...] = jnp.zeros_like(st_s)

    u = u_ref[...].reshape(tc * nb, gw).astype(BF16)
    h_s[...] = _dot(u, bm_ref[...]).reshape(tc, nb, 2 * n)
    lr = lam_ref[0:1, :]
    li = lam_ref[1:2, :]

    def step(i, carry):
        hr, hi = carry
        t = jnp.where(d == 0, i, tc - 1 - i)
        bu = h_s[t]
        nr = lr * hr - li * hi + bu[:, :n]
        ni = lr * hi + li * hr + bu[:, n:]
        h_s[t] = jnp.concatenate([nr, ni], axis=-1)
        return nr, ni

    hr, hi = lax.fori_loop(0, tc, step, (st_s[0], st_s[1]), unroll=4)
    st_s[0] = hr
    st_s[1] = hi
    hh = h_s[...].reshape(tc * nb, 2 * n).astype(BF16)
    y_ref[...] = _dot(hh, cm_ref[...]).reshape(tc, nb, gw)


def _s5(u, bmat, lam, cmat, l_ctx):
    lt, nb, gw = u.shape
    tc = S5_CHUNK
    n_all, n_ctx = lt // tc, l_ctx // tc
    cm = lambda d, j: _dir_chunk(d, j, n_ctx, n_all)
    return pl.pallas_call(
        _s5_body,
        grid=(2, n_all),
        in_specs=[pl.BlockSpec((tc, nb, gw), lambda d, j: (cm(d, j), 0, 0)),
                  pl.BlockSpec((None, gw, 2 * S5_N), lambda d, j: (d, 0, 0)),
                  pl.BlockSpec((None, 2, S5_N), lambda d, j: (d, 0, 0)),
                  pl.BlockSpec((None, 2 * S5_N, gw), lambda d, j: (d, 0, 0))],
        out_specs=pl.BlockSpec((None, tc, nb, gw), lambda d, j: (d, cm(d, j), 0, 0)),
        out_shape=jax.ShapeDtypeStruct((2, lt, nb, gw), F32),
        scratch_shapes=[pltpu.VMEM((tc, nb, 2 * S5_N), F32), pltpu.VMEM((2, nb, S5_N), F32)],
        compiler_params=_cparams(2),
        name="s5",
    )(u, bmat, lam, cmat)


def _s5_params(a_re, a_im, log_dt, b_re, b_im, c_re, c_im):
    dt = jnp.exp(log_dt)[..., None]
    er = jnp.exp(a_re * dt)
    lbr, lbi = er * jnp.cos(a_im * dt), er * jnp.sin(a_im * dt)
    den = a_re * a_re + a_im * a_im
    nr, ni = lbr - 1.0, lbi
    fr = (nr * a_re + ni * a_im) / den
    fi = (ni * a_re - nr * a_im) / den
    bbr = fr[..., None] * b_re - fi[..., None] * b_im
    bbi = fr[..., None] * b_im + fi[..., None] * b_re
    eye = jnp.eye(S5_GROUPS, dtype=F32)

    def in_map(bb):
        m = jnp.einsum('dgpc,gh->dgchp', bb, eye)
        return m.reshape(2, GROUP_W, S5_N)

    def out_map(cc):
        m = jnp.einsum('dgcp,gh->dgphc', cc, eye)
        return m.reshape(2, S5_N, GROUP_W)

    bmat = jnp.concatenate([in_map(bbr), in_map(bbi)], axis=-1).astype(BF16)
    cmat = jnp.concatenate([out_map(c_re), -out_map(c_im)], axis=-2).astype(BF16)
    lam = jnp.stack([lbr.reshape(2, S5_N), lbi.reshape(2, S5_N)], axis=1)
    return bmat, lam, cmat


def _to_colmajor(z, l_ctx):
    lt, nb, ch = z.shape
    rows = (lt - l_ctx) // GRID_W
    lat = z[l_ctx:].reshape(rows, GRID_W, nb, ch).transpose(1, 0, 2, 3).reshape(lt - l_ctx, nb, ch)
    return jnp.concatenate([z[:l_ctx], lat], axis=0)


def _from_colmajor(y, l_ctx):
    lt = y.shape[-3]
    nb, ch = y.shape[-2:]
    rows = (lt - l_ctx) // GRID_W
    lead = y.shape[:-3]
    lat = y[..., l_ctx:, :, :].reshape(lead + (GRID_W, rows, nb, ch))
    lat = jnp.swapaxes(lat, -4, -3).reshape(lead + (lt - l_ctx, nb, ch))
    return jnp.concatenate([y[..., :l_ctx, :, :], lat], axis=-3)


def _out_body(x_ref, mod_ref, ya_ref, r_ref, k_ref, v_ref, g_ref, ob_ref, gb_ref, hc_ref, gc_ref,
              yd_ref, ud_ref, vec_ref, hs_ref, gluw_ref, glub_ref, wo_ref, n2_ref, x1_ref, h2_ref):
    tt, nb, d = x_ref.shape
    gw = GROUP_W
    rows = tt * nb
    hs = hs_ref[...]
    two = lambda ref: (ref[0] + ref[1]).reshape(rows, gw)
    flat = lambda ref: ref[...].reshape(rows, gw)

    def head_norm(y, eps):
        mu = _dot_exact(y, hs) * (1.0 / HEAD_DIM)
        dlt = y - mu
        var = _dot_exact(dlt * dlt, hs) * (1.0 / HEAD_DIM)
        return dlt * lax.rsqrt(var + eps)

    r_k, ln_g, ln_b = vec_ref[0:1, :], vec_ref[1:2, :], vec_ref[2:3, :]
    r, k, v = flat(r_ref), flat(k_ref), flat(v_ref)
    yn = head_norm(two(ya_ref), RWKV_GN_EPS) * ln_g + ln_b
    bonus = _dot_exact(r * k * r_k, hs) * v
    mix_a = (yn + bonus) * flat(g_ref)
    gb = flat(gb_ref)
    mix_b = gb * _sigmoid(gb) * (head_norm(two(ob_ref), RET_GN_EPS) * vec_ref[3:4, :])
    mix_c = _gelu_tanh(flat(gc_ref)) * two(hc_ref)
    yd = vec_ref[4:5, :] * flat(ud_ref) + two(yd_ref)
    glu = _dot(_gelu_tanh(yd).astype(BF16), gluw_ref[...]) + glub_ref[...]
    mix_d = glu[:, :gw] * _sigmoid(glu[:, gw:])

    mix = _dot(mix_a.astype(BF16), wo_ref[0:gw, :])
    mix = mix + _dot(mix_b.astype(BF16), wo_ref[gw:2 * gw, :])
    mix = mix + _dot(mix_c.astype(BF16), wo_ref[2 * gw:3 * gw, :])
    mix = mix + _dot(mix_d.astype(BF16), wo_ref[3 * gw:4 * gw, :])
    x1 = x_ref[...] + mod_ref[2] * mix.reshape(tt, nb, d)
    x1_ref[...] = x1
    ms = jnp.mean(x1 * x1, axis=-1, keepdims=True)
    h2 = (x1 * lax.rsqrt(ms + NORM_EPS) * n2_ref[...]) * (1.0 + mod_ref[4]) + mod_ref[3]
    h2_ref[...] = h2.reshape(rows, d).T.astype(BF16)


def _out_proj(xs, modv, ya, feats, ob, zb, hc, zc, yd, zd, vecs, hs, gluw, glub, wo, n2, n_ctx_tiles, t0_tile):
    lt, nb, d = xs.shape
    tt = ROW_TILE_T
    gw = GROUP_W
    n_tiles = lt // tt - t0_tile
    seg = lambda i: jnp.where(i + t0_tile >= n_ctx_tiles, 1, 0)
    col = lambda c: pl.BlockSpec((tt, nb, gw), lambda i: (i + t0_tile, 0, c))
    two = pl.BlockSpec((2, tt, nb, gw), lambda i: (0, i + t0_tile, 0, 0))
    full = lambda a: pl.BlockSpec(a.shape, lambda i: (0,) * a.ndim)
    r, k, v, g = feats
    return pl.pallas_call(
        _out_body,
        grid=(n_tiles,),
        in_specs=[pl.BlockSpec((tt, nb, d), lambda i: (i + t0_tile, 0, 0)),
                  pl.BlockSpec((None, N_MOD, nb, d), lambda i: (seg(i), 0, 0, 0)),
                  two, col(0), col(0), col(0), col(0),
                  two, col(0), two, col(1), two, col(0),
                  full(vecs), full(hs), full(gluw), full(glub), full(wo), full(n2)],
        out_specs=[pl.BlockSpec((tt, nb, d), lambda i: (i + t0_tile, 0, 0)),
                   pl.BlockSpec((d, tt * nb), lambda i: (0, i + t0_tile))],
        out_shape=[jax.ShapeDtypeStruct((lt, nb, d), F32), jax.ShapeDtypeStruct((d, lt * nb), BF16)],
        compiler_params=_cparams(1),
        name="out_proj",
    )(xs, modv, ya, r, k, v, g, ob, zb, hc, zc, yd, zd, vecs, hs, gluw, glub, wo, n2)


PEER_NO_RANK = 127.0


def _top_values(s, n, with_rank=False):
    vals, cur = [], s
    rank = jnp.full(s.shape, PEER_NO_RANK, F32) if with_rank else None
    for k in range(n):
        m = jnp.max(cur, axis=0, keepdims=True)
        vals.append(m)
        hit = cur == m
        if with_rank:
            rank = jnp.where(hit, float(k), rank)
        cur = jnp.where(hit, NEG_INF, cur)
    return (vals, rank) if with_rank else vals


def _peer_route_body(h_ref, wq_ref, keys_ref, e1_ref, cnt_ref, r2_ref, w2_ref, pack_ref):
    hb = h_ref[...]
    for h in range(PEER_HEADS):
        st = []
        for p in range(2):
            c0 = (2 * h + p) * PEER_DKEY
            q = _dot(wq_ref[c0:c0 + PEER_DKEY, :], hb).astype(BF16)
            st.append(_dot(keys_ref[2 * h + p], q))
        s1, s2 = st
        n_top = PEER_TOPK + 1
        v1 = _top_values(s1, n_top)
        v2, rank2 = _top_values(s2, n_top, with_rank=True)
        pad7 = [jnp.full((7, s1.shape[1]), NEG_INF, F32)]
        v2m = jnp.concatenate(v2 + pad7, axis=0)
        v1_tail = jnp.concatenate(v1[8:] + pad7, axis=0)
        cand = jnp.concatenate([v1[0] + v2m] + [v1[a] + v2m[0:8] for a in range(1, 8)]
                               + [v1_tail + v2[0]], axis=0)
        top = _top_values(cand, n_top)
        tau = 0.5 * (top[PEER_TOPK - 1] + top[PEER_TOPK])
        cmax = top[0]
        z = jnp.sum(jnp.where(cand >= tau, jnp.exp(cand - cmax), 0.0), axis=0, keepdims=True)
        thr = tau - s1
        cnt = jnp.zeros_like(s1)
        for k in range(PEER_TOPK):
            cnt = cnt + jnp.where(v2[k] >= thr, 1.0, 0.0)
        e1_ref[:, h, :] = jnp.exp(s1 - v1[0])
        cnt_ref[:, h, :] = cnt
        pack_ref[0] = rank2.astype(BF16)
        pack_ref[1] = (jnp.exp(s2 - v2[0]) / z).astype(BF16)
        words = pack_ref.bitcast(jnp.uint32)
        r2_ref[h] = words[0]
        w2_ref[h] = words[1]


def _peer_route(h2t, wq, keys, t0_tile):
    d, t_all = h2t.shape
    tt = PEER_TOK_TILE
    nk = PEER_NKEYS
    by_i = pl.BlockSpec((nk, PEER_HEADS, tt), lambda i: (0, 0, i + t0_tile))
    by_h = pl.BlockSpec((PEER_HEADS, nk // 2, tt), lambda i: (0, 0, i + t0_tile))
    return pl.pallas_call(
        _peer_route_body,
        grid=(t_all // tt - t0_tile,),
        in_specs=[pl.BlockSpec((d, tt), lambda i: (0, i + t0_tile)),
                  pl.BlockSpec(wq.shape, lambda i: (0, 0)),
                  pl.BlockSpec(keys.shape, lambda i: (0, 0, 0))],
        out_specs=[by_i, by_i, by_h, by_h],
        out_shape=[jax.ShapeDtypeStruct((nk, PEER_HEADS, t_all), F32)] * 2
                  + [jax.ShapeDtypeStruct((PEER_HEADS, nk // 2, t_all), jnp.uint32)] * 2,
        scratch_shapes=[pltpu.VMEM((2, nk, tt), BF16)],
        compiler_params=_cparams(1),
        name="peer_route",
    )(h2t, wq, keys)


PEER_GATE_ROWS = 32
LANES = 128


BF16_ROWS = 16
SUBLANES = 8


def _peer_gate_block(p_ref, e1_ref, cnt_ref, r2_ref, w2_ref, bc_ref):
    nk = PEER_NKEYS
    n_sub = p_ref.shape[0] // nk
    rc = PEER_GATE_ROWS
    n_pv = rc // BF16_ROWS
    for c0 in range(0, p_ref.shape[1], LANES):
        cols = slice(c0, c0 + LANES)
        for ii in range(n_sub):
            for h in range(PEER_HEADS):
                bc_ref[0, ii, h] = jnp.broadcast_to(cnt_ref[ii, h:h + 1, cols], (BF16_ROWS, LANES)).astype(BF16)
                bc_ref[1, ii, h] = jnp.broadcast_to(e1_ref[ii, h:h + 1, cols], (BF16_ROWS, LANES)).astype(BF16)

        def chunk(q, carry, cols=cols):
            r0 = pl.multiple_of(q * rc, rc)
            gate = [[None] * n_pv for _ in range(n_sub)]
            for h in range(PEER_HEADS):
                rank = [pltpu.bitcast(r2_ref[h, q, SUBLANES * v:SUBLANES * (v + 1), cols], BF16)
                        for v in range(n_pv)]
                w2 = [pltpu.bitcast(w2_ref[h, q, SUBLANES * v:SUBLANES * (v + 1), cols], BF16)
                      for v in range(n_pv)]
                for ii in range(n_sub):
                    cnt = bc_ref[0, ii, h]
                    e1 = bc_ref[1, ii, h]
                    for v in range(n_pv):
                        term = jnp.where(rank[v] < cnt, w2[v] * e1, jnp.zeros_like(e1))
                        gate[ii][v] = term if gate[ii][v] is None else gate[ii][v] + term
            for ii in range(n_sub):
                for v in range(n_pv):
                    rows = pl.ds(ii * nk + r0 + BF16_ROWS * v, BF16_ROWS)
                    p_ref[rows, cols] = p_ref[rows, cols] * gate[ii][v]
            return carry

        lax.fori_loop(0, nk // rc, chunk, 0)


def _peer_ffn_body(h_ref, u_ref, vt_ref, e1_ref, cnt_ref, r2_ref, w2_ref, x_ref, mod_ref, o_ref,
                   acc_ref, p_ref, bc_ref, ht_ref):
    j = pl.program_id(1)

    @pl.when(j == 0)
    def _():
        acc_ref[...] = jnp.zeros_like(acc_ref)
        ht_ref[...] = h_ref[...]

    act = _dot(u_ref[...], ht_ref[...])
    p_ref[...] = _gelu_tanh(act.astype(BF16))
    _peer_gate_block(p_ref, e1_ref, cnt_ref, r2_ref, w2_ref, bc_ref)
    acc_ref[...] += _dot(vt_ref[...], p_ref[...])

    @pl.when(j == pl.num_programs(1) - 1)
    def _():
        tt, d = x_ref.shape
        nb = mod_ref.shape[1]
        y = acc_ref[...].T.reshape(tt // nb, nb, d) * mod_ref[5]
        o_ref[...] = x_ref[...] + y.reshape(tt, d)


def _peer_ffn(h2t, u_b, vt_b, layer, e1, cnt, r2, w2, x1, modv, n_ctx_tok_tiles, t0_tile):
    d, t_all = h2t.shape
    tt = PEER_TOK_TILE
    eb = PEER_EXP_BLOCK
    nk = PEER_NKEYS
    nb = modv.shape[2]
    n_e = u_b.shape[1] // eb
    n_tiles = t_all // tt - t0_tile
    seg = lambda i: jnp.where(i + t0_tile >= n_ctx_tok_tiles, 1, 0)
    by_i = pl.BlockSpec((eb // nk, PEER_HEADS, tt), lambda i, j: (j, 0, i + t0_tile))
    rc = PEER_GATE_ROWS
    by_h = pl.BlockSpec((PEER_HEADS, nk // rc, rc // 2, tt), lambda i, j: (0, 0, 0, i + t0_tile))
    r2, w2 = (a.reshape(PEER_HEADS, nk // rc, rc // 2, t_all) for a in (r2, w2))
    return pl.pallas_call(
        _peer_ffn_body,
        grid=(n_tiles, n_e),
        in_specs=[pl.BlockSpec((d, tt), lambda i, j: (0, i + t0_tile)),
                  pl.BlockSpec((None, eb, d), lambda i, j: (layer, j, 0)),
                  pl.BlockSpec((None, d, eb), lambda i, j: (layer, 0, j)),
                  by_i, by_i, by_h, by_h,
                  pl.BlockSpec((tt, d), lambda i, j: (i + t0_tile, 0)),
                  pl.BlockSpec((None, N_MOD, nb, d), lambda i, j: (seg(i), 0, 0, 0))],
        out_specs=pl.BlockSpec((tt, d), lambda i, j: (i + t0_tile, 0)),
        out_shape=jax.ShapeDtypeStruct((t_all, d), F32),
        scratch_shapes=[pltpu.VMEM((d, tt), F32), pltpu.VMEM((eb, tt), BF16),
                        pltpu.VMEM((2, eb // nk, PEER_HEADS, BF16_ROWS, LANES), BF16),
                        pltpu.VMEM((d, tt), BF16)],
        compiler_params=_cparams(2),
        name="peer_ffn",
    )(h2t, u_b, vt_b, e1, cnt, r2, w2, x1, modv)


def _final_body(x_ref, g_ref, o_ref):
    x = x_ref[...]
    ms = jnp.mean(x * x, axis=-1, keepdims=True)
    o_ref[...] = jnp.swapaxes(x * lax.rsqrt(ms + NORM_EPS) * g_ref[...], 0, 1)


def _final_norm(xs, g, l_ctx):
    lt, nb, d = xs.shape
    tt = ROW_TILE_T
    t0 = l_ctx // tt
    return pl.pallas_call(
        _final_body,
        grid=((lt - l_ctx) // tt,),
        in_specs=[pl.BlockSpec((tt, nb, d), lambda i: (i + t0, 0, 0)), pl.BlockSpec((1, d), lambda i: (0, 0))],
        out_specs=pl.BlockSpec((nb, tt, d), lambda i: (0, i, 0)),
        out_shape=jax.ShapeDtypeStruct((nb, lt - l_ctx, d), F32),
        compiler_params=_cparams(1),
        name="final_norm",
    )(xs, g.reshape(1, d))


def _pad_cols(w, width):
    return jnp.pad(w, [(0, 0)] * (w.ndim - 1) + [(0, width - w.shape[-1])])


def _rwkv_slabs(w):
    gw3 = 3 * GROUP_W
    parts = [w[..., :gw3],
             _pad_cols(w[..., gw3:gw3 + 64], 128),
             _pad_cols(w[..., gw3 + 64:gw3 + 128], 128),
             _pad_cols(w[..., gw3 + 128:gw3 + 192], 128)]
    return jnp.concatenate(parts, axis=-1)


def _layer_weights(l, p):
    d = p['w_in'].shape[1]
    w_in = p['w_in'][l]
    wa_, wb_, wc_, wd_ = jnp.split(w_in, [RWKV_COLS, RWKV_COLS + RET_COLS, RWKV_COLS + RET_COLS + LRU_COLS], axis=-1)
    perm = _ret_perm()
    gw = GROUP_W
    wb_ = jnp.concatenate([wb_[:, 0:gw][:, perm], wb_[:, gw:2 * gw][:, perm], wb_[:, 2 * gw:]], axis=-1)
    out = {'w_in': jnp.concatenate([_rwkv_slabs(wa_), wb_, wc_, wd_], axis=-1).astype(BF16)}
    out['mu'] = _rwkv_slabs(p['rwkv_mu'][l])[:, None, :]
    out['gup'] = jnp.pad(p['rwkv_g_up'][l], ((0, 64), (0, 0))).astype(BF16)
    lora = RWKV_LORA
    out['wup'] = jnp.stack([jnp.pad(p['rwkv_w_up'][l, dd], ((dd * lora, 128 - (dd + 1) * lora), (0, 0)))
                            for dd in range(2)]).astype(BF16)
    out['aup'] = jnp.stack([jnp.pad(p['rwkv_a_up'][l, dd], ((dd * lora, 128 - (dd + 1) * lora), (0, 0)))
                            for dd in range(2)]).astype(BF16)
    out['rwkv_vecs'] = jnp.concatenate([p['rwkv_k_k'][l][None], p['rwkv_k_a'][l][None],
                                        p['rwkv_w0'][l], p['rwkv_a0'][l]], axis=0)
    out['cw'] = jnp.concatenate([p['lru_conv_w'][l], p['lru_conv_b'][l][None]], axis=0)
    out['wa'] = _block_diag(p['lru_wa'][l]).astype(BF16)
    out['wx'] = _block_diag(p['lru_wx'][l]).astype(BF16)
    out['lru_vecs'] = jnp.stack([p['lru_ba'][l], p['lru_bx'][l], p['lru_lambda'][l]], axis=1)
    out['s5'] = _s5_params(p['s5_a_re'][l], p['s5_a_im'][l], p['s5_log_dt'][l], p['s5_b_re'][l],
                           p['s5_b_im'][l], p['s5_c_re'][l], p['s5_c_im'][l])
    out['out_vecs'] = jnp.stack([p['rwkv_r_k'][l].reshape(-1), p['rwkv_ln_g'][l], p['rwkv_ln_b'][l],
                                 p['ret_gn_g'][l], p['s5_d'][l]], axis=0)
    out['gluw'] = p['s5_glu_w'][l].astype(BF16)
    out['glub'] = p['s5_glu_b'][l][None]
    out['wo'] = p['w_out'][l].astype(BF16)
    out['n2'] = p['norm2_g'][l][None]
    out['wq'] = p['peer_wq'][l].T.astype(BF16)
    out['keys'] = p['peer_keys'][l].reshape(PEER_HEADS * 2, PEER_NKEYS, PEER_DKEY).astype(BF16)
    return out


def kernel(x, c, ctx, c_ctx, norm1_g, norm2_g, ada_w, ada_b, w_in, w_out, rwkv_mu, rwkv_w0, rwkv_w_up, rwkv_a0, rwkv_a_up, rwkv_g_up, rwkv_k_k, rwkv_k_a, rwkv_r_k, rwkv_ln_g, rwkv_ln_b, ret_gn_g, lru_conv_w, lru_conv_b, lru_wa, lru_ba, lru_wx, lru_bx, lru_lambda, s5_a_re, s5_a_im, s5_log_dt, s5_b_re, s5_b_im, s5_c_re, s5_c_im, s5_d, s5_glu_w, s5_glu_b, peer_wq, peer_keys, peer_u, peer_v, final_norm_g):
    p = dict(w_in=w_in, w_out=w_out, norm2_g=norm2_g, rwkv_mu=rwkv_mu, rwkv_w0=rwkv_w0, rwkv_w_up=rwkv_w_up,
             rwkv_a0=rwkv_a0, rwkv_a_up=rwkv_a_up, rwkv_g_up=rwkv_g_up, rwkv_k_k=rwkv_k_k, rwkv_k_a=rwkv_k_a,
             rwkv_r_k=rwkv_r_k, rwkv_ln_g=rwkv_ln_g, rwkv_ln_b=rwkv_ln_b, ret_gn_g=ret_gn_g,
             lru_conv_w=lru_conv_w, lru_conv_b=lru_conv_b, lru_wa=lru_wa, lru_ba=lru_ba, lru_wx=lru_wx,
             lru_bx=lru_bx, lru_lambda=lru_lambda, s5_a_re=s5_a_re, s5_a_im=s5_a_im, s5_log_dt=s5_log_dt,
             s5_b_re=s5_b_re, s5_b_im=s5_b_im, s5_c_re=s5_c_re, s5_c_im=s5_c_im, s5_d=s5_d,
             s5_glu_w=s5_glu_w, s5_glu_b=s5_glu_b, peer_wq=peer_wq, peer_keys=peer_keys, peer_u=peer_u,
             peer_v=peer_v)
    nb, l_lat, d = x.shape
    l_ctx = ctx.shape[1]
    lt = l_ctx + l_lat
    depth = w_in.shape[0]
    assert 128 % (nb * N_HEADS) == 0 and l_ctx % 128 == 0 and l_lat % 128 == 0
    assert (l_ctx * nb) % PEER_TOK_TILE == 0 and (l_lat * nb) % PEER_TOK_TILE == 0
    k_lo = LANES // (nb * N_HEADS)
    n_ctx_tiles = l_ctx // ROW_TILE_T
    n_ctx_tok_tiles = l_ctx * nb // PEER_TOK_TILE
    hs = _head_sum_matrix()
    ret_tabs = _ret_tables(lt)
    u_all = peer_u.astype(BF16)
    vt_all = jnp.swapaxes(peer_v, 1, 2).astype(BF16)

    xs = jnp.transpose(jnp.concatenate([ctx, x], axis=1), (1, 0, 2))
    mod_rows = 8 * ((nb + 1 + 7) // 8)
    cvec = jnp.zeros((mod_rows, d), F32).at[:nb].set(c).at[nb].set(c_ctx)

    for l in range(depth):
        wl = _layer_weights(l, p)
        last = l == depth - 1
        mod = _modulation(cvec, ada_w[l], ada_b[l])
        mod_lat = mod[:nb].reshape(nb, N_MOD, d).transpose(1, 0, 2)
        mod_ctx = jnp.broadcast_to(mod[nb].reshape(N_MOD, 1, d), (N_MOD, nb, d))
        modv = jnp.stack([mod_ctx, mod_lat])

        za, zb_qkv, zb_gate, zc, zd = _in_proj(xs, norm1_g[l], modv, wl['w_in'], n_ctx_tiles)

        r, k, v, g, av, w_dec, kd, bv = _rwkv_features(za, wl['mu'], wl['gup'], wl['wup'], wl['aup'],
                                                       wl['rwkv_vecs'], hs, n_ctx_tiles)
        ex = lambda t: _rwkv_expand_k(t, k_lo)
        ya_f, ya_b = _rwkv_scan(ex(r), ex(av), ex(w_dec), ex(kd), ex(bv), ex(v), l_ctx)
        ya = jnp.stack([_rwkv_collapse_k(ya_f, nb, k_lo), _rwkv_collapse_k(ya_b, nb, k_lo)])
        ob = _retention(zb_qkv, ret_tabs, l_ctx)
        hc = _lru(zc, wl['cw'], wl['wa'], wl['wx'], wl['lru_vecs'], l_ctx)
        yd = _from_colmajor(_s5(_to_colmajor(zd, l_ctx), *wl['s5'], l_ctx), l_ctx)

        t0 = n_ctx_tiles if last else 0
        x1, h2 = _out_proj(xs, modv, ya, (r, k, v, g), ob, zb_gate, hc, zc, yd, zd, wl['out_vecs'], hs,
                           wl['gluw'], wl['glub'], wl['wo'], wl['n2'], n_ctx_tiles, t0)
        t0p = n_ctx_tok_tiles if last else 0
        e1, cnt, r2, w2 = _peer_route(h2, wl['wq'], wl['keys'], t0p)
        x2 = _peer_ffn(h2, u_all, vt_all, l, e1, cnt, r2, w2, x1.reshape(lt * nb, d), modv,
                       n_ctx_tok_tiles, t0p)
        xs = x2.reshape(lt, nb, d)

    return _final_norm(xs, final_norm_g, l_ctx)
```
